```python
import math
import jax, jax.numpy as jnp
from jax import lax
import numpy as np

D_MODEL = 4096
BATCH = 4
SEQ = 2048
DEPTH = 2
DEC_BATCH = 8
DEC_SEQ = 4
PAST_LEN = 16384
PAGE_SIZE = 128

SSM_GROUP = 16
SSM_GROUPS = D_MODEL // SSM_GROUP
SSM_STATE = 64
SSM_CHUNK = 128
D_FF = ((8 * D_MODEL // 3 + 255) // 256) * 256
CONV_W = 3
HEAD_DIM = 128
N_HEADS = D_MODEL // HEAD_DIM
N_KV_HEADS = 4
GQA = N_HEADS // N_KV_HEADS
CMP_STRIDE = 16
CMP_BLOCK = 2 * CMP_STRIDE
CMP_HIDDEN = 2 * HEAD_DIM
SEL_BLOCK = 64
N_SEL = 16
WINDOW = 512
Q_BLOCK = 128
N_BRANCH_KV = 6
N_A_LAYERS = DEPTH // 2
N_B_LAYERS = DEPTH - N_A_LAYERS
RMS_EPS = 1e-6

kernel_name = 'yoco_s5_nsa_convffn_step'


def rmsnorm(x, g):
    xf = x.astype(jnp.float32)
    y = xf * lax.rsqrt(jnp.mean(xf * xf, axis=-1, keepdims=True) + RMS_EPS)
    return (y * g.astype(jnp.float32)).astype(x.dtype)


def _ssm_combine(e1, e2):
    a1, b1 = e1
    a2, b2 = e2
    return a2 * a1, a2 * b1 + b2


def s5_mixer(u, h0, lam_re, lam_im, log_step, b_re, b_im, c_re, c_im, d_skip, w_glu):
    f32 = jnp.float32
    bsz, length, _ = u.shape
    lam = lax.complex(lam_re.astype(f32), lam_im.astype(f32))
    dt = jnp.exp(log_step.astype(f32))[:, None]
    lam_bar = jnp.exp(lam * dt)
    b_bar = ((lam_bar - 1.0) / lam)[..., None] * lax.complex(b_re.astype(f32), b_im.astype(f32))
    c = lax.complex(c_re.astype(f32), c_im.astype(f32))
    chunk = math.gcd(length, SSM_CHUNK)
    n_chunks = length // chunk
    uc = u.astype(f32).reshape(bsz, n_chunks, chunk, SSM_GROUPS, SSM_GROUP).transpose(1, 0, 2, 3, 4)

    def scan_block(h, u_c):
        bu = jnp.einsum('bcgn,gpn->bcgp', u_c.astype(jnp.complex64), b_bar)
        bu = bu.at[:, 0].add(lam_bar * h)
        a = jnp.broadcast_to(lam_bar, bu.shape)
        _, hs = lax.associative_scan(_ssm_combine, (a, bu), axis=1)
        y = jnp.real(jnp.einsum('bcgp,gnp->bcgn', hs, c))
        return hs[:, -1], y

    h_last, ys = lax.scan(scan_block, h0, uc)
    y = ys.transpose(1, 0, 2, 3, 4).reshape(bsz, length, D_MODEL) + d_skip.astype(f32) * u.astype(f32)
    z = jax.nn.gelu(y).astype(u.dtype)
    vg = z @ w_glu
    out = vg[..., :D_MODEL] * jax.nn.sigmoid(vg[..., D_MODEL:])
    return out, h_last


def conv_ffn(h, buf, w_in, conv_w, conv_b, w_down):
    length = h.shape[1]
    up = h @ w_in
    val, gate = up[..., :D_FF], up[..., D_FF:]
    g_ext = jnp.concatenate([buf.astype(gate.dtype), gate], axis=1)
    gc = conv_b
    for k in range(CONV_W):
        gc = gc + conv_w[k] * g_ext[:, k:k + length]
    out = (jax.nn.silu(gc) * val) @ w_down
    return out, g_ext[:, -(CONV_W - 1):]


def shared_kv(x, kv_norm, w_kv):
    bsz, t, _ = x.shape
    s = rmsnorm(x, kv_norm)
    return (s @ w_kv).reshape(bsz, t, N_BRANCH_KV, N_KV_HEADS, HEAD_DIM)


def compress_tokens(k, w1, b1, w2, b2, pe):
    bsz, t = k.shape[:2]
    n16 = t // CMP_STRIDE
    half = CMP_STRIDE * HEAD_DIM
    ch = k[:, :n16 * CMP_STRIDE].reshape(bsz, n16, CMP_STRIDE, N_KV_HEADS, HEAD_DIM)
    ch = ch.transpose(0, 1, 3, 2, 4).reshape(bsz, n16, N_KV_HEADS, half)
    pre = ch[:, :-1] @ w1[:half] + ch[:, 1:] @ w1[half:] + (pe.reshape(-1) @ w1 + b1)
    return jax.nn.gelu(pre) @ w2 + b2


def to_blocks(k):
    bsz, t = k.shape[:2]
    nb = -(-t // SEL_BLOCK)
    k = jnp.pad(k, ((0, 0), (0, nb * SEL_BLOCK - t), (0, 0), (0, 0)))
    return k.reshape(bsz, nb, SEL_BLOCK, N_KV_HEADS, HEAD_DIM).transpose(0, 3, 1, 2, 4)


def kv_side(rows, cmp_w1, cmp_b1, cmp_w2, cmp_b2, cmp_pe):
    cmp_k = compress_tokens(rows[:, :, 0], cmp_w1[0], cmp_b1[0], cmp_w2[0], cmp_b2[0], cmp_pe[0])
    cmp_v = compress_tokens(rows[:, :, 1], cmp_w1[1], cmp_b1[1], cmp_w2[1], cmp_b2[1], cmp_pe[1])
    return cmp_k, cmp_v, to_blocks(rows[:, :, 2]), to_blocks(rows[:, :, 3])


def query_side(h, w_qg):
    bsz, t, _ = h.shape
    hd = N_HEADS * HEAD_DIM
    qg = h @ w_qg
    q = qg[..., :hd].reshape(bsz, t, N_HEADS, HEAD_DIM)
    gates = jax.nn.sigmoid(qg[..., hd:].astype(jnp.float32)).reshape(bsz, t, N_HEADS, 3)
    return q, gates


def masked_softmax(s, mask):
    s = jnp.where(mask, s, -jnp.inf)
    m = jnp.max(s, axis=-1, keepdims=True)
    m = jnp.where(jnp.isfinite(m), m, 0.0)
    e = jnp.exp(s - m)
    den = jnp.sum(e, axis=-1, keepdims=True)
    return e / jnp.where(den > 0, den, 1.0)


def nsa_attend(q, gates, q_pos, cmp_k, cmp_v, sel_kb, sel_vb, win_k, win_v, win_pos):
    f32 = jnp.float32
    bsz, tq = q.shape[:2]
    scale = HEAD_DIM ** -0.5
    qg = q.reshape(bsz, tq, N_KV_HEADS, GQA, HEAD_DIM)
    n_cmp = cmp_k.shape[1]
    cstart = jnp.arange(n_cmp) * CMP_STRIDE
    cmp_end = cstart + (CMP_BLOCK - 1)
    s = jnp.einsum('bqhgd,bchd->bhgqc', qg, cmp_k, preferred_element_type=f32) * scale
    p_cmp = masked_softmax(s, cmp_end[None, :] <= q_pos[:, None])
    o_cmp = jnp.einsum('bhgqc,bchd->bqhgd', p_cmp.astype(cmp_v.dtype), cmp_v, preferred_element_type=f32)
    n_blk = sel_kb.shape[2]
    bstart = jnp.arange(n_blk) * SEL_BLOCK
    overlap = ((cstart[:, None] < bstart[None, :] + SEL_BLOCK) & (cstart[:, None] + CMP_BLOCK > bstart[None, :])).astype(f32)
    imp = jnp.einsum('bhgqc,cj->bhqj', p_cmp, overlap)
    q_blk = q_pos[:, None] // SEL_BLOCK
    blk = jnp.arange(n_blk)[None, :]
    forced = (blk == 0) | (blk == q_blk) | (blk == q_blk - 1)
    future = bstart[None, :] > q_pos[:, None]
    imp = jnp.where(future, -jnp.inf, jnp.where(forced, jnp.inf, imp))
    n_sel = min(N_SEL, n_blk)
    _, idx = lax.top_k(imp, n_sel)
    bi = jnp.arange(bsz)[:, None, None, None]
    hi = jnp.arange(N_KV_HEADS)[None, :, None, None]
    ks = sel_kb[bi, hi, idx].reshape(bsz, N_KV_HEADS, tq, n_sel * SEL_BLOCK, HEAD_DIM)
    vs = sel_vb[bi, hi, idx].reshape(bsz, N_KV_HEADS, tq, n_sel * SEL_BLOCK, HEAD_DIM)
    kpos = (idx[..., None] * SEL_BLOCK + jnp.arange(SEL_BLOCK)).reshape(bsz, N_KV_HEADS, tq, n_sel * SEL_BLOCK)
    s = jnp.einsum('bqhgd,bhqkd->bhgqk', qg, ks, preferred_element_type=f32) * scale
    p = masked_softmax(s, (kpos <= q_pos[None, None, :, None])[:, :, None])
    o_sel = jnp.einsum('bhgqk,bhqkd->bqhgd', p.astype(vs.dtype), vs, preferred_element_type=f32)
    s = jnp.einsum('bqhgd,bkhd->bhgqk', qg, win_k, preferred_element_type=f32) * scale
    wmask = ((win_pos[None, :] <= q_pos[:, None]) & (win_pos[None, :] > q_pos[:, None] - WINDOW)
             & (win_pos[None, :] >= 0))
    p = masked_softmax(s, wmask)
    o_win = jnp.einsum('bhgqk,bkhd->bqhgd', p.astype(win_v.dtype), win_v, preferred_element_type=f32)
    g = gates.reshape(bsz, tq, N_KV_HEADS, GQA, 3)
    o = g[..., 0:1] * o_cmp + g[..., 1:2] * o_sel + g[..., 2:3] * o_win
    return o.reshape(bsz, tq, N_HEADS * HEAD_DIM).astype(q.dtype)


def nsa_prompt(h, w_qg, w_o, side, win_rows):
    cmp_k, cmp_v, sel_kb, sel_vb = side
    bsz, t, _ = h.shape
    q, gates = query_side(h, w_qg)
    qb_len = math.gcd(t, Q_BLOCK)
    win_pad = jnp.pad(win_rows, ((0, 0), (WINDOW, 0), (0, 0), (0, 0), (0, 0)))

    def one_block(qb):
        start = qb * qb_len
        q_b = lax.dynamic_slice_in_dim(q, start, qb_len, axis=1)
        g_b = lax.dynamic_slice_in_dim(gates, start, qb_len, axis=1)
        w_b = lax.dynamic_slice_in_dim(win_pad, start, WINDOW + qb_len, axis=1)
        q_pos = start + jnp.arange(qb_len)
        w_pos = start - WINDOW + jnp.arange(WINDOW + qb_len)
        return nsa_attend(q_b, g_b, q_pos, cmp_k, cmp_v, sel_kb, sel_vb, w_b[:, :, 0], w_b[:, :, 1], w_pos)

    o = lax.map(one_block, jnp.arange(t // qb_len))
    o = o.transpose(1, 0, 2, 3).reshape(bsz, t, N_HEADS * HEAD_DIM)
    return o @ w_o


def nsa_sample(h, w_qg, w_o, side, win_full, past_len):
    cmp_k, cmp_v, sel_kb, sel_vb = side
    tq = h.shape[1]
    win_buf = win_full.shape[1] - tq
    q, gates = query_side(h, w_qg)
    q_pos = past_len + jnp.arange(tq)
    w_pos = past_len - win_buf + jnp.arange(win_buf + tq)
    o = nsa_attend(q, gates, q_pos, cmp_k, cmp_v, sel_kb, sel_vb, win_full[:, :, 0], win_full[:, :, 1], w_pos)
    return o @ w_o


def setup_inputs(seed: int = 0) -> dict:
    key = jax.random.key(seed)
    keys = jax.random.split(key, 40)
    cnt = [0]

    def nk():
        cnt[0] += 1
        return keys[cnt[0] - 1]

    def nrm(shape, scale):
        return jax.random.normal(nk(), shape, jnp.float32) * scale

    f32 = jnp.float32
    n_pages = PAST_LEN // PAGE_SIZE
    n_used = DEC_BATCH * n_pages
    n_pool = n_used + max(1, n_used // 4)
    win_buf = min(WINDOW, PAST_LEN)
    hd = N_HEADS * HEAD_DIM
    x_prompt = nrm((BATCH, SEQ, D_MODEL), 1.0)
    x_sample = nrm((DEC_BATCH, DEC_SEQ, D_MODEL), 1.0)
    state_ssm_re = nrm((N_A_LAYERS, DEC_BATCH, SSM_GROUPS, SSM_STATE), 0.5)
    state_ssm_im = nrm((N_A_LAYERS, DEC_BATCH, SSM_GROUPS, SSM_STATE), 0.5)
    state_ffn_conv = nrm((DEPTH, DEC_BATCH, CONV_W - 1, D_FF), 1.0)
    cache_kv = nrm((n_pool, PAGE_SIZE, 4, N_KV_HEADS, HEAD_DIM), 1.0)
    cache_win = nrm((DEC_BATCH, win_buf, 2, N_KV_HEADS, HEAD_DIM), 1.0)
    perm = jax.random.permutation(nk(), n_pool)
    page_table = perm[:n_used].reshape(DEC_BATCH, n_pages).astype(jnp.int32)
    n_idx = jnp.arange(SSM_STATE, dtype=f32)
    return {
        'x_prompt': x_prompt,
        'x_sample': x_sample,
        'state_ssm_re': state_ssm_re,
        'state_ssm_im': state_ssm_im,
        'state_ffn_conv': state_ffn_conv,
        'cache_kv': cache_kv,
        'cache_win': cache_win,
        'page_table': page_table,
        'attn_norm': 1.0 + nrm((DEPTH, D_MODEL), 0.01),
        'ffn_norm': 1.0 + nrm((DEPTH, D_MODEL), 0.01),
        'final_norm': 1.0 + nrm((D_MODEL,), 0.01),
        'ssm_lam_re': -0.5 * jnp.exp(nrm((N_A_LAYERS, SSM_GROUPS, SSM_STATE), 0.02)),
        'ssm_lam_im': jnp.pi * n_idx + nrm((N_A_LAYERS, SSM_GROUPS, SSM_STATE), 0.01),
        'ssm_log_step': jax.random.uniform(nk(), (N_A_LAYERS, SSM_GROUPS), f32, math.log(1e-3), math.log(1e-1)),
        'ssm_b_re': nrm((N_A_LAYERS, SSM_GROUPS, SSM_STATE, SSM_GROUP), (2 * SSM_GROUP) ** -0.5),
        'ssm_b_im': nrm((N_A_LAYERS, SSM_GROUPS, SSM_STATE, SSM_GROUP), (2 * SSM_GROUP) ** -0.5),
        'ssm_c_re': nrm((N_A_LAYERS, SSM_GROUPS, SSM_GROUP, SSM_STATE), SSM_STATE ** -0.5),
        'ssm_c_im': nrm((N_A_LAYERS, SSM_GROUPS, SSM_GROUP, SSM_STATE), SSM_STATE ** -0.5),
        'ssm_d': nrm((N_A_LAYERS, D_MODEL), 1.0),
        'ssm_w_glu': nrm((N_A_LAYERS, D_MODEL, 2 * D_MODEL), D_MODEL ** -0.5),
        'ffn_w_in': nrm((DEPTH, D_MODEL, 2 * D_FF), D_MODEL ** -0.5),
        'ffn_conv_w': nrm((DEPTH, CONV_W, D_FF), CONV_W ** -0.5),
        'ffn_conv_b': nrm((DEPTH, D_FF), 0.01),
        'ffn_w_down': nrm((DEPTH, D_FF, D_MODEL), D_FF ** -0.5),
        'kv_norm': 1.0 + nrm((D_MODEL,), 0.01),
        'w_kv': nrm((D_MODEL, N_BRANCH_KV * N_KV_HEADS * HEAD_DIM), D_MODEL ** -0.5),
        'cmp_w1': nrm((2, CMP_BLOCK * HEAD_DIM, CMP_HIDDEN), (CMP_BLOCK * HEAD_DIM) ** -0.5),
        'cmp_b1': nrm((2, CMP_HIDDEN), 0.01),
        'cmp_w2': nrm((2, CMP_HIDDEN, HEAD_DIM), CMP_HIDDEN ** -0.5),
        'cmp_b2': nrm((2, HEAD_DIM), 0.01),
        'cmp_pe': nrm((2, CMP_BLOCK, HEAD_DIM), 0.02),
        'w_qg': nrm((N_B_LAYERS, D_MODEL, hd + 3 * N_HEADS), D_MODEL ** -0.5),
        'w_o': nrm((N_B_LAYERS, hd, D_MODEL), hd ** -0.5),
    }


def reference(x_prompt, x_sample, state_ssm_re, state_ssm_im, state_ffn_conv, cache_kv, cache_win, page_table,
              attn_norm, ffn_norm, final_norm, ssm_lam_re, ssm_lam_im, ssm_log_step, ssm_b_re, ssm_b_im,
              ssm_c_re, ssm_c_im, ssm_d, ssm_w_glu, ffn_w_in, ffn_conv_w, ffn_conv_b, ffn_w_down,
              kv_norm, w_kv, cmp_w1, cmp_b1, cmp_w2, cmp_b2, cmp_pe, w_qg, w_o):
    f32 = jnp.float32
    bsz = x_prompt.shape[0]
    dec_batch = x_sample.shape[0]
    n_pages = page_table.shape[1]
    past_len = n_pages * PAGE_SIZE
    win_buf = cache_win.shape[1]
    xp, xs = x_prompt, x_sample
    ssm_re_p, ssm_im_p, ssm_re_s, ssm_im_s = [], [], [], []
    conv_p, conv_s = [], []
    for layer in range(DEPTH):
        hp = rmsnorm(xp, attn_norm[layer])
        hs = rmsnorm(xs, attn_norm[layer])
        if layer < N_A_LAYERS:
            a = layer
            prm = (ssm_lam_re[a], ssm_lam_im[a], ssm_log_step[a], ssm_b_re[a], ssm_b_im[a],
                   ssm_c_re[a], ssm_c_im[a], ssm_d[a], ssm_w_glu[a])
            h0p = jnp.zeros((bsz, SSM_GROUPS, SSM_STATE), jnp.complex64)
            h0s = lax.complex(state_ssm_re[a].astype(f32), state_ssm_im[a].astype(f32))
            o_p, hl_p = s5_mixer(hp, h0p, *prm)
            o_s, hl_s = s5_mixer(hs, h0s, *prm)
            ssm_re_p.append(jnp.real(hl_p))
            ssm_im_p.append(jnp.imag(hl_p))
            ssm_re_s.append(jnp.real(hl_s))
            ssm_im_s.append(jnp.imag(hl_s))
        else:
            if layer == N_A_LAYERS:
                kv_p = shared_kv(xp, kv_norm, w_kv)
                kv_s = shared_kv(xs, kv_norm, w_kv)
                kv_rows_p = kv_p[:, :, :4]
                kv_rows_s = kv_s[:, :, :4]
                win_rows_p = kv_p[:, :, 4:]
                past_rows = cache_kv[page_table].reshape(dec_batch, past_len, 4, N_KV_HEADS, HEAD_DIM)
                full_s = jnp.concatenate([past_rows.astype(kv_rows_s.dtype), kv_rows_s], axis=1)
                side_p = kv_side(kv_rows_p, cmp_w1, cmp_b1, cmp_w2, cmp_b2, cmp_pe)
                side_s = kv_side(full_s, cmp_w1, cmp_b1, cmp_w2, cmp_b2, cmp_pe)
                win_full_s = jnp.concatenate([cache_win.astype(kv_s.dtype), kv_s[:, :, 4:]], axis=1)
                win_new_p = win_rows_p[:, max(win_rows_p.shape[1] - WINDOW, 0):]
                win_new_s = win_full_s[:, -win_buf:]
            bl = layer - N_A_LAYERS
            o_p = nsa_prompt(hp, w_qg[bl], w_o[bl], side_p, win_rows_p)
            o_s = nsa_sample(hs, w_qg[bl], w_o[bl], side_s, win_full_s, past_len)
        xp = xp + o_p
        xs = xs + o_s
        fp, cp = conv_ffn(rmsnorm(xp, ffn_norm[layer]), jnp.zeros((bsz, CONV_W - 1, D_FF), xp.dtype),
                          ffn_w_in[layer], ffn_conv_w[layer], ffn_conv_b[layer], ffn_w_down[layer])
        fs, cs = conv_ffn(rmsnorm(xs, ffn_norm[layer]), state_ffn_conv[layer],
                          ffn_w_in[layer], ffn_conv_w[layer], ffn_conv_b[layer], ffn_w_down[layer])
        xp = xp + fp
        xs = xs + fs
        conv_p.append(cp)
        conv_s.append(cs)
    y_prompt = rmsnorm(xp, final_norm)
    y_sample = rmsnorm(xs, final_norm)
    return (y_prompt, y_sample,
            jnp.stack(ssm_re_p), jnp.stack(ssm_im_p), jnp.stack(ssm_re_s), jnp.stack(ssm_im_s),
            jnp.stack(conv_p), jnp.stack(conv_s),
            kv_rows_p, kv_rows_s, win_new_p, win_new_s)
```

```python
import functools
import math

import jax
import jax.numpy as jnp
from jax import lax
from jax.experimental import pallas as pl
from jax.experimental.pallas import tpu as pltpu

F32 = jnp.float32
BF16 = jnp.bfloat16
I32 = jnp.int32

RMS_EPS = 1e-6
LANES = 128
SUBLANES = 8
VMEM_LIMIT = 56 * 1024 * 1024

SSM_GROUP = 16
SSM_STATE = 64
CONV_W = 3
HEAD_DIM = 128
N_KV_HEADS = 4
CMP_STRIDE = 16
CMP_BLOCK = 32
SEL_BLOCK = 64
SEL_SHIFT = SEL_BLOCK.bit_length() - 1
N_SEL = 16
WINDOW = 512
Q_BLOCK = 128
PAGE_SIZE = 128
NEG_BIG = -1e30

S5_SET_GROUPS = 4
S5_SETS = 8
S5_BLK_GROUPS = S5_SET_GROUPS * S5_SETS
S5_BLK_CH = S5_BLK_GROUPS * SSM_GROUP
S5_SET_LANES = S5_SET_GROUPS * SSM_STATE


def _cparams(sem, vmem=VMEM_LIMIT):
    return pltpu.CompilerParams(dimension_semantics=sem, vmem_limit_bytes=vmem)


def _rmsnorm_body(x_ref, g_ref, o_ref):
    x = x_ref[...].astype(F32)
    y = x * lax.rsqrt(jnp.mean(x * x, axis=-1, keepdims=True) + RMS_EPS)
    o_ref[...] = (y * g_ref[...]).astype(o_ref.dtype)


def rmsnorm(x, g, out_dtype):
    m, d = x.shape
    tm = math.gcd(m, 256)
    return pl.pallas_call(
        _rmsnorm_body,
        grid=(m // tm,),
        in_specs=[pl.BlockSpec((tm, d), lambda i: (i, 0)), pl.BlockSpec((1, d), lambda i: (0, 0))],
        out_specs=pl.BlockSpec((tm, d), lambda i: (i, 0)),
        out_shape=jax.ShapeDtypeStruct((m, d), out_dtype),
        compiler_params=_cparams(("parallel",)),
        name="rmsnorm",
    )(x, g.reshape(1, d).astype(F32))


def _mm_body(*refs, has_res, act):
    if has_res:
        x_ref, w_ref, res_ref, o_ref = refs
    else:
        x_ref, w_ref, o_ref = refs
    y = jnp.dot(x_ref[...], w_ref[...], preferred_element_type=F32)
    if act == "sigmoid":
        y = jax.nn.sigmoid(y)
    if has_res:
        y = res_ref[...] + y
    o_ref[...] = y.astype(o_ref.dtype)


def matmul(x, w, res=None, out_dtype=F32, act=None, tm=512, tn=256):
    m, k = x.shape
    n = w.shape[1]
    tm = math.gcd(m, tm)
    tn = math.gcd(n, tn)
    in_specs = [pl.BlockSpec((tm, k), lambda i, j: (i, 0)), pl.BlockSpec((k, tn), lambda i, j: (0, j))]
    args = [x, w]
    if res is not None:
        in_specs.append(pl.BlockSpec((tm, tn), lambda i, j: (i, j)))
        args.append(res)
    return pl.pallas_call(
        functools.partial(_mm_body, has_res=res is not None, act=act),
        grid=(m // tm, n // tn),
        in_specs=in_specs,
        out_specs=pl.BlockSpec((tm, tn), lambda i, j: (i, j)),
        out_shape=jax.ShapeDtypeStruct((m, n), out_dtype),
        compiler_params=_cparams(("parallel", "arbitrary")),
        name="matmul",
    )(*args)


def _glu_body(x_ref, wa_ref, wb_ref, res_ref, o_ref):
    x = x_ref[...]
    a = jnp.dot(x, wa_ref[...], preferred_element_type=F32)
    b = jnp.dot(x, wb_ref[...], preferred_element_type=F32)
    o_ref[...] = res_ref[...] + a * jax.nn.sigmoid(b)


def glu_matmul(x, w, res, tm=1024, tn=256):
    m, k = x.shape
    n = w.shape[1] // 2
    tm = math.gcd(m, tm)
    tn = math.gcd(n, tn)
    nt = n // tn
    return pl.pallas_call(
        _glu_body,
        grid=(m // tm, nt),
        in_specs=[pl.BlockSpec((tm, k), lambda i, j: (i, 0)),
                  pl.BlockSpec((k, tn), lambda i, j: (0, j)),
                  pl.BlockSpec((k, tn), lambda i, j: (0, j + nt)),
                  pl.BlockSpec((tm, tn), lambda i, j: (i, j))],
        out_specs=pl.BlockSpec((tm, tn), lambda i, j: (i, j)),
        out_shape=jax.ShapeDtypeStruct((m, n), F32),
        compiler_params=_cparams(("parallel", "arbitrary")),
        name="glu_matmul",
    )(x, w, w, res)


def _conv_gate(g, g1, g2, cw_ref, cb_ref, val):
    gc = cb_ref[...] + cw_ref[0:1, :] * g2 + cw_ref[1:2, :] * g1 + cw_ref[2:3, :] * g
    return (gc * jax.nn.sigmoid(gc) * val).astype(BF16)


def _ffn_up_seq_body(x_ref, wv_ref, wg_ref, cw_ref, cb_ref, init_ref, act_ref, st_ref, carry_ref, *, tiles_per_seq):
    i = pl.program_id(0)
    j = pl.program_id(1)
    x = x_ref[...]
    val = jnp.dot(x, wv_ref[...], preferred_element_type=F32)
    g = jnp.dot(x, wg_ref[...], preferred_element_type=F32)
    tm = g.shape[0]

    @pl.when(lax.rem(i, tiles_per_seq) == 0)
    def _():
        carry_ref[j] = init_ref[0]

    prev = carry_ref[j]
    p6 = prev[6:7, :]
    p7 = prev[7:8, :]
    row = lax.broadcasted_iota(I32, g.shape, 0)
    g1 = jnp.where(row == 0, p7, pltpu.roll(g, 1, axis=0))
    g2 = jnp.where(row == 0, p6, jnp.where(row == 1, p7, pltpu.roll(g, 2, axis=0)))
    act_ref[...] = _conv_gate(g, g1, g2, cw_ref, cb_ref, val)
    last = g[tm - SUBLANES:, :]
    carry_ref[j] = last
    st_ref[0] = last


def _ffn_up_short_body(x_ref, wv_ref, wg_ref, cw_ref, cb_ref, p1_ref, p2_ref, act_ref, g_ref, *, seq_len):
    x = x_ref[...]
    val = jnp.dot(x, wv_ref[...], preferred_element_type=F32)
    g = jnp.dot(x, wg_ref[...], preferred_element_type=F32)
    tmod = lax.rem(lax.broadcasted_iota(I32, g.shape, 0), seq_len)
    g1 = jnp.where(tmod >= 1, pltpu.roll(g, 1, axis=0), p1_ref[...])
    g2 = jnp.where(tmod >= 2, pltpu.roll(g, 2, axis=0), p2_ref[...])
    act_ref[...] = _conv_gate(g, g1, g2, cw_ref, cb_ref, val)
    g_ref[...] = g


def ffn_up(h, w_in, conv_w, conv_b, buf, seq_len, tm=1024, tn=256):
    m, k = h.shape
    f = w_in.shape[1] // 2
    bsz = m // seq_len
    tn = math.gcd(f, tn)
    nt = f // tn
    cw = jnp.zeros((SUBLANES, f), F32).at[:CONV_W].set(conv_w.astype(F32))
    cb = conv_b.reshape(1, f).astype(F32)
    buf = buf.astype(F32)
    if seq_len % SUBLANES == 0 and seq_len >= 2 * SUBLANES:
        tm = math.gcd(seq_len, tm)
        tps = seq_len // tm
        init = jnp.zeros((bsz, SUBLANES, f), F32).at[:, SUBLANES - 2:].set(buf)
        act, st = pl.pallas_call(
            functools.partial(_ffn_up_seq_body, tiles_per_seq=tps),
            grid=(m // tm, nt),
            in_specs=[pl.BlockSpec((tm, k), lambda i, j: (i, 0)),
                      pl.BlockSpec((k, tn), lambda i, j: (0, j)),
                      pl.BlockSpec((k, tn), lambda i, j: (0, j + nt)),
                      pl.BlockSpec((SUBLANES, tn), lambda i, j: (0, j)),
                      pl.BlockSpec((1, tn), lambda i, j: (0, j)),
                      pl.BlockSpec((1, SUBLANES, tn), lambda i, j: (i // tps, 0, j))],
            out_specs=[pl.BlockSpec((tm, tn), lambda i, j: (i, j)),
                       pl.BlockSpec((1, SUBLANES, tn), lambda i, j: (i, 0, j))],
            out_shape=[jax.ShapeDtypeStruct((m, f), BF16), jax.ShapeDtypeStruct((m // tm, SUBLANES, f), F32)],
            scratch_shapes=[pltpu.VMEM((nt, SUBLANES, tn), F32)],
            compiler_params=_cparams(("arbitrary", "arbitrary")),
            name="ffn_up_seq",
        )(h, w_in, w_in, cw, cb, init)
        return act, st[tps - 1::tps, SUBLANES - 2:]
    assert seq_len >= 2
    pos = jnp.arange(m) % seq_len
    b0 = jnp.repeat(buf[:, 0], seq_len, axis=0)
    b1 = jnp.repeat(buf[:, 1], seq_len, axis=0)
    p1 = jnp.where((pos == 0)[:, None], b1, 0.0)
    p2 = jnp.where((pos == 0)[:, None], b0, jnp.where((pos == 1)[:, None], b1, 0.0))
    act, g = pl.pallas_call(
        functools.partial(_ffn_up_short_body, seq_len=seq_len),
        grid=(nt,),
        in_specs=[pl.BlockSpec((m, k), lambda j: (0, 0)),
                  pl.BlockSpec((k, tn), lambda j: (0, j)),
                  pl.BlockSpec((k, tn), lambda j: (0, j + nt)),
                  pl.BlockSpec((SUBLANES, tn), lambda j: (0, j)),
                  pl.BlockSpec((1, tn), lambda j: (0, j)),
                  pl.BlockSpec((m, tn), lambda j: (0, j)),
                  pl.BlockSpec((m, tn), lambda j: (0, j))],
        out_specs=[pl.BlockSpec((m, tn), lambda j: (0, j)), pl.BlockSpec((m, tn), lambda j: (0, j))],
        out_shape=[jax.ShapeDtypeStruct((m, f), BF16), jax.ShapeDtypeStruct((m, f), F32)],
        compiler_params=_cparams(("parallel",)),
        name="ffn_up_short",
    )(h, w_in, w_in, cw, cb, p1, p2)
    return act, g.reshape(bsz, seq_len, f)[:, seq_len - 2:]


def _s5_body(u_ref, wb_ref, wc_ref, lre_ref, lim_ref, d_ref, h0re_ref, h0im_ref,
             z_ref, hre_ref, him_ref, sr0, sr1, si0, si1, hst_re, hst_im, *, tc_len, last_step):
    tci = pl.program_id(2)
    u = u_ref[0]
    halves = S5_SET_LANES // LANES
    sre = (sr0, sr1)
    sim = (si0, si1)

    def set_rows(j):
        return pl.ds(j, tc_len, stride=S5_SETS)

    for ti in range(S5_BLK_CH // LANES):
        ut = u[:, ti * LANES:(ti + 1) * LANES]
        hi = ut.astype(BF16)
        lo = (ut - hi.astype(F32)).astype(BF16)
        lhs = jnp.concatenate([hi, lo], axis=1)
        for jj in range(2):
            j = ti * 2 + jj
            bre = jnp.dot(lhs, wb_ref[0, j, 0], preferred_element_type=F32)
            bim = jnp.dot(lhs, wb_ref[0, j, 1], preferred_element_type=F32)
            for c in range(halves):
                sre[c][set_rows(j), :] = bre[:, c * LANES:(c + 1) * LANES]
                sim[c][set_rows(j), :] = bim[:, c * LANES:(c + 1) * LANES]

    @pl.when(tci == 0)
    def _():
        hst_re[...] = h0re_ref[0, 0]
        hst_im[...] = h0im_ref[0, 0]

    lam_r = lre_ref[0]
    lam_i = lim_ref[0]
    lr = [lam_r[:, c * LANES:(c + 1) * LANES] for c in range(halves)]
    li = [lam_i[:, c * LANES:(c + 1) * LANES] for c in range(halves)]

    def step(t, carry):
        r0 = pl.multiple_of(t * S5_SETS, S5_SETS)
        out = []
        for c in range(halves):
            hr, hi_ = carry[2 * c], carry[2 * c + 1]
            nr = lr[c] * hr - li[c] * hi_ + sre[c][pl.ds(r0, S5_SETS), :]
            ni = lr[c] * hi_ + li[c] * hr + sim[c][pl.ds(r0, S5_SETS), :]
            sre[c][pl.ds(r0, S5_SETS), :] = nr
            sim[c][pl.ds(r0, S5_SETS), :] = ni
            out += [nr, ni]
        return tuple(out)

    h_in = hst_re[...]
    g_in = hst_im[...]
    init = []
    for c in range(halves):
        init += [h_in[:, c * LANES:(c + 1) * LANES], g_in[:, c * LANES:(c + 1) * LANES]]
    fin = lax.fori_loop(0, tc_len, step, tuple(init), unroll=8)
    for c in range(halves):
        hst_re[:, c * LANES:(c + 1) * LANES] = fin[2 * c]
        hst_im[:, c * LANES:(c + 1) * LANES] = fin[2 * c + 1]

    @pl.when(tci == last_step // tc_len)
    def _():
        r0 = (last_step % tc_len) * S5_SETS
        for c in range(halves):
            hre_ref[0, 0, :, c * LANES:(c + 1) * LANES] = sre[c][pl.ds(r0, S5_SETS), :]
            him_ref[0, 0, :, c * LANES:(c + 1) * LANES] = sim[c][pl.ds(r0, S5_SETS), :]

    for c in range(S5_BLK_CH // LANES):
        parts = []
        for s in range(2):
            j = 2 * c + s
            parts += [buf[set_rows(j), :].astype(BF16) for buf in sre + sim]
        lhs = jnp.concatenate(parts, axis=1)
        y = jnp.dot(lhs, wc_ref[0, c], preferred_element_type=F32)
        y = y + d_ref[:, c * LANES:(c + 1) * LANES] * u[:, c * LANES:(c + 1) * LANES]
        z_ref[0, :, c * LANES:(c + 1) * LANES] = jax.nn.gelu(y).astype(BF16)


def _s5_weights(lam_re, lam_im, log_step, b_re, b_im, c_re, c_im):
    g, p = lam_re.shape
    n = SSM_GROUP
    nblk = g // S5_BLK_GROUPS
    lam = lax.complex(lam_re.astype(F32), lam_im.astype(F32))
    dt = jnp.exp(log_step.astype(F32))[:, None]
    lam_bar = jnp.exp(lam * dt)
    b_bar = ((lam_bar - 1.0) / lam)[..., None] * lax.complex(b_re.astype(F32), b_im.astype(F32))
    eye_q = jnp.eye(S5_SET_GROUPS, dtype=F32)
    eye_h = jnp.eye(2, dtype=F32)

    def b_operand(bm):
        t = bm.reshape(nblk, S5_SETS // 2, 2, S5_SET_GROUPS, p, n)
        w = jnp.einsum("bthqpn,hk,qr->bthkrnqp", t, eye_h, eye_q)
        w = w.reshape(nblk, S5_SETS, LANES, S5_SET_LANES)
        return w

    wb = jnp.stack([b_operand(jnp.real(b_bar)), b_operand(jnp.imag(b_bar))], axis=2)
    wb = wb.astype(BF16)
    wb = jnp.concatenate([wb, wb], axis=3)

    def c_operand(cm):
        t = cm.reshape(nblk, S5_SETS // 2, 2, S5_SET_GROUPS, n, p)
        return jnp.einsum("bcsqnp,st,qr->bcsqptrn", t, eye_h, eye_q)

    cre = c_operand(c_re.astype(F32))
    cim = c_operand(-c_im.astype(F32))
    wc = jnp.stack([cre, cim], axis=3).reshape(nblk, S5_SETS // 2, 4 * S5_SET_LANES, LANES).astype(BF16)
    lre = jnp.real(lam_bar).reshape(nblk, S5_SETS, S5_SET_LANES)
    lim = jnp.imag(lam_bar).reshape(nblk, S5_SETS, S5_SET_LANES)
    return wb, wc, lre, lim


def s5_scan(u, h0_re, h0_im, weights, d_skip, seq_valid, tc_len):
    wb, wc, lre, lim = weights
    bsz, lp, d = u.shape
    nblk = d // S5_BLK_CH
    n_tc = lp // tc_len
    h0_re = h0_re.astype(F32).reshape(bsz, nblk, S5_SETS, S5_SET_LANES)
    h0_im = h0_im.astype(F32).reshape(bsz, nblk, S5_SETS, S5_SET_LANES)
    state_spec = pl.BlockSpec((1, 1, S5_SETS, S5_SET_LANES), lambda k, b, t: (b, k, 0, 0))
    z, hre, him = pl.pallas_call(
        functools.partial(_s5_body, tc_len=tc_len, last_step=seq_valid - 1),
        grid=(nblk, bsz, n_tc),
        in_specs=[pl.BlockSpec((1, tc_len, S5_BLK_CH), lambda k, b, t: (b, t, k)),
                  pl.BlockSpec((1, S5_SETS, 2, 2 * LANES, S5_SET_LANES), lambda k, b, t: (k, 0, 0, 0, 0)),
                  pl.BlockSpec((1, S5_SETS // 2, 4 * S5_SET_LANES, LANES), lambda k, b, t: (k, 0, 0, 0)),
                  pl.BlockSpec((1, S5_SETS, S5_SET_LANES), lambda k, b, t: (k, 0, 0)),
                  pl.BlockSpec((1, S5_SETS, S5_SET_LANES), lambda k, b, t: (k, 0, 0)),
                  pl.BlockSpec((1, S5_BLK_CH), lambda k, b, t: (0, k)),
                  state_spec, state_spec],
        out_specs=[pl.BlockSpec((1, tc_len, S5_BLK_CH), lambda k, b, t: (b, t, k)), state_spec, state_spec],
        out_shape=[jax.ShapeDtypeStruct((bsz, lp, d), BF16),
                   jax.ShapeDtypeStruct((bsz, nblk, S5_SETS, S5_SET_LANES), F32),
                   jax.ShapeDtypeStruct((bsz, nblk, S5_SETS, S5_SET_LANES), F32)],
        scratch_shapes=[pltpu.VMEM((tc_len * S5_SETS, LANES), F32)] * (2 * S5_SET_LANES // LANES) + [
                        pltpu.VMEM((S5_SETS, S5_SET_LANES), F32),
                        pltpu.VMEM((S5_SETS, S5_SET_LANES), F32)],
        compiler_params=_cparams(("arbitrary", "arbitrary", "arbitrary")),
        name="s5_scan",
    )(u, wb, wc, lre, lim, d_skip.reshape(1, d).astype(F32), h0_re, h0_im)
    g = d // SSM_GROUP
    return z, hre.reshape(bsz, g, SSM_STATE), him.reshape(bsz, g, SSM_STATE)


def _cmp_proj_body(pt_ref, x_ref, w_ref, o_ref, xs, stage, *, pages_per_group):
    p = pl.program_id(1)
    slot_in_group = p % pages_per_group
    chunks = PAGE_SIZE // CMP_STRIDE
    r0 = pl.multiple_of(slot_in_group * chunks, chunks)
    for c in range(x_ref.shape[2] // LANES):
        stage[c] = x_ref[0, :, c * LANES:(c + 1) * LANES]
        for j in range(CMP_STRIDE):
            xs[j, pl.ds(r0, chunks), c * LANES:(c + 1) * LANES] = stage[c, pl.ds(j, chunks, stride=CMP_STRIDE), :]

    @pl.when(slot_in_group == pages_per_group - 1)
    def _():
        for s in range(2):
            for h in range(N_KV_HEADS):
                c0 = (s * N_KV_HEADS + h) * HEAD_DIM
                acc = None
                for jp in range(CMP_STRIDE // 2):
                    lhs = jnp.concatenate([xs[2 * jp, :, c0:c0 + HEAD_DIM].astype(BF16),
                                           xs[2 * jp + 1, :, c0:c0 + HEAD_DIM].astype(BF16)], axis=1)
                    t = jnp.dot(lhs, w_ref[s, jp], preferred_element_type=F32)
                    acc = t if acc is None else acc + t
                o_ref[0, s, h] = acc


def cmp_project(rows, page_table, w1cat, pages_per_group=16):
    bsz, npg = page_table.shape
    ppg = math.gcd(npg, pages_per_group)
    chunks = PAGE_SIZE // CMP_STRIDE
    width = 2 * N_KV_HEADS * HEAD_DIM
    hid2 = w1cat.shape[-1]
    grid_spec = pltpu.PrefetchScalarGridSpec(
        num_scalar_prefetch=1,
        grid=(bsz, npg),
        in_specs=[pl.BlockSpec((1, PAGE_SIZE, width), lambda b, p, pt: (pt[b * npg + p], 0, 0)),
                  pl.BlockSpec(w1cat.shape, lambda b, p, pt: (0, 0, 0, 0))],
        out_specs=pl.BlockSpec((1, 2, N_KV_HEADS, ppg * chunks, hid2), lambda b, p, pt: (b, 0, 0, p // ppg, 0)),
        scratch_shapes=[pltpu.VMEM((CMP_STRIDE, ppg * chunks, width), F32),
                        pltpu.VMEM((width // LANES, PAGE_SIZE, LANES), F32)],
    )
    return pl.pallas_call(
        functools.partial(_cmp_proj_body, pages_per_group=ppg),
        grid_spec=grid_spec,
        out_shape=jax.ShapeDtypeStruct((bsz, 2, N_KV_HEADS, npg * chunks, hid2), F32),
        compiler_params=_cparams(("arbitrary", "arbitrary")),
        name="cmp_project",
    )(page_table.reshape(-1).astype(I32), rows, w1cat)


def _cmp_finish_body(ab_ref, pe_ref, w1_ref, b1_ref, w2_ref, b2_ref, o_ref):
    ab = ab_ref[0, 0, 0]
    n16 = ab.shape[0]
    hid = ab.shape[1] // 2
    a = ab[:, :hid]
    bnext = pltpu.roll(ab[:, hid:], n16 - 1, axis=0)
    c = jnp.dot(pe_ref[0], w1_ref[0], preferred_element_type=F32)[0:1, :] + b1_ref[0]
    pre = a + bnext + c
    y = jnp.dot(jax.nn.gelu(pre).astype(BF16), w2_ref[0], preferred_element_type=F32) + b2_ref[0]
    row = lax.broadcasted_iota(I32, y.shape, 0)
    o_ref[0, 0, 0] = jnp.where(row < n16 - 1, y, 0.0).astype(o_ref.dtype)


def cmp_finish(ab, pe, w1, b1, w2, b2):
    bsz, _, kvh, n16, hid2 = ab.shape
    hid = hid2 // 2
    pe_rows = jnp.broadcast_to(pe.reshape(2, 1, -1), (2, SUBLANES, pe.shape[1] * pe.shape[2])).astype(BF16)
    return pl.pallas_call(
        _cmp_finish_body,
        grid=(2, bsz, kvh),
        in_specs=[pl.BlockSpec((1, 1, 1, n16, hid2), lambda s, b, h: (b, s, h, 0, 0)),
                  pl.BlockSpec((1, SUBLANES, pe_rows.shape[2]), lambda s, b, h: (s, 0, 0)),
                  pl.BlockSpec((1,) + w1.shape[1:], lambda s, b, h: (s, 0, 0)),
                  pl.BlockSpec((1, 1, hid), lambda s, b, h: (s, 0, 0)),
                  pl.BlockSpec((1, hid, HEAD_DIM), lambda s, b, h: (s, 0, 0)),
                  pl.BlockSpec((1, 1, HEAD_DIM), lambda s, b, h: (s, 0, 0))],
        out_specs=pl.BlockSpec((1, 1, 1, n16, HEAD_DIM), lambda s, b, h: (b, s, h, 0, 0)),
        out_shape=jax.ShapeDtypeStruct((bsz, 2, kvh, n16, HEAD_DIM), BF16),
        compiler_params=_cparams(("arbitrary", "arbitrary", "arbitrary")),
        name="cmp_finish",
    )(ab, pe_rows, w1.astype(BF16), b1.reshape(2, 1, hid).astype(F32), w2.astype(BF16),
      b2.reshape(2, 1, HEAD_DIM).astype(F32))


def _masked_softmax(s, mask):
    sm = jnp.where(mask, s, NEG_BIG)
    m = jnp.max(sm, axis=-1, keepdims=True)
    m = jnp.where(m > 0.5 * NEG_BIG, m, 0.0)
    e = jnp.where(mask, jnp.exp(sm - m), 0.0)
    den = jnp.sum(e, axis=-1, keepdims=True)
    return e / jnp.where(den > 0, den, 1.0)


def _dot_nt(a, b):
    return lax.dot_general(a, b, (((1,), (1,)), ((), ())), preferred_element_type=F32)


def _overlap(n_c, n_j, c_axis):
    shape = (n_c, n_j) if c_axis == 0 else (n_j, n_c)
    c = lax.broadcasted_iota(I32, shape, c_axis) * CMP_STRIDE
    j = lax.broadcasted_iota(I32, shape, 1 - c_axis) * SEL_BLOCK
    return ((c < j + SEL_BLOCK) & (c + CMP_BLOCK > j)).astype(F32)


def _block_scores(imp, q_pos, n_blk):
    blk = lax.broadcasted_iota(I32, imp.shape, 1)
    q_blk = jnp.right_shift(q_pos, SEL_SHIFT)
    forced = (blk == 0) | (blk == q_blk) | (blk == q_blk - 1)
    future = blk * SEL_BLOCK > q_pos
    v = jnp.where(future, NEG_BIG, jnp.where(forced, -NEG_BIG, imp))
    return jnp.where(blk < n_blk, v, 2.0 * NEG_BIG)


def _rank_desc(v, n):
    lane = lax.broadcasted_iota(I32, v.shape, 1)
    rank = jnp.zeros(v.shape, F32)
    for i in range(n):
        vi = v[:, i:i + 1]
        before = (vi > v) | ((vi == v) & (lane > i))
        rank = rank + jnp.where(before, 1.0, 0.0)
    return rank


def _flash_update(s, mask, v_tile, m_ref, l_ref, acc_ref):
    g, nq, nk = s.shape
    sm = jnp.where(mask, s, NEG_BIG)
    m_old = m_ref[...]
    m_new = jnp.maximum(m_old, jnp.max(sm, axis=-1, keepdims=True))
    alpha = jnp.exp(m_old - m_new)
    e = jnp.where(mask, jnp.exp(sm - m_new), 0.0)
    l_ref[...] = alpha * l_ref[...] + jnp.sum(e, axis=-1, keepdims=True)
    pv = jnp.dot(e.reshape(g * nq, nk).astype(BF16), v_tile, preferred_element_type=F32)
    acc_ref[...] = alpha * acc_ref[...] + pv.reshape(g, nq, HEAD_DIM)
    m_ref[...] = m_new


def _flash_init(m_ref, l_ref, acc_ref):
    m_ref[...] = jnp.full(m_ref.shape, NEG_BIG, F32)
    l_ref[...] = jnp.zeros(l_ref.shape, F32)
    acc_ref[...] = jnp.zeros(acc_ref.shape, F32)


def _flash_result(l_ref, acc_ref):
    l = l_ref[...]
    return acc_ref[...] / jnp.where(l > 0, l, 1.0)


def _nsa_prompt_body(q_ref, gt_ref, ck_ref, cv_ref, ks_ref, vs_ref, kw_ref, vw_ref, o_ref,
                     m_ref, l_ref, acc_ref, *, n_cmp, n_blk, gqa, sel_tile):
    qb = pl.program_id(2)
    nq = Q_BLOCK
    scale = HEAD_DIM ** -0.5
    q = jnp.concatenate([q_ref[0, :, g * HEAD_DIM:(g + 1) * HEAD_DIM] for g in range(gqa)], axis=0)
    q_pos = qb * nq + lax.broadcasted_iota(I32, (nq, 1), 0)

    ck = ck_ref[0, 0, 0]
    cv = cv_ref[0, 0, 0]
    n16 = ck.shape[0]
    s = (_dot_nt(q, ck) * scale).reshape(gqa, nq, n16)
    cidx = lax.broadcasted_iota(I32, (nq, n16), 1)
    cmask = (cidx * CMP_STRIDE + (CMP_BLOCK - 1) <= q_pos) & (cidx < n_cmp)
    p = _masked_softmax(s, cmask[None])
    o_cmp = jnp.dot(p.reshape(gqa * nq, n16).astype(BF16), cv, preferred_element_type=F32).reshape(gqa, nq, HEAD_DIM)

    psum = jnp.sum(p, axis=0)
    imp = jnp.dot(psum, _overlap(n16, LANES, 0), preferred_element_type=F32, precision=lax.Precision.HIGHEST)
    rank = _rank_desc(_block_scores(imp, q_pos, n_blk), n_blk)
    lane = lax.broadcasted_iota(I32, rank.shape, 1)
    sel = jnp.where((rank < min(N_SEL, n_blk)) & (lane < n_blk), 1.0, 0.0).astype(BF16)

    _flash_init(m_ref, l_ref, acc_ref)
    n_sel_tiles = (qb * nq + nq + sel_tile - 1) // sel_tile

    def sel_step(kt, _):
        k0 = pl.multiple_of(kt * sel_tile, sel_tile)
        kt_bf = ks_ref[0, pl.ds(k0, sel_tile), :].astype(BF16)
        vt_bf = vs_ref[0, pl.ds(k0, sel_tile), :].astype(BF16)
        s = (_dot_nt(q, kt_bf) * scale).reshape(gqa, nq, sel_tile)
        kpos = k0 + lax.broadcasted_iota(I32, (nq, sel_tile), 1)
        expand = (lax.broadcasted_iota(I32, (LANES, sel_tile), 0) ==
                  jnp.right_shift(k0 + lax.broadcasted_iota(I32, (LANES, sel_tile), 1), SEL_SHIFT))
        in_sel = jnp.dot(sel, jnp.where(expand, 1.0, 0.0).astype(BF16), preferred_element_type=F32)
        mask = (in_sel > 0.5) & (kpos <= q_pos)
        _flash_update(s, mask[None], vt_bf, m_ref, l_ref, acc_ref)
        return 0

    lax.fori_loop(0, n_sel_tiles, sel_step, 0)
    o_sel = _flash_result(l_ref, acc_ref)

    _flash_init(m_ref, l_ref, acc_ref)
    n_back = WINDOW // nq
    w_lo = jnp.maximum(qb - n_back, 0)

    def win_step(w, _):
        k0 = pl.multiple_of(w * nq, nq)
        kt_bf = kw_ref[0, pl.ds(k0, nq), :].astype(BF16)
        vt_bf = vw_ref[0, pl.ds(k0, nq), :].astype(BF16)
        s = (_dot_nt(q, kt_bf) * scale).reshape(gqa, nq, nq)
        kpos = k0 + lax.broadcasted_iota(I32, (nq, nq), 1)
        mask = (kpos <= q_pos) & (kpos > q_pos - WINDOW)
        _flash_update(s, mask[None], vt_bf, m_ref, l_ref, acc_ref)
        return 0

    lax.fori_loop(w_lo, qb + 1, win_step, 0)
    o_win = _flash_result(l_ref, acc_ref)

    gt = gt_ref[0, 0]
    for g in range(gqa):
        o = (gt[:, 3 * g:3 * g + 1] * o_cmp[g] + gt[:, 3 * g + 1:3 * g + 2] * o_sel[g]
             + gt[:, 3 * g + 2:3 * g + 3] * o_win[g])
        o_ref[0, :, g * HEAD_DIM:(g + 1) * HEAD_DIM] = o.astype(o_ref.dtype)


def nsa_prompt_attend(q, gates, cmp, kv, sel_tile=512):
    bsz, t, hd_all = q.shape
    kvh = N_KV_HEADS
    gqa = hd_all // HEAD_DIM // kvh
    n16 = cmp.shape[3]
    n_blk = -(-t // SEL_BLOCK)
    assert t % Q_BLOCK == 0 and n_blk <= LANES and n16 * CMP_STRIDE == t
    sel_tile = math.gcd(t, sel_tile)
    kv_spec = lambda slot: pl.BlockSpec((1, t, HEAD_DIM), lambda b, h, i, slot=slot: (b, 0, slot * kvh + h))
    cmp_spec = lambda s: pl.BlockSpec((1, 1, 1, n16, HEAD_DIM), lambda b, h, i, s=s: (b, s, h, 0, 0))
    return pl.pallas_call(
        functools.partial(_nsa_prompt_body, n_cmp=n16 - 1, n_blk=n_blk, gqa=gqa, sel_tile=sel_tile),
        grid=(bsz, kvh, t // Q_BLOCK),
        in_specs=[pl.BlockSpec((1, Q_BLOCK, gqa * HEAD_DIM), lambda b, h, i: (b, i, h)),
                  pl.BlockSpec((1, 1, Q_BLOCK, LANES), lambda b, h, i: (b, h, i, 0)),
                  cmp_spec(0), cmp_spec(1), kv_spec(2), kv_spec(3), kv_spec(4), kv_spec(5)],
        out_specs=pl.BlockSpec((1, Q_BLOCK, gqa * HEAD_DIM), lambda b, h, i: (b, i, h)),
        out_shape=jax.ShapeDtypeStruct((bsz, t, hd_all), BF16),
        scratch_shapes=[pltpu.VMEM((gqa, Q_BLOCK, 1), F32), pltpu.VMEM((gqa, Q_BLOCK, 1), F32),
                        pltpu.VMEM((gqa, Q_BLOCK, HEAD_DIM), F32)],
        compiler_params=_cparams(("parallel", "parallel", "arbitrary")),
        name="nsa_prompt",
    )(q, gates, cmp, cmp, kv, kv, kv, kv)


def _nsa_sample_select_body(q_ref, ck_ref, cv_ref, ocmp_ref, idx_ref, *, n_cmp, n_blk, blk_lanes, past_len, tq_pad, gqa):
    scale = HEAD_DIM ** -0.5
    q = jnp.concatenate([q_ref[0, :, g * HEAD_DIM:(g + 1) * HEAD_DIM] for g in range(gqa)], axis=0)
    ck = ck_ref[0, 0, 0]
    cv = cv_ref[0, 0, 0]
    n16 = ck.shape[0]
    q_pos = past_len + lax.broadcasted_iota(I32, (tq_pad, 1), 0)
    s = (_dot_nt(q, ck) * scale).reshape(gqa, tq_pad, n16)
    cidx = lax.broadcasted_iota(I32, (tq_pad, n16), 1)
    cmask = (cidx * CMP_STRIDE + (CMP_BLOCK - 1) <= q_pos) & (cidx < n_cmp)
    p = _masked_softmax(s, cmask[None])
    ocmp_ref[0, 0] = jnp.dot(p.reshape(gqa * tq_pad, n16).astype(BF16), cv, preferred_element_type=F32)
    psum = jnp.sum(p, axis=0)
    imp = jnp.dot(psum, _overlap(n16, blk_lanes, 0), preferred_element_type=F32, precision=lax.Precision.HIGHEST)
    rank = _rank_desc(_block_scores(imp, q_pos, n_blk), n_blk)
    lane = lax.broadcasted_iota(I32, rank.shape, 1)
    lane_f = lane.astype(F32)
    out_lane = lax.broadcasted_iota(I32, (tq_pad, LANES), 1)
    idx = jnp.zeros((tq_pad, LANES), F32)
    for k in range(min(N_SEL, n_blk)):
        hit = (rank == float(k)) & (lane < n_blk)
        idx_k = jnp.sum(jnp.where(hit, lane_f, 0.0), axis=-1, keepdims=True)
        idx = jnp.where(out_lane == k, idx_k, idx)
    idx_ref[0, 0] = idx.astype(I32)


def _nsa_sample_attend_body(idx_ref, pt_ref, q_ref, gt_ref, ocmp_ref, kwc_ref, vwc_ref,
                            kwn_ref, vwn_ref, cache_ref, newblk_ref, o_ref, kbuf, vbuf, kwbuf, vwbuf, sems,
                            *, n_sel, n_cache_blk, n_pages, past_len, win_buf, tq, tq_pad, gqa):
    b = pl.program_id(0)
    h = pl.program_id(1)
    scale = HEAD_DIM ** -0.5
    blk_per_page = PAGE_SIZE // SEL_BLOCK

    bufs = ((2, kbuf, 0), (3, vbuf, 1))

    def from_new(t, k, slot, buf, sem):
        return pltpu.make_async_copy(newblk_ref.at[b, :, slot * N_KV_HEADS + h], buf.at[t, k], sems.at[sem])

    for t in range(tq):
        for k in range(n_sel):
            blk = idx_ref[((b * N_KV_HEADS + h) * tq + t) * n_sel + k]
            in_cache = blk < n_cache_blk
            blk_c = jnp.minimum(blk, n_cache_blk - 1)
            page = pt_ref[b * n_pages + lax.div(blk_c, blk_per_page)]
            off = pl.multiple_of(lax.rem(blk_c, blk_per_page) * SEL_BLOCK, SEL_BLOCK)
            for slot, buf, sem in bufs:
                @pl.when(in_cache)
                def _():
                    pltpu.make_async_copy(cache_ref.at[page, pl.ds(off, SEL_BLOCK), slot * N_KV_HEADS + h],
                                          buf.at[t, k], sems.at[sem]).start()

                @pl.when(jnp.logical_not(in_cache))
                def _():
                    from_new(t, k, slot, buf, sem).start()
    q = jnp.concatenate([q_ref[0, :, g * HEAD_DIM:(g + 1) * HEAD_DIM] for g in range(gqa)], axis=0)
    rows = gqa * tq_pad
    row_t = lax.rem(lax.broadcasted_iota(I32, (rows, 1), 0), tq_pad)
    q_pos = past_len + row_t

    kwbuf[...] = jnp.zeros(kwbuf.shape, F32)
    vwbuf[...] = jnp.zeros(vwbuf.shape, F32)
    kwbuf[0:win_buf, :] = kwc_ref[0]
    vwbuf[0:win_buf, :] = vwc_ref[0]
    kwbuf[win_buf:win_buf + tq_pad, :] = kwn_ref[0]
    vwbuf[win_buf:win_buf + tq_pad, :] = vwn_ref[0]
    nw = kwbuf.shape[0]
    wlane = lax.broadcasted_iota(I32, (rows, nw), 1)
    wpos = past_len - win_buf + wlane
    wmask = (wpos <= q_pos) & (wpos > q_pos - WINDOW) & (wpos >= 0) & (wlane < win_buf + tq)
    s = _dot_nt(q, kwbuf[...].astype(BF16)) * scale
    p = _masked_softmax(s, wmask)
    o_win = jnp.dot(p.astype(BF16), vwbuf[...].astype(BF16), preferred_element_type=F32)

    for t in range(tq):
        for k in range(n_sel):
            for slot, buf, sem in bufs:
                from_new(t, k, slot, buf, sem).wait()

    o_sel = jnp.zeros((rows, HEAD_DIM), F32)
    nk = n_sel * SEL_BLOCK
    klane = lax.broadcasted_iota(I32, (rows, nk), 1)
    for t in range(tq):
        kpos = jnp.zeros((rows, nk), I32)
        for k in range(n_sel):
            blk = idx_ref[((b * N_KV_HEADS + h) * tq + t) * n_sel + k]
            kpos = jnp.where(jnp.right_shift(klane, SEL_SHIFT) == k,
                             blk * SEL_BLOCK + jnp.bitwise_and(klane, SEL_BLOCK - 1), kpos)
        mask = (kpos <= q_pos) & (row_t == t)
        kt = kbuf[t].reshape(nk, HEAD_DIM).astype(BF16)
        vt = vbuf[t].reshape(nk, HEAD_DIM).astype(BF16)
        p = _masked_softmax(_dot_nt(q, kt) * scale, mask)
        o_sel = o_sel + jnp.dot(p.astype(BF16), vt, preferred_element_type=F32)

    o_cmp = ocmp_ref[0, 0]
    gt = gt_ref[0, 0]
    for g in range(gqa):
        r = slice(g * tq_pad, (g + 1) * tq_pad)
        o = (gt[:, 3 * g:3 * g + 1] * o_cmp[r] + gt[:, 3 * g + 1:3 * g + 2] * o_sel[r]
             + gt[:, 3 * g + 2:3 * g + 3] * o_win[r])
        o_ref[0, :, g * HEAD_DIM:(g + 1) * HEAD_DIM] = o.astype(o_ref.dtype)


def nsa_sample_attend(q, gates, cmp, kv_new, cache_kv, cache_win, page_table, tq):
    bsz, tq_pad, hd_all = q.shape
    kvh = N_KV_HEADS
    gqa = hd_all // HEAD_DIM // kvh
    n_pages = page_table.shape[1]
    past_len = n_pages * PAGE_SIZE
    n16 = cmp.shape[3]
    n_cmp = n16 - 1
    n_cache_blk = past_len // SEL_BLOCK
    n_blk = -(-(past_len + tq) // SEL_BLOCK)
    assert n_blk == n_cache_blk + 1 and past_len % SEL_BLOCK == 0 and tq <= SEL_BLOCK
    n_sel = min(N_SEL, n_blk)
    blk_lanes = -(-n_blk // LANES) * LANES
    win_buf = cache_win.shape[1]
    cmp_spec = lambda s: pl.BlockSpec((1, 1, 1, n16, HEAD_DIM), lambda b, h, s=s: (b, s, h, 0, 0))
    ocmp, idx = pl.pallas_call(
        functools.partial(_nsa_sample_select_body, n_cmp=n_cmp, n_blk=n_blk, blk_lanes=blk_lanes,
                          past_len=past_len, tq_pad=tq_pad, gqa=gqa),
        grid=(bsz, kvh),
        in_specs=[pl.BlockSpec((1, tq_pad, gqa * HEAD_DIM), lambda b, h: (b, 0, h)), cmp_spec(0), cmp_spec(1)],
        out_specs=[pl.BlockSpec((1, 1, gqa * tq_pad, HEAD_DIM), lambda b, h: (b, h, 0, 0)),
                   pl.BlockSpec((1, 1, tq_pad, LANES), lambda b, h: (b, h, 0, 0))],
        out_shape=[jax.ShapeDtypeStruct((bsz, kvh, gqa * tq_pad, HEAD_DIM), F32),
                   jax.ShapeDtypeStruct((bsz, kvh, tq_pad, LANES), I32)],
        compiler_params=_cparams(("parallel", "parallel")),
        name="nsa_sample_select",
    )(q, cmp, cmp)
    idx_flat = idx[:, :, :tq, :n_sel].reshape(-1)

    n_slots = cache_kv.shape[2]
    cache4 = cache_kv.reshape(cache_kv.shape[0], PAGE_SIZE, n_slots * kvh, HEAD_DIM)
    new_rows = kv_new[:, :tq, :n_slots * kvh * HEAD_DIM].reshape(bsz, tq, n_slots * kvh, HEAD_DIM)
    newblk = jnp.zeros((bsz, SEL_BLOCK, n_slots * kvh, HEAD_DIM), F32).at[:, :tq].set(new_rows)
    cwin = cache_win.astype(F32).reshape(bsz, win_buf, 2 * kvh * HEAD_DIM)
    nw = -(-(win_buf + tq_pad) // LANES) * LANES
    kv_spec = lambda slot: pl.BlockSpec((1, tq_pad, HEAD_DIM), lambda b, h, *_, slot=slot: (b, 0, slot * kvh + h))
    cw_spec = lambda slot: pl.BlockSpec((1, win_buf, HEAD_DIM), lambda b, h, *_, slot=slot: (b, 0, slot * kvh + h))
    grid_spec = pltpu.PrefetchScalarGridSpec(
        num_scalar_prefetch=2,
        grid=(bsz, kvh),
        in_specs=[pl.BlockSpec((1, tq_pad, gqa * HEAD_DIM), lambda b, h, *_: (b, 0, h)),
                  pl.BlockSpec((1, 1, tq_pad, LANES), lambda b, h, *_: (b, h, 0, 0)),
                  pl.BlockSpec((1, 1, gqa * tq_pad, HEAD_DIM), lambda b, h, *_: (b, h, 0, 0)),
                  cw_spec(0), cw_spec(1), kv_spec(4), kv_spec(5),
                  pl.BlockSpec(memory_space=pl.ANY), pl.BlockSpec(memory_space=pl.ANY)],
        out_specs=pl.BlockSpec((1, tq_pad, gqa * HEAD_DIM), lambda b, h, *_: (b, 0, h)),
        scratch_shapes=[pltpu.VMEM((tq, n_sel, SEL_BLOCK, HEAD_DIM), F32),
                        pltpu.VMEM((tq, n_sel, SEL_BLOCK, HEAD_DIM), F32),
                        pltpu.VMEM((nw, HEAD_DIM), F32), pltpu.VMEM((nw, HEAD_DIM), F32),
                        pltpu.SemaphoreType.DMA((2,))],
    )
    return pl.pallas_call(
        functools.partial(_nsa_sample_attend_body, n_sel=n_sel, n_cache_blk=n_cache_blk, n_pages=n_pages,
                          past_len=past_len, win_buf=win_buf, tq=tq, tq_pad=tq_pad, gqa=gqa),
        grid_spec=grid_spec,
        out_shape=jax.ShapeDtypeStruct((bsz, tq_pad, hd_all), BF16),
        compiler_params=_cparams(("arbitrary", "arbitrary")),
        name="nsa_sample_attend",
    )(idx_flat, page_table.reshape(-1).astype(I32), q, gates, ocmp, cwin, cwin, kv_new, kv_new, cache4, newblk)


def _pad_rows(x, rows):
    return jnp.pad(x, ((0, 0), (0, rows - x.shape[1]), (0, 0)))


@jax.jit
def _step(x_prompt, x_sample, state_ssm_re, state_ssm_im, state_ffn_conv, cache_kv, cache_win, page_table,
          attn_norm, ffn_norm, final_norm, ssm_lam_re, ssm_lam_im, ssm_log_step, ssm_b_re, ssm_b_im,
          ssm_c_re, ssm_c_im, ssm_d, ssm_w_glu, ffn_w_in, ffn_conv_w, ffn_conv_b, ffn_w_down,
          kv_norm, w_kv, cmp_w1, cmp_b1, cmp_w2, cmp_b2, cmp_pe, w_qg, w_o):
    bp, tp, d = x_prompt.shape
    bs, ts, _ = x_sample.shape
    depth = attn_norm.shape[0]
    n_a = ssm_lam_re.shape[0]
    kvh = N_KV_HEADS
    hd_all = w_o.shape[1]
    n_heads = hd_all // HEAD_DIM
    gqa = n_heads // kvh
    ts_pad = -(-ts // SUBLANES) * SUBLANES
    n_pages = page_table.shape[1]

    xp = x_prompt.reshape(bp * tp, d).astype(F32)
    xs = x_sample.reshape(bs * ts, d).astype(F32)

    w1cat = jnp.concatenate([cmp_w1[:, :CMP_STRIDE * HEAD_DIM], cmp_w1[:, CMP_STRIDE * HEAD_DIM:]], axis=-1)
    w1cat = w1cat.reshape(2, CMP_STRIDE // 2, 2 * HEAD_DIM, w1cat.shape[-1]).astype(BF16)

    ssm_re_p, ssm_im_p, ssm_re_s, ssm_im_s, conv_p, conv_s = [], [], [], [], [], []
    outs = {}
    for layer in range(depth):
        if layer < n_a:
            a = layer
            wts = _s5_weights(ssm_lam_re[a], ssm_lam_im[a], ssm_log_step[a], ssm_b_re[a], ssm_b_im[a],
                              ssm_c_re[a], ssm_c_im[a])
            w_glu = ssm_w_glu[a].astype(BF16)
            up = rmsnorm(xp, attn_norm[layer], F32).reshape(bp, tp, d)
            zeros = jnp.zeros((bp, d // SSM_GROUP, SSM_STATE), F32)
            zp, hre, him = s5_scan(up, zeros, zeros, wts, ssm_d[a], tp, math.gcd(tp, 512))
            ssm_re_p.append(hre)
            ssm_im_p.append(him)
            xp = glu_matmul(zp.reshape(bp * tp, d), w_glu, xp)
            us = _pad_rows(rmsnorm(xs, attn_norm[layer], F32).reshape(bs, ts, d), ts_pad)
            zs, hre, him = s5_scan(us, state_ssm_re[a], state_ssm_im[a], wts, ssm_d[a], ts, ts_pad)
            ssm_re_s.append(hre)
            ssm_im_s.append(him)
            xs = glu_matmul(zs[:, :ts].reshape(bs * ts, d), w_glu, xs)
        else:
            bl = layer - n_a
            if layer == n_a:
                w_kv_b = w_kv.astype(BF16)
                kv_p = matmul(rmsnorm(xp, kv_norm, BF16), w_kv_b, tm=1024, tn=512)
                kv_s = matmul(rmsnorm(xs, kv_norm, BF16), w_kv_b, tm=1024, tn=512)
                n_kv4 = 4 * kvh * HEAD_DIM
                outs["kv_rows_p"] = kv_p[:, :n_kv4].reshape(bp, tp, 4, kvh, HEAD_DIM)
                outs["kv_rows_s"] = kv_s[:, :n_kv4].reshape(bs, ts, 4, kvh, HEAD_DIM)
                win_p = kv_p[:, n_kv4:].reshape(bp, tp, 2, kvh, HEAD_DIM)
                outs["win_p"] = win_p[:, max(tp - WINDOW, 0):]
                win_s = kv_s[:, n_kv4:].reshape(bs, ts, 2, kvh, HEAD_DIM)
                win_buf = cache_win.shape[1]
                outs["win_s"] = jnp.concatenate([cache_win.astype(F32), win_s], axis=1)[:, -win_buf:]
                kv_p3 = kv_p.reshape(bp, tp, kv_p.shape[1])
                kv_s3 = _pad_rows(kv_s.reshape(bs, ts, kv_s.shape[1]), ts_pad)
                assert tp % PAGE_SIZE == 0 and (n_pages * PAGE_SIZE + ts) // CMP_STRIDE == n_pages * PAGE_SIZE // CMP_STRIDE
                pt_p = jnp.arange(bp * (tp // PAGE_SIZE), dtype=I32).reshape(bp, tp // PAGE_SIZE)
                ab_p = cmp_project(kv_p.reshape(bp * tp // PAGE_SIZE, PAGE_SIZE, kv_p.shape[1]), pt_p, w1cat)
                cmp_p = cmp_finish(ab_p, cmp_pe, cmp_w1, cmp_b1, cmp_w2, cmp_b2)
                cache3 = cache_kv.astype(F32).reshape(cache_kv.shape[0], PAGE_SIZE, -1)
                ab_s = cmp_project(cache3, page_table, w1cat)
                cmp_s = cmp_finish(ab_s, cmp_pe, cmp_w1, cmp_b1, cmp_w2, cmp_b2)
            w_q = w_qg[bl][:, :hd_all].astype(BF16)
            n_gate = w_qg.shape[2] - hd_all
            w_g = jnp.pad(w_qg[bl][:, hd_all:], ((0, 0), (0, LANES - n_gate))).astype(BF16)
            w_o_b = w_o[bl].astype(BF16)

            def gate_layout(gt, bsz, t):
                gt = gt[:, :n_gate].reshape(bsz, t, kvh, 3 * gqa).transpose(0, 2, 1, 3)
                return jnp.pad(gt, ((0, 0), (0, 0), (0, 0), (0, LANES - 3 * gqa)))

            hp = rmsnorm(xp, attn_norm[layer], BF16)
            q_p = matmul(hp, w_q, out_dtype=BF16, tm=1024, tn=512).reshape(bp, tp, hd_all)
            g_p = gate_layout(matmul(hp, w_g, act="sigmoid", tm=1024), bp, tp)
            o_p = nsa_prompt_attend(q_p, g_p, cmp_p, kv_p3)
            xp = matmul(o_p.reshape(bp * tp, hd_all), w_o_b, res=xp, tm=1024, tn=512)

            hs = rmsnorm(xs, attn_norm[layer], BF16)
            q_s = _pad_rows(matmul(hs, w_q, out_dtype=BF16).reshape(bs, ts, hd_all), ts_pad)
            g_s = gate_layout(matmul(hs, w_g, act="sigmoid"), bs, ts)
            g_s = jnp.pad(g_s, ((0, 0), (0, 0), (0, ts_pad - ts), (0, 0)))
            o_s = nsa_sample_attend(q_s, g_s, cmp_s, kv_s3, cache_kv.astype(F32), cache_win, page_table, ts)
            xs = matmul(o_s[:, :ts].reshape(bs * ts, hd_all), w_o_b, res=xs)

        w_in = ffn_w_in[layer].astype(BF16)
        w_down = ffn_w_down[layer].astype(BF16)
        f = w_down.shape[0]
        act, st = ffn_up(rmsnorm(xp, ffn_norm[layer], BF16), w_in, ffn_conv_w[layer], ffn_conv_b[layer],
                         jnp.zeros((bp, CONV_W - 1, f), F32), tp)
        conv_p.append(st)
        xp = matmul(act, w_down, res=xp, tm=512, tn=256)
        act, st = ffn_up(rmsnorm(xs, ffn_norm[layer], BF16), w_in, ffn_conv_w[layer], ffn_conv_b[layer],
                         state_ffn_conv[layer], ts)
        conv_s.append(st)
        xs = matmul(act, w_down, res=xs, tm=512, tn=256)

    y_p = rmsnorm(xp, final_norm, F32).reshape(bp, tp, d)
    y_s = rmsnorm(xs, final_norm, F32).reshape(bs, ts, d)
    return (y_p, y_s, jnp.stack(ssm_re_p), jnp.stack(ssm_im_p), jnp.stack(ssm_re_s), jnp.stack(ssm_im_s),
            jnp.stack(conv_p), jnp.stack(conv_s), outs["kv_rows_p"], outs["kv_rows_s"], outs["win_p"], outs["win_s"])


def kernel(x_prompt, x_sample, state_ssm_re, state_ssm_im, state_ffn_conv, cache_kv, cache_win, page_table, attn_norm, ffn_norm, final_norm, ssm_lam_re, ssm_lam_im, ssm_log_step, ssm_b_re, ssm_b_im, ssm_c_re, ssm_c_im, ssm_d, ssm_w_glu, ffn_w_in, ffn_conv_w, ffn_conv_b, ffn_w_down, kv_norm, w_kv, cmp_w1, cmp_b1, cmp_w2, cmp_b2, cmp_pe, w_qg, w_o):
    return _step(x_prompt, x_sample, state_ssm_re, state_ssm_im, state_ffn_conv, cache_kv, cache_win, page_table,
                 attn_norm, ffn_norm, final_norm, ssm_lam_re, ssm_lam_im, ssm_log_step, ssm_b_re, ssm_b_im,
                 ssm_c_re, ssm_c_im, ssm_d, ssm_w_glu, ffn_w_in, ffn_conv_w, ffn_conv_b, ffn_w_down,
                 kv_norm, w_kv, cmp_w1, cmp_b1, cmp_w2, cmp_b2, cmp_pe, w_qg, w_o)
```

```python
import functools
import math

import jax
import jax.numpy as jnp
from jax import lax
from jax.experimental import pallas as pl
from jax.experimental.pallas import tpu as pltpu

F32 = jnp.float32
BF16 = jnp.bfloat16
I32 = jnp.int32

RMS_EPS = 1e-6
LANES = 128
SUBLANES = 8
VMEM_LIMIT = 56 * 1024 * 1024

SSM_GROUP = 16
SSM_STATE = 64
CONV_W = 3
HEAD_DIM = 128
N_KV_HEADS = 4
CMP_STRIDE = 16
CMP_BLOCK = 32
SEL_BLOCK = 64
SEL_SHIFT = SEL_BLOCK.bit_length() - 1
N_SEL = 16
WINDOW = 512
Q_BLOCK = 128
PAGE_SIZE = 128
NEG_BIG = -1e30
M_FLOOR = 0.1 * NEG_BIG
LOG2_E = math.log2(math.e)
ONES_ROWS = 16

S5_SET_GROUPS = 4
S5_SETS = 8
S5_BLK_GROUPS = S5_SET_GROUPS * S5_SETS
S5_BLK_CH = S5_BLK_GROUPS * SSM_GROUP
S5_SET_LANES = S5_SET_GROUPS * SSM_STATE


def _cparams(sem, vmem=VMEM_LIMIT):
    return pltpu.CompilerParams(dimension_semantics=sem, vmem_limit_bytes=vmem)


def _rmsnorm_body(x_ref, g_ref, o_ref):
    x = x_ref[...].astype(F32)
    y = x * lax.rsqrt(jnp.mean(x * x, axis=-1, keepdims=True) + RMS_EPS)
    o_ref[...] = (y * g_ref[...]).astype(o_ref.dtype)


def rmsnorm(x, g, out_dtype):
    m, d = x.shape
    tm = math.gcd(m, 256)
    return pl.pallas_call(
        _rmsnorm_body,
        grid=(m // tm,),
        in_specs=[pl.BlockSpec((tm, d), lambda i: (i, 0)), pl.BlockSpec((1, d), lambda i: (0, 0))],
        out_specs=pl.BlockSpec((tm, d), lambda i: (i, 0)),
        out_shape=jax.ShapeDtypeStruct((m, d), out_dtype),
        compiler_params=_cparams(("parallel",)),
        name="rmsnorm",
    )(x, g.reshape(1, d).astype(F32))


def _mm_body(*refs, has_res, act):
    if has_res:
        x_ref, w_ref, res_ref, o_ref = refs
    else:
        x_ref, w_ref, o_ref = refs
    y = jnp.dot(x_ref[...], w_ref[...], preferred_element_type=F32)
    if act == "sigmoid":
        y = jax.nn.sigmoid(y)
    if has_res:
        y = res_ref[...] + y
    o_ref[...] = y.astype(o_ref.dtype)


def matmul(x, w, res=None, out_dtype=F32, act=None, tm=512, tn=256):
    m, k = x.shape
    n = w.shape[1]
    tm = math.gcd(m, tm)
    tn = math.gcd(n, tn)
    in_specs = [pl.BlockSpec((tm, k), lambda i, j: (i, 0)), pl.BlockSpec((k, tn), lambda i, j: (0, j))]
    args = [x, w]
    if res is not None:
        in_specs.append(pl.BlockSpec((tm, tn), lambda i, j: (i, j)))
        args.append(res)
    return pl.pallas_call(
        functools.partial(_mm_body, has_res=res is not None, act=act),
        grid=(m // tm, n // tn),
        in_specs=in_specs,
        out_specs=pl.BlockSpec((tm, tn), lambda i, j: (i, j)),
        out_shape=jax.ShapeDtypeStruct((m, n), out_dtype),
        compiler_params=_cparams(("parallel", "arbitrary")),
        name="matmul",
    )(*args)


def _mm_split_body(x_ref, w_ref, o1_ref, o2_ref, *, n1_tiles):
    j = pl.program_id(1)
    y = jnp.dot(x_ref[...], w_ref[...], preferred_element_type=F32)

    @pl.when(j < n1_tiles)
    def _():
        o1_ref[...] = y

    @pl.when(j >= n1_tiles)
    def _():
        o2_ref[...] = y


def matmul_split(x, w, n1, tm=1024, tn=512):
    m, k = x.shape
    n = w.shape[1]
    tm = math.gcd(m, tm)
    tn = math.gcd(math.gcd(n1, n - n1), tn)
    t1 = n1 // tn
    return pl.pallas_call(
        functools.partial(_mm_split_body, n1_tiles=t1),
        grid=(m // tm, n // tn),
        in_specs=[pl.BlockSpec((tm, k), lambda i, j: (i, 0)), pl.BlockSpec((k, tn), lambda i, j: (0, j))],
        out_specs=[pl.BlockSpec((tm, tn), lambda i, j: (i, jnp.minimum(j, t1 - 1))),
                   pl.BlockSpec((tm, tn), lambda i, j: (i, jnp.maximum(j - t1, 0)))],
        out_shape=[jax.ShapeDtypeStruct((m, n1), F32), jax.ShapeDtypeStruct((m, n - n1), F32)],
        compiler_params=_cparams(("parallel", "arbitrary")),
        name="matmul_split",
    )(x, w)


def _glu_body(x_ref, wa_ref, wb_ref, res_ref, o_ref):
    x = x_ref[...]
    a = jnp.dot(x, wa_ref[...], preferred_element_type=F32)
    b = jnp.dot(x, wb_ref[...], preferred_element_type=F32)
    o_ref[...] = res_ref[...] + a * jax.nn.sigmoid(b)


def glu_matmul(x, w, res, tm=1024, tn=256):
    m, k = x.shape
    n = w.shape[1] // 2
    tm = math.gcd(m, tm)
    tn = math.gcd(n, tn)
    nt = n // tn
    return pl.pallas_call(
        _glu_body,
        grid=(m // tm, nt),
        in_specs=[pl.BlockSpec((tm, k), lambda i, j: (i, 0)),
                  pl.BlockSpec((k, tn), lambda i, j: (0, j)),
                  pl.BlockSpec((k, tn), lambda i, j: (0, j + nt)),
                  pl.BlockSpec((tm, tn), lambda i, j: (i, j))],
        out_specs=pl.BlockSpec((tm, tn), lambda i, j: (i, j)),
        out_shape=jax.ShapeDtypeStruct((m, n), F32),
        compiler_params=_cparams(("parallel", "arbitrary")),
        name="glu_matmul",
    )(x, w, w, res)


def _conv_gate(g, g1, g2, cw_ref, cb_ref, val):
    gc = cb_ref[...] + cw_ref[0:1, :] * g2 + cw_ref[1:2, :] * g1 + cw_ref[2:3, :] * g
    return (gc * jax.nn.sigmoid(gc) * val).astype(BF16)


def _ffn_up_seq_body(x_ref, wv_ref, wg_ref, cw_ref, cb_ref, init_ref, act_ref, st_ref, carry_ref, *, tiles_per_seq):
    i = pl.program_id(0)
    j = pl.program_id(1)
    x = x_ref[...]
    val = jnp.dot(x, wv_ref[...], preferred_element_type=F32)
    g = jnp.dot(x, wg_ref[...], preferred_element_type=F32)
    tm = g.shape[0]

    @pl.when(lax.rem(i, tiles_per_seq) == 0)
    def _():
        carry_ref[j] = init_ref[0]

    prev = carry_ref[j]
    p6 = prev[6:7, :]
    p7 = prev[7:8, :]
    row = lax.broadcasted_iota(I32, g.shape, 0)
    g1 = jnp.where(row == 0, p7, pltpu.roll(g, 1, axis=0))
    g2 = jnp.where(row == 0, p6, jnp.where(row == 1, p7, pltpu.roll(g, 2, axis=0)))
    act_ref[...] = _conv_gate(g, g1, g2, cw_ref, cb_ref, val)
    last = g[tm - SUBLANES:, :]
    carry_ref[j] = last
    st_ref[0] = last


def _ffn_up_short_body(x_ref, wv_ref, wg_ref, cw_ref, cb_ref, p1_ref, p2_ref, act_ref, g_ref, *, seq_len):
    x = x_ref[...]
    val = jnp.dot(x, wv_ref[...], preferred_element_type=F32)
    g = jnp.dot(x, wg_ref[...], preferred_element_type=F32)
    tmod = lax.rem(lax.broadcasted_iota(I32, g.shape, 0), seq_len)
    g1 = jnp.where(tmod >= 1, pltpu.roll(g, 1, axis=0), p1_ref[...])
    g2 = jnp.where(tmod >= 2, pltpu.roll(g, 2, axis=0), p2_ref[...])
    act_ref[...] = _conv_gate(g, g1, g2, cw_ref, cb_ref, val)
    g_ref[...] = g


def ffn_up(h, w_in, conv_w, conv_b, buf, seq_len, tm=1024, tn=256):
    m, k = h.shape
    f = w_in.shape[1] // 2
    bsz = m // seq_len
    tn = math.gcd(f, tn)
    nt = f // tn
    cw = jnp.zeros((SUBLANES, f), F32).at[:CONV_W].set(conv_w.astype(F32))
    cb = conv_b.reshape(1, f).astype(F32)
    buf = buf.astype(F32)
    if seq_len % SUBLANES == 0 and seq_len >= 2 * SUBLANES:
        tm = math.gcd(seq_len, tm)
        tps = seq_len // tm
        init = jnp.zeros((bsz, SUBLANES, f), F32).at[:, SUBLANES - 2:].set(buf)
        act, st = pl.pallas_call(
            functools.partial(_ffn_up_seq_body, tiles_per_seq=tps),
            grid=(m // tm, nt),
            in_specs=[pl.BlockSpec((tm, k), lambda i, j: (i, 0)),
                      pl.BlockSpec((k, tn), lambda i, j: (0, j)),
                      pl.BlockSpec((k, tn), lambda i, j: (0, j + nt)),
                      pl.BlockSpec((SUBLANES, tn), lambda i, j: (0, j)),
                      pl.BlockSpec((1, tn), lambda i, j: (0, j)),
                      pl.BlockSpec((1, SUBLANES, tn), lambda i, j: (i // tps, 0, j))],
            out_specs=[pl.BlockSpec((tm, tn), lambda i, j: (i, j)),
                       pl.BlockSpec((1, SUBLANES, tn), lambda i, j: (i, 0, j))],
            out_shape=[jax.ShapeDtypeStruct((m, f), BF16), jax.ShapeDtypeStruct((m // tm, SUBLANES, f), F32)],
            scratch_shapes=[pltpu.VMEM((nt, SUBLANES, tn), F32)],
            compiler_params=_cparams(("arbitrary", "arbitrary")),
            name="ffn_up_seq",
        )(h, w_in, w_in, cw, cb, init)
        return act, st[tps - 1::tps, SUBLANES - 2:]
    assert seq_len >= 2
    pos = jnp.arange(m) % seq_len
    b0 = jnp.repeat(buf[:, 0], seq_len, axis=0)
    b1 = jnp.repeat(buf[:, 1], seq_len, axis=0)
    p1 = jnp.where((pos == 0)[:, None], b1, 0.0)
    p2 = jnp.where((pos == 0)[:, None], b0, jnp.where((pos == 1)[:, None], b1, 0.0))
    act, g = pl.pallas_call(
        functools.partial(_ffn_up_short_body, seq_len=seq_len),
        grid=(nt,),
        in_specs=[pl.BlockSpec((m, k), lambda j: (0, 0)),
                  pl.BlockSpec((k, tn), lambda j: (0, j)),
                  pl.BlockSpec((k, tn), lambda j: (0, j + nt)),
                  pl.BlockSpec((SUBLANES, tn), lambda j: (0, j)),
                  pl.BlockSpec((1, tn), lambda j: (0, j)),
                  pl.BlockSpec((m, tn), lambda j: (0, j)),
                  pl.BlockSpec((m, tn), lambda j: (0, j))],
        out_specs=[pl.BlockSpec((m, tn), lambda j: (0, j)), pl.BlockSpec((m, tn), lambda j: (0, j))],
        out_shape=[jax.ShapeDtypeStruct((m, f), BF16), jax.ShapeDtypeStruct((m, f), F32)],
        compiler_params=_cparams(("parallel",)),
        name="ffn_up_short",
    )(h, w_in, w_in, cw, cb, p1, p2)
    return act, g.reshape(bsz, seq_len, f)[:, seq_len - 2:]


def _s5_body(u_ref, wb_ref, wc_ref, lre_ref, lim_ref, d_ref, h0re_ref, h0im_ref,
             z_ref, hre_ref, him_ref, sr0, sr1, si0, si1, hst_re, hst_im, *, tc_len, last_step):
    tci = pl.program_id(2)
    u = u_ref[0]
    halves = S5_SET_LANES // LANES
    sre = (sr0, sr1)
    sim = (si0, si1)

    def set_rows(j):
        return pl.ds(j, tc_len, stride=S5_SETS)

    for ti in range(S5_BLK_CH // LANES):
        ut = u[:, ti * LANES:(ti + 1) * LANES]
        hi = ut.astype(BF16)
        lo = (ut - hi.astype(F32)).astype(BF16)
        lhs = jnp.concatenate([hi, lo], axis=1)
        for jj in range(2):
            j = ti * 2 + jj
            bre = jnp.dot(lhs, wb_ref[0, j, 0], preferred_element_type=F32)
            bim = jnp.dot(lhs, wb_ref[0, j, 1], preferred_element_type=F32)
            for c in range(halves):
                sre[c][set_rows(j), :] = bre[:, c * LANES:(c + 1) * LANES]
                sim[c][set_rows(j), :] = bim[:, c * LANES:(c + 1) * LANES]

    @pl.when(tci == 0)
    def _():
        hst_re[...] = h0re_ref[0, 0]
        hst_im[...] = h0im_ref[0, 0]

    lam_r = lre_ref[0]
    lam_i = lim_ref[0]
    lr = [lam_r[:, c * LANES:(c + 1) * LANES] for c in range(halves)]
    li = [lam_i[:, c * LANES:(c + 1) * LANES] for c in range(halves)]

    def step(t, carry):
        r0 = pl.multiple_of(t * S5_SETS, S5_SETS)
        out = []
        for c in range(halves):
            hr, hi_ = carry[2 * c], carry[2 * c + 1]
            nr = lr[c] * hr - li[c] * hi_ + sre[c][pl.ds(r0, S5_SETS), :]
            ni = lr[c] * hi_ + li[c] * hr + sim[c][pl.ds(r0, S5_SETS), :]
            sre[c][pl.ds(r0, S5_SETS), :] = nr
            sim[c][pl.ds(r0, S5_SETS), :] = ni
            out += [nr, ni]
        return tuple(out)

    h_in = hst_re[...]
    g_in = hst_im[...]
    init = []
    for c in range(halves):
        init += [h_in[:, c * LANES:(c + 1) * LANES], g_in[:, c * LANES:(c + 1) * LANES]]
    fin = lax.fori_loop(0, tc_len, step, tuple(init), unroll=8)
    for c in range(halves):
        hst_re[:, c * LANES:(c + 1) * LANES] = fin[2 * c]
        hst_im[:, c * LANES:(c + 1) * LANES] = fin[2 * c + 1]

    @pl.when(tci == last_step // tc_len)
    def _():
        r0 = (last_step % tc_len) * S5_SETS
        for c in range(halves):
            hre_ref[0, 0, :, c * LANES:(c + 1) * LANES] = sre[c][pl.ds(r0, S5_SETS), :]
            him_ref[0, 0, :, c * LANES:(c + 1) * LANES] = sim[c][pl.ds(r0, S5_SETS), :]

    for c in range(S5_BLK_CH // LANES):
        parts = []
        for s in range(2):
            j = 2 * c + s
            parts += [buf[set_rows(j), :].astype(BF16) for buf in sre + sim]
        lhs = jnp.concatenate(parts, axis=1)
        y = jnp.dot(lhs, wc_ref[0, c], preferred_element_type=F32)
        y = y + d_ref[:, c * LANES:(c + 1) * LANES] * u[:, c * LANES:(c + 1) * LANES]
        z_ref[0, :, c * LANES:(c + 1) * LANES] = jax.nn.gelu(y).astype(BF16)


def _s5_weights(lam_re, lam_im, log_step, b_re, b_im, c_re, c_im):
    g, p = lam_re.shape
    n = SSM_GROUP
    nblk = g // S5_BLK_GROUPS
    lam = lax.complex(lam_re.astype(F32), lam_im.astype(F32))
    dt = jnp.exp(log_step.astype(F32))[:, None]
    lam_bar = jnp.exp(lam * dt)
    b_bar = ((lam_bar - 1.0) / lam)[..., None] * lax.complex(b_re.astype(F32), b_im.astype(F32))
    eye_q = jnp.eye(S5_SET_GROUPS, dtype=F32)
    eye_h = jnp.eye(2, dtype=F32)

    def b_operand(bm):
        t = bm.reshape(nblk, S5_SETS // 2, 2, S5_SET_GROUPS, p, n)
        w = jnp.einsum("bthqpn,hk,qr->bthkrnqp", t, eye_h, eye_q)
        w = w.reshape(nblk, S5_SETS, LANES, S5_SET_LANES)
        return w

    wb = jnp.stack([b_operand(jnp.real(b_bar)), b_operand(jnp.imag(b_bar))], axis=2)
    wb = wb.astype(BF16)
    wb = jnp.concatenate([wb, wb], axis=3)

    def c_operand(cm):
        t = cm.reshape(nblk, S5_SETS // 2, 2, S5_SET_GROUPS, n, p)
        return jnp.einsum("bcsqnp,st,qr->bcsqptrn", t, eye_h, eye_q)

    cre = c_operand(c_re.astype(F32))
    cim = c_operand(-c_im.astype(F32))
    wc = jnp.stack([cre, cim], axis=3).reshape(nblk, S5_SETS // 2, 4 * S5_SET_LANES, LANES).astype(BF16)
    lre = jnp.real(lam_bar).reshape(nblk, S5_SETS, S5_SET_LANES)
    lim = jnp.imag(lam_bar).reshape(nblk, S5_SETS, S5_SET_LANES)
    return wb, wc, lre, lim


def s5_scan(u, h0_re, h0_im, weights, d_skip, seq_valid, tc_len):
    wb, wc, lre, lim = weights
    bsz, lp, d = u.shape
    nblk = d // S5_BLK_CH
    n_tc = lp // tc_len
    h0_re = h0_re.astype(F32).reshape(bsz, nblk, S5_SETS, S5_SET_LANES)
    h0_im = h0_im.astype(F32).reshape(bsz, nblk, S5_SETS, S5_SET_LANES)
    state_spec = pl.BlockSpec((1, 1, S5_SETS, S5_SET_LANES), lambda k, b, t: (b, k, 0, 0))
    z, hre, him = pl.pallas_call(
        functools.partial(_s5_body, tc_len=tc_len, last_step=seq_valid - 1),
        grid=(nblk, bsz, n_tc),
        in_specs=[pl.BlockSpec((1, tc_len, S5_BLK_CH), lambda k, b, t: (b, t, k)),
                  pl.BlockSpec((1, S5_SETS, 2, 2 * LANES, S5_SET_LANES), lambda k, b, t: (k, 0, 0, 0, 0)),
                  pl.BlockSpec((1, S5_SETS // 2, 4 * S5_SET_LANES, LANES), lambda k, b, t: (k, 0, 0, 0)),
                  pl.BlockSpec((1, S5_SETS, S5_SET_LANES), lambda k, b, t: (k, 0, 0)),
                  pl.BlockSpec((1, S5_SETS, S5_SET_LANES), lambda k, b, t: (k, 0, 0)),
                  pl.BlockSpec((1, S5_BLK_CH), lambda k, b, t: (0, k)),
                  state_spec, state_spec],
        out_specs=[pl.BlockSpec((1, tc_len, S5_BLK_CH), lambda k, b, t: (b, t, k)), state_spec, state_spec],
        out_shape=[jax.ShapeDtypeStruct((bsz, lp, d), BF16),
                   jax.ShapeDtypeStruct((bsz, nblk, S5_SETS, S5_SET_LANES), F32),
                   jax.ShapeDtypeStruct((bsz, nblk, S5_SETS, S5_SET_LANES), F32)],
        scratch_shapes=[pltpu.VMEM((tc_len * S5_SETS, LANES), F32)] * (2 * S5_SET_LANES // LANES) + [
                        pltpu.VMEM((S5_SETS, S5_SET_LANES), F32),
                        pltpu.VMEM((S5_SETS, S5_SET_LANES), F32)],
        compiler_params=_cparams(("arbitrary", "arbitrary", "arbitrary")),
        name="s5_scan",
    )(u, wb, wc, lre, lim, d_skip.reshape(1, d).astype(F32), h0_re, h0_im)
    g = d // SSM_GROUP
    return z, hre.reshape(bsz, g, SSM_STATE), him.reshape(bsz, g, SSM_STATE)


def _cmp_proj_body(pt_ref, x_hbm, w_ref, o_ref, stage, sems, *, pages_per_step):
    n_heads2 = 2 * N_KV_HEADS
    chunks = PAGE_SIZE // CMP_STRIDE
    step = pl.program_id(0) * pl.num_programs(1) + pl.program_id(1)
    n_steps = pl.num_programs(0) * pl.num_programs(1)
    slot = lax.rem(step, 2)

    def page_copy(st, sl, p, c):
        page = pt_ref[st * pages_per_step + p]
        return pltpu.make_async_copy(x_hbm.at[page, :, c, :], stage.at[sl, p, c], sems.at[sl])

    def start_step(st, sl):
        for p in range(pages_per_step):
            for c in range(n_heads2):
                page_copy(st, sl, p, c).start()

    @pl.when(step == 0)
    def _():
        start_step(step, slot)

    @pl.when(step + 1 < n_steps)
    def _():
        start_step(step + 1, 1 - slot)

    for p in range(pages_per_step):
        for c in range(n_heads2):
            page_copy(step, slot, p, c).wait()

    def chunk_rows(c, j):
        return jnp.concatenate([stage[slot, p, c, pl.ds(j, chunks, stride=CMP_STRIDE), :]
                                for p in range(pages_per_step)], axis=0)

    for c in range(n_heads2):
        s, h = divmod(c, N_KV_HEADS)
        acc = None
        for jp in range(CMP_STRIDE // 2):
            lhs = jnp.concatenate([chunk_rows(c, 2 * jp), chunk_rows(c, 2 * jp + 1)], axis=1).astype(BF16)
            t = jnp.dot(lhs, w_ref[s, jp], preferred_element_type=F32)
            acc = t if acc is None else acc + t
        o_ref[0, s, h] = acc


def cmp_project(rows, page_table, w1cat, pages_per_step=16):
    bsz, npg = page_table.shape
    pps = math.gcd(npg, pages_per_step)
    chunks = PAGE_SIZE // CMP_STRIDE
    hid2 = w1cat.shape[-1]
    grid_spec = pltpu.PrefetchScalarGridSpec(
        num_scalar_prefetch=1,
        grid=(bsz, npg // pps),
        in_specs=[pl.BlockSpec(memory_space=pl.ANY),
                  pl.BlockSpec(w1cat.shape, lambda b, g, pt: (0, 0, 0, 0))],
        out_specs=pl.BlockSpec((1, 2, N_KV_HEADS, pps * chunks, hid2), lambda b, g, pt: (b, 0, 0, g, 0)),
        scratch_shapes=[pltpu.VMEM((2, pps, 2 * N_KV_HEADS, PAGE_SIZE, HEAD_DIM), F32),
                        pltpu.SemaphoreType.DMA((2,))],
    )
    return pl.pallas_call(
        functools.partial(_cmp_proj_body, pages_per_step=pps),
        grid_spec=grid_spec,
        out_shape=jax.ShapeDtypeStruct((bsz, 2, N_KV_HEADS, npg * chunks, hid2), F32),
        compiler_params=_cparams(("arbitrary", "arbitrary")),
        name="cmp_project",
    )(page_table.reshape(-1).astype(I32), rows, w1cat)


def _cmp_finish_body(ab_ref, pe_ref, w1_ref, b1_ref, w2_ref, b2_ref, o_ref):
    ab = ab_ref[0, 0, 0]
    n16 = ab.shape[0]
    hid = ab.shape[1] // 2
    a = ab[:, :hid]
    bnext = pltpu.roll(ab[:, hid:], n16 - 1, axis=0)
    c = jnp.dot(pe_ref[0], w1_ref[0], preferred_element_type=F32)[0:1, :] + b1_ref[0]
    pre = a + bnext + c
    y = jnp.dot(jax.nn.gelu(pre).astype(BF16), w2_ref[0], preferred_element_type=F32) + b2_ref[0]
    row = lax.broadcasted_iota(I32, y.shape, 0)
    o_ref[0, 0, 0] = jnp.where(row < n16 - 1, y, 0.0).astype(o_ref.dtype)


def cmp_finish(ab, pe, w1, b1, w2, b2):
    bsz, _, kvh, n16, hid2 = ab.shape
    hid = hid2 // 2
    pe_rows = jnp.broadcast_to(pe.reshape(2, 1, -1), (2, SUBLANES, pe.shape[1] * pe.shape[2])).astype(BF16)
    return pl.pallas_call(
        _cmp_finish_body,
        grid=(2, bsz, kvh),
        in_specs=[pl.BlockSpec((1, 1, 1, n16, hid2), lambda s, b, h: (b, s, h, 0, 0)),
                  pl.BlockSpec((1, SUBLANES, pe_rows.shape[2]), lambda s, b, h: (s, 0, 0)),
                  pl.BlockSpec((1,) + w1.shape[1:], lambda s, b, h: (s, 0, 0)),
                  pl.BlockSpec((1, 1, hid), lambda s, b, h: (s, 0, 0)),
                  pl.BlockSpec((1, hid, HEAD_DIM), lambda s, b, h: (s, 0, 0)),
                  pl.BlockSpec((1, 1, HEAD_DIM), lambda s, b, h: (s, 0, 0))],
        out_specs=pl.BlockSpec((1, 1, 1, n16, HEAD_DIM), lambda s, b, h: (b, s, h, 0, 0)),
        out_shape=jax.ShapeDtypeStruct((bsz, 2, kvh, n16, HEAD_DIM), BF16),
        compiler_params=_cparams(("arbitrary", "arbitrary", "arbitrary")),
        name="cmp_finish",
    )(ab, pe_rows, w1.astype(BF16), b1.reshape(2, 1, hid).astype(F32), w2.astype(BF16),
      b2.reshape(2, 1, HEAD_DIM).astype(F32))


def _masked_softmax(s, mask):
    sm = jnp.where(mask, s, NEG_BIG)
    m = jnp.max(sm, axis=-1, keepdims=True)
    m = jnp.where(m > 0.5 * NEG_BIG, m, 0.0)
    e = jnp.where(mask, jnp.exp(sm - m), 0.0)
    den = jnp.sum(e, axis=-1, keepdims=True)
    return e / jnp.where(den > 0, den, 1.0)


def _dot_nt(a, b):
    return lax.dot_general(a, b, (((1,), (1,)), ((), ())), preferred_element_type=F32)


def _overlap(n_c, n_j, c_axis):
    shape = (n_c, n_j) if c_axis == 0 else (n_j, n_c)
    c = lax.broadcasted_iota(I32, shape, c_axis) * CMP_STRIDE
    j = lax.broadcasted_iota(I32, shape, 1 - c_axis) * SEL_BLOCK
    return ((c < j + SEL_BLOCK) & (c + CMP_BLOCK > j)).astype(F32)


def _block_scores(imp, q_pos, n_blk):
    blk = lax.broadcasted_iota(I32, imp.shape, 1)
    q_blk = jnp.right_shift(q_pos, SEL_SHIFT)
    forced = (blk == 0) | (blk == q_blk) | (blk == q_blk - 1)
    future = blk * SEL_BLOCK > q_pos
    v = jnp.where(future, NEG_BIG, jnp.where(forced, -NEG_BIG, imp))
    return jnp.where(blk < n_blk, v, 2.0 * NEG_BIG)


def _rank_desc(v, n):
    lane = lax.broadcasted_iota(I32, v.shape, 1)
    rank = jnp.zeros(v.shape, F32)
    for i in range(n):
        vi = v[:, i:i + 1]
        before = (vi > v) | ((vi == v) & (lane > i))
        rank = rank + jnp.where(before, 1.0, 0.0)
    return rank


def _nsa_prompt_body(qt_ref, gt_ref, ck_ref, cvt_ref, ks_ref, vst_ref, kw_ref, vwt_ref, o_ref, m_ref, acc_ref,
                     *, n_cmp, n_blk, gqa, sel_tile, win_keys):
    qb = pl.program_id(2)
    nq = Q_BLOCK
    gq = gqa * nq
    c2 = (HEAD_DIM ** -0.5) * LOG2_E
    qt = qt_ref[0, 0, 0]
    q_pos = qb * nq + lax.broadcasted_iota(I32, (1, nq), 1)

    def per_head(x):
        return jnp.concatenate([x] * gqa, axis=1)

    def normalise(acc):
        den = acc[HEAD_DIM:HEAD_DIM + 1, :]
        return acc[:HEAD_DIM, :] * (1.0 / jnp.where(den > 0, den, 1.0))

    ck = ck_ref[0, 0, 0]
    n16 = ck.shape[0]
    s = jnp.dot(ck, qt, preferred_element_type=F32)
    cidx = lax.broadcasted_iota(I32, (n16, 1), 0)
    cbias = jnp.where((cidx * CMP_STRIDE + (CMP_BLOCK - 1) <= q_pos) & (cidx < n_cmp), 0.0, NEG_BIG)
    sb = s + per_head(cbias)
    m = jnp.maximum(jnp.max(sb, axis=0, keepdims=True), M_FLOOR)
    e = jnp.exp2((sb - m) * c2)
    den = jnp.sum(e, axis=0, keepdims=True)
    p = e * jnp.where(den > 0, 1.0 / den, 0.0)
    o_cmp = jnp.dot(cvt_ref[0, 0], p.astype(BF16), preferred_element_type=F32)

    psum = p[:, 0:nq]
    for g in range(1, gqa):
        psum = psum + p[:, g * nq:(g + 1) * nq]
    imp = jnp.dot(_overlap(n16, LANES, 1), psum, preferred_element_type=F32, precision=lax.Precision.HIGHEST)
    blk = lax.broadcasted_iota(I32, (LANES, 1), 0)
    q_blk = jnp.right_shift(q_pos, SEL_SHIFT)
    forced = (blk == 0) | (blk == q_blk) | (blk == q_blk - 1)
    v = jnp.where(blk * SEL_BLOCK > q_pos, NEG_BIG, jnp.where(forced, -NEG_BIG, imp))
    v = jnp.where(blk < n_blk, v, 2.0 * NEG_BIG)
    rank = jnp.zeros(v.shape, F32)
    for i in range(n_blk):
        vi = v[i:i + 1, :]
        rank = rank + jnp.where((vi > v) | ((vi == v) & (blk > i)), 1.0, 0.0)
    sel_bias = jnp.where((rank < min(N_SEL, n_blk)) & (blk < n_blk), 0.0, NEG_BIG).astype(BF16)

    m_ref[...] = jnp.full(m_ref.shape, M_FLOOR, F32)
    acc_ref[...] = jnp.zeros(acc_ref.shape, F32)
    tiles = sel_tile // LANES

    def sel_step(kt, _):
        k0 = pl.multiple_of(kt * sel_tile, sel_tile)
        s = jnp.dot(ks_ref[0, pl.ds(k0, sel_tile), :].astype(BF16), qt, preferred_element_type=F32)
        key = k0 + lax.broadcasted_iota(I32, (sel_tile, 1), 0)
        expand = jnp.where(jnp.right_shift(key, SEL_SHIFT) == lax.broadcasted_iota(I32, (sel_tile, LANES), 1), 1.0, 0.0)
        bias = jnp.dot(expand.astype(BF16), sel_bias, preferred_element_type=F32)
        bias = jnp.where(key <= q_pos, bias, NEG_BIG)
        sb = s + per_head(bias)
        m_old = m_ref[0:1, :]
        m_new = jnp.maximum(m_old, jnp.max(sb, axis=0, keepdims=True))
        alpha = jnp.exp2((m_old - m_new) * c2)
        e = jnp.exp2((sb - m_new) * c2).astype(BF16)
        vt = jnp.concatenate([vst_ref[0, 0, kt * tiles + i] for i in range(tiles)], axis=1)
        acc_ref[...] = alpha * acc_ref[...] + jnp.dot(vt, e, preferred_element_type=F32)
        m_ref[...] = jnp.broadcast_to(m_new, m_ref.shape)
        return 0

    lax.fori_loop(0, (qb * nq + nq + sel_tile - 1) // sel_tile, sel_step, 0)
    o_sel = normalise(acc_ref[...])

    w_tile = jnp.maximum(qb + 1 - win_keys // nq, 0)
    w0 = pl.multiple_of(w_tile * nq, nq)
    s = jnp.dot(kw_ref[0, pl.ds(w0, win_keys), :].astype(BF16), qt, preferred_element_type=F32)
    key = w0 + lax.broadcasted_iota(I32, (win_keys, 1), 0)
    wbias = jnp.where((key <= q_pos) & (key > q_pos - WINDOW), 0.0, NEG_BIG)
    sb = s + per_head(wbias)
    m = jnp.maximum(jnp.max(sb, axis=0, keepdims=True), M_FLOOR)
    e = jnp.exp2((sb - m) * c2).astype(BF16)
    vt = jnp.concatenate([vwt_ref[0, 0, w_tile + i] for i in range(win_keys // LANES)], axis=1)
    o_win = normalise(jnp.dot(vt, e, preferred_element_type=F32))

    gt = gt_ref[0, 0, 0]
    for g in range(gqa):
        cols = slice(g * nq, (g + 1) * nq)
        o = (gt[3 * g:3 * g + 1, :] * o_cmp[:, cols] + gt[3 * g + 1:3 * g + 2, :] * o_sel[:, cols]
             + gt[3 * g + 2:3 * g + 3, :] * o_win[:, cols])
        o_ref[0, 0, 0, :, cols] = o.astype(o_ref.dtype)


def _transposed_values(rows, bsz, t):
    v = rows.reshape(bsz, t // LANES, LANES, N_KV_HEADS, HEAD_DIM).transpose(0, 3, 1, 4, 2).astype(BF16)
    ones = jnp.ones(v.shape[:3] + (ONES_ROWS, LANES), BF16)
    return jnp.concatenate([v, ones], axis=3)


def nsa_prompt_attend(q, gates, cmp, k_sel, v_sel, k_win, v_win, bsz, t, sel_tile=512):
    hd_all = q.shape[1]
    kvh = N_KV_HEADS
    gqa = hd_all // HEAD_DIM // kvh
    n16 = cmp.shape[3]
    n_blk = -(-t // SEL_BLOCK)
    nqb = t // Q_BLOCK
    gq = gqa * Q_BLOCK
    vrows = HEAD_DIM + ONES_ROWS
    assert t % Q_BLOCK == 0 and n_blk <= LANES and n16 * CMP_STRIDE == t
    sel_tile = math.gcd(t, sel_tile)
    win_keys = min(WINDOW + Q_BLOCK, t)
    qt = q.reshape(bsz, nqb, Q_BLOCK, kvh, gqa, HEAD_DIM).transpose(0, 3, 1, 5, 4, 2).reshape(bsz, kvh, nqb, HEAD_DIM, gq)
    gt = gates[:, :3 * kvh * gqa].reshape(bsz, nqb, Q_BLOCK, kvh, 3 * gqa).transpose(0, 3, 1, 4, 2)
    gt = jnp.pad(gt, ((0, 0), (0, 0), (0, 0), (0, -3 * gqa % SUBLANES), (0, 0)))
    cvt = cmp[:, 1].transpose(0, 1, 3, 2)
    (ks_arr, ks_col), (vs_arr, vs_col), (kw_arr, kw_col), (vw_arr, vw_col) = k_sel, v_sel, k_win, v_win
    width = kvh * HEAD_DIM
    vst = _transposed_values(vs_arr[:, vs_col * HEAD_DIM:vs_col * HEAD_DIM + width], bsz, t)
    vwt = _transposed_values(vw_arr[:, vw_col * HEAD_DIM:vw_col * HEAD_DIM + width], bsz, t)
    k_spec = lambda col: pl.BlockSpec((1, t, HEAD_DIM), lambda b, h, i, col=col: (b, 0, col + h))
    vt_spec = pl.BlockSpec((1, 1, t // LANES, vrows, LANES), lambda b, h, i: (b, h, 0, 0, 0))
    ot = pl.pallas_call(
        functools.partial(_nsa_prompt_body, n_cmp=n16 - 1, n_blk=n_blk, gqa=gqa, sel_tile=sel_tile, win_keys=win_keys),
        grid=(bsz, kvh, nqb),
        in_specs=[pl.BlockSpec((1, 1, 1, HEAD_DIM, gq), lambda b, h, i: (b, h, i, 0, 0)),
                  pl.BlockSpec((1, 1, 1, gt.shape[3], Q_BLOCK), lambda b, h, i: (b, h, i, 0, 0)),
                  pl.BlockSpec((1, 1, 1, n16, HEAD_DIM), lambda b, h, i: (b, 0, h, 0, 0)),
                  pl.BlockSpec((1, 1, HEAD_DIM, n16), lambda b, h, i: (b, h, 0, 0)),
                  k_spec(ks_col), vt_spec, k_spec(kw_col), vt_spec],
        out_specs=pl.BlockSpec((1, 1, 1, HEAD_DIM, gq), lambda b, h, i: (b, h, i, 0, 0)),
        out_shape=jax.ShapeDtypeStruct((bsz, kvh, nqb, HEAD_DIM, gq), BF16),
        scratch_shapes=[pltpu.VMEM((SUBLANES, gq), F32), pltpu.VMEM((vrows, gq), F32)],
        compiler_params=_cparams(("parallel", "parallel", "arbitrary")),
        name="nsa_prompt",
    )(qt, gt, cmp, cvt, ks_arr.reshape(bsz, t, -1), vst, kw_arr.reshape(bsz, t, -1), vwt)
    return ot.reshape(bsz, kvh, nqb, HEAD_DIM, gqa, Q_BLOCK).transpose(0, 2, 5, 1, 4, 3).reshape(bsz * t, hd_all)


def _nsa_sample_select_body(q_ref, ck_ref, cv_ref, ocmp_ref, idx_ref, *, n_cmp, n_blk, blk_lanes, past_len, tq_pad, gqa):
    scale = HEAD_DIM ** -0.5
    q = jnp.concatenate([q_ref[0, :, g * HEAD_DIM:(g + 1) * HEAD_DIM] for g in range(gqa)], axis=0)
    ck = ck_ref[0, 0, 0]
    cv = cv_ref[0, 0, 0]
    n16 = ck.shape[0]
    q_pos = past_len + lax.broadcasted_iota(I32, (tq_pad, 1), 0)
    s = (_dot_nt(q, ck) * scale).reshape(gqa, tq_pad, n16)
    cidx = lax.broadcasted_iota(I32, (tq_pad, n16), 1)
    cmask = (cidx * CMP_STRIDE + (CMP_BLOCK - 1) <= q_pos) & (cidx < n_cmp)
    p = _masked_softmax(s, cmask[None])
    ocmp_ref[0, 0] = jnp.dot(p.reshape(gqa * tq_pad, n16).astype(BF16), cv, preferred_element_type=F32)
    psum = jnp.sum(p, axis=0)
    imp = jnp.dot(psum, _overlap(n16, blk_lanes, 0), preferred_element_type=F32, precision=lax.Precision.HIGHEST)
    rank = _rank_desc(_block_scores(imp, q_pos, n_blk), n_blk)
    lane = lax.broadcasted_iota(I32, rank.shape, 1)
    lane_f = lane.astype(F32)
    out_lane = lax.broadcasted_iota(I32, (tq_pad, LANES), 1)
    idx = jnp.zeros((tq_pad, LANES), F32)
    for k in range(min(N_SEL, n_blk)):
        hit = (rank == float(k)) & (lane < n_blk)
        idx_k = jnp.sum(jnp.where(hit, lane_f, 0.0), axis=-1, keepdims=True)
        idx = jnp.where(out_lane == k, idx_k, idx)
    idx_ref[0, 0] = idx.astype(I32)


def _nsa_sample_attend_body(idx_ref, pt_ref, q_ref, gt_ref, ocmp_ref, kwc_ref, vwc_ref,
                            kwn_ref, vwn_ref, cache_ref, newblk_ref, o_ref, kbuf, vbuf, kwbuf, vwbuf, sems,
                            *, n_sel, n_cache_blk, n_pages, past_len, win_buf, tq, tq_pad, gqa):
    b = pl.program_id(0)
    h = pl.program_id(1)
    scale = HEAD_DIM ** -0.5
    blk_per_page = PAGE_SIZE // SEL_BLOCK

    bufs = ((2, kbuf, 0), (3, vbuf, 1))

    def from_new(t, k, slot, buf, sem):
        return pltpu.make_async_copy(newblk_ref.at[b, :, slot * N_KV_HEADS + h], buf.at[t, k], sems.at[sem])

    for t in range(tq):
        for k in range(n_sel):
            blk = idx_ref[((b * N_KV_HEADS + h) * tq + t) * n_sel + k]
            in_cache = blk < n_cache_blk
            blk_c = jnp.minimum(blk, n_cache_blk - 1)
            page = pt_ref[b * n_pages + lax.div(blk_c, blk_per_page)]
            off = pl.multiple_of(lax.rem(blk_c, blk_per_page) * SEL_BLOCK, SEL_BLOCK)
            for slot, buf, sem in bufs:
                @pl.when(in_cache)
                def _():
                    pltpu.make_async_copy(cache_ref.at[page, pl.ds(off, SEL_BLOCK), slot * N_KV_HEADS + h],
                                          buf.at[t, k], sems.at[sem]).start()

                @pl.when(jnp.logical_not(in_cache))
                def _():
                    from_new(t, k, slot, buf, sem).start()
    q = jnp.concatenate([q_ref[0, :, g * HEAD_DIM:(g + 1) * HEAD_DIM] for g in range(gqa)], axis=0)
    rows = gqa * tq_pad
    row_t = lax.rem(lax.broadcasted_iota(I32, (rows, 1), 0), tq_pad)
    q_pos = past_len + row_t

    kwbuf[...] = jnp.zeros(kwbuf.shape, F32)
    vwbuf[...] = jnp.zeros(vwbuf.shape, F32)
    kwbuf[0:win_buf, :] = kwc_ref[0]
    vwbuf[0:win_buf, :] = vwc_ref[0]
    kwbuf[win_buf:win_buf + tq_pad, :] = kwn_ref[0]
    vwbuf[win_buf:win_buf + tq_pad, :] = vwn_ref[0]
    nw = kwbuf.shape[0]
    wlane = lax.broadcasted_iota(I32, (rows, nw), 1)
    wpos = past_len - win_buf + wlane
    wmask = (wpos <= q_pos) & (wpos > q_pos - WINDOW) & (wpos >= 0) & (wlane < win_buf + tq)
    s = _dot_nt(q, kwbuf[...].astype(BF16)) * scale
    p = _masked_softmax(s, wmask)
    o_win = jnp.dot(p.astype(BF16), vwbuf[...].astype(BF16), preferred_element_type=F32)

    for t in range(tq):
        for k in range(n_sel):
            for slot, buf, sem in bufs:
                from_new(t, k, slot, buf, sem).wait()

    o_sel = jnp.zeros((rows, HEAD_DIM), F32)
    nk = n_sel * SEL_BLOCK
    klane = lax.broadcasted_iota(I32, (rows, nk), 1)
    for t in range(tq):
        kpos = jnp.zeros((rows, nk), I32)
        for k in range(n_sel):
            blk = idx_ref[((b * N_KV_HEADS + h) * tq + t) * n_sel + k]
            kpos = jnp.where(jnp.right_shift(klane, SEL_SHIFT) == k,
                             blk * SEL_BLOCK + jnp.bitwise_and(klane, SEL_BLOCK - 1), kpos)
        mask = (kpos <= q_pos) & (row_t == t)
        kt = kbuf[t].reshape(nk, HEAD_DIM).astype(BF16)
        vt = vbuf[t].reshape(nk, HEAD_DIM).astype(BF16)
        p = _masked_softmax(_dot_nt(q, kt) * scale, mask)
        o_sel = o_sel + jnp.dot(p.astype(BF16), vt, preferred_element_type=F32)

    o_cmp = ocmp_ref[0, 0]
    gt = gt_ref[0, 0]
    for g in range(gqa):
        r = slice(g * tq_pad, (g + 1) * tq_pad)
        o = (gt[:, 3 * g:3 * g + 1] * o_cmp[r] + gt[:, 3 * g + 1:3 * g + 2] * o_sel[r]
             + gt[:, 3 * g + 2:3 * g + 3] * o_win[r])
        o_ref[0, :, g * HEAD_DIM:(g + 1) * HEAD_DIM] = o.astype(o_ref.dtype)


def nsa_sample_attend(q, gates, cmp, new_rows, win_new, cache_kv, cache_win, page_table, tq):
    bsz, tq_pad, hd_all = q.shape
    kvh = N_KV_HEADS
    gqa = hd_all // HEAD_DIM // kvh
    n_pages = page_table.shape[1]
    past_len = n_pages * PAGE_SIZE
    n16 = cmp.shape[3]
    n_cmp = n16 - 1
    n_cache_blk = past_len // SEL_BLOCK
    n_blk = -(-(past_len + tq) // SEL_BLOCK)
    assert n_blk == n_cache_blk + 1 and past_len % SEL_BLOCK == 0 and tq <= SEL_BLOCK
    n_sel = min(N_SEL, n_blk)
    blk_lanes = -(-n_blk // LANES) * LANES
    win_buf = cache_win.shape[1]
    cmp_spec = lambda s: pl.BlockSpec((1, 1, 1, n16, HEAD_DIM), lambda b, h, s=s: (b, s, h, 0, 0))
    ocmp, idx = pl.pallas_call(
        functools.partial(_nsa_sample_select_body, n_cmp=n_cmp, n_blk=n_blk, blk_lanes=blk_lanes,
                          past_len=past_len, tq_pad=tq_pad, gqa=gqa),
        grid=(bsz, kvh),
        in_specs=[pl.BlockSpec((1, tq_pad, gqa * HEAD_DIM), lambda b, h: (b, 0, h)), cmp_spec(0), cmp_spec(1)],
        out_specs=[pl.BlockSpec((1, 1, gqa * tq_pad, HEAD_DIM), lambda b, h: (b, h, 0, 0)),
                   pl.BlockSpec((1, 1, tq_pad, LANES), lambda b, h: (b, h, 0, 0))],
        out_shape=[jax.ShapeDtypeStruct((bsz, kvh, gqa * tq_pad, HEAD_DIM), F32),
                   jax.ShapeDtypeStruct((bsz, kvh, tq_pad, LANES), I32)],
        compiler_params=_cparams(("parallel", "parallel")),
        name="nsa_sample_select",
    )(q, cmp, cmp)
    idx_flat = idx[:, :, :tq, :n_sel].reshape(-1)

    n_slots = cache_kv.shape[2]
    cache4 = cache_kv.reshape(cache_kv.shape[0], PAGE_SIZE, n_slots * kvh, HEAD_DIM)
    newblk = jnp.zeros((bsz, SEL_BLOCK, n_slots * kvh, HEAD_DIM), F32).at[:, :tq].set(new_rows)
    cwin = cache_win.astype(F32).reshape(bsz, win_buf, 2 * kvh * HEAD_DIM)
    nw = -(-(win_buf + tq_pad) // LANES) * LANES
    kv_spec = lambda slot: pl.BlockSpec((1, tq_pad, HEAD_DIM), lambda b, h, *_, slot=slot: (b, 0, slot * kvh + h))
    cw_spec = lambda slot: pl.BlockSpec((1, win_buf, HEAD_DIM), lambda b, h, *_, slot=slot: (b, 0, slot * kvh + h))
    grid_spec = pltpu.PrefetchScalarGridSpec(
        num_scalar_prefetch=2,
        grid=(bsz, kvh),
        in_specs=[pl.BlockSpec((1, tq_pad, gqa * HEAD_DIM), lambda b, h, *_: (b, 0, h)),
                  pl.BlockSpec((1, 1, tq_pad, LANES), lambda b, h, *_: (b, h, 0, 0)),
                  pl.BlockSpec((1, 1, gqa * tq_pad, HEAD_DIM), lambda b, h, *_: (b, h, 0, 0)),
                  cw_spec(0), cw_spec(1), kv_spec(0), kv_spec(1),
                  pl.BlockSpec(memory_space=pl.ANY), pl.BlockSpec(memory_space=pl.ANY)],
        out_specs=pl.BlockSpec((1, tq_pad, gqa * HEAD_DIM), lambda b, h, *_: (b, 0, h)),
        scratch_shapes=[pltpu.VMEM((tq, n_sel, SEL_BLOCK, HEAD_DIM), F32),
                        pltpu.VMEM((tq, n_sel, SEL_BLOCK, HEAD_DIM), F32),
                        pltpu.VMEM((nw, HEAD_DIM), F32), pltpu.VMEM((nw, HEAD_DIM), F32),
                        pltpu.SemaphoreType.DMA((2,))],
    )
    return pl.pallas_call(
        functools.partial(_nsa_sample_attend_body, n_sel=n_sel, n_cache_blk=n_cache_blk, n_pages=n_pages,
                          past_len=past_len, win_buf=win_buf, tq=tq, tq_pad=tq_pad, gqa=gqa),
        grid_spec=grid_spec,
        out_shape=jax.ShapeDtypeStruct((bsz, tq_pad, hd_all), BF16),
        compiler_params=_cparams(("arbitrary", "arbitrary")),
        name="nsa_sample_attend",
    )(idx_flat, page_table.reshape(-1).astype(I32), q, gates, ocmp, cwin, cwin, win_new, win_new, cache4, newblk)


def _pad_rows(x, rows):
    return jnp.pad(x, ((0, 0), (0, rows - x.shape[1]), (0, 0)))


@jax.jit
def _step(x_prompt, x_sample, state_ssm_re, state_ssm_im, state_ffn_conv, cache_kv, cache_win, page_table,
          attn_norm, ffn_norm, final_norm, ssm_lam_re, ssm_lam_im, ssm_log_step, ssm_b_re, ssm_b_im,
          ssm_c_re, ssm_c_im, ssm_d, ssm_w_glu, ffn_w_in, ffn_conv_w, ffn_conv_b, ffn_w_down,
          kv_norm, w_kv, cmp_w1, cmp_b1, cmp_w2, cmp_b2, cmp_pe, w_qg, w_o):
    bp, tp, d = x_prompt.shape
    bs, ts, _ = x_sample.shape
    depth = attn_norm.shape[0]
    n_a = ssm_lam_re.shape[0]
    kvh = N_KV_HEADS
    hd_all = w_o.shape[1]
    n_heads = hd_all // HEAD_DIM
    gqa = n_heads // kvh
    ts_pad = -(-ts // SUBLANES) * SUBLANES
    n_pages = page_table.shape[1]

    xp = x_prompt.reshape(bp * tp, d).astype(F32)
    xs = x_sample.reshape(bs * ts, d).astype(F32)

    w1cat = jnp.concatenate([cmp_w1[:, :CMP_STRIDE * HEAD_DIM], cmp_w1[:, CMP_STRIDE * HEAD_DIM:]], axis=-1)
    w1cat = w1cat.reshape(2, CMP_STRIDE // 2, 2 * HEAD_DIM, w1cat.shape[-1]).astype(BF16)

    ssm_re_p, ssm_im_p, ssm_re_s, ssm_im_s, conv_p, conv_s = [], [], [], [], [], []
    outs = {}
    for layer in range(depth):
        if layer < n_a:
            a = layer
            wts = _s5_weights(ssm_lam_re[a], ssm_lam_im[a], ssm_log_step[a], ssm_b_re[a], ssm_b_im[a],
                              ssm_c_re[a], ssm_c_im[a])
            w_glu = ssm_w_glu[a].astype(BF16)
            up = rmsnorm(xp, attn_norm[layer], F32).reshape(bp, tp, d)
            zeros = jnp.zeros((bp, d // SSM_GROUP, SSM_STATE), F32)
            zp, hre, him = s5_scan(up, zeros, zeros, wts, ssm_d[a], tp, math.gcd(tp, 512))
            ssm_re_p.append(hre)
            ssm_im_p.append(him)
            xp = glu_matmul(zp.reshape(bp * tp, d), w_glu, xp)
            us = _pad_rows(rmsnorm(xs, attn_norm[layer], F32).reshape(bs, ts, d), ts_pad)
            zs, hre, him = s5_scan(us, state_ssm_re[a], state_ssm_im[a], wts, ssm_d[a], ts, ts_pad)
            ssm_re_s.append(hre)
            ssm_im_s.append(him)
            xs = glu_matmul(zs[:, :ts].reshape(bs * ts, d), w_glu, xs)
        else:
            bl = layer - n_a
            if layer == n_a:
                w_kv_b = w_kv.astype(BF16)
                n_kv4 = 4 * kvh * HEAD_DIM
                kvr_p, kvw_p = matmul_split(rmsnorm(xp, kv_norm, BF16), w_kv_b, n_kv4)
                kvr_s, kvw_s = matmul_split(rmsnorm(xs, kv_norm, BF16), w_kv_b, n_kv4)
                outs["kv_rows_p"] = kvr_p.reshape(bp, tp, 4, kvh, HEAD_DIM)
                outs["kv_rows_s"] = kvr_s.reshape(bs, ts, 4, kvh, HEAD_DIM)
                outs["win_p"] = kvw_p.reshape(bp, tp, 2, kvh, HEAD_DIM)[:, max(tp - WINDOW, 0):]
                win_s = kvw_s.reshape(bs, ts, 2, kvh, HEAD_DIM)
                win_buf = cache_win.shape[1]
                outs["win_s"] = jnp.concatenate([cache_win.astype(F32), win_s], axis=1)[:, -win_buf:]
                kvw_s3 = _pad_rows(kvw_s.reshape(bs, ts, kvw_s.shape[1]), ts_pad)
                assert tp % PAGE_SIZE == 0 and (n_pages * PAGE_SIZE + ts) // CMP_STRIDE == n_pages * PAGE_SIZE // CMP_STRIDE
                pt_p = jnp.arange(bp * (tp // PAGE_SIZE), dtype=I32).reshape(bp, tp // PAGE_SIZE)
                ab_p = cmp_project(kvr_p.reshape(bp * tp // PAGE_SIZE, PAGE_SIZE, 4 * kvh, HEAD_DIM), pt_p, w1cat)
                cmp_p = cmp_finish(ab_p, cmp_pe, cmp_w1, cmp_b1, cmp_w2, cmp_b2)
                cache4 = cache_kv.astype(F32).reshape(cache_kv.shape[0], PAGE_SIZE, -1, HEAD_DIM)
                ab_s = cmp_project(cache4, page_table, w1cat)
                cmp_s = cmp_finish(ab_s, cmp_pe, cmp_w1, cmp_b1, cmp_w2, cmp_b2)
            w_q = w_qg[bl][:, :hd_all].astype(BF16)
            n_gate = w_qg.shape[2] - hd_all
            w_g = jnp.pad(w_qg[bl][:, hd_all:], ((0, 0), (0, LANES - n_gate))).astype(BF16)
            w_o_b = w_o[bl].astype(BF16)

            def gate_layout(gt, bsz, t):
                gt = gt[:, :n_gate].reshape(bsz, t, kvh, 3 * gqa).transpose(0, 2, 1, 3)
                return jnp.pad(gt, ((0, 0), (0, 0), (0, 0), (0, LANES - 3 * gqa)))

            hp = rmsnorm(xp, attn_norm[layer], BF16)
            q_p = matmul(hp, w_q, out_dtype=BF16, tm=1024, tn=512)
            g_p = matmul(hp, w_g, act="sigmoid", tm=1024)
            o_p = nsa_prompt_attend(q_p, g_p, cmp_p, (kvr_p, 2 * kvh), (kvr_p, 3 * kvh), (kvw_p, 0), (kvw_p, kvh), bp, tp)
            xp = matmul(o_p, w_o_b, res=xp, tm=1024, tn=512)

            hs = rmsnorm(xs, attn_norm[layer], BF16)
            q_s = _pad_rows(matmul(hs, w_q, out_dtype=BF16).reshape(bs, ts, hd_all), ts_pad)
            g_s = gate_layout(matmul(hs, w_g, act="sigmoid"), bs, ts)
            g_s = jnp.pad(g_s, ((0, 0), (0, 0), (0, ts_pad - ts), (0, 0)))
            o_s = nsa_sample_attend(q_s, g_s, cmp_s, kvr_s.reshape(bs, ts, 4 * kvh, HEAD_DIM), kvw_s3,
                                    cache_kv.astype(F32), cache_win, page_table, ts)
            xs = matmul(o_s[:, :ts].reshape(bs * ts, hd_all), w_o_b, res=xs)

        w_in = ffn_w_in[layer].astype(BF16)
        w_down = ffn_w_down[layer].astype(BF16)
        f = w_down.shape[0]
        act, st = ffn_up(rmsnorm(xp, ffn_norm[layer], BF16), w_in, ffn_conv_w[layer], ffn_conv_b[layer],
                         jnp.zeros((bp, CONV_W - 1, f), F32), tp)
        conv_p.append(st)
        xp = matmul(act, w_down, res=xp, tm=512, tn=256)
        act, st = ffn_up(rmsnorm(xs, ffn_norm[layer], BF16), w_in, ffn_conv_w[layer], ffn_conv_b[layer],
                         state_ffn_conv[layer], ts)
        conv_s.append(st)
        xs = matmul(act, w_down, res=xs, tm=512, tn=256)

    y_p = rmsnorm(xp, final_norm, F32).reshape(bp, tp, d)
    y_s = rmsnorm(xs, final_norm, F32).reshape(bs, ts, d)
    return (y_p, y_s, jnp.stack(ssm_re_p), jnp.stack(ssm_im_p), jnp.stack(ssm_re_s), jnp.stack(ssm_im_s),
            jnp.stack(conv_p), jnp.stack(conv_s), outs["kv_rows_p"], outs["kv_rows_s"], outs["win_p"], outs["win_s"])


def kernel(x_prompt, x_sample, state_ssm_re, state_ssm_im, state_ffn_conv, cache_kv, cache_win, page_table, attn_norm, ffn_norm, final_norm, ssm_lam_re, ssm_lam_im, ssm_log_step, ssm_b_re, ssm_b_im, ssm_c_re, ssm_c_im, ssm_d, ssm_w_glu, ffn_w_in, ffn_conv_w, ffn_conv_b, ffn_w_down, kv_norm, w_kv, cmp_w1, cmp_b1, cmp_w2, cmp_b2, cmp_pe, w_qg, w_o):
    return _step(x_prompt, x_sample, state_ssm_re, state_ssm_im, state_ffn_conv, cache_kv, cache_win, page_table,
                 attn_norm, ffn_norm, final_norm, ssm_lam_re, ssm_lam_im, ssm_log_step, ssm_b_re, ssm_b_im,
                 ssm_c_re, ssm_c_im, ssm_d, ssm_w_glu, ffn_w_in, ffn_conv_w, ffn_conv_b, ffn_w_down,
                 kv_norm, w_kv, cmp_w1, cmp_b1, cmp_w2, cmp_b2, cmp_pe, w_qg, w_o)
```

```python
import functools
import math

import jax
import jax.numpy as jnp
from jax import lax
from jax.experimental import pallas as pl
from jax.experimental.pallas import tpu as pltpu

F32 = jnp.float32
BF16 = jnp.bfloat16
I32 = jnp.int32

RMS_EPS = 1e-6
LANES = 128
SUBLANES = 8
VMEM_LIMIT = 56 * 1024 * 1024

SSM_GROUP = 16
SSM_STATE = 64
CONV_W = 3
HEAD_DIM = 128
N_KV_HEADS = 4
CMP_STRIDE = 16
CMP_BLOCK = 32
SEL_BLOCK = 64
SEL_SHIFT = SEL_BLOCK.bit_length() - 1
N_SEL = 16
WINDOW = 512
Q_BLOCK = 128
PAGE_SIZE = 128
NEG_BIG = -1e30
M_FLOOR = 0.1 * NEG_BIG
LOG2_E = math.log2(math.e)
ONES_ROWS = 16

S5_SET_GROUPS = 4
S5_SETS = 8
S5_BLK_GROUPS = S5_SET_GROUPS * S5_SETS
S5_BLK_CH = S5_BLK_GROUPS * SSM_GROUP
S5_SET_LANES = S5_SET_GROUPS * SSM_STATE


def _cparams(sem, vmem=VMEM_LIMIT):
    return pltpu.CompilerParams(dimension_semantics=sem, vmem_limit_bytes=vmem)


def _rmsnorm_body(x_ref, g_ref, o_ref):
    x = x_ref[...].astype(F32)
    y = x * lax.rsqrt(jnp.mean(x * x, axis=-1, keepdims=True) + RMS_EPS)
    o_ref[...] = (y * g_ref[...]).astype(o_ref.dtype)


def rmsnorm(x, g, out_dtype):
    m, d = x.shape
    tm = math.gcd(m, 256)
    return pl.pallas_call(
        _rmsnorm_body,
        grid=(m // tm,),
        in_specs=[pl.BlockSpec((tm, d), lambda i: (i, 0)), pl.BlockSpec((1, d), lambda i: (0, 0))],
        out_specs=pl.BlockSpec((tm, d), lambda i: (i, 0)),
        out_shape=jax.ShapeDtypeStruct((m, d), out_dtype),
        compiler_params=_cparams(("parallel",)),
        name="rmsnorm",
    )(x, g.reshape(1, d).astype(F32))


def _mm_body(*refs, has_res, act):
    if has_res:
        x_ref, w_ref, res_ref, o_ref = refs
    else:
        x_ref, w_ref, o_ref = refs
    y = jnp.dot(x_ref[...], w_ref[...], preferred_element_type=F32)
    if act == "sigmoid":
        y = jax.nn.sigmoid(y)
    if has_res:
        y = res_ref[...] + y
    o_ref[...] = y.astype(o_ref.dtype)


def matmul(x, w, res=None, out_dtype=F32, act=None, tm=512, tn=256):
    m, k = x.shape
    n = w.shape[1]
    tm = math.gcd(m, tm)
    tn = math.gcd(n, tn)
    in_specs = [pl.BlockSpec((tm, k), lambda i, j: (i, 0)), pl.BlockSpec((k, tn), lambda i, j: (0, j))]
    args = [x, w]
    if res is not None:
        in_specs.append(pl.BlockSpec((tm, tn), lambda i, j: (i, j)))
        args.append(res)
    return pl.pallas_call(
        functools.partial(_mm_body, has_res=res is not None, act=act),
        grid=(m // tm, n // tn),
        in_specs=in_specs,
        out_specs=pl.BlockSpec((tm, tn), lambda i, j: (i, j)),
        out_shape=jax.ShapeDtypeStruct((m, n), out_dtype),
        compiler_params=_cparams(("parallel", "arbitrary")),
        name="matmul",
    )(*args)


def _mm_split_body(x_ref, w_ref, o1_ref, o2_ref, *, n1_tiles):
    j = pl.program_id(1)
    y = jnp.dot(x_ref[...], w_ref[...], preferred_element_type=F32)

    @pl.when(j < n1_tiles)
    def _():
        o1_ref[...] = y

    @pl.when(j >= n1_tiles)
    def _():
        o2_ref[...] = y


def matmul_split(x, w, n1, tm=1024, tn=512):
    m, k = x.shape
    n = w.shape[1]
    tm = math.gcd(m, tm)
    tn = math.gcd(math.gcd(n1, n - n1), tn)
    t1 = n1 // tn
    return pl.pallas_call(
        functools.partial(_mm_split_body, n1_tiles=t1),
        grid=(m // tm, n // tn),
        in_specs=[pl.BlockSpec((tm, k), lambda i, j: (i, 0)), pl.BlockSpec((k, tn), lambda i, j: (0, j))],
        out_specs=[pl.BlockSpec((tm, tn), lambda i, j: (i, jnp.minimum(j, t1 - 1))),
                   pl.BlockSpec((tm, tn), lambda i, j: (i, jnp.maximum(j - t1, 0)))],
        out_shape=[jax.ShapeDtypeStruct((m, n1), F32), jax.ShapeDtypeStruct((m, n - n1), F32)],
        compiler_params=_cparams(("parallel", "arbitrary")),
        name="matmul_split",
    )(x, w)


def _glu_body(x_ref, wa_ref, wb_ref, res_ref, o_ref):
    x = x_ref[...]
    a = jnp.dot(x, wa_ref[...], preferred_element_type=F32)
    b = jnp.dot(x, wb_ref[...], preferred_element_type=F32)
    o_ref[...] = res_ref[...] + a * jax.nn.sigmoid(b)


def glu_matmul(x, w, res, tm=1024, tn=256):
    m, k = x.shape
    n = w.shape[1] // 2
    tm = math.gcd(m, tm)
    tn = math.gcd(n, tn)
    nt = n // tn
    return pl.pallas_call(
        _glu_body,
        grid=(m // tm, nt),
        in_specs=[pl.BlockSpec((tm, k), lambda i, j: (i, 0)),
                  pl.BlockSpec((k, tn), lambda i, j: (0, j)),
                  pl.BlockSpec((k, tn), lambda i, j: (0, j + nt)),
                  pl.BlockSpec((tm, tn), lambda i, j: (i, j))],
        out_specs=pl.BlockSpec((tm, tn), lambda i, j: (i, j)),
        out_shape=jax.ShapeDtypeStruct((m, n), F32),
        compiler_params=_cparams(("parallel", "arbitrary")),
        name="glu_matmul",
    )(x, w, w, res)


def _conv_gate(g, g1, g2, cw_ref, cb_ref, val):
    gc = cb_ref[...] + cw_ref[0:1, :] * g2 + cw_ref[1:2, :] * g1 + cw_ref[2:3, :] * g
    return (gc * jax.nn.sigmoid(gc) * val).astype(BF16)


def _ffn_up_body(x_ref, xs_ref, wv_ref, wg_ref, cw_ref, cb_ref, p1_ref, p2_ref, cwp_ref, cbp_ref, init_ref,
                 act_ref, st_ref, acts_ref, gs_ref, wv_s, wg_s, val_a, g_a, val_b, g_b,
                 *, n_m, tiles_per_seq, short_len, chunk):
    s = pl.program_id(0)
    i = lax.rem(s, n_m)

    @pl.when(s == 0)
    def _():
        val_b[...] = jnp.zeros(val_b.shape, F32)
        g_b[...] = jnp.zeros(g_b.shape, F32)

    @pl.when(i == 0)
    def _():
        wv_s[...] = wv_ref[...].astype(BF16)
        wg_s[...] = wg_ref[...].astype(BF16)
        xs = xs_ref[...]
        val = jnp.dot(xs, wv_s[...], preferred_element_type=F32)
        g = jnp.dot(xs, wg_s[...], preferred_element_type=F32)
        tmod = lax.rem(lax.broadcasted_iota(I32, g.shape, 0), short_len)
        g1 = jnp.where(tmod >= 1, pltpu.roll(g, 1, axis=0), p1_ref[...])
        g2 = jnp.where(tmod >= 2, pltpu.roll(g, 2, axis=0), p2_ref[...])
        acts_ref[...] = _conv_gate(g, g1, g2, cw_ref, cb_ref, val)
        gs_ref[...] = g

    def step(val_w, g_w, val_r, g_r):
        tm = x_ref.shape[0]
        for r in range(0, tm, chunk):
            x = x_ref[r:r + chunk, :]
            val_w[r:r + chunk, :] = jnp.dot(x, wv_s[...], preferred_element_type=F32)
            g_w[SUBLANES + r:SUBLANES + r + chunk, :] = jnp.dot(x, wg_s[...], preferred_element_type=F32)
            gext = g_r[r:r + chunk + SUBLANES, :]
            g1 = pltpu.roll(gext, 1, axis=0)[SUBLANES:]
            g2 = pltpu.roll(gext, 2, axis=0)[SUBLANES:]
            act_ref[r:r + chunk, :] = _conv_gate(gext[SUBLANES:], g1, g2, cwp_ref, cbp_ref, val_r[r:r + chunk, :])
        tail = g_r[tm:, :]
        st_ref[0] = tail
        g_w[:SUBLANES, :] = jnp.where(lax.rem(i, tiles_per_seq) == 0, init_ref[0], tail)

    parity = lax.rem(s, 2)

    @pl.when(parity == 0)
    def _():
        step(val_a, g_a, val_b, g_b)

    @pl.when(parity == 1)
    def _():
        step(val_b, g_b, val_a, g_a)


def ffn_up(h, hs, w_in, conv_w, conv_b, buf, buf_s, seq_len, seq_short, tm=1024, tn=256):
    m, k = h.shape
    ms = hs.shape[0]
    f = w_in.shape[1] // 2
    bsz = m // seq_len
    bs = ms // seq_short
    assert seq_len % SUBLANES == 0 and seq_len >= 2 * SUBLANES and seq_short >= 2
    tn = math.gcd(f, tn)
    n_n = f // tn
    tm = math.gcd(seq_len, tm)
    tps = seq_len // tm
    n_m = m // tm
    cw = jnp.zeros((SUBLANES, f), F32).at[:CONV_W].set(conv_w.astype(F32))
    cb = conv_b.reshape(1, f).astype(F32)
    init = jnp.zeros((bsz, SUBLANES, f), F32).at[:, SUBLANES - 2:].set(buf.astype(F32))
    buf_s = buf_s.astype(F32)
    pos = jnp.arange(ms) % seq_short
    b0 = jnp.repeat(buf_s[:, 0], seq_short, axis=0)
    b1 = jnp.repeat(buf_s[:, 1], seq_short, axis=0)
    p1 = jnp.where((pos == 0)[:, None], b1, 0.0)
    p2 = jnp.where((pos == 0)[:, None], b0, jnp.where((pos == 1)[:, None], b1, 0.0))

    def col(s):
        return jnp.minimum(s // n_m, n_n - 1)

    def prev(s):
        sp = jnp.maximum(s - 1, 0)
        return sp % n_m, sp // n_m

    act, st, act_s, g_s = pl.pallas_call(
        functools.partial(_ffn_up_body, n_m=n_m, tiles_per_seq=tps, short_len=seq_short, chunk=math.gcd(tm, 128)),
        grid=(n_n * n_m + 1,),
        in_specs=[pl.BlockSpec((tm, k), lambda s: (s % n_m, 0)),
                  pl.BlockSpec((ms, k), lambda s: (0, 0)),
                  pl.BlockSpec((k, tn), lambda s: (0, col(s))),
                  pl.BlockSpec((k, tn), lambda s: (0, col(s) + n_n)),
                  pl.BlockSpec((SUBLANES, tn), lambda s: (0, col(s))),
                  pl.BlockSpec((1, tn), lambda s: (0, col(s))),
                  pl.BlockSpec((ms, tn), lambda s: (0, col(s))),
                  pl.BlockSpec((ms, tn), lambda s: (0, col(s))),
                  pl.BlockSpec((SUBLANES, tn), lambda s: (0, prev(s)[1])),
                  pl.BlockSpec((1, tn), lambda s: (0, prev(s)[1])),
                  pl.BlockSpec((1, SUBLANES, tn), lambda s: ((s % n_m) // tps, 0, col(s)))],
        out_specs=[pl.BlockSpec((tm, tn), lambda s: prev(s)),
                   pl.BlockSpec((1, SUBLANES, tn), lambda s: (prev(s)[0], 0, prev(s)[1])),
                   pl.BlockSpec((ms, tn), lambda s: (0, col(s))),
                   pl.BlockSpec((ms, tn), lambda s: (0, col(s)))],
        out_shape=[jax.ShapeDtypeStruct((m, f), BF16), jax.ShapeDtypeStruct((n_m, SUBLANES, f), F32),
                   jax.ShapeDtypeStruct((ms, f), BF16), jax.ShapeDtypeStruct((ms, f), F32)],
        scratch_shapes=[pltpu.VMEM((k, tn), BF16), pltpu.VMEM((k, tn), BF16)]
                       + [pltpu.VMEM((tm, tn), F32), pltpu.VMEM((tm + SUBLANES, tn), F32)] * 2,
        compiler_params=_cparams(("arbitrary",)),
        name="ffn_up",
    )(h, hs, w_in, w_in, cw, cb, p1, p2, cw, cb, init)
    return (act, st[tps - 1::tps, SUBLANES - 2:], act_s, g_s.reshape(bs, seq_short, f)[:, seq_short - 2:])


def _s5_body(u_ref, wb_ref, wc_ref, lre_ref, lim_ref, d_ref, h0re_ref, h0im_ref,
             z_ref, hre_ref, him_ref, sr0, sr1, si0, si1, hst_re, hst_im, *, tc_len, last_step):
    tci = pl.program_id(2)
    u = u_ref[0]
    halves = S5_SET_LANES // LANES
    sre = (sr0, sr1)
    sim = (si0, si1)

    def set_rows(j):
        return pl.ds(j, tc_len, stride=S5_SETS)

    for ti in range(S5_BLK_CH // LANES):
        ut = u[:, ti * LANES:(ti + 1) * LANES]
        hi = ut.astype(BF16)
        lo = (ut - hi.astype(F32)).astype(BF16)
        lhs = jnp.concatenate([hi, lo], axis=1)
        for jj in range(2):
            j = ti * 2 + jj
            bre = jnp.dot(lhs, wb_ref[0, j, 0], preferred_element_type=F32)
            bim = jnp.dot(lhs, wb_ref[0, j, 1], preferred_element_type=F32)
            for c in range(halves):
                sre[c][set_rows(j), :] = bre[:, c * LANES:(c + 1) * LANES]
                sim[c][set_rows(j), :] = bim[:, c * LANES:(c + 1) * LANES]

    @pl.when(tci == 0)
    def _():
        hst_re[...] = h0re_ref[0, 0]
        hst_im[...] = h0im_ref[0, 0]

    lam_r = lre_ref[0]
    lam_i = lim_ref[0]
    lr = [lam_r[:, c * LANES:(c + 1) * LANES] for c in range(halves)]
    li = [lam_i[:, c * LANES:(c + 1) * LANES] for c in range(halves)]

    def step(t, carry):
        r0 = pl.multiple_of(t * S5_SETS, S5_SETS)
        out = []
        for c in range(halves):
            hr, hi_ = carry[2 * c], carry[2 * c + 1]
            nr = lr[c] * hr - li[c] * hi_ + sre[c][pl.ds(r0, S5_SETS), :]
            ni = lr[c] * hi_ + li[c] * hr + sim[c][pl.ds(r0, S5_SETS), :]
            sre[c][pl.ds(r0, S5_SETS), :] = nr
            sim[c][pl.ds(r0, S5_SETS), :] = ni
            out += [nr, ni]
        return tuple(out)

    h_in = hst_re[...]
    g_in = hst_im[...]
    init = []
    for c in range(halves):
        init += [h_in[:, c * LANES:(c + 1) * LANES], g_in[:, c * LANES:(c + 1) * LANES]]
    fin = lax.fori_loop(0, tc_len, step, tuple(init), unroll=8)
    for c in range(halves):
        hst_re[:, c * LANES:(c + 1) * LANES] = fin[2 * c]
        hst_im[:, c * LANES:(c + 1) * LANES] = fin[2 * c + 1]

    @pl.when(tci == last_step // tc_len)
    def _():
        r0 = (last_step % tc_len) * S5_SETS
        for c in range(halves):
            hre_ref[0, 0, :, c * LANES:(c + 1) * LANES] = sre[c][pl.ds(r0, S5_SETS), :]
            him_ref[0, 0, :, c * LANES:(c + 1) * LANES] = sim[c][pl.ds(r0, S5_SETS), :]

    for c in range(S5_BLK_CH // LANES):
        parts = []
        for s in range(2):
            j = 2 * c + s
            parts += [buf[set_rows(j), :].astype(BF16) for buf in sre + sim]
        lhs = jnp.concatenate(parts, axis=1)
        y = jnp.dot(lhs, wc_ref[0, c], preferred_element_type=F32)
        y = y + d_ref[:, c * LANES:(c + 1) * LANES] * u[:, c * LANES:(c + 1) * LANES]
        z_ref[0, :, c * LANES:(c + 1) * LANES] = jax.nn.gelu(y).astype(BF16)


def _s5_weights(lam_re, lam_im, log_step, b_re, b_im, c_re, c_im):
    g, p = lam_re.shape
    n = SSM_GROUP
    nblk = g // S5_BLK_GROUPS
    lam = lax.complex(lam_re.astype(F32), lam_im.astype(F32))
    dt = jnp.exp(log_step.astype(F32))[:, None]
    lam_bar = jnp.exp(lam * dt)
    b_bar = ((lam_bar - 1.0) / lam)[..., None] * lax.complex(b_re.astype(F32), b_im.astype(F32))
    eye_q = jnp.eye(S5_SET_GROUPS, dtype=F32)
    eye_h = jnp.eye(2, dtype=F32)

    def b_operand(bm):
        t = bm.reshape(nblk, S5_SETS // 2, 2, S5_SET_GROUPS, p, n)
        w = jnp.einsum("bthqpn,hk,qr->bthkrnqp", t, eye_h, eye_q)
        w = w.reshape(nblk, S5_SETS, LANES, S5_SET_LANES)
        return w

    wb = jnp.stack([b_operand(jnp.real(b_bar)), b_operand(jnp.imag(b_bar))], axis=2)
    wb = wb.astype(BF16)
    wb = jnp.concatenate([wb, wb], axis=3)

    def c_operand(cm):
        t = cm.reshape(nblk, S5_SETS // 2, 2, S5_SET_GROUPS, n, p)
        return jnp.einsum("bcsqnp,st,qr->bcsqptrn", t, eye_h, eye_q)

    cre = c_operand(c_re.astype(F32))
    cim = c_operand(-c_im.astype(F32))
    wc = jnp.stack([cre, cim], axis=3).reshape(nblk, S5_SETS // 2, 4 * S5_SET_LANES, LANES).astype(BF16)
    lre = jnp.real(lam_bar).reshape(nblk, S5_SETS, S5_SET_LANES)
    lim = jnp.imag(lam_bar).reshape(nblk, S5_SETS, S5_SET_LANES)
    return wb, wc, lre, lim


def s5_scan(u, h0_re, h0_im, weights, d_skip, seq_valid, tc_len):
    wb, wc, lre, lim = weights
    bsz, lp, d = u.shape
    nblk = d // S5_BLK_CH
    n_tc = lp // tc_len
    h0_re = h0_re.astype(F32).reshape(bsz, nblk, S5_SETS, S5_SET_LANES)
    h0_im = h0_im.astype(F32).reshape(bsz, nblk, S5_SETS, S5_SET_LANES)
    state_spec = pl.BlockSpec((1, 1, S5_SETS, S5_SET_LANES), lambda k, b, t: (b, k, 0, 0))
    z, hre, him = pl.pallas_call(
        functools.partial(_s5_body, tc_len=tc_len, last_step=seq_valid - 1),
        grid=(nblk, bsz, n_tc),
        in_specs=[pl.BlockSpec((1, tc_len, S5_BLK_CH), lambda k, b, t: (b, t, k)),
                  pl.BlockSpec((1, S5_SETS, 2, 2 * LANES, S5_SET_LANES), lambda k, b, t: (k, 0, 0, 0, 0)),
                  pl.BlockSpec((1, S5_SETS // 2, 4 * S5_SET_LANES, LANES), lambda k, b, t: (k, 0, 0, 0)),
                  pl.BlockSpec((1, S5_SETS, S5_SET_LANES), lambda k, b, t: (k, 0, 0)),
                  pl.BlockSpec((1, S5_SETS, S5_SET_LANES), lambda k, b, t: (k, 0, 0)),
                  pl.BlockSpec((1, S5_BLK_CH), lambda k, b, t: (0, k)),
                  state_spec, state_spec],
        out_specs=[pl.BlockSpec((1, tc_len, S5_BLK_CH), lambda k, b, t: (b, t, k)), state_spec, state_spec],
        out_shape=[jax.ShapeDtypeStruct((bsz, lp, d), BF16),
                   jax.ShapeDtypeStruct((bsz, nblk, S5_SETS, S5_SET_LANES), F32),
                   jax.ShapeDtypeStruct((bsz, nblk, S5_SETS, S5_SET_LANES), F32)],
        scratch_shapes=[pltpu.VMEM((tc_len * S5_SETS, LANES), F32)] * (2 * S5_SET_LANES // LANES) + [
                        pltpu.VMEM((S5_SETS, S5_SET_LANES), F32),
                        pltpu.VMEM((S5_SETS, S5_SET_LANES), F32)],
        compiler_params=_cparams(("arbitrary", "arbitrary", "arbitrary")),
        name="s5_scan",
    )(u, wb, wc, lre, lim, d_skip.reshape(1, d).astype(F32), h0_re, h0_im)
    g = d // SSM_GROUP
    return z, hre.reshape(bsz, g, SSM_STATE), him.reshape(bsz, g, SSM_STATE)


def _cmp_proj_body(pt_ref, x_hbm, w_ref, o_ref, stage, sems, *, pages_per_step):
    n_heads2 = 2 * N_KV_HEADS
    chunks = PAGE_SIZE // CMP_STRIDE
    step = pl.program_id(0) * pl.num_programs(1) + pl.program_id(1)
    n_steps = pl.num_programs(0) * pl.num_programs(1)
    slot = lax.rem(step, 2)

    def page_copy(st, sl, p, c):
        page = pt_ref[st * pages_per_step + p]
        return pltpu.make_async_copy(x_hbm.at[page, :, c, :], stage.at[sl, p, c], sems.at[sl])

    def start_step(st, sl):
        for p in range(pages_per_step):
            for c in range(n_heads2):
                page_copy(st, sl, p, c).start()

    @pl.when(step == 0)
    def _():
        start_step(step, slot)

    @pl.when(step + 1 < n_steps)
    def _():
        start_step(step + 1, 1 - slot)

    for p in range(pages_per_step):
        for c in range(n_heads2):
            page_copy(step, slot, p, c).wait()

    def chunk_rows(c, j):
        return jnp.concatenate([stage[slot, p, c, pl.ds(j, chunks, stride=CMP_STRIDE), :]
                                for p in range(pages_per_step)], axis=0)

    for c in range(n_heads2):
        s, h = divmod(c, N_KV_HEADS)
        acc = None
        for jp in range(CMP_STRIDE // 2):
            lhs = jnp.concatenate([chunk_rows(c, 2 * jp), chunk_rows(c, 2 * jp + 1)], axis=1).astype(BF16)
            t = jnp.dot(lhs, w_ref[s, jp], preferred_element_type=F32)
            acc = t if acc is None else acc + t
        o_ref[0, s, h] = acc


def cmp_project(rows, page_table, w1cat, pages_per_step=16):
    bsz, npg = page_table.shape
    pps = math.gcd(npg, pages_per_step)
    chunks = PAGE_SIZE // CMP_STRIDE
    hid2 = w1cat.shape[-1]
    grid_spec = pltpu.PrefetchScalarGridSpec(
        num_scalar_prefetch=1,
        grid=(bsz, npg // pps),
        in_specs=[pl.BlockSpec(memory_space=pl.ANY),
                  pl.BlockSpec(w1cat.shape, lambda b, g, pt: (0, 0, 0, 0))],
        out_specs=pl.BlockSpec((1, 2, N_KV_HEADS, pps * chunks, hid2), lambda b, g, pt: (b, 0, 0, g, 0)),
        scratch_shapes=[pltpu.VMEM((2, pps, 2 * N_KV_HEADS, PAGE_SIZE, HEAD_DIM), F32),
                        pltpu.SemaphoreType.DMA((2,))],
    )
    return pl.pallas_call(
        functools.partial(_cmp_proj_body, pages_per_step=pps),
        grid_spec=grid_spec,
        out_shape=jax.ShapeDtypeStruct((bsz, 2, N_KV_HEADS, npg * chunks, hid2), F32),
        compiler_params=_cparams(("arbitrary", "arbitrary")),
        name="cmp_project",
    )(page_table.reshape(-1).astype(I32), rows, w1cat)


def _cmp_finish_body(ab_ref, pe_ref, w1_ref, b1_ref, w2_ref, b2_ref, o_ref):
    ab = ab_ref[0, 0, 0]
    n16 = ab.shape[0]
    hid = ab.shape[1] // 2
    a = ab[:, :hid]
    bnext = pltpu.roll(ab[:, hid:], n16 - 1, axis=0)
    c = jnp.dot(pe_ref[0], w1_ref[0], preferred_element_type=F32)[0:1, :] + b1_ref[0]
    pre = a + bnext + c
    y = jnp.dot(jax.nn.gelu(pre).astype(BF16), w2_ref[0], preferred_element_type=F32) + b2_ref[0]
    row = lax.broadcasted_iota(I32, y.shape, 0)
    o_ref[0, 0, 0] = jnp.where(row < n16 - 1, y, 0.0).astype(o_ref.dtype)


def cmp_finish(ab, pe, w1, b1, w2, b2):
    bsz, _, kvh, n16, hid2 = ab.shape
    hid = hid2 // 2
    pe_rows = jnp.broadcast_to(pe.reshape(2, 1, -1), (2, SUBLANES, pe.shape[1] * pe.shape[2])).astype(BF16)
    return pl.pallas_call(
        _cmp_finish_body,
        grid=(2, bsz, kvh),
        in_specs=[pl.BlockSpec((1, 1, 1, n16, hid2), lambda s, b, h: (b, s, h, 0, 0)),
                  pl.BlockSpec((1, SUBLANES, pe_rows.shape[2]), lambda s, b, h: (s, 0, 0)),
                  pl.BlockSpec((1,) + w1.shape[1:], lambda s, b, h: (s, 0, 0)),
                  pl.BlockSpec((1, 1, hid), lambda s, b, h: (s, 0, 0)),
                  pl.BlockSpec((1, hid, HEAD_DIM), lambda s, b, h: (s, 0, 0)),
                  pl.BlockSpec((1, 1, HEAD_DIM), lambda s, b, h: (s, 0, 0))],
        out_specs=pl.BlockSpec((1, 1, 1, n16, HEAD_DIM), lambda s, b, h: (b, s, h, 0, 0)),
        out_shape=jax.ShapeDtypeStruct((bsz, 2, kvh, n16, HEAD_DIM), BF16),
        compiler_params=_cparams(("arbitrary", "arbitrary", "arbitrary")),
        name="cmp_finish",
    )(ab, pe_rows, w1.astype(BF16), b1.reshape(2, 1, hid).astype(F32), w2.astype(BF16),
      b2.reshape(2, 1, HEAD_DIM).astype(F32))


def _masked_softmax(s, mask):
    sm = jnp.where(mask, s, NEG_BIG)
    m = jnp.max(sm, axis=-1, keepdims=True)
    m = jnp.where(m > 0.5 * NEG_BIG, m, 0.0)
    e = jnp.where(mask, jnp.exp(sm - m), 0.0)
    den = jnp.sum(e, axis=-1, keepdims=True)
    return e / jnp.where(den > 0, den, 1.0)


def _dot_nt(a, b):
    return lax.dot_general(a, b, (((1,), (1,)), ((), ())), preferred_element_type=F32)


def _overlap(n_c, n_j, c_axis):
    shape = (n_c, n_j) if c_axis == 0 else (n_j, n_c)
    c = lax.broadcasted_iota(I32, shape, c_axis) * CMP_STRIDE
    j = lax.broadcasted_iota(I32, shape, 1 - c_axis) * SEL_BLOCK
    return ((c < j + SEL_BLOCK) & (c + CMP_BLOCK > j)).astype(F32)


def _block_scores(imp, q_pos, n_blk):
    blk = lax.broadcasted_iota(I32, imp.shape, 1)
    q_blk = jnp.right_shift(q_pos, SEL_SHIFT)
    forced = (blk == 0) | (blk == q_blk) | (blk == q_blk - 1)
    future = blk * SEL_BLOCK > q_pos
    v = jnp.where(future, NEG_BIG, jnp.where(forced, -NEG_BIG, imp))
    return jnp.where(blk < n_blk, v, 2.0 * NEG_BIG)


def _rank_desc(v, n):
    lane = lax.broadcasted_iota(I32, v.shape, 1)
    rank = jnp.zeros(v.shape, F32)
    for i in range(n):
        vi = v[:, i:i + 1]
        before = (vi > v) | ((vi == v) & (lane > i))
        rank = rank + jnp.where(before, 1.0, 0.0)
    return rank


def _nsa_prompt_body(qt_ref, gt_ref, ck_ref, cvt_ref, ks_ref, vst_ref, kw_ref, vwt_ref, o_ref, m_ref, acc_ref,
                     *, n_cmp, n_blk, gqa, sel_tile, win_keys):
    qb = pl.program_id(2)
    nq = Q_BLOCK
    gq = gqa * nq
    c2 = (HEAD_DIM ** -0.5) * LOG2_E
    qt = qt_ref[0, 0, 0]
    q_pos = qb * nq + lax.broadcasted_iota(I32, (1, nq), 1)

    def per_head(x):
        return jnp.concatenate([x] * gqa, axis=1)

    def normalise(acc):
        den = acc[HEAD_DIM:HEAD_DIM + 1, :]
        return acc[:HEAD_DIM, :] * (1.0 / jnp.where(den > 0, den, 1.0))

    ck = ck_ref[0, 0, 0]
    n16 = ck.shape[0]
    s = jnp.dot(ck, qt, preferred_element_type=F32)
    cidx = lax.broadcasted_iota(I32, (n16, 1), 0)
    cbias = jnp.where((cidx * CMP_STRIDE + (CMP_BLOCK - 1) <= q_pos) & (cidx < n_cmp), 0.0, NEG_BIG)
    sb = s + per_head(cbias)
    m = jnp.maximum(jnp.max(sb, axis=0, keepdims=True), M_FLOOR)
    e = jnp.exp2((sb - m) * c2)
    den = jnp.sum(e, axis=0, keepdims=True)
    p = e * jnp.where(den > 0, 1.0 / den, 0.0)
    o_cmp = jnp.dot(cvt_ref[0, 0], p.astype(BF16), preferred_element_type=F32)

    psum = p[:, 0:nq]
    for g in range(1, gqa):
        psum = psum + p[:, g * nq:(g + 1) * nq]
    imp = jnp.dot(_overlap(n16, LANES, 1), psum, preferred_element_type=F32, precision=lax.Precision.HIGHEST)
    blk = lax.broadcasted_iota(I32, (LANES, 1), 0)
    q_blk = jnp.right_shift(q_pos, SEL_SHIFT)
    forced = (blk == 0) | (blk == q_blk) | (blk == q_blk - 1)
    v = jnp.where(blk * SEL_BLOCK > q_pos, NEG_BIG, jnp.where(forced, -NEG_BIG, imp))
    v = jnp.where(blk < n_blk, v, 2.0 * NEG_BIG)
    rank = jnp.zeros(v.shape, F32)
    for i in range(n_blk):
        vi = v[i:i + 1, :]
        rank = rank + jnp.where((vi > v) | ((vi == v) & (blk > i)), 1.0, 0.0)
    sel_bias = jnp.where((rank < min(N_SEL, n_blk)) & (blk < n_blk), 0.0, NEG_BIG).astype(BF16)

    m_ref[...] = jnp.full(m_ref.shape, M_FLOOR, F32)
    acc_ref[...] = jnp.zeros(acc_ref.shape, F32)
    tiles = sel_tile // LANES

    def sel_step(kt, _):
        k0 = pl.multiple_of(kt * sel_tile, sel_tile)
        s = jnp.dot(ks_ref[0, pl.ds(k0, sel_tile), :].astype(BF16), qt, preferred_element_type=F32)
        key = k0 + lax.broadcasted_iota(I32, (sel_tile, 1), 0)
        expand = jnp.where(jnp.right_shift(key, SEL_SHIFT) == lax.broadcasted_iota(I32, (sel_tile, LANES), 1), 1.0, 0.0)
        bias = jnp.dot(expand.astype(BF16), sel_bias, preferred_element_type=F32)
        bias = jnp.where(key <= q_pos, bias, NEG_BIG)
        sb = s + per_head(bias)
        m_old = m_ref[0:1, :]
        m_new = jnp.maximum(m_old, jnp.max(sb, axis=0, keepdims=True))
        alpha = jnp.exp2((m_old - m_new) * c2)
        e = jnp.exp2((sb - m_new) * c2).astype(BF16)
        vt = jnp.concatenate([vst_ref[0, 0, kt * tiles + i] for i in range(tiles)], axis=1)
        acc_ref[...] = alpha * acc_ref[...] + jnp.dot(vt, e, preferred_element_type=F32)
        m_ref[...] = jnp.broadcast_to(m_new, m_ref.shape)
        return 0

    lax.fori_loop(0, (qb * nq + nq + sel_tile - 1) // sel_tile, sel_step, 0)
    o_sel = normalise(acc_ref[...])

    w_tile = jnp.maximum(qb + 1 - win_keys // nq, 0)
    w0 = pl.multiple_of(w_tile * nq, nq)
    s = jnp.dot(kw_ref[0, pl.ds(w0, win_keys), :].astype(BF16), qt, preferred_element_type=F32)
    key = w0 + lax.broadcasted_iota(I32, (win_keys, 1), 0)
    wbias = jnp.where((key <= q_pos) & (key > q_pos - WINDOW), 0.0, NEG_BIG)
    sb = s + per_head(wbias)
    m = jnp.maximum(jnp.max(sb, axis=0, keepdims=True), M_FLOOR)
    e = jnp.exp2((sb - m) * c2).astype(BF16)
    vt = jnp.concatenate([vwt_ref[0, 0, w_tile + i] for i in range(win_keys // LANES)], axis=1)
    o_win = normalise(jnp.dot(vt, e, preferred_element_type=F32))

    gt = gt_ref[0, 0, 0]
    for g in range(gqa):
        cols = slice(g * nq, (g + 1) * nq)
        o = (gt[3 * g:3 * g + 1, :] * o_cmp[:, cols] + gt[3 * g + 1:3 * g + 2, :] * o_sel[:, cols]
             + gt[3 * g + 2:3 * g + 3, :] * o_win[:, cols])
        o_ref[0, 0, 0, :, cols] = o.astype(o_ref.dtype)


def _transposed_values(rows, bsz, t):
    v = rows.reshape(bsz, t // LANES, LANES, N_KV_HEADS, HEAD_DIM).transpose(0, 3, 1, 4, 2).astype(BF16)
    ones = jnp.ones(v.shape[:3] + (ONES_ROWS, LANES), BF16)
    return jnp.concatenate([v, ones], axis=3)


def nsa_prompt_attend(q, gates, cmp, k_sel, v_sel, k_win, v_win, bsz, t, sel_tile=512):
    hd_all = q.shape[1]
    kvh = N_KV_HEADS
    gqa = hd_all // HEAD_DIM // kvh
    n16 = cmp.shape[3]
    n_blk = -(-t // SEL_BLOCK)
    nqb = t // Q_BLOCK
    gq = gqa * Q_BLOCK
    vrows = HEAD_DIM + ONES_ROWS
    assert t % Q_BLOCK == 0 and n_blk <= LANES and n16 * CMP_STRIDE == t
    sel_tile = math.gcd(t, sel_tile)
    win_keys = min(WINDOW + Q_BLOCK, t)
    qt = q.reshape(bsz, nqb, Q_BLOCK, kvh, gqa, HEAD_DIM).transpose(0, 3, 1, 5, 4, 2).reshape(bsz, kvh, nqb, HEAD_DIM, gq)
    gt = gates[:, :3 * kvh * gqa].reshape(bsz, nqb, Q_BLOCK, kvh, 3 * gqa).transpose(0, 3, 1, 4, 2)
    gt = jnp.pad(gt, ((0, 0), (0, 0), (0, 0), (0, -3 * gqa % SUBLANES), (0, 0)))
    cvt = cmp[:, 1].transpose(0, 1, 3, 2)
    (ks_arr, ks_col), (vs_arr, vs_col), (kw_arr, kw_col), (vw_arr, vw_col) = k_sel, v_sel, k_win, v_win
    width = kvh * HEAD_DIM
    vst = _transposed_values(vs_arr[:, vs_col * HEAD_DIM:vs_col * HEAD_DIM + width], bsz, t)
    vwt = _transposed_values(vw_arr[:, vw_col * HEAD_DIM:vw_col * HEAD_DIM + width], bsz, t)
    k_spec = lambda col: pl.BlockSpec((1, t, HEAD_DIM), lambda b, h, i, col=col: (b, 0, col + h))
    vt_spec = pl.BlockSpec((1, 1, t // LANES, vrows, LANES), lambda b, h, i: (b, h, 0, 0, 0))
    ot = pl.pallas_call(
        functools.partial(_nsa_prompt_body, n_cmp=n16 - 1, n_blk=n_blk, gqa=gqa, sel_tile=sel_tile, win_keys=win_keys),
        grid=(bsz, kvh, nqb),
        in_specs=[pl.BlockSpec((1, 1, 1, HEAD_DIM, gq), lambda b, h, i: (b, h, i, 0, 0)),
                  pl.BlockSpec((1, 1, 1, gt.shape[3], Q_BLOCK), lambda b, h, i: (b, h, i, 0, 0)),
                  pl.BlockSpec((1, 1, 1, n16, HEAD_DIM), lambda b, h, i: (b, 0, h, 0, 0)),
                  pl.BlockSpec((1, 1, HEAD_DIM, n16), lambda b, h, i: (b, h, 0, 0)),
                  k_spec(ks_col), vt_spec, k_spec(kw_col), vt_spec],
        out_specs=pl.BlockSpec((1, 1, 1, HEAD_DIM, gq), lambda b, h, i: (b, h, i, 0, 0)),
        out_shape=jax.ShapeDtypeStruct((bsz, kvh, nqb, HEAD_DIM, gq), BF16),
        scratch_shapes=[pltpu.VMEM((SUBLANES, gq), F32), pltpu.VMEM((vrows, gq), F32)],
        compiler_params=_cparams(("parallel", "parallel", "arbitrary")),
        name="nsa_prompt",
    )(qt, gt, cmp, cvt, ks_arr.reshape(bsz, t, -1), vst, kw_arr.reshape(bsz, t, -1), vwt)
    return ot.reshape(bsz, kvh, nqb, HEAD_DIM, gqa, Q_BLOCK).transpose(0, 2, 5, 1, 4, 3).reshape(bsz * t, hd_all)


def _nsa_sample_select_body(q_ref, ck_ref, cv_ref, ocmp_ref, idx_ref, *, n_cmp, n_blk, blk_lanes, past_len, tq_pad, gqa):
    scale = HEAD_DIM ** -0.5
    q = jnp.concatenate([q_ref[0, :, g * HEAD_DIM:(g + 1) * HEAD_DIM] for g in range(gqa)], axis=0)
    ck = ck_ref[0, 0, 0]
    cv = cv_ref[0, 0, 0]
    n16 = ck.shape[0]
    q_pos = past_len + lax.broadcasted_iota(I32, (tq_pad, 1), 0)
    s = (_dot_nt(q, ck) * scale).reshape(gqa, tq_pad, n16)
    cidx = lax.broadcasted_iota(I32, (tq_pad, n16), 1)
    cmask = (cidx * CMP_STRIDE + (CMP_BLOCK - 1) <= q_pos) & (cidx < n_cmp)
    p = _masked_softmax(s, cmask[None])
    ocmp_ref[0, 0] = jnp.dot(p.reshape(gqa * tq_pad, n16).astype(BF16), cv, preferred_element_type=F32)
    psum = jnp.sum(p, axis=0)
    imp = jnp.dot(psum, _overlap(n16, blk_lanes, 0), preferred_element_type=F32, precision=lax.Precision.HIGHEST)
    rank = _rank_desc(_block_scores(imp, q_pos, n_blk), n_blk)
    lane = lax.broadcasted_iota(I32, rank.shape, 1)
    lane_f = lane.astype(F32)
    out_lane = lax.broadcasted_iota(I32, (tq_pad, LANES), 1)
    idx = jnp.zeros((tq_pad, LANES), F32)
    for k in range(min(N_SEL, n_blk)):
        hit = (rank == float(k)) & (lane < n_blk)
        idx_k = jnp.sum(jnp.where(hit, lane_f, 0.0), axis=-1, keepdims=True)
        idx = jnp.where(out_lane == k, idx_k, idx)
    idx_ref[0, 0] = idx.astype(I32)


def _nsa_sample_attend_body(idx_ref, pt_ref, q_ref, gt_ref, ocmp_ref, kwc_ref, vwc_ref,
                            kwn_ref, vwn_ref, cache_ref, newblk_ref, o_ref, kbuf, vbuf, kwbuf, vwbuf, sems,
                            *, n_sel, n_cache_blk, n_pages, past_len, win_buf, tq, tq_pad, gqa):
    b = pl.program_id(0)
    h = pl.program_id(1)
    scale = HEAD_DIM ** -0.5
    blk_per_page = PAGE_SIZE // SEL_BLOCK

    bufs = ((2, kbuf, 0), (3, vbuf, 1))

    def from_new(t, k, slot, buf, sem):
        return pltpu.make_async_copy(newblk_ref.at[b, :, slot * N_KV_HEADS + h], buf.at[t, k], sems.at[sem])

    for t in range(tq):
        for k in range(n_sel):
            blk = idx_ref[((b * N_KV_HEADS + h) * tq + t) * n_sel + k]
            in_cache = blk < n_cache_blk
            blk_c = jnp.minimum(blk, n_cache_blk - 1)
            page = pt_ref[b * n_pages + lax.div(blk_c, blk_per_page)]
            off = pl.multiple_of(lax.rem(blk_c, blk_per_page) * SEL_BLOCK, SEL_BLOCK)
            for slot, buf, sem in bufs:
                @pl.when(in_cache)
                def _():
                    pltpu.make_async_copy(cache_ref.at[page, pl.ds(off, SEL_BLOCK), slot * N_KV_HEADS + h],
                                          buf.at[t, k], sems.at[sem]).start()

                @pl.when(jnp.logical_not(in_cache))
                def _():
                    from_new(t, k, slot, buf, sem).start()
    q = jnp.concatenate([q_ref[0, :, g * HEAD_DIM:(g + 1) * HEAD_DIM] for g in range(gqa)], axis=0)
    rows = gqa * tq_pad
    row_t = lax.rem(lax.broadcasted_iota(I32, (rows, 1), 0), tq_pad)
    q_pos = past_len + row_t

    kwbuf[...] = jnp.zeros(kwbuf.shape, F32)
    vwbuf[...] = jnp.zeros(vwbuf.shape, F32)
    kwbuf[0:win_buf, :] = kwc_ref[0]
    vwbuf[0:win_buf, :] = vwc_ref[0]
    kwbuf[win_buf:win_buf + tq_pad, :] = kwn_ref[0]
    vwbuf[win_buf:win_buf + tq_pad, :] = vwn_ref[0]
    nw = kwbuf.shape[0]
    wlane = lax.broadcasted_iota(I32, (rows, nw), 1)
    wpos = past_len - win_buf + wlane
    wmask = (wpos <= q_pos) & (wpos > q_pos - WINDOW) & (wpos >= 0) & (wlane < win_buf + tq)
    s = _dot_nt(q, kwbuf[...].astype(BF16)) * scale
    p = _masked_softmax(s, wmask)
    o_win = jnp.dot(p.astype(BF16), vwbuf[...].astype(BF16), preferred_element_type=F32)

    for t in range(tq):
        for k in range(n_sel):
            for slot, buf, sem in bufs:
                from_new(t, k, slot, buf, sem).wait()

    o_sel = jnp.zeros((rows, HEAD_DIM), F32)
    nk = n_sel * SEL_BLOCK
    klane = lax.broadcasted_iota(I32, (rows, nk), 1)
    for t in range(tq):
        kpos = jnp.zeros((rows, nk), I32)
        for k in range(n_sel):
            blk = idx_ref[((b * N_KV_HEADS + h) * tq + t) * n_sel + k]
            kpos = jnp.where(jnp.right_shift(klane, SEL_SHIFT) == k,
                             blk * SEL_BLOCK + jnp.bitwise_and(klane, SEL_BLOCK - 1), kpos)
        mask = (kpos <= q_pos) & (row_t == t)
        kt = kbuf[t].reshape(nk, HEAD_DIM).astype(BF16)
        vt = vbuf[t].reshape(nk, HEAD_DIM).astype(BF16)
        p = _masked_softmax(_dot_nt(q, kt) * scale, mask)
        o_sel = o_sel + jnp.dot(p.astype(BF16), vt, preferred_element_type=F32)

    o_cmp = ocmp_ref[0, 0]
    gt = gt_ref[0, 0]
    for g in range(gqa):
        r = slice(g * tq_pad, (g + 1) * tq_pad)
        o = (gt[:, 3 * g:3 * g + 1] * o_cmp[r] + gt[:, 3 * g + 1:3 * g + 2] * o_sel[r]
             + gt[:, 3 * g + 2:3 * g + 3] * o_win[r])
        o_ref[0, :, g * HEAD_DIM:(g + 1) * HEAD_DIM] = o.astype(o_ref.dtype)


def nsa_sample_attend(q, gates, cmp, new_rows, win_new, cache_kv, cache_win, page_table, tq):
    bsz, tq_pad, hd_all = q.shape
    kvh = N_KV_HEADS
    gqa = hd_all // HEAD_DIM // kvh
    n_pages = page_table.shape[1]
    past_len = n_pages * PAGE_SIZE
    n16 = cmp.shape[3]
    n_cmp = n16 - 1
    n_cache_blk = past_len // SEL_BLOCK
    n_blk = -(-(past_len + tq) // SEL_BLOCK)
    assert n_blk == n_cache_blk + 1 and past_len % SEL_BLOCK == 0 and tq <= SEL_BLOCK
    n_sel = min(N_SEL, n_blk)
    blk_lanes = -(-n_blk // LANES) * LANES
    win_buf = cache_win.shape[1]
    cmp_spec = lambda s: pl.BlockSpec((1, 1, 1, n16, HEAD_DIM), lambda b, h, s=s: (b, s, h, 0, 0))
    ocmp, idx = pl.pallas_call(
        functools.partial(_nsa_sample_select_body, n_cmp=n_cmp, n_blk=n_blk, blk_lanes=blk_lanes,
                          past_len=past_len, tq_pad=tq_pad, gqa=gqa),
        grid=(bsz, kvh),
        in_specs=[pl.BlockSpec((1, tq_pad, gqa * HEAD_DIM), lambda b, h: (b, 0, h)), cmp_spec(0), cmp_spec(1)],
        out_specs=[pl.BlockSpec((1, 1, gqa * tq_pad, HEAD_DIM), lambda b, h: (b, h, 0, 0)),
                   pl.BlockSpec((1, 1, tq_pad, LANES), lambda b, h: (b, h, 0, 0))],
        out_shape=[jax.ShapeDtypeStruct((bsz, kvh, gqa * tq_pad, HEAD_DIM), F32),
                   jax.ShapeDtypeStruct((bsz, kvh, tq_pad, LANES), I32)],
        compiler_params=_cparams(("parallel", "parallel")),
        name="nsa_sample_select",
    )(q, cmp, cmp)
    idx_flat = idx[:, :, :tq, :n_sel].reshape(-1)

    n_slots = cache_kv.shape[2]
    cache4 = cache_kv.reshape(cache_kv.shape[0], PAGE_SIZE, n_slots * kvh, HEAD_DIM)
    newblk = jnp.zeros((bsz, SEL_BLOCK, n_slots * kvh, HEAD_DIM), F32).at[:, :tq].set(new_rows)
    cwin = cache_win.astype(F32).reshape(bsz, win_buf, 2 * kvh * HEAD_DIM)
    nw = -(-(win_buf + tq_pad) // LANES) * LANES
    kv_spec = lambda slot: pl.BlockSpec((1, tq_pad, HEAD_DIM), lambda b, h, *_, slot=slot: (b, 0, slot * kvh + h))
    cw_spec = lambda slot: pl.BlockSpec((1, win_buf, HEAD_DIM), lambda b, h, *_, slot=slot: (b, 0, slot * kvh + h))
    grid_spec = pltpu.PrefetchScalarGridSpec(
        num_scalar_prefetch=2,
        grid=(bsz, kvh),
        in_specs=[pl.BlockSpec((1, tq_pad, gqa * HEAD_DIM), lambda b, h, *_: (b, 0, h)),
                  pl.BlockSpec((1, 1, tq_pad, LANES), lambda b, h, *_: (b, h, 0, 0)),
                  pl.BlockSpec((1, 1, gqa * tq_pad, HEAD_DIM), lambda b, h, *_: (b, h, 0, 0)),
                  cw_spec(0), cw_spec(1), kv_spec(0), kv_spec(1),
                  pl.BlockSpec(memory_space=pl.ANY), pl.BlockSpec(memory_space=pl.ANY)],
        out_specs=pl.BlockSpec((1, tq_pad, gqa * HEAD_DIM), lambda b, h, *_: (b, 0, h)),
        scratch_shapes=[pltpu.VMEM((tq, n_sel, SEL_BLOCK, HEAD_DIM), F32),
                        pltpu.VMEM((tq, n_sel, SEL_BLOCK, HEAD_DIM), F32),
                        pltpu.VMEM((nw, HEAD_DIM), F32), pltpu.VMEM((nw, HEAD_DIM), F32),
                        pltpu.SemaphoreType.DMA((2,))],
    )
    return pl.pallas_call(
        functools.partial(_nsa_sample_attend_body, n_sel=n_sel, n_cache_blk=n_cache_blk, n_pages=n_pages,
                          past_len=past_len, win_buf=win_buf, tq=tq, tq_pad=tq_pad, gqa=gqa),
        grid_spec=grid_spec,
        out_shape=jax.ShapeDtypeStruct((bsz, tq_pad, hd_all), BF16),
        compiler_params=_cparams(("arbitrary", "arbitrary")),
        name="nsa_sample_attend",
    )(idx_flat, page_table.reshape(-1).astype(I32), q, gates, ocmp, cwin, cwin, win_new, win_new, cache4, newblk)


def _pad_rows(x, rows):
    return jnp.pad(x, ((0, 0), (0, rows - x.shape[1]), (0, 0)))


@jax.jit
def _step(x_prompt, x_sample, state_ssm_re, state_ssm_im, state_ffn_conv, cache_kv, cache_win, page_table,
          attn_norm, ffn_norm, final_norm, ssm_lam_re, ssm_lam_im, ssm_log_step, ssm_b_re, ssm_b_im,
          ssm_c_re, ssm_c_im, ssm_d, ssm_w_glu, ffn_w_in, ffn_conv_w, ffn_conv_b, ffn_w_down,
          kv_norm, w_kv, cmp_w1, cmp_b1, cmp_w2, cmp_b2, cmp_pe, w_qg, w_o):
    bp, tp, d = x_prompt.shape
    bs, ts, _ = x_sample.shape
    depth = attn_norm.shape[0]
    n_a = ssm_lam_re.shape[0]
    kvh = N_KV_HEADS
    hd_all = w_o.shape[1]
    n_heads = hd_all // HEAD_DIM
    gqa = n_heads // kvh
    ts_pad = -(-ts // SUBLANES) * SUBLANES
    n_pages = page_table.shape[1]

    xp = x_prompt.reshape(bp * tp, d).astype(F32)
    xs = x_sample.reshape(bs * ts, d).astype(F32)

    w1cat = jnp.concatenate([cmp_w1[:, :CMP_STRIDE * HEAD_DIM], cmp_w1[:, CMP_STRIDE * HEAD_DIM:]], axis=-1)
    w1cat = w1cat.reshape(2, CMP_STRIDE // 2, 2 * HEAD_DIM, w1cat.shape[-1]).astype(BF16)

    ssm_re_p, ssm_im_p, ssm_re_s, ssm_im_s, conv_p, conv_s = [], [], [], [], [], []
    outs = {}
    for layer in range(depth):
        if layer < n_a:
            a = layer
            wts = _s5_weights(ssm_lam_re[a], ssm_lam_im[a], ssm_log_step[a], ssm_b_re[a], ssm_b_im[a],
                              ssm_c_re[a], ssm_c_im[a])
            w_glu = ssm_w_glu[a].astype(BF16)
            up = rmsnorm(xp, attn_norm[layer], F32).reshape(bp, tp, d)
            zeros = jnp.zeros((bp, d // SSM_GROUP, SSM_STATE), F32)
            zp, hre, him = s5_scan(up, zeros, zeros, wts, ssm_d[a], tp, math.gcd(tp, 512))
            ssm_re_p.append(hre)
            ssm_im_p.append(him)
            xp = glu_matmul(zp.reshape(bp * tp, d), w_glu, xp)
            us = _pad_rows(rmsnorm(xs, attn_norm[layer], F32).reshape(bs, ts, d), ts_pad)
            zs, hre, him = s5_scan(us, state_ssm_re[a], state_ssm_im[a], wts, ssm_d[a], ts, ts_pad)
            ssm_re_s.append(hre)
            ssm_im_s.append(him)
            xs = glu_matmul(zs[:, :ts].reshape(bs * ts, d), w_glu, xs)
        else:
            bl = layer - n_a
            if layer == n_a:
                w_kv_b = w_kv.astype(BF16)
                n_kv4 = 4 * kvh * HEAD_DIM
                kvr_p, kvw_p = matmul_split(rmsnorm(xp, kv_norm, BF16), w_kv_b, n_kv4)
                kvr_s, kvw_s = matmul_split(rmsnorm(xs, kv_norm, BF16), w_kv_b, n_kv4)
                outs["kv_rows_p"] = kvr_p.reshape(bp, tp, 4, kvh, HEAD_DIM)
                outs["kv_rows_s"] = kvr_s.reshape(bs, ts, 4, kvh, HEAD_DIM)
                outs["win_p"] = kvw_p.reshape(bp, tp, 2, kvh, HEAD_DIM)[:, max(tp - WINDOW, 0):]
                win_s = kvw_s.reshape(bs, ts, 2, kvh, HEAD_DIM)
                win_buf = cache_win.shape[1]
                outs["win_s"] = jnp.concatenate([cache_win.astype(F32), win_s], axis=1)[:, -win_buf:]
                kvw_s3 = _pad_rows(kvw_s.reshape(bs, ts, kvw_s.shape[1]), ts_pad)
                assert tp % PAGE_SIZE == 0 and (n_pages * PAGE_SIZE + ts) // CMP_STRIDE == n_pages * PAGE_SIZE // CMP_STRIDE
                pt_p = jnp.arange(bp * (tp // PAGE_SIZE), dtype=I32).reshape(bp, tp // PAGE_SIZE)
                ab_p = cmp_project(kvr_p.reshape(bp * tp // PAGE_SIZE, PAGE_SIZE, 4 * kvh, HEAD_DIM), pt_p, w1cat)
                cmp_p = cmp_finish(ab_p, cmp_pe, cmp_w1, cmp_b1, cmp_w2, cmp_b2)
                cache4 = cache_kv.astype(F32).reshape(cache_kv.shape[0], PAGE_SIZE, -1, HEAD_DIM)
                ab_s = cmp_project(cache4, page_table, w1cat)
                cmp_s = cmp_finish(ab_s, cmp_pe, cmp_w1, cmp_b1, cmp_w2, cmp_b2)
            w_q = w_qg[bl][:, :hd_all].astype(BF16)
            n_gate = w_qg.shape[2] - hd_all
            w_g = jnp.pad(w_qg[bl][:, hd_all:], ((0, 0), (0, LANES - n_gate))).astype(BF16)
            w_o_b = w_o[bl].astype(BF16)

            def gate_layout(gt, bsz, t):
                gt = gt[:, :n_gate].reshape(bsz, t, kvh, 3 * gqa).transpose(0, 2, 1, 3)
                return jnp.pad(gt, ((0, 0), (0, 0), (0, 0), (0, LANES - 3 * gqa)))

            hp = rmsnorm(xp, attn_norm[layer], BF16)
            q_p = matmul(hp, w_q, out_dtype=BF16, tm=1024, tn=512)
            g_p = matmul(hp, w_g, act="sigmoid", tm=1024)
            o_p = nsa_prompt_attend(q_p, g_p, cmp_p, (kvr_p, 2 * kvh), (kvr_p, 3 * kvh), (kvw_p, 0), (kvw_p, kvh), bp, tp)
            xp = matmul(o_p, w_o_b, res=xp, tm=1024, tn=512)

            hs = rmsnorm(xs, attn_norm[layer], BF16)
            q_s = _pad_rows(matmul(hs, w_q, out_dtype=BF16).reshape(bs, ts, hd_all), ts_pad)
            g_s = gate_layout(matmul(hs, w_g, act="sigmoid"), bs, ts)
            g_s = jnp.pad(g_s, ((0, 0), (0, 0), (0, ts_pad - ts), (0, 0)))
            o_s = nsa_sample_attend(q_s, g_s, cmp_s, kvr_s.reshape(bs, ts, 4 * kvh, HEAD_DIM), kvw_s3,
                                    cache_kv.astype(F32), cache_win, page_table, ts)
            xs = matmul(o_s[:, :ts].reshape(bs * ts, hd_all), w_o_b, res=xs)

        w_down = ffn_w_down[layer].astype(BF16)
        f = w_down.shape[0]
        act, st, act_s, st_s = ffn_up(rmsnorm(xp, ffn_norm[layer], BF16), rmsnorm(xs, ffn_norm[layer], BF16),
                                      ffn_w_in[layer].astype(F32), ffn_conv_w[layer], ffn_conv_b[layer],
                                      jnp.zeros((bp, CONV_W - 1, f), F32), state_ffn_conv[layer], tp, ts)
        conv_p.append(st)
        conv_s.append(st_s)
        xp = matmul(act, w_down, res=xp, tm=512, tn=256)
        xs = matmul(act_s, w_down, res=xs, tm=512, tn=256)

    y_p = rmsnorm(xp, final_norm, F32).reshape(bp, tp, d)
    y_s = rmsnorm(xs, final_norm, F32).reshape(bs, ts, d)
    return (y_p, y_s, jnp.stack(ssm_re_p), jnp.stack(ssm_im_p), jnp.stack(ssm_re_s), jnp.stack(ssm_im_s),
            jnp.stack(conv_p), jnp.stack(conv_s), outs["kv_rows_p"], outs["kv_rows_s"], outs["win_p"], outs["win_s"])


def kernel(x_prompt, x_sample, state_ssm_re, state_ssm_im, state_ffn_conv, cache_kv, cache_win, page_table, attn_norm, ffn_norm, final_norm, ssm_lam_re, ssm_lam_im, ssm_log_step, ssm_b_re, ssm_b_im, ssm_c_re, ssm_c_im, ssm_d, ssm_w_glu, ffn_w_in, ffn_conv_w, ffn_conv_b, ffn_w_down, kv_norm, w_kv, cmp_w1, cmp_b1, cmp_w2, cmp_b2, cmp_pe, w_qg, w_o):
    return _step(x_prompt, x_sample, state_ssm_re, state_ssm_im, state_ffn_conv, cache_kv, cache_win, page_table,
                 attn_norm, ffn_norm, final_norm, ssm_lam_re, ssm_lam_im, ssm_log_step, ssm_b_re, ssm_b_im,
                 ssm_c_re, ssm_c_im, ssm_d, ssm_w_glu, ffn_w_in, ffn_conv_w, ffn_conv_b, ffn_w_down,
                 kv_norm, w_kv, cmp_w1, cmp_b1, cmp_w2, cmp_b2, cmp_pe, w_qg, w_o)
```

```python
import functools
import math

import jax
import jax.numpy as jnp
from jax import lax
from jax.experimental import pallas as pl
from jax.experimental.pallas import tpu as pltpu

F32 = jnp.float32
BF16 = jnp.bfloat16
I32 = jnp.int32

RMS_EPS = 1e-6
LANES = 128
SUBLANES = 8
VMEM_LIMIT = 56 * 1024 * 1024

SSM_GROUP = 16
SSM_STATE = 64
CONV_W = 3
HEAD_DIM = 128
N_KV_HEADS = 4
CMP_STRIDE = 16
CMP_BLOCK = 32
SEL_BLOCK = 64
SEL_SHIFT = SEL_BLOCK.bit_length() - 1
N_SEL = 16
WINDOW = 512
Q_BLOCK = 128
PAGE_SIZE = 128
NEG_BIG = -1e30
M_FLOOR = 0.1 * NEG_BIG
LOG2_E = math.log2(math.e)
ONES_ROWS = 16

S5_SET_GROUPS = 4
S5_SETS = 8
S5_BLK_GROUPS = S5_SET_GROUPS * S5_SETS
S5_BLK_CH = S5_BLK_GROUPS * SSM_GROUP
S5_SET_LANES = S5_SET_GROUPS * SSM_STATE


def _cparams(sem, vmem=VMEM_LIMIT):
    return pltpu.CompilerParams(dimension_semantics=sem, vmem_limit_bytes=vmem)


def _rmsnorm_body(x_ref, g_ref, o_ref):
    x = x_ref[...].astype(F32)
    y = x * lax.rsqrt(jnp.mean(x * x, axis=-1, keepdims=True) + RMS_EPS)
    o_ref[...] = (y * g_ref[...]).astype(o_ref.dtype)


def rmsnorm(x, g, out_dtype):
    m, d = x.shape
    tm = math.gcd(m, 256)
    return pl.pallas_call(
        _rmsnorm_body,
        grid=(m // tm,),
        in_specs=[pl.BlockSpec((tm, d), lambda i: (i, 0)), pl.BlockSpec((1, d), lambda i: (0, 0))],
        out_specs=pl.BlockSpec((tm, d), lambda i: (i, 0)),
        out_shape=jax.ShapeDtypeStruct((m, d), out_dtype),
        compiler_params=_cparams(("parallel",)),
        name="rmsnorm",
    )(x, g.reshape(1, d).astype(F32))


def _mm_body(*refs, has_res, act):
    if has_res:
        x_ref, w_ref, res_ref, o_ref = refs
    else:
        x_ref, w_ref, o_ref = refs
    y = jnp.dot(x_ref[...], w_ref[...], preferred_element_type=F32)
    if act == "sigmoid":
        y = jax.nn.sigmoid(y)
    if has_res:
        y = res_ref[...] + y
    o_ref[...] = y.astype(o_ref.dtype)


def matmul(x, w, res=None, out_dtype=F32, act=None, tm=512, tn=256):
    m, k = x.shape
    n = w.shape[1]
    tm = math.gcd(m, tm)
    tn = math.gcd(n, tn)
    in_specs = [pl.BlockSpec((tm, k), lambda i, j: (i, 0)), pl.BlockSpec((k, tn), lambda i, j: (0, j))]
    args = [x, w]
    if res is not None:
        in_specs.append(pl.BlockSpec((tm, tn), lambda i, j: (i, j)))
        args.append(res)
    return pl.pallas_call(
        functools.partial(_mm_body, has_res=res is not None, act=act),
        grid=(m // tm, n // tn),
        in_specs=in_specs,
        out_specs=pl.BlockSpec((tm, tn), lambda i, j: (i, j)),
        out_shape=jax.ShapeDtypeStruct((m, n), out_dtype),
        compiler_params=_cparams(("parallel", "arbitrary")),
        name="matmul",
    )(*args)


def _mm_split_body(x_ref, w_ref, o1_ref, o2_ref, *, n1_tiles):
    j = pl.program_id(1)
    y = jnp.dot(x_ref[...], w_ref[...], preferred_element_type=F32)

    @pl.when(j < n1_tiles)
    def _():
        o1_ref[...] = y

    @pl.when(j >= n1_tiles)
    def _():
        o2_ref[...] = y


def matmul_split(x, w, n1, tm=1024, tn=512):
    m, k = x.shape
    n = w.shape[1]
    tm = math.gcd(m, tm)
    tn = math.gcd(math.gcd(n1, n - n1), tn)
    t1 = n1 // tn
    return pl.pallas_call(
        functools.partial(_mm_split_body, n1_tiles=t1),
        grid=(m // tm, n // tn),
        in_specs=[pl.BlockSpec((tm, k), lambda i, j: (i, 0)), pl.BlockSpec((k, tn), lambda i, j: (0, j))],
        out_specs=[pl.BlockSpec((tm, tn), lambda i, j: (i, jnp.minimum(j, t1 - 1))),
                   pl.BlockSpec((tm, tn), lambda i, j: (i, jnp.maximum(j - t1, 0)))],
        out_shape=[jax.ShapeDtypeStruct((m, n1), F32), jax.ShapeDtypeStruct((m, n - n1), F32)],
        compiler_params=_cparams(("parallel", "arbitrary")),
        name="matmul_split",
    )(x, w)


def _glu_body(x_ref, wa_ref, wb_ref, res_ref, o_ref):
    tm = x_ref.shape[0]
    chunk = math.gcd(tm, 128)
    for r in range(0, tm, chunk):
        x = x_ref[r:r + chunk, :]
        a = jnp.dot(x, wa_ref[...], preferred_element_type=F32)
        b = jnp.dot(x, wb_ref[...], preferred_element_type=F32)
        o_ref[r:r + chunk, :] = res_ref[r:r + chunk, :] + a * jax.nn.sigmoid(b)


def glu_matmul(x, w, res, tm=1024, tn=256):
    m, k = x.shape
    n = w.shape[1] // 2
    tm = math.gcd(m, tm)
    tn = math.gcd(n, tn)
    nt = n // tn
    return pl.pallas_call(
        _glu_body,
        grid=(m // tm, nt),
        in_specs=[pl.BlockSpec((tm, k), lambda i, j: (i, 0)),
                  pl.BlockSpec((k, tn), lambda i, j: (0, j)),
                  pl.BlockSpec((k, tn), lambda i, j: (0, j + nt)),
                  pl.BlockSpec((tm, tn), lambda i, j: (i, j))],
        out_specs=pl.BlockSpec((tm, tn), lambda i, j: (i, j)),
        out_shape=jax.ShapeDtypeStruct((m, n), F32),
        compiler_params=_cparams(("parallel", "arbitrary")),
        name="glu_matmul",
    )(x, w, w, res)


def _conv_gate(g, g1, g2, cw_ref, cb_ref, val):
    gc = cb_ref[...] + cw_ref[0:1, :] * g2 + cw_ref[1:2, :] * g1 + cw_ref[2:3, :] * g
    return (gc * jax.nn.sigmoid(gc) * val).astype(BF16)


def _ffn_up_body(*refs, n_m, tiles_per_seq, short_len, chunk, n_slabs):
    x_slabs = refs[:n_slabs]
    (xs_ref, wv_ref, wg_ref, cw_ref, cb_ref, p1_ref, p2_ref, cwp_ref, cbp_ref, init_ref,
     act_ref, st_ref, acts_ref, gs_ref, wv_s, wg_s, val_a, g_a, val_b, g_b) = refs[n_slabs:]
    slab_rows = x_slabs[0].shape[0]
    s = pl.program_id(0)
    i = lax.rem(s, n_m)

    @pl.when(s == 0)
    def _():
        val_b[...] = jnp.zeros(val_b.shape, F32)
        g_b[...] = jnp.zeros(g_b.shape, F32)

    @pl.when(i == 0)
    def _():
        wv_s[...] = wv_ref[...].astype(BF16)
        wg_s[...] = wg_ref[...].astype(BF16)
        xs = xs_ref[...]
        val = jnp.dot(xs, wv_s[...], preferred_element_type=F32)
        g = jnp.dot(xs, wg_s[...], preferred_element_type=F32)
        tmod = lax.rem(lax.broadcasted_iota(I32, g.shape, 0), short_len)
        g1 = jnp.where(tmod >= 1, pltpu.roll(g, 1, axis=0), p1_ref[...])
        g2 = jnp.where(tmod >= 2, pltpu.roll(g, 2, axis=0), p2_ref[...])
        acts_ref[...] = _conv_gate(g, g1, g2, cw_ref, cb_ref, val)
        gs_ref[...] = g

    def step(val_w, g_w, val_r, g_r):
        tm = n_slabs * slab_rows
        for r in range(0, tm, chunk):
            x = x_slabs[r // slab_rows][r % slab_rows:r % slab_rows + chunk, :]
            val_w[r:r + chunk, :] = jnp.dot(x, wv_s[...], preferred_element_type=F32)
            g_w[SUBLANES + r:SUBLANES + r + chunk, :] = jnp.dot(x, wg_s[...], preferred_element_type=F32)
            gext = g_r[r:r + chunk + SUBLANES, :]
            g1 = pltpu.roll(gext, 1, axis=0)[SUBLANES:]
            g2 = pltpu.roll(gext, 2, axis=0)[SUBLANES:]
            act_ref[r:r + chunk, :] = _conv_gate(gext[SUBLANES:], g1, g2, cwp_ref, cbp_ref, val_r[r:r + chunk, :])
        tail = g_r[tm:, :]
        st_ref[0] = tail
        g_w[:SUBLANES, :] = jnp.where(lax.rem(i, tiles_per_seq) == 0, init_ref[0], tail)

    parity = lax.rem(s, 2)

    @pl.when(parity == 0)
    def _():
        step(val_a, g_a, val_b, g_b)

    @pl.when(parity == 1)
    def _():
        step(val_b, g_b, val_a, g_a)


def ffn_up(h, hs, w_in_all, layer, conv_w, conv_b, buf, buf_s, seq_len, seq_short, tm=1024, tn=256, n_slabs=4):
    m, k = h.shape
    ms = hs.shape[0]
    f = w_in_all.shape[2] // 2
    bsz = m // seq_len
    bs = ms // seq_short
    assert seq_len % SUBLANES == 0 and seq_len >= 2 * SUBLANES and seq_short >= 2
    tn = math.gcd(f, tn)
    n_n = f // tn
    tm = math.gcd(seq_len, tm)
    tps = seq_len // tm
    n_m = m // tm
    chunk = math.gcd(tm, 128)
    n_slabs = math.gcd(tm // chunk, n_slabs)
    slab = tm // n_slabs
    cw = jnp.zeros((SUBLANES, f), F32).at[:CONV_W].set(conv_w.astype(F32))
    cb = conv_b.reshape(1, f).astype(F32)
    init = jnp.zeros((bsz, SUBLANES, f), F32).at[:, SUBLANES - 2:].set(buf.astype(F32))
    buf_s = buf_s.astype(F32)
    pos = jnp.arange(ms) % seq_short
    b0 = jnp.repeat(buf_s[:, 0], seq_short, axis=0)
    b1 = jnp.repeat(buf_s[:, 1], seq_short, axis=0)
    p1 = jnp.where((pos == 0)[:, None], b1, 0.0)
    p2 = jnp.where((pos == 0)[:, None], b0, jnp.where((pos == 1)[:, None], b1, 0.0))

    def col(s):
        return jnp.minimum(s // n_m, n_n - 1)

    def prev(s):
        sp = jnp.maximum(s - 1, 0)
        return sp % n_m, sp // n_m

    act, st, act_s, g_s = pl.pallas_call(
        functools.partial(_ffn_up_body, n_m=n_m, tiles_per_seq=tps, short_len=seq_short, chunk=chunk, n_slabs=n_slabs),
        grid=(n_n * n_m + 1,),
        in_specs=[pl.BlockSpec((slab, k), lambda s, c=c: ((s % n_m) * n_slabs + c, 0)) for c in range(n_slabs)] + [
                  pl.BlockSpec((ms, k), lambda s: (0, 0)),
                  pl.BlockSpec((None, k, tn), lambda s: (layer, 0, col(s))),
                  pl.BlockSpec((None, k, tn), lambda s: (layer, 0, col(s) + n_n)),
                  pl.BlockSpec((SUBLANES, tn), lambda s: (0, col(s))),
                  pl.BlockSpec((1, tn), lambda s: (0, col(s))),
                  pl.BlockSpec((ms, tn), lambda s: (0, col(s))),
                  pl.BlockSpec((ms, tn), lambda s: (0, col(s))),
                  pl.BlockSpec((SUBLANES, tn), lambda s: (0, prev(s)[1])),
                  pl.BlockSpec((1, tn), lambda s: (0, prev(s)[1])),
                  pl.BlockSpec((1, SUBLANES, tn), lambda s: ((s % n_m) // tps, 0, col(s)))],
        out_specs=[pl.BlockSpec((tm, tn), lambda s: prev(s)),
                   pl.BlockSpec((1, SUBLANES, tn), lambda s: (prev(s)[0], 0, prev(s)[1])),
                   pl.BlockSpec((ms, tn), lambda s: (0, col(s))),
                   pl.BlockSpec((ms, tn), lambda s: (0, col(s)))],
        out_shape=[jax.ShapeDtypeStruct((m, f), BF16), jax.ShapeDtypeStruct((n_m, SUBLANES, f), F32),
                   jax.ShapeDtypeStruct((ms, f), BF16), jax.ShapeDtypeStruct((ms, f), F32)],
        scratch_shapes=[pltpu.VMEM((k, tn), BF16), pltpu.VMEM((k, tn), BF16)]
                       + [pltpu.VMEM((tm, tn), F32), pltpu.VMEM((tm + SUBLANES, tn), F32)] * 2,
        compiler_params=_cparams(("arbitrary",)),
        name="ffn_up",
    )(*([h] * n_slabs), hs, w_in_all, w_in_all, cw, cb, p1, p2, cw, cb, init)
    return (act, st[tps - 1::tps, SUBLANES - 2:], act_s, g_s.reshape(bs, seq_short, f)[:, seq_short - 2:])


def _s5_body(u_ref, wb_ref, wc_ref, lre_ref, lim_ref, d_ref, h0re_ref, h0im_ref,
             z_ref, hre_ref, him_ref, sr0, sr1, si0, si1, hst_re, hst_im, *, tc_len, last_step):
    tci = pl.program_id(2)
    u = u_ref[0]
    halves = S5_SET_LANES // LANES
    sre = (sr0, sr1)
    sim = (si0, si1)

    def set_rows(j):
        return pl.ds(j, tc_len, stride=S5_SETS)

    for ti in range(S5_BLK_CH // LANES):
        ut = u[:, ti * LANES:(ti + 1) * LANES]
        hi = ut.astype(BF16)
        lo = (ut - hi.astype(F32)).astype(BF16)
        lhs = jnp.concatenate([hi, lo], axis=1)
        for jj in range(2):
            j = ti * 2 + jj
            bre = jnp.dot(lhs, wb_ref[0, j, 0], preferred_element_type=F32)
            bim = jnp.dot(lhs, wb_ref[0, j, 1], preferred_element_type=F32)
            for c in range(halves):
                sre[c][set_rows(j), :] = bre[:, c * LANES:(c + 1) * LANES]
                sim[c][set_rows(j), :] = bim[:, c * LANES:(c + 1) * LANES]

    @pl.when(tci == 0)
    def _():
        hst_re[...] = h0re_ref[0, 0]
        hst_im[...] = h0im_ref[0, 0]

    lam_r = lre_ref[0]
    lam_i = lim_ref[0]
    lr = [lam_r[:, c * LANES:(c + 1) * LANES] for c in range(halves)]
    li = [lam_i[:, c * LANES:(c + 1) * LANES] for c in range(halves)]

    def step(t, carry):
        r0 = pl.multiple_of(t * S5_SETS, S5_SETS)
        out = []
        for c in range(halves):
            hr, hi_ = carry[2 * c], carry[2 * c + 1]
            nr = lr[c] * hr - li[c] * hi_ + sre[c][pl.ds(r0, S5_SETS), :]
            ni = lr[c] * hi_ + li[c] * hr + sim[c][pl.ds(r0, S5_SETS), :]
            sre[c][pl.ds(r0, S5_SETS), :] = nr
            sim[c][pl.ds(r0, S5_SETS), :] = ni
            out += [nr, ni]
        return tuple(out)

    h_in = hst_re[...]
    g_in = hst_im[...]
    init = []
    for c in range(halves):
        init += [h_in[:, c * LANES:(c + 1) * LANES], g_in[:, c * LANES:(c + 1) * LANES]]
    fin = lax.fori_loop(0, tc_len, step, tuple(init), unroll=8)
    for c in range(halves):
        hst_re[:, c * LANES:(c + 1) * LANES] = fin[2 * c]
        hst_im[:, c * LANES:(c + 1) * LANES] = fin[2 * c + 1]

    @pl.when(tci == last_step // tc_len)
    def _():
        r0 = (last_step % tc_len) * S5_SETS
        for c in range(halves):
            hre_ref[0, 0, :, c * LANES:(c + 1) * LANES] = sre[c][pl.ds(r0, S5_SETS), :]
            him_ref[0, 0, :, c * LANES:(c + 1) * LANES] = sim[c][pl.ds(r0, S5_SETS), :]

    for c in range(S5_BLK_CH // LANES):
        parts = []
        for s in range(2):
            j = 2 * c + s
            parts += [buf[set_rows(j), :].astype(BF16) for buf in sre + sim]
        lhs = jnp.concatenate(parts, axis=1)
        y = jnp.dot(lhs, wc_ref[0, c], preferred_element_type=F32)
        y = y + d_ref[:, c * LANES:(c + 1) * LANES] * u[:, c * LANES:(c + 1) * LANES]
        z_ref[0, :, c * LANES:(c + 1) * LANES] = jax.nn.gelu(y).astype(BF16)


def _s5_weights(lam_re, lam_im, log_step, b_re, b_im, c_re, c_im):
    g, p = lam_re.shape
    n = SSM_GROUP
    nblk = g // S5_BLK_GROUPS
    lam = lax.complex(lam_re.astype(F32), lam_im.astype(F32))
    dt = jnp.exp(log_step.astype(F32))[:, None]
    lam_bar = jnp.exp(lam * dt)
    b_bar = ((lam_bar - 1.0) / lam)[..., None] * lax.complex(b_re.astype(F32), b_im.astype(F32))
    eye_q = jnp.eye(S5_SET_GROUPS, dtype=F32)
    eye_h = jnp.eye(2, dtype=F32)

    def b_operand(bm):
        t = bm.reshape(nblk, S5_SETS // 2, 2, S5_SET_GROUPS, p, n)
        w = jnp.einsum("bthqpn,hk,qr->bthkrnqp", t, eye_h, eye_q)
        w = w.reshape(nblk, S5_SETS, LANES, S5_SET_LANES)
        return w

    wb = jnp.stack([b_operand(jnp.real(b_bar)), b_operand(jnp.imag(b_bar))], axis=2)
    wb = wb.astype(BF16)
    wb = jnp.concatenate([wb, wb], axis=3)

    def c_operand(cm):
        t = cm.reshape(nblk, S5_SETS // 2, 2, S5_SET_GROUPS, n, p)
        return jnp.einsum("bcsqnp,st,qr->bcsqptrn", t, eye_h, eye_q)

    cre = c_operand(c_re.astype(F32))
    cim = c_operand(-c_im.astype(F32))
    wc = jnp.stack([cre, cim], axis=3).reshape(nblk, S5_SETS // 2, 4 * S5_SET_LANES, LANES).astype(BF16)
    lre = jnp.real(lam_bar).reshape(nblk, S5_SETS, S5_SET_LANES)
    lim = jnp.imag(lam_bar).reshape(nblk, S5_SETS, S5_SET_LANES)
    return wb, wc, lre, lim


def s5_scan(u, h0_re, h0_im, weights, d_skip, seq_valid, tc_len):
    wb, wc, lre, lim = weights
    bsz, lp, d = u.shape
    nblk = d // S5_BLK_CH
    n_tc = lp // tc_len
    h0_re = h0_re.astype(F32).reshape(bsz, nblk, S5_SETS, S5_SET_LANES)
    h0_im = h0_im.astype(F32).reshape(bsz, nblk, S5_SETS, S5_SET_LANES)
    state_spec = pl.BlockSpec((1, 1, S5_SETS, S5_SET_LANES), lambda k, b, t: (b, k, 0, 0))
    z, hre, him = pl.pallas_call(
        functools.partial(_s5_body, tc_len=tc_len, last_step=seq_valid - 1),
        grid=(nblk, bsz, n_tc),
        in_specs=[pl.BlockSpec((1, tc_len, S5_BLK_CH), lambda k, b, t: (b, t, k)),
                  pl.BlockSpec((1, S5_SETS, 2, 2 * LANES, S5_SET_LANES), lambda k, b, t: (k, 0, 0, 0, 0)),
                  pl.BlockSpec((1, S5_SETS // 2, 4 * S5_SET_LANES, LANES), lambda k, b, t: (k, 0, 0, 0)),
                  pl.BlockSpec((1, S5_SETS, S5_SET_LANES), lambda k, b, t: (k, 0, 0)),
                  pl.BlockSpec((1, S5_SETS, S5_SET_LANES), lambda k, b, t: (k, 0, 0)),
                  pl.BlockSpec((1, S5_BLK_CH), lambda k, b, t: (0, k)),
                  state_spec, state_spec],
        out_specs=[pl.BlockSpec((1, tc_len, S5_BLK_CH), lambda k, b, t: (b, t, k)), state_spec, state_spec],
        out_shape=[jax.ShapeDtypeStruct((bsz, lp, d), BF16),
                   jax.ShapeDtypeStruct((bsz, nblk, S5_SETS, S5_SET_LANES), F32),
                   jax.ShapeDtypeStruct((bsz, nblk, S5_SETS, S5_SET_LANES), F32)],
        scratch_shapes=[pltpu.VMEM((tc_len * S5_SETS, LANES), F32)] * (2 * S5_SET_LANES // LANES) + [
                        pltpu.VMEM((S5_SETS, S5_SET_LANES), F32),
                        pltpu.VMEM((S5_SETS, S5_SET_LANES), F32)],
        compiler_params=_cparams(("arbitrary", "arbitrary", "arbitrary")),
        name="s5_scan",
    )(u, wb, wc, lre, lim, d_skip.reshape(1, d).astype(F32), h0_re, h0_im)
    g = d // SSM_GROUP
    return z, hre.reshape(bsz, g, SSM_STATE), him.reshape(bsz, g, SSM_STATE)


def _cmp_proj_body(pt_ref, x_hbm, w_ref, o_ref, stage, sems, *, pages_per_step):
    n_heads2 = 2 * N_KV_HEADS
    chunks = PAGE_SIZE // CMP_STRIDE
    step = pl.program_id(0) * pl.num_programs(1) + pl.program_id(1)
    n_steps = pl.num_programs(0) * pl.num_programs(1)
    slot = lax.rem(step, 2)

    def page_copy(st, sl, p, c):
        page = pt_ref[st * pages_per_step + p]
        return pltpu.make_async_copy(x_hbm.at[page, :, c, :], stage.at[sl, p, c], sems.at[sl])

    def start_step(st, sl):
        for p in range(pages_per_step):
            for c in range(n_heads2):
                page_copy(st, sl, p, c).start()

    @pl.when(step == 0)
    def _():
        start_step(step, slot)

    @pl.when(step + 1 < n_steps)
    def _():
        start_step(step + 1, 1 - slot)

    for p in range(pages_per_step):
        for c in range(n_heads2):
            page_copy(step, slot, p, c).wait()

    def chunk_rows(c, j):
        return jnp.concatenate([stage[slot, p, c, pl.ds(j, chunks, stride=CMP_STRIDE), :]
                                for p in range(pages_per_step)], axis=0)

    for c in range(n_heads2):
        s, h = divmod(c, N_KV_HEADS)
        acc = None
        for jp in range(CMP_STRIDE // 2):
            lhs = jnp.concatenate([chunk_rows(c, 2 * jp), chunk_rows(c, 2 * jp + 1)], axis=1).astype(BF16)
            t = jnp.dot(lhs, w_ref[s, jp], preferred_element_type=F32)
            acc = t if acc is None else acc + t
        o_ref[0, s, h] = acc


def cmp_project(rows, page_table, w1cat, pages_per_step=16):
    bsz, npg = page_table.shape
    pps = math.gcd(npg, pages_per_step)
    chunks = PAGE_SIZE // CMP_STRIDE
    hid2 = w1cat.shape[-1]
    grid_spec = pltpu.PrefetchScalarGridSpec(
        num_scalar_prefetch=1,
        grid=(bsz, npg // pps),
        in_specs=[pl.BlockSpec(memory_space=pl.ANY),
                  pl.BlockSpec(w1cat.shape, lambda b, g, pt: (0, 0, 0, 0))],
        out_specs=pl.BlockSpec((1, 2, N_KV_HEADS, pps * chunks, hid2), lambda b, g, pt: (b, 0, 0, g, 0)),
        scratch_shapes=[pltpu.VMEM((2, pps, 2 * N_KV_HEADS, PAGE_SIZE, HEAD_DIM), F32),
                        pltpu.SemaphoreType.DMA((2,))],
    )
    return pl.pallas_call(
        functools.partial(_cmp_proj_body, pages_per_step=pps),
        grid_spec=grid_spec,
        out_shape=jax.ShapeDtypeStruct((bsz, 2, N_KV_HEADS, npg * chunks, hid2), F32),
        compiler_params=_cparams(("arbitrary", "arbitrary")),
        name="cmp_project",
    )(page_table.reshape(-1).astype(I32), rows, w1cat)


def _cmp_finish_body(ab_ref, pe_ref, w1_ref, b1_ref, w2_ref, b2_ref, o_ref):
    ab = ab_ref[0, 0, 0]
    n16 = ab.shape[0]
    hid = ab.shape[1] // 2
    a = ab[:, :hid]
    bnext = pltpu.roll(ab[:, hid:], n16 - 1, axis=0)
    c = jnp.dot(pe_ref[0], w1_ref[0], preferred_element_type=F32)[0:1, :] + b1_ref[0]
    pre = a + bnext + c
    y = jnp.dot(jax.nn.gelu(pre).astype(BF16), w2_ref[0], preferred_element_type=F32) + b2_ref[0]
    row = lax.broadcasted_iota(I32, y.shape, 0)
    o_ref[0, 0, 0] = jnp.where(row < n16 - 1, y, 0.0).astype(o_ref.dtype)


def cmp_finish(ab, pe, w1, b1, w2, b2):
    bsz, _, kvh, n16, hid2 = ab.shape
    hid = hid2 // 2
    pe_rows = jnp.broadcast_to(pe.reshape(2, 1, -1), (2, SUBLANES, pe.shape[1] * pe.shape[2])).astype(BF16)
    return pl.pallas_call(
        _cmp_finish_body,
        grid=(2, bsz, kvh),
        in_specs=[pl.BlockSpec((1, 1, 1, n16, hid2), lambda s, b, h: (b, s, h, 0, 0)),
                  pl.BlockSpec((1, SUBLANES, pe_rows.shape[2]), lambda s, b, h: (s, 0, 0)),
                  pl.BlockSpec((1,) + w1.shape[1:], lambda s, b, h: (s, 0, 0)),
                  pl.BlockSpec((1, 1, hid), lambda s, b, h: (s, 0, 0)),
                  pl.BlockSpec((1, hid, HEAD_DIM), lambda s, b, h: (s, 0, 0)),
                  pl.BlockSpec((1, 1, HEAD_DIM), lambda s, b, h: (s, 0, 0))],
        out_specs=pl.BlockSpec((1, 1, 1, n16, HEAD_DIM), lambda s, b, h: (b, s, h, 0, 0)),
        out_shape=jax.ShapeDtypeStruct((bsz, 2, kvh, n16, HEAD_DIM), BF16),
        compiler_params=_cparams(("arbitrary", "arbitrary", "arbitrary")),
        name="cmp_finish",
    )(ab, pe_rows, w1.astype(BF16), b1.reshape(2, 1, hid).astype(F32), w2.astype(BF16),
      b2.reshape(2, 1, HEAD_DIM).astype(F32))


def _masked_softmax(s, mask):
    sm = jnp.where(mask, s, NEG_BIG)
    m = jnp.max(sm, axis=-1, keepdims=True)
    m = jnp.where(m > 0.5 * NEG_BIG, m, 0.0)
    e = jnp.where(mask, jnp.exp(sm - m), 0.0)
    den = jnp.sum(e, axis=-1, keepdims=True)
    return e / jnp.where(den > 0, den, 1.0)


def _dot_nt(a, b):
    return lax.dot_general(a, b, (((1,), (1,)), ((), ())), preferred_element_type=F32)


def _overlap(n_c, n_j, c_axis):
    shape = (n_c, n_j) if c_axis == 0 else (n_j, n_c)
    c = lax.broadcasted_iota(I32, shape, c_axis) * CMP_STRIDE
    j = lax.broadcasted_iota(I32, shape, 1 - c_axis) * SEL_BLOCK
    return ((c < j + SEL_BLOCK) & (c + CMP_BLOCK > j)).astype(F32)


def _block_scores(imp, q_pos, n_blk):
    blk = lax.broadcasted_iota(I32, imp.shape, 1)
    q_blk = jnp.right_shift(q_pos, SEL_SHIFT)
    forced = (blk == 0) | (blk == q_blk) | (blk == q_blk - 1)
    future = blk * SEL_BLOCK > q_pos
    v = jnp.where(future, NEG_BIG, jnp.where(forced, -NEG_BIG, imp))
    return jnp.where(blk < n_blk, v, 2.0 * NEG_BIG)


def _rank_desc(v, n):
    lane = lax.broadcasted_iota(I32, v.shape, 1)
    rank = jnp.zeros(v.shape, F32)
    for i in range(n):
        vi = v[:, i:i + 1]
        before = (vi > v) | ((vi == v) & (lane > i))
        rank = rank + jnp.where(before, 1.0, 0.0)
    return rank


def _nsa_prompt_body(q_ref, gt_ref, ck_ref, cvt_ref, ks_ref, vs_ref, kw_ref, vw_ref, o_ref, m_ref, acc_ref,
                     vst_ref, vwt_ref, *, n_cmp, n_blk, gqa, sel_tile, win_keys):
    qb = pl.program_id(2)
    nq = Q_BLOCK
    gq = gqa * nq
    c2 = (HEAD_DIM ** -0.5) * LOG2_E

    @pl.when(qb == 0)
    def _():
        ones = jnp.ones((ONES_ROWS, LANES), BF16)
        for src, dst in ((vs_ref, vst_ref), (vw_ref, vwt_ref)):
            for t in range(dst.shape[0]):
                dst[t, :HEAD_DIM, :] = src[0, t * LANES:(t + 1) * LANES, :].T.astype(BF16)
                dst[t, HEAD_DIM:, :] = ones

    qt = jnp.concatenate([q_ref[0, :, g * HEAD_DIM:(g + 1) * HEAD_DIM].astype(F32).T.astype(BF16)
                          for g in range(gqa)], axis=1)
    q_pos = qb * nq + lax.broadcasted_iota(I32, (1, nq), 1)

    def per_head(x):
        return jnp.concatenate([x] * gqa, axis=1)

    def normalise(acc):
        den = acc[HEAD_DIM:HEAD_DIM + 1, :]
        return acc[:HEAD_DIM, :] * (1.0 / jnp.where(den > 0, den, 1.0))

    ck = ck_ref[0, 0, 0]
    n16 = ck.shape[0]
    s = jnp.dot(ck, qt, preferred_element_type=F32)
    cidx = lax.broadcasted_iota(I32, (n16, 1), 0)
    cbias = jnp.where((cidx * CMP_STRIDE + (CMP_BLOCK - 1) <= q_pos) & (cidx < n_cmp), 0.0, NEG_BIG)
    sb = s + per_head(cbias)
    m = jnp.maximum(jnp.max(sb, axis=0, keepdims=True), M_FLOOR)
    e = jnp.exp2((sb - m) * c2)
    den = jnp.sum(e, axis=0, keepdims=True)
    p = e * jnp.where(den > 0, 1.0 / den, 0.0)
    o_cmp = jnp.dot(cvt_ref[0, 0], p.astype(BF16), preferred_element_type=F32)

    psum = p[:, 0:nq]
    for g in range(1, gqa):
        psum = psum + p[:, g * nq:(g + 1) * nq]
    imp = jnp.dot(_overlap(n16, LANES, 1), psum, preferred_element_type=F32, precision=lax.Precision.HIGHEST)
    blk = lax.broadcasted_iota(I32, (LANES, 1), 0)
    q_blk = jnp.right_shift(q_pos, SEL_SHIFT)
    forced = (blk == 0) | (blk == q_blk) | (blk == q_blk - 1)
    v = jnp.where(blk * SEL_BLOCK > q_pos, NEG_BIG, jnp.where(forced, -NEG_BIG, imp))
    v = jnp.where(blk < n_blk, v, 2.0 * NEG_BIG)
    rank = jnp.zeros(v.shape, F32)
    for i in range(n_blk):
        vi = v[i:i + 1, :]
        rank = rank + jnp.where((vi > v) | ((vi == v) & (blk > i)), 1.0, 0.0)
    sel_bias = jnp.where((rank < min(N_SEL, n_blk)) & (blk < n_blk), 0.0, NEG_BIG).astype(BF16)

    m_ref[...] = jnp.full(m_ref.shape, M_FLOOR, F32)
    acc_ref[...] = jnp.zeros(acc_ref.shape, F32)
    tiles = sel_tile // LANES

    def sel_step(kt, _):
        k0 = pl.multiple_of(kt * sel_tile, sel_tile)
        s = jnp.dot(ks_ref[0, pl.ds(k0, sel_tile), :].astype(BF16), qt, preferred_element_type=F32)
        key = k0 + lax.broadcasted_iota(I32, (sel_tile, 1), 0)
        expand = jnp.where(jnp.right_shift(key, SEL_SHIFT) == lax.broadcasted_iota(I32, (sel_tile, LANES), 1), 1.0, 0.0)
        bias = jnp.dot(expand.astype(BF16), sel_bias, preferred_element_type=F32)
        bias = jnp.where(key <= q_pos, bias, NEG_BIG)
        sb = s + per_head(bias)
        m_old = m_ref[0:1, :]
        m_new = jnp.maximum(m_old, jnp.max(sb, axis=0, keepdims=True))
        alpha = jnp.exp2((m_old - m_new) * c2)
        e = jnp.exp2((sb - m_new) * c2).astype(BF16)
        vt = jnp.concatenate([vst_ref[kt * tiles + i] for i in range(tiles)], axis=1)
        acc_ref[...] = alpha * acc_ref[...] + jnp.dot(vt, e, preferred_element_type=F32)
        m_ref[...] = jnp.broadcast_to(m_new, m_ref.shape)
        return 0

    lax.fori_loop(0, (qb * nq + nq + sel_tile - 1) // sel_tile, sel_step, 0)
    o_sel = normalise(acc_ref[...])

    w_tile = jnp.maximum(qb + 1 - win_keys // nq, 0)
    w0 = pl.multiple_of(w_tile * nq, nq)
    s = jnp.dot(kw_ref[0, pl.ds(w0, win_keys), :].astype(BF16), qt, preferred_element_type=F32)
    key = w0 + lax.broadcasted_iota(I32, (win_keys, 1), 0)
    wbias = jnp.where((key <= q_pos) & (key > q_pos - WINDOW), 0.0, NEG_BIG)
    sb = s + per_head(wbias)
    m = jnp.maximum(jnp.max(sb, axis=0, keepdims=True), M_FLOOR)
    e = jnp.exp2((sb - m) * c2).astype(BF16)
    vt = jnp.concatenate([vwt_ref[w_tile + i] for i in range(win_keys // LANES)], axis=1)
    o_win = normalise(jnp.dot(vt, e, preferred_element_type=F32))

    gt = gt_ref[0, 0, 0]
    for g in range(gqa):
        cols = slice(g * nq, (g + 1) * nq)
        o = (gt[3 * g:3 * g + 1, :] * o_cmp[:, cols] + gt[3 * g + 1:3 * g + 2, :] * o_sel[:, cols]
             + gt[3 * g + 2:3 * g + 3, :] * o_win[:, cols])
        o_ref[0, :, g * HEAD_DIM:(g + 1) * HEAD_DIM] = o.T.astype(o_ref.dtype)


def nsa_prompt_attend(q, gates, cmp, k_sel, v_sel, k_win, v_win, bsz, t, sel_tile=512):
    hd_all = q.shape[1]
    kvh = N_KV_HEADS
    gqa = hd_all // HEAD_DIM // kvh
    n16 = cmp.shape[3]
    n_blk = -(-t // SEL_BLOCK)
    nqb = t // Q_BLOCK
    gq = gqa * Q_BLOCK
    vrows = HEAD_DIM + ONES_ROWS
    assert t % Q_BLOCK == 0 and n_blk <= LANES and n16 * CMP_STRIDE == t
    sel_tile = math.gcd(t, sel_tile)
    win_keys = min(WINDOW + Q_BLOCK, t)
    gt = gates[:, :3 * kvh * gqa].reshape(bsz, nqb, Q_BLOCK, kvh, 3 * gqa).transpose(0, 3, 1, 4, 2)
    gt = jnp.pad(gt, ((0, 0), (0, 0), (0, 0), (0, -3 * gqa % SUBLANES), (0, 0)))
    cvt = cmp[:, 1].transpose(0, 1, 3, 2)
    (ks_arr, ks_col), (vs_arr, vs_col), (kw_arr, kw_col), (vw_arr, vw_col) = k_sel, v_sel, k_win, v_win
    k_spec = lambda col: pl.BlockSpec((1, t, HEAD_DIM), lambda b, h, i, col=col: (b, 0, col + h))
    return pl.pallas_call(
        functools.partial(_nsa_prompt_body, n_cmp=n16 - 1, n_blk=n_blk, gqa=gqa, sel_tile=sel_tile, win_keys=win_keys),
        grid=(bsz, kvh, nqb),
        in_specs=[pl.BlockSpec((1, Q_BLOCK, gqa * HEAD_DIM), lambda b, h, i: (b, i, h)),
                  pl.BlockSpec((1, 1, 1, gt.shape[3], Q_BLOCK), lambda b, h, i: (b, h, i, 0, 0)),
                  pl.BlockSpec((1, 1, 1, n16, HEAD_DIM), lambda b, h, i: (b, 0, h, 0, 0)),
                  pl.BlockSpec((1, 1, HEAD_DIM, n16), lambda b, h, i: (b, h, 0, 0)),
                  k_spec(ks_col), k_spec(vs_col), k_spec(kw_col), k_spec(vw_col)],
        out_specs=pl.BlockSpec((1, Q_BLOCK, gqa * HEAD_DIM), lambda b, h, i: (b, i, h)),
        out_shape=jax.ShapeDtypeStruct((bsz, t, hd_all), BF16),
        scratch_shapes=[pltpu.VMEM((SUBLANES, gq), F32), pltpu.VMEM((vrows, gq), F32),
                        pltpu.VMEM((t // LANES, vrows, LANES), BF16), pltpu.VMEM((t // LANES, vrows, LANES), BF16)],
        compiler_params=_cparams(("parallel", "parallel", "arbitrary")),
        name="nsa_prompt",
    )(q.reshape(bsz, t, hd_all), gt, cmp, cvt, ks_arr.reshape(bsz, t, -1), vs_arr.reshape(bsz, t, -1),
      kw_arr.reshape(bsz, t, -1), vw_arr.reshape(bsz, t, -1)).reshape(bsz * t, hd_all)


def _nsa_sample_select_body(q_ref, ck_ref, cv_ref, ocmp_ref, idx_ref, *, n_cmp, n_blk, blk_lanes, past_len, tq_pad, gqa):
    scale = HEAD_DIM ** -0.5
    q = jnp.concatenate([q_ref[0, :, g * HEAD_DIM:(g + 1) * HEAD_DIM] for g in range(gqa)], axis=0)
    ck = ck_ref[0, 0, 0]
    cv = cv_ref[0, 0, 0]
    n16 = ck.shape[0]
    q_pos = past_len + lax.broadcasted_iota(I32, (tq_pad, 1), 0)
    s = (_dot_nt(q, ck) * scale).reshape(gqa, tq_pad, n16)
    cidx = lax.broadcasted_iota(I32, (tq_pad, n16), 1)
    cmask = (cidx * CMP_STRIDE + (CMP_BLOCK - 1) <= q_pos) & (cidx < n_cmp)
    p = _masked_softmax(s, cmask[None])
    ocmp_ref[0, 0] = jnp.dot(p.reshape(gqa * tq_pad, n16).astype(BF16), cv, preferred_element_type=F32)
    psum = jnp.sum(p, axis=0)
    imp = jnp.dot(psum, _overlap(n16, blk_lanes, 0), preferred_element_type=F32, precision=lax.Precision.HIGHEST)
    rank = _rank_desc(_block_scores(imp, q_pos, n_blk), n_blk)
    lane = lax.broadcasted_iota(I32, rank.shape, 1)
    lane_f = lane.astype(F32)
    out_lane = lax.broadcasted_iota(I32, (tq_pad, LANES), 1)
    idx = jnp.zeros((tq_pad, LANES), F32)
    for k in range(min(N_SEL, n_blk)):
        hit = (rank == float(k)) & (lane < n_blk)
        idx_k = jnp.sum(jnp.where(hit, lane_f, 0.0), axis=-1, keepdims=True)
        idx = jnp.where(out_lane == k, idx_k, idx)
    idx_ref[0, 0] = idx.astype(I32)


def _nsa_sample_attend_body(idx_ref, pt_ref, q_ref, gt_ref, ocmp_ref, kwc_ref, vwc_ref,
                            kwn_ref, vwn_ref, cache_ref, newblk_ref, o_ref, kbuf, vbuf, kwbuf, vwbuf, sems,
                            *, n_sel, n_cache_blk, n_pages, past_len, win_buf, tq, tq_pad, gqa):
    b = pl.program_id(0)
    h = pl.program_id(1)
    scale = HEAD_DIM ** -0.5
    blk_per_page = PAGE_SIZE // SEL_BLOCK

    bufs = ((2, kbuf, 0), (3, vbuf, 1))

    def from_new(t, k, slot, buf, sem):
        return pltpu.make_async_copy(newblk_ref.at[b, :, slot * N_KV_HEADS + h], buf.at[t, k], sems.at[sem])

    for t in range(tq):
        for k in range(n_sel):
            blk = idx_ref[((b * N_KV_HEADS + h) * tq + t) * n_sel + k]
            in_cache = blk < n_cache_blk
            blk_c = jnp.minimum(blk, n_cache_blk - 1)
            page = pt_ref[b * n_pages + lax.div(blk_c, blk_per_page)]
            off = pl.multiple_of(lax.rem(blk_c, blk_per_page) * SEL_BLOCK, SEL_BLOCK)
            for slot, buf, sem in bufs:
                @pl.when(in_cache)
                def _():
                    pltpu.make_async_copy(cache_ref.at[page, pl.ds(off, SEL_BLOCK), slot * N_KV_HEADS + h],
                                          buf.at[t, k], sems.at[sem]).start()

                @pl.when(jnp.logical_not(in_cache))
                def _():
                    from_new(t, k, slot, buf, sem).start()
    q = jnp.concatenate([q_ref[0, :, g * HEAD_DIM:(g + 1) * HEAD_DIM] for g in range(gqa)], axis=0)
    rows = gqa * tq_pad
    row_t = lax.rem(lax.broadcasted_iota(I32, (rows, 1), 0), tq_pad)
    q_pos = past_len + row_t

    kwbuf[...] = jnp.zeros(kwbuf.shape, F32)
    vwbuf[...] = jnp.zeros(vwbuf.shape, F32)
    kwbuf[0:win_buf, :] = kwc_ref[0]
    vwbuf[0:win_buf, :] = vwc_ref[0]
    kwbuf[win_buf:win_buf + tq_pad, :] = kwn_ref[0]
    vwbuf[win_buf:win_buf + tq_pad, :] = vwn_ref[0]
    nw = kwbuf.shape[0]
    wlane = lax.broadcasted_iota(I32, (rows, nw), 1)
    wpos = past_len - win_buf + wlane
    wmask = (wpos <= q_pos) & (wpos > q_pos - WINDOW) & (wpos >= 0) & (wlane < win_buf + tq)
    s = _dot_nt(q, kwbuf[...].astype(BF16)) * scale
    p = _masked_softmax(s, wmask)
    o_win = jnp.dot(p.astype(BF16), vwbuf[...].astype(BF16), preferred_element_type=F32)

    for t in range(tq):
        for k in range(n_sel):
            for slot, buf, sem in bufs:
                from_new(t, k, slot, buf, sem).wait()

    o_sel = jnp.zeros((rows, HEAD_DIM), F32)
    nk = n_sel * SEL_BLOCK
    klane = lax.broadcasted_iota(I32, (rows, nk), 1)
    for t in range(tq):
        kpos = jnp.zeros((rows, nk), I32)
        for k in range(n_sel):
            blk = idx_ref[((b * N_KV_HEADS + h) * tq + t) * n_sel + k]
            kpos = jnp.where(jnp.right_shift(klane, SEL_SHIFT) == k,
                             blk * SEL_BLOCK + jnp.bitwise_and(klane, SEL_BLOCK - 1), kpos)
        mask = (kpos <= q_pos) & (row_t == t)
        kt = kbuf[t].reshape(nk, HEAD_DIM).astype(BF16)
        vt = vbuf[t].reshape(nk, HEAD_DIM).astype(BF16)
        p = _masked_softmax(_dot_nt(q, kt) * scale, mask)
        o_sel = o_sel + jnp.dot(p.astype(BF16), vt, preferred_element_type=F32)

    o_cmp = ocmp_ref[0, 0]
    gt = gt_ref[0, 0]
    for g in range(gqa):
        r = slice(g * tq_pad, (g + 1) * tq_pad)
        o = (gt[:, 3 * g:3 * g + 1] * o_cmp[r] + gt[:, 3 * g + 1:3 * g + 2] * o_sel[r]
             + gt[:, 3 * g + 2:3 * g + 3] * o_win[r])
        o_ref[0, :, g * HEAD_DIM:(g + 1) * HEAD_DIM] = o.astype(o_ref.dtype)


def nsa_sample_attend(q, gates, cmp, new_rows, win_new, cache_kv, cache_win, page_table, tq):
    bsz, tq_pad, hd_all = q.shape
    kvh = N_KV_HEADS
    gqa = hd_all // HEAD_DIM // kvh
    n_pages = page_table.shape[1]
    past_len = n_pages * PAGE_SIZE
    n16 = cmp.shape[3]
    n_cmp = n16 - 1
    n_cache_blk = past_len // SEL_BLOCK
    n_blk = -(-(past_len + tq) // SEL_BLOCK)
    assert n_blk == n_cache_blk + 1 and past_len % SEL_BLOCK == 0 and tq <= SEL_BLOCK
    n_sel = min(N_SEL, n_blk)
    blk_lanes = -(-n_blk // LANES) * LANES
    win_buf = cache_win.shape[1]
    cmp_spec = lambda s: pl.BlockSpec((1, 1, 1, n16, HEAD_DIM), lambda b, h, s=s: (b, s, h, 0, 0))
    ocmp, idx = pl.pallas_call(
        functools.partial(_nsa_sample_select_body, n_cmp=n_cmp, n_blk=n_blk, blk_lanes=blk_lanes,
                          past_len=past_len, tq_pad=tq_pad, gqa=gqa),
        grid=(bsz, kvh),
        in_specs=[pl.BlockSpec((1, tq_pad, gqa * HEAD_DIM), lambda b, h: (b, 0, h)), cmp_spec(0), cmp_spec(1)],
        out_specs=[pl.BlockSpec((1, 1, gqa * tq_pad, HEAD_DIM), lambda b, h: (b, h, 0, 0)),
                   pl.BlockSpec((1, 1, tq_pad, LANES), lambda b, h: (b, h, 0, 0))],
        out_shape=[jax.ShapeDtypeStruct((bsz, kvh, gqa * tq_pad, HEAD_DIM), F32),
                   jax.ShapeDtypeStruct((bsz, kvh, tq_pad, LANES), I32)],
        compiler_params=_cparams(("parallel", "parallel")),
        name="nsa_sample_select",
    )(q, cmp, cmp)
    idx_flat = idx[:, :, :tq, :n_sel].reshape(-1)

    n_slots = cache_kv.shape[2]
    cache4 = cache_kv.reshape(cache_kv.shape[0], PAGE_SIZE, n_slots * kvh, HEAD_DIM)
    newblk = jnp.zeros((bsz, SEL_BLOCK, n_slots * kvh, HEAD_DIM), F32).at[:, :tq].set(new_rows)
    cwin = cache_win.astype(F32).reshape(bsz, win_buf, 2 * kvh * HEAD_DIM)
    nw = -(-(win_buf + tq_pad) // LANES) * LANES
    kv_spec = lambda slot: pl.BlockSpec((1, tq_pad, HEAD_DIM), lambda b, h, *_, slot=slot: (b, 0, slot * kvh + h))
    cw_spec = lambda slot: pl.BlockSpec((1, win_buf, HEAD_DIM), lambda b, h, *_, slot=slot: (b, 0, slot * kvh + h))
    grid_spec = pltpu.PrefetchScalarGridSpec(
        num_scalar_prefetch=2,
        grid=(bsz, kvh),
        in_specs=[pl.BlockSpec((1, tq_pad, gqa * HEAD_DIM), lambda b, h, *_: (b, 0, h)),
                  pl.BlockSpec((1, 1, tq_pad, LANES), lambda b, h, *_: (b, h, 0, 0)),
                  pl.BlockSpec((1, 1, gqa * tq_pad, HEAD_DIM), lambda b, h, *_: (b, h, 0, 0)),
                  cw_spec(0), cw_spec(1), kv_spec(0), kv_spec(1),
                  pl.BlockSpec(memory_space=pl.ANY), pl.BlockSpec(memory_space=pl.ANY)],
        out_specs=pl.BlockSpec((1, tq_pad, gqa * HEAD_DIM), lambda b, h, *_: (b, 0, h)),
        scratch_shapes=[pltpu.VMEM((tq, n_sel, SEL_BLOCK, HEAD_DIM), F32),
                        pltpu.VMEM((tq, n_sel, SEL_BLOCK, HEAD_DIM), F32),
                        pltpu.VMEM((nw, HEAD_DIM), F32), pltpu.VMEM((nw, HEAD_DIM), F32),
                        pltpu.SemaphoreType.DMA((2,))],
    )
    return pl.pallas_call(
        functools.partial(_nsa_sample_attend_body, n_sel=n_sel, n_cache_blk=n_cache_blk, n_pages=n_pages,
                          past_len=past_len, win_buf=win_buf, tq=tq, tq_pad=tq_pad, gqa=gqa),
        grid_spec=grid_spec,
        out_shape=jax.ShapeDtypeStruct((bsz, tq_pad, hd_all), BF16),
        compiler_params=_cparams(("arbitrary", "arbitrary")),
        name="nsa_sample_attend",
    )(idx_flat, page_table.reshape(-1).astype(I32), q, gates, ocmp, cwin, cwin, win_new, win_new, cache4, newblk)


def _pad_rows(x, rows):
    return jnp.pad(x, ((0, 0), (0, rows - x.shape[1]), (0, 0)))


@jax.jit
def _step(x_prompt, x_sample, state_ssm_re, state_ssm_im, state_ffn_conv, cache_kv, cache_win, page_table,
          attn_norm, ffn_norm, final_norm, ssm_lam_re, ssm_lam_im, ssm_log_step, ssm_b_re, ssm_b_im,
          ssm_c_re, ssm_c_im, ssm_d, ssm_w_glu, ffn_w_in, ffn_conv_w, ffn_conv_b, ffn_w_down,
          kv_norm, w_kv, cmp_w1, cmp_b1, cmp_w2, cmp_b2, cmp_pe, w_qg, w_o):
    bp, tp, d = x_prompt.shape
    bs, ts, _ = x_sample.shape
    depth = attn_norm.shape[0]
    n_a = ssm_lam_re.shape[0]
    kvh = N_KV_HEADS
    hd_all = w_o.shape[1]
    n_heads = hd_all // HEAD_DIM
    gqa = n_heads // kvh
    ts_pad = -(-ts // SUBLANES) * SUBLANES
    n_pages = page_table.shape[1]

    xp = x_prompt.reshape(bp * tp, d).astype(F32)
    xs = x_sample.reshape(bs * ts, d).astype(F32)

    w1cat = jnp.concatenate([cmp_w1[:, :CMP_STRIDE * HEAD_DIM], cmp_w1[:, CMP_STRIDE * HEAD_DIM:]], axis=-1)
    w1cat = w1cat.reshape(2, CMP_STRIDE // 2, 2 * HEAD_DIM, w1cat.shape[-1]).astype(BF16)

    ssm_re_p, ssm_im_p, ssm_re_s, ssm_im_s, conv_p, conv_s = [], [], [], [], [], []
    outs = {}
    for layer in range(depth):
        if layer < n_a:
            a = layer
            wts = _s5_weights(ssm_lam_re[a], ssm_lam_im[a], ssm_log_step[a], ssm_b_re[a], ssm_b_im[a],
                              ssm_c_re[a], ssm_c_im[a])
            w_glu = ssm_w_glu[a].astype(BF16)
            up = rmsnorm(xp, attn_norm[layer], F32).reshape(bp, tp, d)
            zeros = jnp.zeros((bp, d // SSM_GROUP, SSM_STATE), F32)
            zp, hre, him = s5_scan(up, zeros, zeros, wts, ssm_d[a], tp, math.gcd(tp, 512))
            ssm_re_p.append(hre)
            ssm_im_p.append(him)
            xp = glu_matmul(zp.reshape(bp * tp, d), w_glu, xp)
            us = _pad_rows(rmsnorm(xs, attn_norm[layer], F32).reshape(bs, ts, d), ts_pad)
            zs, hre, him = s5_scan(us, state_ssm_re[a], state_ssm_im[a], wts, ssm_d[a], ts, ts_pad)
            ssm_re_s.append(hre)
            ssm_im_s.append(him)
            xs = glu_matmul(zs[:, :ts].reshape(bs * ts, d), w_glu, xs)
        else:
            bl = layer - n_a
            if layer == n_a:
                w_kv_b = w_kv.astype(BF16)
                n_kv4 = 4 * kvh * HEAD_DIM
                kvr_p, kvw_p = matmul_split(rmsnorm(xp, kv_norm, BF16), w_kv_b, n_kv4)
                kvr_s, kvw_s = matmul_split(rmsnorm(xs, kv_norm, BF16), w_kv_b, n_kv4)
                outs["kv_rows_p"] = kvr_p.reshape(bp, tp, 4, kvh, HEAD_DIM)
                outs["kv_rows_s"] = kvr_s.reshape(bs, ts, 4, kvh, HEAD_DIM)
                outs["win_p"] = kvw_p.reshape(bp, tp, 2, kvh, HEAD_DIM)[:, max(tp - WINDOW, 0):]
                win_s = kvw_s.reshape(bs, ts, 2, kvh, HEAD_DIM)
                win_buf = cache_win.shape[1]
                outs["win_s"] = jnp.concatenate([cache_win.astype(F32), win_s], axis=1)[:, -win_buf:]
                kvw_s3 = _pad_rows(kvw_s.reshape(bs, ts, kvw_s.shape[1]), ts_pad)
                assert tp % PAGE_SIZE == 0 and (n_pages * PAGE_SIZE + ts) // CMP_STRIDE == n_pages * PAGE_SIZE // CMP_STRIDE
                pt_p = jnp.arange(bp * (tp // PAGE_SIZE), dtype=I32).reshape(bp, tp // PAGE_SIZE)
                ab_p = cmp_project(kvr_p.reshape(bp * tp // PAGE_SIZE, PAGE_SIZE, 4 * kvh, HEAD_DIM), pt_p, w1cat)
                cmp_p = cmp_finish(ab_p, cmp_pe, cmp_w1, cmp_b1, cmp_w2, cmp_b2)
                cache4 = cache_kv.astype(F32).reshape(cache_kv.shape[0], PAGE_SIZE, -1, HEAD_DIM)
                ab_s = cmp_project(cache4, page_table, w1cat)
                cmp_s = cmp_finish(ab_s, cmp_pe, cmp_w1, cmp_b1, cmp_w2, cmp_b2)
            w_q = w_qg[bl][:, :hd_all].astype(BF16)
            n_gate = w_qg.shape[2] - hd_all
            w_g = jnp.pad(w_qg[bl][:, hd_all:], ((0, 0), (0, LANES - n_gate))).astype(BF16)
            w_o_b = w_o[bl].astype(BF16)

            def gate_layout(gt, bsz, t):
                gt = gt[:, :n_gate].reshape(bsz, t, kvh, 3 * gqa).transpose(0, 2, 1, 3)
                return jnp.pad(gt, ((0, 0), (0, 0), (0, 0), (0, LANES - 3 * gqa)))

            hp = rmsnorm(xp, attn_norm[layer], BF16)
            q_p = matmul(hp, w_q, out_dtype=BF16, tm=1024, tn=512)
            g_p = matmul(hp, w_g, act="sigmoid", tm=1024)
            o_p = nsa_prompt_attend(q_p, g_p, cmp_p, (kvr_p, 2 * kvh), (kvr_p, 3 * kvh), (kvw_p, 0), (kvw_p, kvh), bp, tp)
            xp = matmul(o_p, w_o_b, res=xp, tm=1024, tn=512)

            hs = rmsnorm(xs, attn_norm[layer], BF16)
            q_s = _pad_rows(matmul(hs, w_q, out_dtype=BF16).reshape(bs, ts, hd_all), ts_pad)
            g_s = gate_layout(matmul(hs, w_g, act="sigmoid"), bs, ts)
            g_s = jnp.pad(g_s, ((0, 0), (0, 0), (0, ts_pad - ts), (0, 0)))
            o_s = nsa_sample_attend(q_s, g_s, cmp_s, kvr_s.reshape(bs, ts, 4 * kvh, HEAD_DIM), kvw_s3,
                                    cache_kv.astype(F32), cache_win, page_table, ts)
            xs = matmul(o_s[:, :ts].reshape(bs * ts, hd_all), w_o_b, res=xs)

        w_down = ffn_w_down[layer].astype(BF16)
        f = w_down.shape[0]
        act, st, act_s, st_s = ffn_up(rmsnorm(xp, ffn_norm[layer], BF16), rmsnorm(xs, ffn_norm[layer], BF16),
                                      ffn_w_in.astype(F32), layer, ffn_conv_w[layer], ffn_conv_b[layer],
                                      jnp.zeros((bp, CONV_W - 1, f), F32), state_ffn_conv[layer], tp, ts)
        conv_p.append(st)
        conv_s.append(st_s)
        xp = matmul(act, w_down, res=xp, tm=512, tn=256)
        xs = matmul(act_s, w_down, res=xs, tm=512, tn=256)

    y_p = rmsnorm(xp, final_norm, F32).reshape(bp, tp, d)
    y_s = rmsnorm(xs, final_norm, F32).reshape(bs, ts, d)
    return (y_p, y_s, jnp.stack(ssm_re_p), jnp.stack(ssm_im_p), jnp.stack(ssm_re_s), jnp.stack(ssm_im_s),
            jnp.stack(conv_p), jnp.stack(conv_s), outs["kv_rows_p"], outs["kv_rows_s"], outs["win_p"], outs["win_s"])


def kernel(x_prompt, x_sample, state_ssm_re, state_ssm_im, state_ffn_conv, cache_kv, cache_win, page_table, attn_norm, ffn_norm, final_norm, ssm_lam_re, ssm_lam_im, ssm_log_step, ssm_b_re, ssm_b_im, ssm_c_re, ssm_c_im, ssm_d, ssm_w_glu, ffn_w_in, ffn_conv_w, ffn_conv_b, ffn_w_down, kv_norm, w_kv, cmp_w1, cmp_b1, cmp_w2, cmp_b2, cmp_pe, w_qg, w_o):
    return _step(x_prompt, x_sample, state_ssm_re, state_ssm_im, state_ffn_conv, cache_kv, cache_win, page_table,
                 attn_norm, ffn_norm, final_norm, ssm_lam_re, ssm_lam_im, ssm_log_step, ssm_b_re, ssm_b_im,
                 ssm_c_re, ssm_c_im, ssm_d, ssm_w_glu, ffn_w_in, ffn_conv_w, ffn_conv_b, ffn_w_down,
                 kv_norm, w_kv, cmp_w1, cmp_b1, cmp_w2, cmp_b2, cmp_pe, w_qg, w_o)
```

```python
import functools
import math

import jax
import jax.numpy as jnp
from jax import lax
from jax.experimental import pallas as pl
from jax.experimental.pallas import tpu as pltpu

F32 = jnp.float32
BF16 = jnp.bfloat16
I32 = jnp.int32

RMS_EPS = 1e-6
LANES = 128
SUBLANES = 8
VMEM_LIMIT = 56 * 1024 * 1024

SSM_GROUP = 16
SSM_STATE = 64
CONV_W = 3
HEAD_DIM = 128
N_KV_HEADS = 4
CMP_STRIDE = 16
CMP_BLOCK = 32
SEL_BLOCK = 64
SEL_SHIFT = SEL_BLOCK.bit_length() - 1
N_SEL = 16
WINDOW = 512
Q_BLOCK = 128
PAGE_SIZE = 128
NEG_BIG = -1e30
M_FLOOR = 0.1 * NEG_BIG
LOG2_E = math.log2(math.e)
ONES_ROWS = 16

S5_SET_GROUPS = 4
S5_SETS = 8
S5_BLK_GROUPS = S5_SET_GROUPS * S5_SETS
S5_BLK_CH = S5_BLK_GROUPS * SSM_GROUP
S5_SET_LANES = S5_SET_GROUPS * SSM_STATE


def _cparams(sem, vmem=VMEM_LIMIT):
    return pltpu.CompilerParams(dimension_semantics=sem, vmem_limit_bytes=vmem)


def _rmsnorm_body(x_ref, g_ref, *o_refs):
    x = x_ref[...].astype(F32)
    y = x * lax.rsqrt(jnp.mean(x * x, axis=-1, keepdims=True) + RMS_EPS)
    for k, o_ref in enumerate(o_refs):
        o_ref[...] = (y * g_ref[k:k + 1, :]).astype(o_ref.dtype)


def rmsnorm(x, gains, out_dtype):
    m, d = x.shape
    tm = math.gcd(m, 256)
    n = len(gains)
    g = jnp.stack([gi.reshape(d).astype(F32) for gi in gains])
    outs = pl.pallas_call(
        _rmsnorm_body,
        grid=(m // tm,),
        in_specs=[pl.BlockSpec((tm, d), lambda i: (i, 0)), pl.BlockSpec((n, d), lambda i: (0, 0))],
        out_specs=[pl.BlockSpec((tm, d), lambda i: (i, 0))] * n,
        out_shape=[jax.ShapeDtypeStruct((m, d), out_dtype)] * n,
        compiler_params=_cparams(("parallel",)),
        name="rmsnorm",
    )(x, g)
    return outs


def _mm_body(*refs, has_res, act):
    if has_res:
        x_ref, w_ref, res_ref, o_ref = refs
    else:
        x_ref, w_ref, o_ref = refs
    tm = x_ref.shape[0]
    chunk = math.gcd(tm, 128)
    for r in range(0, tm, chunk):
        y = jnp.dot(x_ref[r:r + chunk, :], w_ref[...], preferred_element_type=F32)
        if act == "sigmoid":
            y = jax.nn.sigmoid(y)
        if has_res:
            y = res_ref[r:r + chunk, :] + y
        o_ref[r:r + chunk, :] = y.astype(o_ref.dtype)


def matmul(x, w, res=None, out_dtype=F32, act=None, tm=512, tn=256):
    m, k = x.shape
    n = w.shape[1]
    tm = math.gcd(m, tm)
    tn = math.gcd(n, tn)
    in_specs = [pl.BlockSpec((tm, k), lambda i, j: (i, 0)), pl.BlockSpec((k, tn), lambda i, j: (0, j))]
    args = [x, w]
    if res is not None:
        in_specs.append(pl.BlockSpec((tm, tn), lambda i, j: (i, j)))
        args.append(res)
    return pl.pallas_call(
        functools.partial(_mm_body, has_res=res is not None, act=act),
        grid=(m // tm, n // tn),
        in_specs=in_specs,
        out_specs=pl.BlockSpec((tm, tn), lambda i, j: (i, j)),
        out_shape=jax.ShapeDtypeStruct((m, n), out_dtype),
        compiler_params=_cparams(("parallel", "arbitrary")),
        name="matmul",
    )(*args)


def _mm_khalves_body(x_ref, w_ref, res_ref, o_ref, acc_ref):
    kk = pl.program_id(1)
    j = pl.program_id(2)
    tm = x_ref.shape[0]
    chunk = math.gcd(tm, 128)

    @pl.when(kk == 0)
    def _():
        for r in range(0, tm, chunk):
            acc_ref[j, r:r + chunk, :] = jnp.dot(x_ref[r:r + chunk, :], w_ref[...], preferred_element_type=F32)

    @pl.when(kk == 1)
    def _():
        for r in range(0, tm, chunk):
            y = jnp.dot(x_ref[r:r + chunk, :], w_ref[...], preferred_element_type=F32)
            o_ref[r:r + chunk, :] = res_ref[r:r + chunk, :] + acc_ref[j, r:r + chunk, :] + y


def matmul_khalves(x, w, res, tm=1024, tn=256):
    m, k = x.shape
    n = w.shape[1]
    tm = math.gcd(m, tm)
    tn = math.gcd(n, tn)
    tk = k // 2
    assert k % (2 * LANES) == 0
    return pl.pallas_call(
        _mm_khalves_body,
        grid=(m // tm, 2, n // tn),
        in_specs=[pl.BlockSpec((tm, tk), lambda i, kk, j: (i, kk)),
                  pl.BlockSpec((tk, tn), lambda i, kk, j: (kk, j)),
                  pl.BlockSpec((tm, tn), lambda i, kk, j: (i, j * kk))],
        out_specs=pl.BlockSpec((tm, tn), lambda i, kk, j: (i, j * kk)),
        out_shape=jax.ShapeDtypeStruct((m, n), F32),
        scratch_shapes=[pltpu.VMEM((n // tn, tm, tn), F32)],
        compiler_params=_cparams(("parallel", "arbitrary", "arbitrary")),
        name="matmul_khalves",
    )(x, w, res)


def _mm_split_body(x_ref, w_ref, o1_ref, o2_ref, *, n1_tiles):
    j = pl.program_id(1)
    y = jnp.dot(x_ref[...], w_ref[...], preferred_element_type=F32)

    @pl.when(j < n1_tiles)
    def _():
        o1_ref[...] = y

    @pl.when(j >= n1_tiles)
    def _():
        o2_ref[...] = y


def matmul_split(x, w, n1, tm=1024, tn=512):
    m, k = x.shape
    n = w.shape[1]
    tm = math.gcd(m, tm)
    tn = math.gcd(math.gcd(n1, n - n1), tn)
    t1 = n1 // tn
    return pl.pallas_call(
        functools.partial(_mm_split_body, n1_tiles=t1),
        grid=(m // tm, n // tn),
        in_specs=[pl.BlockSpec((tm, k), lambda i, j: (i, 0)), pl.BlockSpec((k, tn), lambda i, j: (0, j))],
        out_specs=[pl.BlockSpec((tm, tn), lambda i, j: (i, jnp.minimum(j, t1 - 1))),
                   pl.BlockSpec((tm, tn), lambda i, j: (i, jnp.maximum(j - t1, 0)))],
        out_shape=[jax.ShapeDtypeStruct((m, n1), F32), jax.ShapeDtypeStruct((m, n - n1), F32)],
        compiler_params=_cparams(("parallel", "arbitrary")),
        name="matmul_split",
    )(x, w)


def _glu_body(x_ref, wa_ref, wb_ref, res_ref, o_ref):
    tm = x_ref.shape[0]
    chunk = math.gcd(tm, 128)
    for r in range(0, tm, chunk):
        x = x_ref[r:r + chunk, :]
        a = jnp.dot(x, wa_ref[...], preferred_element_type=F32)
        b = jnp.dot(x, wb_ref[...], preferred_element_type=F32)
        o_ref[r:r + chunk, :] = res_ref[r:r + chunk, :] + a * jax.nn.sigmoid(b)


def glu_matmul(x, w, res, tm=1024, tn=256):
    m, k = x.shape
    n = w.shape[1] // 2
    tm = math.gcd(m, tm)
    tn = math.gcd(n, tn)
    nt = n // tn
    return pl.pallas_call(
        _glu_body,
        grid=(m // tm, nt),
        in_specs=[pl.BlockSpec((tm, k), lambda i, j: (i, 0)),
                  pl.BlockSpec((k, tn), lambda i, j: (0, j)),
                  pl.BlockSpec((k, tn), lambda i, j: (0, j + nt)),
                  pl.BlockSpec((tm, tn), lambda i, j: (i, j))],
        out_specs=pl.BlockSpec((tm, tn), lambda i, j: (i, j)),
        out_shape=jax.ShapeDtypeStruct((m, n), F32),
        compiler_params=_cparams(("parallel", "arbitrary")),
        name="glu_matmul",
    )(x, w, w, res)


def _conv_gate(g, g1, g2, cw_ref, cb_ref, val):
    gc = cb_ref[...] + cw_ref[0:1, :] * g2 + cw_ref[1:2, :] * g1 + cw_ref[2:3, :] * g
    return (gc * jax.nn.sigmoid(gc) * val).astype(BF16)


def _ffn_up_body(*refs, n_m, tiles_per_seq, short_len, chunk, n_slabs):
    x_slabs = refs[:n_slabs]
    (xs_ref, wv_ref, wg_ref, cw_ref, cb_ref, p1_ref, p2_ref, cwp_ref, cbp_ref, init_ref,
     act_ref, st_ref, acts_ref, gs_ref, wv_s, wg_s, val_a, g_a, val_b, g_b) = refs[n_slabs:]
    slab_rows = x_slabs[0].shape[0]
    s = pl.program_id(0)
    i = lax.rem(s, n_m)

    @pl.when(s == 0)
    def _():
        val_b[...] = jnp.zeros(val_b.shape, F32)
        g_b[...] = jnp.zeros(g_b.shape, F32)

    @pl.when(i == 0)
    def _():
        wv_s[...] = wv_ref[...].astype(BF16)
        wg_s[...] = wg_ref[...].astype(BF16)
        xs = xs_ref[...]
        val = jnp.dot(xs, wv_s[...], preferred_element_type=F32)
        g = jnp.dot(xs, wg_s[...], preferred_element_type=F32)
        tmod = lax.rem(lax.broadcasted_iota(I32, g.shape, 0), short_len)
        g1 = jnp.where(tmod >= 1, pltpu.roll(g, 1, axis=0), p1_ref[...])
        g2 = jnp.where(tmod >= 2, pltpu.roll(g, 2, axis=0), p2_ref[...])
        acts_ref[...] = _conv_gate(g, g1, g2, cw_ref, cb_ref, val)
        gs_ref[...] = g

    def step(val_w, g_w, val_r, g_r):
        tm = n_slabs * slab_rows
        for r in range(0, tm, chunk):
            x = x_slabs[r // slab_rows][r % slab_rows:r % slab_rows + chunk, :]
            val_w[r:r + chunk, :] = jnp.dot(x, wv_s[...], preferred_element_type=F32)
            g_w[SUBLANES + r:SUBLANES + r + chunk, :] = jnp.dot(x, wg_s[...], preferred_element_type=F32)
            gext = g_r[r:r + chunk + SUBLANES, :]
            g1 = pltpu.roll(gext, 1, axis=0)[SUBLANES:]
            g2 = pltpu.roll(gext, 2, axis=0)[SUBLANES:]
            act_ref[r:r + chunk, :] = _conv_gate(gext[SUBLANES:], g1, g2, cwp_ref, cbp_ref, val_r[r:r + chunk, :])
        tail = g_r[tm:, :]
        st_ref[0] = tail
        g_w[:SUBLANES, :] = jnp.where(lax.rem(i, tiles_per_seq) == 0, init_ref[0], tail)

    parity = lax.rem(s, 2)

    @pl.when(parity == 0)
    def _():
        step(val_a, g_a, val_b, g_b)

    @pl.when(parity == 1)
    def _():
        step(val_b, g_b, val_a, g_a)


def ffn_up(h, hs, w_in_all, layer, conv_w, conv_b, buf, buf_s, seq_len, seq_short, tm=1024, tn=256, n_slabs=4):
    m, k = h.shape
    ms = hs.shape[0]
    f = w_in_all.shape[2] // 2
    bsz = m // seq_len
    bs = ms // seq_short
    assert seq_len % SUBLANES == 0 and seq_len >= 2 * SUBLANES and seq_short >= 2
    tn = math.gcd(f, tn)
    n_n = f // tn
    tm = math.gcd(seq_len, tm)
    tps = seq_len // tm
    n_m = m // tm
    chunk = math.gcd(tm, 128)
    n_slabs = math.gcd(tm // chunk, n_slabs)
    slab = tm // n_slabs
    cw = jnp.zeros((SUBLANES, f), F32).at[:CONV_W].set(conv_w.astype(F32))
    cb = conv_b.reshape(1, f).astype(F32)
    init = jnp.zeros((bsz, SUBLANES, f), F32).at[:, SUBLANES - 2:].set(buf.astype(F32))
    buf_s = buf_s.astype(F32)
    pos = jnp.arange(ms) % seq_short
    b0 = jnp.repeat(buf_s[:, 0], seq_short, axis=0)
    b1 = jnp.repeat(buf_s[:, 1], seq_short, axis=0)
    p1 = jnp.where((pos == 0)[:, None], b1, 0.0)
    p2 = jnp.where((pos == 0)[:, None], b0, jnp.where((pos == 1)[:, None], b1, 0.0))

    def col(s):
        return jnp.minimum(s // n_m, n_n - 1)

    def prev(s):
        sp = jnp.maximum(s - 1, 0)
        return sp % n_m, sp // n_m

    act, st, act_s, g_s = pl.pallas_call(
        functools.partial(_ffn_up_body, n_m=n_m, tiles_per_seq=tps, short_len=seq_short, chunk=chunk, n_slabs=n_slabs),
        grid=(n_n * n_m + 1,),
        in_specs=[pl.BlockSpec((slab, k), lambda s, c=c: ((s % n_m) * n_slabs + c, 0)) for c in range(n_slabs)] + [
                  pl.BlockSpec((ms, k), lambda s: (0, 0)),
                  pl.BlockSpec((None, k, tn), lambda s: (layer, 0, col(s))),
                  pl.BlockSpec((None, k, tn), lambda s: (layer, 0, col(s) + n_n)),
                  pl.BlockSpec((SUBLANES, tn), lambda s: (0, col(s))),
                  pl.BlockSpec((1, tn), lambda s: (0, col(s))),
                  pl.BlockSpec((ms, tn), lambda s: (0, col(s))),
                  pl.BlockSpec((ms, tn), lambda s: (0, col(s))),
                  pl.BlockSpec((SUBLANES, tn), lambda s: (0, prev(s)[1])),
                  pl.BlockSpec((1, tn), lambda s: (0, prev(s)[1])),
                  pl.BlockSpec((1, SUBLANES, tn), lambda s: ((s % n_m) // tps, 0, col(s)))],
        out_specs=[pl.BlockSpec((tm, tn), lambda s: prev(s)),
                   pl.BlockSpec((1, SUBLANES, tn), lambda s: (prev(s)[0], 0, prev(s)[1])),
                   pl.BlockSpec((ms, tn), lambda s: (0, col(s))),
                   pl.BlockSpec((ms, tn), lambda s: (0, col(s)))],
        out_shape=[jax.ShapeDtypeStruct((m, f), BF16), jax.ShapeDtypeStruct((n_m, SUBLANES, f), F32),
                   jax.ShapeDtypeStruct((ms, f), BF16), jax.ShapeDtypeStruct((ms, f), F32)],
        scratch_shapes=[pltpu.VMEM((k, tn), BF16), pltpu.VMEM((k, tn), BF16)]
                       + [pltpu.VMEM((tm, tn), F32), pltpu.VMEM((tm + SUBLANES, tn), F32)] * 2,
        compiler_params=_cparams(("arbitrary",)),
        name="ffn_up",
    )(*([h] * n_slabs), hs, w_in_all, w_in_all, cw, cb, p1, p2, cw, cb, init)
    return (act, st[tps - 1::tps, SUBLANES - 2:], act_s, g_s.reshape(bs, seq_short, f)[:, seq_short - 2:])


def _s5_body(*refs, tc_len, last_step, nb):
    (u_ref, wb_ref, wc_ref, lre_ref, lim_ref, d_ref, h0re_ref, h0im_ref, z_ref, hre_ref, him_ref) = refs[:11]
    halves = S5_SET_LANES // LANES
    n_buf = 2 * halves
    bufs = refs[11:11 + nb * n_buf]
    hst_re, hst_im = refs[11 + nb * n_buf:]
    tci = pl.program_id(2)

    def sre(b, c):
        return bufs[b * n_buf + c]

    def sim(b, c):
        return bufs[b * n_buf + halves + c]

    def set_rows(j):
        return pl.ds(j, tc_len, stride=S5_SETS)

    for b in range(nb):
        for ti in range(S5_BLK_CH // LANES):
            ut = u_ref[b, :, ti * LANES:(ti + 1) * LANES]
            hi = ut.astype(BF16)
            lo = (ut - hi.astype(F32)).astype(BF16)
            lhs = jnp.concatenate([hi, lo], axis=1)
            for jj in range(2):
                j = ti * 2 + jj
                bre = jnp.dot(lhs, wb_ref[0, j, 0], preferred_element_type=F32)
                bim = jnp.dot(lhs, wb_ref[0, j, 1], preferred_element_type=F32)
                for c in range(halves):
                    sre(b, c)[set_rows(j), :] = bre[:, c * LANES:(c + 1) * LANES]
                    sim(b, c)[set_rows(j), :] = bim[:, c * LANES:(c + 1) * LANES]

    @pl.when(tci == 0)
    def _():
        for b in range(nb):
            hst_re[b] = h0re_ref[b, 0]
            hst_im[b] = h0im_ref[b, 0]

    lam_r = lre_ref[0]
    lam_i = lim_ref[0]
    lr = [lam_r[:, c * LANES:(c + 1) * LANES] for c in range(halves)]
    li = [lam_i[:, c * LANES:(c + 1) * LANES] for c in range(halves)]

    def step(t, carry):
        r0 = pl.multiple_of(t * S5_SETS, S5_SETS)
        out = []
        for b in range(nb):
            for c in range(halves):
                hr, hi_ = carry[2 * (b * halves + c)], carry[2 * (b * halves + c) + 1]
                nr = lr[c] * hr - li[c] * hi_ + sre(b, c)[pl.ds(r0, S5_SETS), :]
                ni = lr[c] * hi_ + li[c] * hr + sim(b, c)[pl.ds(r0, S5_SETS), :]
                sre(b, c)[pl.ds(r0, S5_SETS), :] = nr
                sim(b, c)[pl.ds(r0, S5_SETS), :] = ni
                out += [nr, ni]
        return tuple(out)

    init = []
    for b in range(nb):
        h_in = hst_re[b]
        g_in = hst_im[b]
        for c in range(halves):
            init += [h_in[:, c * LANES:(c + 1) * LANES], g_in[:, c * LANES:(c + 1) * LANES]]
    fin = lax.fori_loop(0, tc_len, step, tuple(init), unroll=8)
    for b in range(nb):
        for c in range(halves):
            hst_re[b, :, c * LANES:(c + 1) * LANES] = fin[2 * (b * halves + c)]
            hst_im[b, :, c * LANES:(c + 1) * LANES] = fin[2 * (b * halves + c) + 1]

    @pl.when(tci == last_step // tc_len)
    def _():
        r0 = (last_step % tc_len) * S5_SETS
        for b in range(nb):
            for c in range(halves):
                hre_ref[b, 0, :, c * LANES:(c + 1) * LANES] = sre(b, c)[pl.ds(r0, S5_SETS), :]
                him_ref[b, 0, :, c * LANES:(c + 1) * LANES] = sim(b, c)[pl.ds(r0, S5_SETS), :]

    for b in range(nb):
        for c in range(S5_BLK_CH // LANES):
            parts = []
            for s in range(2):
                j = 2 * c + s
                parts += [sre(b, k)[set_rows(j), :].astype(BF16) for k in range(halves)]
                parts += [sim(b, k)[set_rows(j), :].astype(BF16) for k in range(halves)]
            lhs = jnp.concatenate(parts, axis=1)
            y = jnp.dot(lhs, wc_ref[0, c], preferred_element_type=F32)
            y = y + d_ref[:, c * LANES:(c + 1) * LANES] * u_ref[b, :, c * LANES:(c + 1) * LANES]
            z_ref[b, :, c * LANES:(c + 1) * LANES] = jax.nn.gelu(y).astype(BF16)


def _s5_weights(lam_re, lam_im, log_step, b_re, b_im, c_re, c_im):
    g, p = lam_re.shape
    n = SSM_GROUP
    nblk = g // S5_BLK_GROUPS
    lam = lax.complex(lam_re.astype(F32), lam_im.astype(F32))
    dt = jnp.exp(log_step.astype(F32))[:, None]
    lam_bar = jnp.exp(lam * dt)
    b_bar = ((lam_bar - 1.0) / lam)[..., None] * lax.complex(b_re.astype(F32), b_im.astype(F32))
    eye_q = jnp.eye(S5_SET_GROUPS, dtype=F32)
    eye_h = jnp.eye(2, dtype=F32)

    def b_operand(bm):
        t = bm.reshape(nblk, S5_SETS // 2, 2, S5_SET_GROUPS, p, n)
        w = jnp.einsum("bthqpn,hk,qr->bthkrnqp", t, eye_h, eye_q)
        w = w.reshape(nblk, S5_SETS, LANES, S5_SET_LANES)
        return w

    wb = jnp.stack([b_operand(jnp.real(b_bar)), b_operand(jnp.imag(b_bar))], axis=2)
    wb = wb.astype(BF16)
    wb = jnp.concatenate([wb, wb], axis=3)

    def c_operand(cm):
        t = cm.reshape(nblk, S5_SETS // 2, 2, S5_SET_GROUPS, n, p)
        return jnp.einsum("bcsqnp,st,qr->bcsqptrn", t, eye_h, eye_q)

    cre = c_operand(c_re.astype(F32))
    cim = c_operand(-c_im.astype(F32))
    wc = jnp.stack([cre, cim], axis=3).reshape(nblk, S5_SETS // 2, 4 * S5_SET_LANES, LANES).astype(BF16)
    lre = jnp.real(lam_bar).reshape(nblk, S5_SETS, S5_SET_LANES)
    lim = jnp.imag(lam_bar).reshape(nblk, S5_SETS, S5_SET_LANES)
    return wb, wc, lre, lim


def s5_scan(u, h0_re, h0_im, weights, d_skip, seq_valid, tc_len):
    wb, wc, lre, lim = weights
    bsz, lp, d = u.shape
    nblk = d // S5_BLK_CH
    n_tc = lp // tc_len
    h0_re = h0_re.astype(F32).reshape(bsz, nblk, S5_SETS, S5_SET_LANES)
    h0_im = h0_im.astype(F32).reshape(bsz, nblk, S5_SETS, S5_SET_LANES)
    nb = math.gcd(bsz, 2)
    state_spec = pl.BlockSpec((nb, 1, S5_SETS, S5_SET_LANES), lambda k, b, t: (b, k, 0, 0))
    z, hre, him = pl.pallas_call(
        functools.partial(_s5_body, tc_len=tc_len, last_step=seq_valid - 1, nb=nb),
        grid=(nblk, bsz // nb, n_tc),
        in_specs=[pl.BlockSpec((nb, tc_len, S5_BLK_CH), lambda k, b, t: (b, t, k)),
                  pl.BlockSpec((1, S5_SETS, 2, 2 * LANES, S5_SET_LANES), lambda k, b, t: (k, 0, 0, 0, 0)),
                  pl.BlockSpec((1, S5_SETS // 2, 4 * S5_SET_LANES, LANES), lambda k, b, t: (k, 0, 0, 0)),
                  pl.BlockSpec((1, S5_SETS, S5_SET_LANES), lambda k, b, t: (k, 0, 0)),
                  pl.BlockSpec((1, S5_SETS, S5_SET_LANES), lambda k, b, t: (k, 0, 0)),
                  pl.BlockSpec((1, S5_BLK_CH), lambda k, b, t: (0, k)),
                  state_spec, state_spec],
        out_specs=[pl.BlockSpec((nb, tc_len, S5_BLK_CH), lambda k, b, t: (b, t, k)), state_spec, state_spec],
        out_shape=[jax.ShapeDtypeStruct((bsz, lp, d), BF16),
                   jax.ShapeDtypeStruct((bsz, nblk, S5_SETS, S5_SET_LANES), F32),
                   jax.ShapeDtypeStruct((bsz, nblk, S5_SETS, S5_SET_LANES), F32)],
        scratch_shapes=[pltpu.VMEM((tc_len * S5_SETS, LANES), F32)] * (nb * 2 * S5_SET_LANES // LANES) + [
                        pltpu.VMEM((nb, S5_SETS, S5_SET_LANES), F32),
                        pltpu.VMEM((nb, S5_SETS, S5_SET_LANES), F32)],
        compiler_params=_cparams(("arbitrary", "arbitrary", "arbitrary")),
        name="s5_scan",
    )(u, wb, wc, lre, lim, d_skip.reshape(1, d).astype(F32), h0_re, h0_im)
    g = d // SSM_GROUP
    return z, hre.reshape(bsz, g, SSM_STATE), him.reshape(bsz, g, SSM_STATE)


def _cmp_proj_body(pt_ref, x_hbm, w_ref, o_ref, stage, sems, *, pages_per_step):
    n_heads2 = 2 * N_KV_HEADS
    chunks = PAGE_SIZE // CMP_STRIDE
    step = pl.program_id(0) * pl.num_programs(1) + pl.program_id(1)
    n_steps = pl.num_programs(0) * pl.num_programs(1)
    slot = lax.rem(step, 2)

    def page_copy(st, sl, p, c):
        page = pt_ref[st * pages_per_step + p]
        return pltpu.make_async_copy(x_hbm.at[page, :, c, :], stage.at[sl, p, c], sems.at[sl])

    def start_step(st, sl):
        for p in range(pages_per_step):
            for c in range(n_heads2):
                page_copy(st, sl, p, c).start()

    @pl.when(step == 0)
    def _():
        start_step(step, slot)

    @pl.when(step + 1 < n_steps)
    def _():
        start_step(step + 1, 1 - slot)

    for p in range(pages_per_step):
        for c in range(n_heads2):
            page_copy(step, slot, p, c).wait()

    def chunk_rows(c, j):
        return jnp.concatenate([stage[slot, p, c, pl.ds(j, chunks, stride=CMP_STRIDE), :]
                                for p in range(pages_per_step)], axis=0)

    for c in range(n_heads2):
        s, h = divmod(c, N_KV_HEADS)
        acc = None
        for jp in range(CMP_STRIDE // 2):
            lhs = jnp.concatenate([chunk_rows(c, 2 * jp), chunk_rows(c, 2 * jp + 1)], axis=1).astype(BF16)
            t = jnp.dot(lhs, w_ref[s, jp], preferred_element_type=F32)
            acc = t if acc is None else acc + t
        o_ref[0, s, h] = acc


def cmp_project(rows, page_table, w1cat, pages_per_step=16):
    bsz, npg = page_table.shape
    pps = math.gcd(npg, pages_per_step)
    chunks = PAGE_SIZE // CMP_STRIDE
    hid2 = w1cat.shape[-1]
    grid_spec = pltpu.PrefetchScalarGridSpec(
        num_scalar_prefetch=1,
        grid=(bsz, npg // pps),
        in_specs=[pl.BlockSpec(memory_space=pl.ANY),
                  pl.BlockSpec(w1cat.shape, lambda b, g, pt: (0, 0, 0, 0))],
        out_specs=pl.BlockSpec((1, 2, N_KV_HEADS, pps * chunks, hid2), lambda b, g, pt: (b, 0, 0, g, 0)),
        scratch_shapes=[pltpu.VMEM((2, pps, 2 * N_KV_HEADS, PAGE_SIZE, HEAD_DIM), F32),
                        pltpu.SemaphoreType.DMA((2,))],
    )
    return pl.pallas_call(
        functools.partial(_cmp_proj_body, pages_per_step=pps),
        grid_spec=grid_spec,
        out_shape=jax.ShapeDtypeStruct((bsz, 2, N_KV_HEADS, npg * chunks, hid2), F32),
        compiler_params=_cparams(("arbitrary", "arbitrary")),
        name="cmp_project",
    )(page_table.reshape(-1).astype(I32), rows, w1cat)


def _cmp_finish_body(ab_ref, pe_ref, w1_ref, b1_ref, w2_ref, b2_ref, o_ref):
    ab = ab_ref[0, 0, 0]
    n16 = ab.shape[0]
    hid = ab.shape[1] // 2
    a = ab[:, :hid]
    bnext = pltpu.roll(ab[:, hid:], n16 - 1, axis=0)
    c = jnp.dot(pe_ref[0], w1_ref[0], preferred_element_type=F32)[0:1, :] + b1_ref[0]
    pre = a + bnext + c
    y = jnp.dot(jax.nn.gelu(pre).astype(BF16), w2_ref[0], preferred_element_type=F32) + b2_ref[0]
    row = lax.broadcasted_iota(I32, y.shape, 0)
    o_ref[0, 0, 0] = jnp.where(row < n16 - 1, y, 0.0).astype(o_ref.dtype)


def cmp_finish(ab, pe, w1, b1, w2, b2):
    bsz, _, kvh, n16, hid2 = ab.shape
    hid = hid2 // 2
    pe_rows = jnp.broadcast_to(pe.reshape(2, 1, -1), (2, SUBLANES, pe.shape[1] * pe.shape[2])).astype(BF16)
    return pl.pallas_call(
        _cmp_finish_body,
        grid=(2, bsz, kvh),
        in_specs=[pl.BlockSpec((1, 1, 1, n16, hid2), lambda s, b, h: (b, s, h, 0, 0)),
                  pl.BlockSpec((1, SUBLANES, pe_rows.shape[2]), lambda s, b, h: (s, 0, 0)),
                  pl.BlockSpec((1,) + w1.shape[1:], lambda s, b, h: (s, 0, 0)),
                  pl.BlockSpec((1, 1, hid), lambda s, b, h: (s, 0, 0)),
                  pl.BlockSpec((1, hid, HEAD_DIM), lambda s, b, h: (s, 0, 0)),
                  pl.BlockSpec((1, 1, HEAD_DIM), lambda s, b, h: (s, 0, 0))],
        out_specs=pl.BlockSpec((1, 1, 1, n16, HEAD_DIM), lambda s, b, h: (b, s, h, 0, 0)),
        out_shape=jax.ShapeDtypeStruct((bsz, 2, kvh, n16, HEAD_DIM), BF16),
        compiler_params=_cparams(("arbitrary", "arbitrary", "arbitrary")),
        name="cmp_finish",
    )(ab, pe_rows, w1.astype(BF16), b1.reshape(2, 1, hid).astype(F32), w2.astype(BF16),
      b2.reshape(2, 1, HEAD_DIM).astype(F32))


def _masked_softmax(s, mask):
    sm = jnp.where(mask, s, NEG_BIG)
    m = jnp.max(sm, axis=-1, keepdims=True)
    m = jnp.where(m > 0.5 * NEG_BIG, m, 0.0)
    e = jnp.where(mask, jnp.exp(sm - m), 0.0)
    den = jnp.sum(e, axis=-1, keepdims=True)
    return e / jnp.where(den > 0, den, 1.0)


def _dot_nt(a, b):
    return lax.dot_general(a, b, (((1,), (1,)), ((), ())), preferred_element_type=F32)


def _overlap(n_c, n_j, c_axis):
    shape = (n_c, n_j) if c_axis == 0 else (n_j, n_c)
    c = lax.broadcasted_iota(I32, shape, c_axis) * CMP_STRIDE
    j = lax.broadcasted_iota(I32, shape, 1 - c_axis) * SEL_BLOCK
    return ((c < j + SEL_BLOCK) & (c + CMP_BLOCK > j)).astype(F32)


def _block_scores(imp, q_pos, n_blk):
    blk = lax.broadcasted_iota(I32, imp.shape, 1)
    q_blk = jnp.right_shift(q_pos, SEL_SHIFT)
    forced = (blk == 0) | (blk == q_blk) | (blk == q_blk - 1)
    future = blk * SEL_BLOCK > q_pos
    v = jnp.where(future, NEG_BIG, jnp.where(forced, -NEG_BIG, imp))
    return jnp.where(blk < n_blk, v, 2.0 * NEG_BIG)


def _rank_desc(v, n):
    lane = lax.broadcasted_iota(I32, v.shape, 1)
    rank = jnp.zeros(v.shape, F32)
    for i in range(n):
        vi = v[:, i:i + 1]
        before = (vi > v) | ((vi == v) & (lane > i))
        rank = rank + jnp.where(before, 1.0, 0.0)
    return rank


def _nsa_prompt_body(q_ref, gt_ref, ck_ref, cvt_ref, ks_ref, vs_ref, kw_ref, vw_ref, o_ref, m_ref, acc_ref,
                     vst_ref, vwt_ref, *, n_cmp, n_blk, gqa, sel_tile, win_keys):
    qb = pl.program_id(2)
    nq = Q_BLOCK
    gq = gqa * nq
    c2 = (HEAD_DIM ** -0.5) * LOG2_E

    @pl.when(qb == 0)
    def _():
        ones = jnp.ones((ONES_ROWS, LANES), BF16)
        for src, dst in ((vs_ref, vst_ref), (vw_ref, vwt_ref)):
            for t in range(dst.shape[0]):
                dst[t, :HEAD_DIM, :] = src[0, t * LANES:(t + 1) * LANES, :].T.astype(BF16)
                dst[t, HEAD_DIM:, :] = ones

    qt = jnp.concatenate([q_ref[0, :, g * HEAD_DIM:(g + 1) * HEAD_DIM].astype(F32).T.astype(BF16)
                          for g in range(gqa)], axis=1)
    q_pos = qb * nq + lax.broadcasted_iota(I32, (1, nq), 1)

    def per_head(x):
        return jnp.concatenate([x] * gqa, axis=1)

    def normalise(acc):
        den = acc[HEAD_DIM:HEAD_DIM + 1, :]
        return acc[:HEAD_DIM, :] * (1.0 / jnp.where(den > 0, den, 1.0))

    ck = ck_ref[0, 0, 0]
    n16 = ck.shape[0]
    s = jnp.dot(ck, qt, preferred_element_type=F32)
    cidx = lax.broadcasted_iota(I32, (n16, 1), 0)
    cbias = jnp.where((cidx * CMP_STRIDE + (CMP_BLOCK - 1) <= q_pos) & (cidx < n_cmp), 0.0, NEG_BIG)
    sb = s + per_head(cbias)
    m = jnp.maximum(jnp.max(sb, axis=0, keepdims=True), M_FLOOR)
    e = jnp.exp2((sb - m) * c2)
    den = jnp.sum(e, axis=0, keepdims=True)
    p = e * jnp.where(den > 0, 1.0 / den, 0.0)
    o_cmp = jnp.dot(cvt_ref[0, 0], p.astype(BF16), preferred_element_type=F32)

    psum = p[:, 0:nq]
    for g in range(1, gqa):
        psum = psum + p[:, g * nq:(g + 1) * nq]
    imp = jnp.dot(_overlap(n16, LANES, 1), psum, preferred_element_type=F32, precision=lax.Precision.HIGHEST)
    blk = lax.broadcasted_iota(I32, (LANES, 1), 0)
    q_blk = jnp.right_shift(q_pos, SEL_SHIFT)
    forced = (blk == 0) | (blk == q_blk) | (blk == q_blk - 1)
    v = jnp.where(blk * SEL_BLOCK > q_pos, NEG_BIG, jnp.where(forced, -NEG_BIG, imp))
    v = jnp.where(blk < n_blk, v, 2.0 * NEG_BIG)
    rank = jnp.zeros(v.shape, F32)
    for i in range(n_blk):
        vi = v[i:i + 1, :]
        rank = rank + jnp.where((vi > v) | ((vi == v) & (blk > i)), 1.0, 0.0)
    sel_bias = jnp.where((rank < min(N_SEL, n_blk)) & (blk < n_blk), 0.0, NEG_BIG).astype(BF16)

    m_ref[...] = jnp.full(m_ref.shape, M_FLOOR, F32)
    acc_ref[...] = jnp.zeros(acc_ref.shape, F32)
    tiles = sel_tile // LANES

    def sel_step(kt, _):
        k0 = pl.multiple_of(kt * sel_tile, sel_tile)
        s = jnp.dot(ks_ref[0, pl.ds(k0, sel_tile), :].astype(BF16), qt, preferred_element_type=F32)
        key = k0 + lax.broadcasted_iota(I32, (sel_tile, 1), 0)
        expand = jnp.where(jnp.right_shift(key, SEL_SHIFT) == lax.broadcasted_iota(I32, (sel_tile, LANES), 1), 1.0, 0.0)
        bias = jnp.dot(expand.astype(BF16), sel_bias, preferred_element_type=F32)
        bias = jnp.where(key <= q_pos, bias, NEG_BIG)
        sb = s + per_head(bias)
        m_old = m_ref[0:1, :]
        m_new = jnp.maximum(m_old, jnp.max(sb, axis=0, keepdims=True))
        alpha = jnp.exp2((m_old - m_new) * c2)
        e = jnp.exp2((sb - m_new) * c2).astype(BF16)
        vt = jnp.concatenate([vst_ref[kt * tiles + i] for i in range(tiles)], axis=1)
        acc_ref[...] = alpha * acc_ref[...] + jnp.dot(vt, e, preferred_element_type=F32)
        m_ref[...] = jnp.broadcast_to(m_new, m_ref.shape)
        return 0

    lax.fori_loop(0, (qb * nq + nq + sel_tile - 1) // sel_tile, sel_step, 0)
    o_sel = normalise(acc_ref[...])

    w_tile = jnp.maximum(qb + 1 - win_keys // nq, 0)
    w0 = pl.multiple_of(w_tile * nq, nq)
    s = jnp.dot(kw_ref[0, pl.ds(w0, win_keys), :].astype(BF16), qt, preferred_element_type=F32)
    key = w0 + lax.broadcasted_iota(I32, (win_keys, 1), 0)
    wbias = jnp.where((key <= q_pos) & (key > q_pos - WINDOW), 0.0, NEG_BIG)
    sb = s + per_head(wbias)
    m = jnp.maximum(jnp.max(sb, axis=0, keepdims=True), M_FLOOR)
    e = jnp.exp2((sb - m) * c2).astype(BF16)
    vt = jnp.concatenate([vwt_ref[w_tile + i] for i in range(win_keys // LANES)], axis=1)
    o_win = normalise(jnp.dot(vt, e, preferred_element_type=F32))

    gt = gt_ref[0, 0, 0]
    for g in range(gqa):
        cols = slice(g * nq, (g + 1) * nq)
        o = (gt[3 * g:3 * g + 1, :] * o_cmp[:, cols] + gt[3 * g + 1:3 * g + 2, :] * o_sel[:, cols]
             + gt[3 * g + 2:3 * g + 3, :] * o_win[:, cols])
        o_ref[0, :, g * HEAD_DIM:(g + 1) * HEAD_DIM] = o.T.astype(o_ref.dtype)


def nsa_prompt_attend(q, gates, cmp, k_sel, v_sel, k_win, v_win, bsz, t, sel_tile=512):
    hd_all = q.shape[1]
    kvh = N_KV_HEADS
    gqa = hd_all // HEAD_DIM // kvh
    n16 = cmp.shape[3]
    n_blk = -(-t // SEL_BLOCK)
    nqb = t // Q_BLOCK
    gq = gqa * Q_BLOCK
    vrows = HEAD_DIM + ONES_ROWS
    assert t % Q_BLOCK == 0 and n_blk <= LANES and n16 * CMP_STRIDE == t
    sel_tile = math.gcd(t, sel_tile)
    win_keys = min(WINDOW + Q_BLOCK, t)
    gt = gates[:, :3 * kvh * gqa].reshape(bsz, nqb, Q_BLOCK, kvh, 3 * gqa).transpose(0, 3, 1, 4, 2)
    gt = jnp.pad(gt, ((0, 0), (0, 0), (0, 0), (0, -3 * gqa % SUBLANES), (0, 0)))
    cvt = cmp[:, 1].transpose(0, 1, 3, 2)
    (ks_arr, ks_col), (vs_arr, vs_col), (kw_arr, kw_col), (vw_arr, vw_col) = k_sel, v_sel, k_win, v_win
    k_spec = lambda col: pl.BlockSpec((1, t, HEAD_DIM), lambda b, h, i, col=col: (b, 0, col + h))
    return pl.pallas_call(
        functools.partial(_nsa_prompt_body, n_cmp=n16 - 1, n_blk=n_blk, gqa=gqa, sel_tile=sel_tile, win_keys=win_keys),
        grid=(bsz, kvh, nqb),
        in_specs=[pl.BlockSpec((1, Q_BLOCK, gqa * HEAD_DIM), lambda b, h, i: (b, i, h)),
                  pl.BlockSpec((1, 1, 1, gt.shape[3], Q_BLOCK), lambda b, h, i: (b, h, i, 0, 0)),
                  pl.BlockSpec((1, 1, 1, n16, HEAD_DIM), lambda b, h, i: (b, 0, h, 0, 0)),
                  pl.BlockSpec((1, 1, HEAD_DIM, n16), lambda b, h, i: (b, h, 0, 0)),
                  k_spec(ks_col), k_spec(vs_col), k_spec(kw_col), k_spec(vw_col)],
        out_specs=pl.BlockSpec((1, Q_BLOCK, gqa * HEAD_DIM), lambda b, h, i: (b, i, h)),
        out_shape=jax.ShapeDtypeStruct((bsz, t, hd_all), BF16),
        scratch_shapes=[pltpu.VMEM((SUBLANES, gq), F32), pltpu.VMEM((vrows, gq), F32),
                        pltpu.VMEM((t // LANES, vrows, LANES), BF16), pltpu.VMEM((t // LANES, vrows, LANES), BF16)],
        compiler_params=_cparams(("parallel", "parallel", "arbitrary")),
        name="nsa_prompt",
    )(q.reshape(bsz, t, hd_all), gt, cmp, cvt, ks_arr.reshape(bsz, t, -1), vs_arr.reshape(bsz, t, -1),
      kw_arr.reshape(bsz, t, -1), vw_arr.reshape(bsz, t, -1)).reshape(bsz * t, hd_all)


def _nsa_sample_select_body(q_ref, ck_ref, cv_ref, ocmp_ref, idx_ref, *, n_cmp, n_blk, blk_lanes, past_len, tq_pad, gqa):
    scale = HEAD_DIM ** -0.5
    q = jnp.concatenate([q_ref[0, :, g * HEAD_DIM:(g + 1) * HEAD_DIM] for g in range(gqa)], axis=0)
    ck = ck_ref[0, 0, 0]
    cv = cv_ref[0, 0, 0]
    n16 = ck.shape[0]
    q_pos = past_len + lax.broadcasted_iota(I32, (tq_pad, 1), 0)
    s = (_dot_nt(q, ck) * scale).reshape(gqa, tq_pad, n16)
    cidx = lax.broadcasted_iota(I32, (tq_pad, n16), 1)
    cmask = (cidx * CMP_STRIDE + (CMP_BLOCK - 1) <= q_pos) & (cidx < n_cmp)
    p = _masked_softmax(s, cmask[None])
    ocmp_ref[0, 0] = jnp.dot(p.reshape(gqa * tq_pad, n16).astype(BF16), cv, preferred_element_type=F32)
    psum = jnp.sum(p, axis=0)
    imp = jnp.dot(psum, _overlap(n16, blk_lanes, 0), preferred_element_type=F32, precision=lax.Precision.HIGHEST)
    rank = _rank_desc(_block_scores(imp, q_pos, n_blk), n_blk)
    lane = lax.broadcasted_iota(I32, rank.shape, 1)
    lane_f = lane.astype(F32)
    out_lane = lax.broadcasted_iota(I32, (tq_pad, LANES), 1)
    idx = jnp.zeros((tq_pad, LANES), F32)
    for k in range(min(N_SEL, n_blk)):
        hit = (rank == float(k)) & (lane < n_blk)
        idx_k = jnp.sum(jnp.where(hit, lane_f, 0.0), axis=-1, keepdims=True)
        idx = jnp.where(out_lane == k, idx_k, idx)
    idx_ref[0, 0] = idx.astype(I32)


def _nsa_sample_attend_body(idx_ref, pt_ref, q_ref, gt_ref, ocmp_ref, kwc_ref, vwc_ref,
                            kwn_ref, vwn_ref, cache_ref, newblk_ref, o_ref, kbuf, vbuf, kwbuf, vwbuf, sems,
                            *, n_sel, n_cache_blk, n_pages, past_len, win_buf, tq, tq_pad, gqa):
    b = pl.program_id(0)
    h = pl.program_id(1)
    scale = HEAD_DIM ** -0.5
    blk_per_page = PAGE_SIZE // SEL_BLOCK

    bufs = ((2, kbuf, 0), (3, vbuf, 1))

    def from_new(t, k, slot, buf, sem):
        return pltpu.make_async_copy(newblk_ref.at[b, :, slot * N_KV_HEADS + h], buf.at[t, k], sems.at[sem])

    for t in range(tq):
        for k in range(n_sel):
            blk = idx_ref[((b * N_KV_HEADS + h) * tq + t) * n_sel + k]
            in_cache = blk < n_cache_blk
            blk_c = jnp.minimum(blk, n_cache_blk - 1)
            page = pt_ref[b * n_pages + lax.div(blk_c, blk_per_page)]
            off = pl.multiple_of(lax.rem(blk_c, blk_per_page) * SEL_BLOCK, SEL_BLOCK)
            for slot, buf, sem in bufs:
                @pl.when(in_cache)
                def _():
                    pltpu.make_async_copy(cache_ref.at[page, pl.ds(off, SEL_BLOCK), slot * N_KV_HEADS + h],
                                          buf.at[t, k], sems.at[sem]).start()

                @pl.when(jnp.logical_not(in_cache))
                def _():
                    from_new(t, k, slot, buf, sem).start()
    q = jnp.concatenate([q_ref[0, :, g * HEAD_DIM:(g + 1) * HEAD_DIM] for g in range(gqa)], axis=0)
    rows = gqa * tq_pad
    row_t = lax.rem(lax.broadcasted_iota(I32, (rows, 1), 0), tq_pad)
    q_pos = past_len + row_t

    kwbuf[...] = jnp.zeros(kwbuf.shape, F32)
    vwbuf[...] = jnp.zeros(vwbuf.shape, F32)
    kwbuf[0:win_buf, :] = kwc_ref[0]
    vwbuf[0:win_buf, :] = vwc_ref[0]
    kwbuf[win_buf:win_buf + tq_pad, :] = kwn_ref[0]
    vwbuf[win_buf:win_buf + tq_pad, :] = vwn_ref[0]
    nw = kwbuf.shape[0]
    wlane = lax.broadcasted_iota(I32, (rows, nw), 1)
    wpos = past_len - win_buf + wlane
    wmask = (wpos <= q_pos) & (wpos > q_pos - WINDOW) & (wpos >= 0) & (wlane < win_buf + tq)
    s = _dot_nt(q, kwbuf[...].astype(BF16)) * scale
    p = _masked_softmax(s, wmask)
    o_win = jnp.dot(p.astype(BF16), vwbuf[...].astype(BF16), preferred_element_type=F32)

    for t in range(tq):
        for k in range(n_sel):
            for slot, buf, sem in bufs:
                from_new(t, k, slot, buf, sem).wait()

    o_sel = jnp.zeros((rows, HEAD_DIM), F32)
    nk = n_sel * SEL_BLOCK
    klane = lax.broadcasted_iota(I32, (rows, nk), 1)
    for t in range(tq):
        kpos = jnp.zeros((rows, nk), I32)
        for k in range(n_sel):
            blk = idx_ref[((b * N_KV_HEADS + h) * tq + t) * n_sel + k]
            kpos = jnp.where(jnp.right_shift(klane, SEL_SHIFT) == k,
                             blk * SEL_BLOCK + jnp.bitwise_and(klane, SEL_BLOCK - 1), kpos)
        mask = (kpos <= q_pos) & (row_t == t)
        kt = kbuf[t].reshape(nk, HEAD_DIM).astype(BF16)
        vt = vbuf[t].reshape(nk, HEAD_DIM).astype(BF16)
        p = _masked_softmax(_dot_nt(q, kt) * scale, mask)
        o_sel = o_sel + jnp.dot(p.astype(BF16), vt, preferred_element_type=F32)

    o_cmp = ocmp_ref[0, 0]
    gt = gt_ref[0, 0]
    for g in range(gqa):
        r = slice(g * tq_pad, (g + 1) * tq_pad)
        o = (gt[:, 3 * g:3 * g + 1] * o_cmp[r] + gt[:, 3 * g + 1:3 * g + 2] * o_sel[r]
             + gt[:, 3 * g + 2:3 * g + 3] * o_win[r])
        o_ref[0, :, g * HEAD_DIM:(g + 1) * HEAD_DIM] = o.astype(o_ref.dtype)


def nsa_sample_attend(q, gates, cmp, new_rows, win_new, cache_kv, cache_win, page_table, tq):
    bsz, tq_pad, hd_all = q.shape
    kvh = N_KV_HEADS
    gqa = hd_all // HEAD_DIM // kvh
    n_pages = page_table.shape[1]
    past_len = n_pages * PAGE_SIZE
    n16 = cmp.shape[3]
    n_cmp = n16 - 1
    n_cache_blk = past_len // SEL_BLOCK
    n_blk = -(-(past_len + tq) // SEL_BLOCK)
    assert n_blk == n_cache_blk + 1 and past_len % SEL_BLOCK == 0 and tq <= SEL_BLOCK
    n_sel = min(N_SEL, n_blk)
    blk_lanes = -(-n_blk // LANES) * LANES
    win_buf = cache_win.shape[1]
    cmp_spec = lambda s: pl.BlockSpec((1, 1, 1, n16, HEAD_DIM), lambda b, h, s=s: (b, s, h, 0, 0))
    ocmp, idx = pl.pallas_call(
        functools.partial(_nsa_sample_select_body, n_cmp=n_cmp, n_blk=n_blk, blk_lanes=blk_lanes,
                          past_len=past_len, tq_pad=tq_pad, gqa=gqa),
        grid=(bsz, kvh),
        in_specs=[pl.BlockSpec((1, tq_pad, gqa * HEAD_DIM), lambda b, h: (b, 0, h)), cmp_spec(0), cmp_spec(1)],
        out_specs=[pl.BlockSpec((1, 1, gqa * tq_pad, HEAD_DIM), lambda b, h: (b, h, 0, 0)),
                   pl.BlockSpec((1, 1, tq_pad, LANES), lambda b, h: (b, h, 0, 0))],
        out_shape=[jax.ShapeDtypeStruct((bsz, kvh, gqa * tq_pad, HEAD_DIM), F32),
                   jax.ShapeDtypeStruct((bsz, kvh, tq_pad, LANES), I32)],
        compiler_params=_cparams(("parallel", "parallel")),
        name="nsa_sample_select",
    )(q, cmp, cmp)
    idx_flat = idx[:, :, :tq, :n_sel].reshape(-1)

    n_slots = cache_kv.shape[2]
    cache4 = cache_kv.reshape(cache_kv.shape[0], PAGE_SIZE, n_slots * kvh, HEAD_DIM)
    newblk = jnp.zeros((bsz, SEL_BLOCK, n_slots * kvh, HEAD_DIM), F32).at[:, :tq].set(new_rows)
    cwin = cache_win.astype(F32).reshape(bsz, win_buf, 2 * kvh * HEAD_DIM)
    nw = -(-(win_buf + tq_pad) // LANES) * LANES
    kv_spec = lambda slot: pl.BlockSpec((1, tq_pad, HEAD_DIM), lambda b, h, *_, slot=slot: (b, 0, slot * kvh + h))
    cw_spec = lambda slot: pl.BlockSpec((1, win_buf, HEAD_DIM), lambda b, h, *_, slot=slot: (b, 0, slot * kvh + h))
    grid_spec = pltpu.PrefetchScalarGridSpec(
        num_scalar_prefetch=2,
        grid=(bsz, kvh),
        in_specs=[pl.BlockSpec((1, tq_pad, gqa * HEAD_DIM), lambda b, h, *_: (b, 0, h)),
                  pl.BlockSpec((1, 1, tq_pad, LANES), lambda b, h, *_: (b, h, 0, 0)),
                  pl.BlockSpec((1, 1, gqa * tq_pad, HEAD_DIM), lambda b, h, *_: (b, h, 0, 0)),
                  cw_spec(0), cw_spec(1), kv_spec(0), kv_spec(1),
                  pl.BlockSpec(memory_space=pl.ANY), pl.BlockSpec(memory_space=pl.ANY)],
        out_specs=pl.BlockSpec((1, tq_pad, gqa * HEAD_DIM), lambda b, h, *_: (b, 0, h)),
        scratch_shapes=[pltpu.VMEM((tq, n_sel, SEL_BLOCK, HEAD_DIM), F32),
                        pltpu.VMEM((tq, n_sel, SEL_BLOCK, HEAD_DIM), F32),
                        pltpu.VMEM((nw, HEAD_DIM), F32), pltpu.VMEM((nw, HEAD_DIM), F32),
                        pltpu.SemaphoreType.DMA((2,))],
    )
    return pl.pallas_call(
        functools.partial(_nsa_sample_attend_body, n_sel=n_sel, n_cache_blk=n_cache_blk, n_pages=n_pages,
                          past_len=past_len, win_buf=win_buf, tq=tq, tq_pad=tq_pad, gqa=gqa),
        grid_spec=grid_spec,
        out_shape=jax.ShapeDtypeStruct((bsz, tq_pad, hd_all), BF16),
        compiler_params=_cparams(("arbitrary", "arbitrary")),
        name="nsa_sample_attend",
    )(idx_flat, page_table.reshape(-1).astype(I32), q, gates, ocmp, cwin, cwin, win_new, win_new, cache4, newblk)


def _pad_rows(x, rows):
    return jnp.pad(x, ((0, 0), (0, rows - x.shape[1]), (0, 0)))


@jax.jit
def _step(x_prompt, x_sample, state_ssm_re, state_ssm_im, state_ffn_conv, cache_kv, cache_win, page_table,
          attn_norm, ffn_norm, final_norm, ssm_lam_re, ssm_lam_im, ssm_log_step, ssm_b_re, ssm_b_im,
          ssm_c_re, ssm_c_im, ssm_d, ssm_w_glu, ffn_w_in, ffn_conv_w, ffn_conv_b, ffn_w_down,
          kv_norm, w_kv, cmp_w1, cmp_b1, cmp_w2, cmp_b2, cmp_pe, w_qg, w_o):
    bp, tp, d = x_prompt.shape
    bs, ts, _ = x_sample.shape
    depth = attn_norm.shape[0]
    n_a = ssm_lam_re.shape[0]
    kvh = N_KV_HEADS
    hd_all = w_o.shape[1]
    n_heads = hd_all // HEAD_DIM
    gqa = n_heads // kvh
    ts_pad = -(-ts // SUBLANES) * SUBLANES
    n_pages = page_table.shape[1]

    xp = x_prompt.reshape(bp * tp, d).astype(F32)
    xs = x_sample.reshape(bs * ts, d).astype(F32)

    w1cat = jnp.concatenate([cmp_w1[:, :CMP_STRIDE * HEAD_DIM], cmp_w1[:, CMP_STRIDE * HEAD_DIM:]], axis=-1)
    w1cat = w1cat.reshape(2, CMP_STRIDE // 2, 2 * HEAD_DIM, w1cat.shape[-1]).astype(BF16)

    ssm_re_p, ssm_im_p, ssm_re_s, ssm_im_s, conv_p, conv_s = [], [], [], [], [], []
    outs = {}
    for layer in range(depth):
        if layer < n_a:
            a = layer
            wts = _s5_weights(ssm_lam_re[a], ssm_lam_im[a], ssm_log_step[a], ssm_b_re[a], ssm_b_im[a],
                              ssm_c_re[a], ssm_c_im[a])
            w_glu = ssm_w_glu[a].astype(BF16)
            up = rmsnorm(xp, [attn_norm[layer]], F32)[0].reshape(bp, tp, d)
            zeros = jnp.zeros((bp, d // SSM_GROUP, SSM_STATE), F32)
            zp, hre, him = s5_scan(up, zeros, zeros, wts, ssm_d[a], tp, math.gcd(tp, 512))
            ssm_re_p.append(hre)
            ssm_im_p.append(him)
            xp = glu_matmul(zp.reshape(bp * tp, d), w_glu, xp)
            us = _pad_rows(rmsnorm(xs, [attn_norm[layer]], F32)[0].reshape(bs, ts, d), ts_pad)
            zs, hre, him = s5_scan(us, state_ssm_re[a], state_ssm_im[a], wts, ssm_d[a], ts, ts_pad)
            ssm_re_s.append(hre)
            ssm_im_s.append(him)
            xs = glu_matmul(zs[:, :ts].reshape(bs * ts, d), w_glu, xs)
        else:
            bl = layer - n_a
            if layer == n_a:
                w_kv_b = w_kv.astype(BF16)
                n_kv4 = 4 * kvh * HEAD_DIM
                sp, hp_first = rmsnorm(xp, [kv_norm, attn_norm[layer]], BF16)
                ss, hs_first = rmsnorm(xs, [kv_norm, attn_norm[layer]], BF16)
                kvr_p, kvw_p = matmul_split(sp, w_kv_b, n_kv4)
                kvr_s, kvw_s = matmul_split(ss, w_kv_b, n_kv4)
                outs["kv_rows_p"] = kvr_p.reshape(bp, tp, 4, kvh, HEAD_DIM)
                outs["kv_rows_s"] = kvr_s.reshape(bs, ts, 4, kvh, HEAD_DIM)
                outs["win_p"] = kvw_p.reshape(bp, tp, 2, kvh, HEAD_DIM)[:, max(tp - WINDOW, 0):]
                win_s = kvw_s.reshape(bs, ts, 2, kvh, HEAD_DIM)
                win_buf = cache_win.shape[1]
                outs["win_s"] = jnp.concatenate([cache_win.astype(F32), win_s], axis=1)[:, -win_buf:]
                kvw_s3 = _pad_rows(kvw_s.reshape(bs, ts, kvw_s.shape[1]), ts_pad)
                assert tp % PAGE_SIZE == 0 and (n_pages * PAGE_SIZE + ts) // CMP_STRIDE == n_pages * PAGE_SIZE // CMP_STRIDE
                pt_p = jnp.arange(bp * (tp // PAGE_SIZE), dtype=I32).reshape(bp, tp // PAGE_SIZE)
                ab_p = cmp_project(kvr_p.reshape(bp * tp // PAGE_SIZE, PAGE_SIZE, 4 * kvh, HEAD_DIM), pt_p, w1cat)
                cmp_p = cmp_finish(ab_p, cmp_pe, cmp_w1, cmp_b1, cmp_w2, cmp_b2)
                cache4 = cache_kv.astype(F32).reshape(cache_kv.shape[0], PAGE_SIZE, -1, HEAD_DIM)
                ab_s = cmp_project(cache4, page_table, w1cat)
                cmp_s = cmp_finish(ab_s, cmp_pe, cmp_w1, cmp_b1, cmp_w2, cmp_b2)
            w_q = w_qg[bl][:, :hd_all].astype(BF16)
            n_gate = w_qg.shape[2] - hd_all
            w_g = jnp.pad(w_qg[bl][:, hd_all:], ((0, 0), (0, LANES - n_gate))).astype(BF16)
            w_o_b = w_o[bl].astype(BF16)

            def gate_layout(gt, bsz, t):
                gt = gt[:, :n_gate].reshape(bsz, t, kvh, 3 * gqa).transpose(0, 2, 1, 3)
                return jnp.pad(gt, ((0, 0), (0, 0), (0, 0), (0, LANES - 3 * gqa)))

            hp = hp_first if layer == n_a else rmsnorm(xp, [attn_norm[layer]], BF16)[0]
            q_p = matmul(hp, w_q, out_dtype=BF16, tm=1024, tn=512)
            g_p = matmul(hp, w_g, act="sigmoid", tm=1024)
            o_p = nsa_prompt_attend(q_p, g_p, cmp_p, (kvr_p, 2 * kvh), (kvr_p, 3 * kvh), (kvw_p, 0), (kvw_p, kvh), bp, tp)
            xp = matmul(o_p, w_o_b, res=xp, tm=1024, tn=512)

            hs = hs_first if layer == n_a else rmsnorm(xs, [attn_norm[layer]], BF16)[0]
            q_s = _pad_rows(matmul(hs, w_q, out_dtype=BF16).reshape(bs, ts, hd_all), ts_pad)
            g_s = gate_layout(matmul(hs, w_g, act="sigmoid"), bs, ts)
            g_s = jnp.pad(g_s, ((0, 0), (0, 0), (0, ts_pad - ts), (0, 0)))
            o_s = nsa_sample_attend(q_s, g_s, cmp_s, kvr_s.reshape(bs, ts, 4 * kvh, HEAD_DIM), kvw_s3,
                                    cache_kv.astype(F32), cache_win, page_table, ts)
            xs = matmul(o_s[:, :ts].reshape(bs * ts, hd_all), w_o_b, res=xs)

        w_down = ffn_w_down[layer].astype(BF16)
        f = w_down.shape[0]
        act, st, act_s, st_s = ffn_up(rmsnorm(xp, [ffn_norm[layer]], BF16)[0], rmsnorm(xs, [ffn_norm[layer]], BF16)[0],
                                      ffn_w_in.astype(F32), layer, ffn_conv_w[layer], ffn_conv_b[layer],
                                      jnp.zeros((bp, CONV_W - 1, f), F32), state_ffn_conv[layer], tp, ts)
        conv_p.append(st)
        conv_s.append(st_s)
        xp = matmul_khalves(act, w_down, xp)
        xs = matmul(act_s, w_down, res=xs, tm=512, tn=256)

    y_p = rmsnorm(xp, [final_norm], F32)[0].reshape(bp, tp, d)
    y_s = rmsnorm(xs, [final_norm], F32)[0].reshape(bs, ts, d)
    return (y_p, y_s, jnp.stack(ssm_re_p), jnp.stack(ssm_im_p), jnp.stack(ssm_re_s), jnp.stack(ssm_im_s),
            jnp.stack(conv_p), jnp.stack(conv_s), outs["kv_rows_p"], outs["kv_rows_s"], outs["win_p"], outs["win_s"])


def kernel(x_prompt, x_sample, state_ssm_re, state_ssm_im, state_ffn_conv, cache_kv, cache_win, page_table, attn_norm, ffn_norm, final_norm, ssm_lam_re, ssm_lam_im, ssm_log_step, ssm_b_re, ssm_b_im, ssm_c_re, ssm_c_im, ssm_d, ssm_w_glu, ffn_w_in, ffn_conv_w, ffn_conv_b, ffn_w_down, kv_norm, w_kv, cmp_w1, cmp_b1, cmp_w2, cmp_b2, cmp_pe, w_qg, w_o):
    return _step(x_prompt, x_sample, state_ssm_re, state_ssm_im, state_ffn_conv, cache_kv, cache_win, page_table,
                 attn_norm, ffn_norm, final_norm, ssm_lam_re, ssm_lam_im, ssm_log_step, ssm_b_re, ssm_b_im,
                 ssm_c_re, ssm_c_im, ssm_d, ssm_w_glu, ffn_w_in, ffn_conv_w, ffn_conv_b, ffn_w_down,
                 kv_norm, w_kv, cmp_w1, cmp_b1, cmp_w2, cmp_b2, cmp_pe, w_qg, w_o)
```

```python
import functools
import math

import jax
import jax.numpy as jnp
from jax import lax
from jax.experimental import pallas as pl
from jax.experimental.pallas import tpu as pltpu

F32 = jnp.float32
BF16 = jnp.bfloat16
I32 = jnp.int32

RMS_EPS = 1e-6
LANES = 128
SUBLANES = 8
VMEM_LIMIT = 56 * 1024 * 1024

SSM_GROUP = 16
SSM_STATE = 64
CONV_W = 3
HEAD_DIM = 128
N_KV_HEADS = 4
CMP_STRIDE = 16
CMP_BLOCK = 32
SEL_BLOCK = 64
SEL_SHIFT = SEL_BLOCK.bit_length() - 1
N_SEL = 16
WINDOW = 512
Q_BLOCK = 128
PAGE_SIZE = 128
NEG_BIG = -1e30
M_FLOOR = 0.1 * NEG_BIG
LOG2_E = math.log2(math.e)
ONES_ROWS = 16

S5_SET_GROUPS = 4
S5_SETS = 8
S5_BLK_GROUPS = S5_SET_GROUPS * S5_SETS
S5_BLK_CH = S5_BLK_GROUPS * SSM_GROUP
S5_SET_LANES = S5_SET_GROUPS * SSM_STATE


def _cparams(sem, vmem=VMEM_LIMIT):
    return pltpu.CompilerParams(dimension_semantics=sem, vmem_limit_bytes=vmem)


def _div(s, n):
    return jnp.right_shift(s, n.bit_length() - 1) if n & (n - 1) == 0 else s // n


def _mod(s, n):
    return jnp.bitwise_and(s, n - 1) if n & (n - 1) == 0 else s % n


def _cast_body(w_ref, o_ref):
    o_ref[...] = w_ref[...].astype(o_ref.dtype)


def cast_weight(w_all, layer, col0=0, ncols=None, tr=512):
    _, k, n = w_all.shape
    ncols = n - col0 if ncols is None else ncols
    tr = math.gcd(k, tr)
    tc = math.gcd(math.gcd(ncols, col0) if col0 else ncols, 4096)
    c0 = col0 // tc
    return pl.pallas_call(
        _cast_body,
        grid=(k // tr, ncols // tc),
        in_specs=[pl.BlockSpec((None, tr, tc), lambda i, j: (layer, i, c0 + j))],
        out_specs=pl.BlockSpec((tr, tc), lambda i, j: (i, j)),
        out_shape=jax.ShapeDtypeStruct((k, ncols), BF16),
        compiler_params=_cparams(("parallel", "parallel")),
        name="cast_weight",
    )(w_all.astype(F32))


def _rmsnorm_body(x_ref, g_ref, *o_refs):
    x = x_ref[...].astype(F32)
    y = x * lax.rsqrt(jnp.mean(x * x, axis=-1, keepdims=True) + RMS_EPS)
    for k, o_ref in enumerate(o_refs):
        o_ref[...] = (y * g_ref[k:k + 1, :]).astype(o_ref.dtype)


def rmsnorm(x, gains, out_dtype):
    m, d = x.shape
    tm = math.gcd(m, 256)
    n = len(gains)
    g = jnp.stack([gi.reshape(d).astype(F32) for gi in gains])
    outs = pl.pallas_call(
        _rmsnorm_body,
        grid=(m // tm,),
        in_specs=[pl.BlockSpec((tm, d), lambda i: (i, 0)), pl.BlockSpec((n, d), lambda i: (0, 0))],
        out_specs=[pl.BlockSpec((tm, d), lambda i: (i, 0))] * n,
        out_shape=[jax.ShapeDtypeStruct((m, d), out_dtype)] * n,
        compiler_params=_cparams(("parallel",)),
        name="rmsnorm",
    )(x, g)
    return outs


def _mm_body(*refs, has_res, act):
    if has_res:
        x_ref, w_ref, res_ref, o_ref = refs
    else:
        x_ref, w_ref, o_ref = refs
    tm = x_ref.shape[0]
    chunk = math.gcd(tm, 128)
    for r in range(0, tm, chunk):
        y = jnp.dot(x_ref[r:r + chunk, :], w_ref[...], preferred_element_type=F32)
        if act == "sigmoid":
            y = jax.nn.sigmoid(y)
        if has_res:
            y = res_ref[r:r + chunk, :] + y
        o_ref[r:r + chunk, :] = y.astype(o_ref.dtype)


def matmul(x, w, res=None, out_dtype=F32, act=None, tm=512, tn=256):
    m, k = x.shape
    n = w.shape[1]
    tm = math.gcd(m, tm)
    tn = math.gcd(n, tn)
    in_specs = [pl.BlockSpec((tm, k), lambda i, j: (i, 0)), pl.BlockSpec((k, tn), lambda i, j: (0, j))]
    args = [x, w]
    if res is not None:
        in_specs.append(pl.BlockSpec((tm, tn), lambda i, j: (i, j)))
        args.append(res)
    return pl.pallas_call(
        functools.partial(_mm_body, has_res=res is not None, act=act),
        grid=(m // tm, n // tn),
        in_specs=in_specs,
        out_specs=pl.BlockSpec((tm, tn), lambda i, j: (i, j)),
        out_shape=jax.ShapeDtypeStruct((m, n), out_dtype),
        compiler_params=_cparams(("parallel", "arbitrary")),
        name="matmul",
    )(*args)


def _mm_split_body(x_ref, w_ref, o1_ref, o2_ref, *, n1_tiles):
    j = pl.program_id(1)
    y = jnp.dot(x_ref[...], w_ref[...], preferred_element_type=F32)

    @pl.when(j < n1_tiles)
    def _():
        o1_ref[...] = y

    @pl.when(j >= n1_tiles)
    def _():
        o2_ref[...] = y


def matmul_split(x, w, n1, tm=1024, tn=512):
    m, k = x.shape
    n = w.shape[1]
    tm = math.gcd(m, tm)
    tn = math.gcd(math.gcd(n1, n - n1), tn)
    t1 = n1 // tn
    return pl.pallas_call(
        functools.partial(_mm_split_body, n1_tiles=t1),
        grid=(m // tm, n // tn),
        in_specs=[pl.BlockSpec((tm, k), lambda i, j: (i, 0)), pl.BlockSpec((k, tn), lambda i, j: (0, j))],
        out_specs=[pl.BlockSpec((tm, tn), lambda i, j: (i, jnp.minimum(j, t1 - 1))),
                   pl.BlockSpec((tm, tn), lambda i, j: (i, jnp.maximum(j - t1, 0)))],
        out_shape=[jax.ShapeDtypeStruct((m, n1), F32), jax.ShapeDtypeStruct((m, n - n1), F32)],
        compiler_params=_cparams(("parallel", "arbitrary")),
        name="matmul_split",
    )(x, w)


def _glu_body(x_ref, wa_ref, wb_ref, res_ref, o_ref):
    tm = x_ref.shape[0]
    chunk = math.gcd(tm, 128)
    for r in range(0, tm, chunk):
        x = x_ref[r:r + chunk, :]
        a = jnp.dot(x, wa_ref[...], preferred_element_type=F32)
        b = jnp.dot(x, wb_ref[...], preferred_element_type=F32)
        o_ref[r:r + chunk, :] = res_ref[r:r + chunk, :] + a * jax.nn.sigmoid(b)


def glu_matmul(x, w, res, tm=1024, tn=512):
    m, k = x.shape
    n = w.shape[1] // 2
    tm = math.gcd(m, tm)
    tn = math.gcd(n, tn)
    nt = n // tn
    return pl.pallas_call(
        _glu_body,
        grid=(m // tm, nt),
        in_specs=[pl.BlockSpec((tm, k), lambda i, j: (i, 0)),
                  pl.BlockSpec((k, tn), lambda i, j: (0, j)),
                  pl.BlockSpec((k, tn), lambda i, j: (0, j + nt)),
                  pl.BlockSpec((tm, tn), lambda i, j: (i, j))],
        out_specs=pl.BlockSpec((tm, tn), lambda i, j: (i, j)),
        out_shape=jax.ShapeDtypeStruct((m, n), F32),
        compiler_params=_cparams(("parallel", "arbitrary")),
        name="glu_matmul",
    )(x, w, w, res)


def _conv_gate(g, g1, g2, cw_ref, cb_ref, val):
    gc = cb_ref[...] + cw_ref[0:1, :] * g2 + cw_ref[1:2, :] * g1 + cw_ref[2:3, :] * g
    return (gc * jax.nn.sigmoid(gc) * val).astype(BF16)


def _ffn_up_body(*refs, n_m, tiles_per_seq, short_len, chunk, n_slabs):
    x_slabs = refs[:n_slabs]
    (xs_ref, wv_ref, wg_ref, cw_ref, cb_ref, p1_ref, p2_ref, cwp_ref, cbp_ref, init_ref,
     act_ref, st_ref, acts_ref, gs_ref, wv_s, wg_s, val_a, g_a, val_b, g_b) = refs[n_slabs:]
    slab_rows = x_slabs[0].shape[0]
    s = pl.program_id(0)
    i = _mod(s, n_m)

    @pl.when(s == 0)
    def _():
        val_b[...] = jnp.zeros(val_b.shape, F32)
        g_b[...] = jnp.zeros(g_b.shape, F32)

    @pl.when(i == 0)
    def _():
        wv_s[...] = wv_ref[...].astype(BF16)
        wg_s[...] = wg_ref[...].astype(BF16)
        xs = xs_ref[...]
        val = jnp.dot(xs, wv_s[...], preferred_element_type=F32)
        g = jnp.dot(xs, wg_s[...], preferred_element_type=F32)
        tmod = lax.rem(lax.broadcasted_iota(I32, g.shape, 0), short_len)
        g1 = jnp.where(tmod >= 1, pltpu.roll(g, 1, axis=0), p1_ref[...])
        g2 = jnp.where(tmod >= 2, pltpu.roll(g, 2, axis=0), p2_ref[...])
        acts_ref[...] = _conv_gate(g, g1, g2, cw_ref, cb_ref, val)
        gs_ref[...] = g

    def step(val_w, g_w, val_r, g_r):
        tm = n_slabs * slab_rows
        for r in range(0, tm, chunk):
            x = x_slabs[r // slab_rows][r % slab_rows:r % slab_rows + chunk, :]
            val_w[r:r + chunk, :] = jnp.dot(x, wv_s[...], preferred_element_type=F32)
            g_w[SUBLANES + r:SUBLANES + r + chunk, :] = jnp.dot(x, wg_s[...], preferred_element_type=F32)
            gext = g_r[r:r + chunk + SUBLANES, :]
            g1 = pltpu.roll(gext, 1, axis=0)[SUBLANES:]
            g2 = pltpu.roll(gext, 2, axis=0)[SUBLANES:]
            act_ref[r:r + chunk, :] = _conv_gate(gext[SUBLANES:], g1, g2, cwp_ref, cbp_ref, val_r[r:r + chunk, :])
        tail = g_r[tm:, :]
        st_ref[0] = tail
        g_w[:SUBLANES, :] = jnp.where(_mod(i, tiles_per_seq) == 0, init_ref[0], tail)

    parity = _mod(s, 2)

    @pl.when(parity == 0)
    def _():
        step(val_a, g_a, val_b, g_b)

    @pl.when(parity == 1)
    def _():
        step(val_b, g_b, val_a, g_a)


def ffn_up(h, hs, w_in_all, layer, conv_w, conv_b, buf, buf_s, seq_len, seq_short, tm=1024, tn=256, n_slabs=1):
    m, k = h.shape
    ms = hs.shape[0]
    f = w_in_all.shape[2] // 2
    bsz = m // seq_len
    bs = ms // seq_short
    assert seq_len % SUBLANES == 0 and seq_len >= 2 * SUBLANES and seq_short >= 2
    tn = math.gcd(f, tn)
    n_n = f // tn
    tm = math.gcd(seq_len, tm)
    tps = seq_len // tm
    n_m = m // tm
    chunk = math.gcd(tm, 128)
    n_slabs = math.gcd(tm // chunk, n_slabs)
    slab = tm // n_slabs
    cw = jnp.zeros((SUBLANES, f), F32).at[:CONV_W].set(conv_w.astype(F32))
    cb = conv_b.reshape(1, f).astype(F32)
    init = jnp.zeros((bsz, SUBLANES, f), F32).at[:, SUBLANES - 2:].set(buf.astype(F32))
    buf_s = buf_s.astype(F32)
    pos = jnp.arange(ms) % seq_short
    b0 = jnp.repeat(buf_s[:, 0], seq_short, axis=0)
    b1 = jnp.repeat(buf_s[:, 1], seq_short, axis=0)
    p1 = jnp.where((pos == 0)[:, None], b1, 0.0)
    p2 = jnp.where((pos == 0)[:, None], b0, jnp.where((pos == 1)[:, None], b1, 0.0))

    def col(s):
        return jnp.minimum(_div(s, n_m), n_n - 1)

    def prev(s):
        sp = jnp.maximum(s - 1, 0)
        return _mod(sp, n_m), _div(sp, n_m)

    act, st, act_s, g_s = pl.pallas_call(
        functools.partial(_ffn_up_body, n_m=n_m, tiles_per_seq=tps, short_len=seq_short, chunk=chunk, n_slabs=n_slabs),
        grid=(n_n * n_m + 1,),
        in_specs=[pl.BlockSpec((slab, k), lambda s, c=c: (_mod(s, n_m) * n_slabs + c, 0)) for c in range(n_slabs)] + [
                  pl.BlockSpec((ms, k), lambda s: (0, 0)),
                  pl.BlockSpec((None, k, tn), lambda s: (layer, 0, col(s))),
                  pl.BlockSpec((None, k, tn), lambda s: (layer, 0, col(s) + n_n)),
                  pl.BlockSpec((SUBLANES, tn), lambda s: (0, col(s))),
                  pl.BlockSpec((1, tn), lambda s: (0, col(s))),
                  pl.BlockSpec((ms, tn), lambda s: (0, col(s))),
                  pl.BlockSpec((ms, tn), lambda s: (0, col(s))),
                  pl.BlockSpec((SUBLANES, tn), lambda s: (0, prev(s)[1])),
                  pl.BlockSpec((1, tn), lambda s: (0, prev(s)[1])),
                  pl.BlockSpec((1, SUBLANES, tn), lambda s: (_div(_mod(s, n_m), tps), 0, col(s)))],
        out_specs=[pl.BlockSpec((tm, tn), lambda s: prev(s)),
                   pl.BlockSpec((1, SUBLANES, tn), lambda s: (prev(s)[0], 0, prev(s)[1])),
                   pl.BlockSpec((ms, tn), lambda s: (0, col(s))),
                   pl.BlockSpec((ms, tn), lambda s: (0, col(s)))],
        out_shape=[jax.ShapeDtypeStruct((m, f), BF16), jax.ShapeDtypeStruct((n_m, SUBLANES, f), F32),
                   jax.ShapeDtypeStruct((ms, f), BF16), jax.ShapeDtypeStruct((ms, f), F32)],
        scratch_shapes=[pltpu.VMEM((k, tn), BF16), pltpu.VMEM((k, tn), BF16)]
                       + [pltpu.VMEM((tm, tn), F32), pltpu.VMEM((tm + SUBLANES, tn), F32)] * 2,
        compiler_params=_cparams(("arbitrary",)),
        name="ffn_up",
    )(*([h] * n_slabs), hs, w_in_all, w_in_all, cw, cb, p1, p2, cw, cb, init)
    return (act, st[tps - 1::tps, SUBLANES - 2:], act_s, g_s.reshape(bs, seq_short, f)[:, seq_short - 2:])


def _s5_body(*refs, tc_len, last_step, nb):
    (u_ref, wb_ref, wc_ref, lre_ref, lim_ref, d_ref, h0re_ref, h0im_ref, z_ref, hre_ref, him_ref) = refs[:11]
    halves = S5_SET_LANES // LANES
    n_buf = 2 * halves
    bufs = refs[11:11 + nb * n_buf]
    hst_re, hst_im = refs[11 + nb * n_buf:]
    tci = pl.program_id(2)

    def sre(b, c):
        return bufs[b * n_buf + c]

    def sim(b, c):
        return bufs[b * n_buf + halves + c]

    def set_rows(j):
        return pl.ds(j, tc_len, stride=S5_SETS)

    for b in range(nb):
        for ti in range(S5_BLK_CH // LANES):
            ut = u_ref[b, :, ti * LANES:(ti + 1) * LANES]
            hi = ut.astype(BF16)
            lo = (ut - hi.astype(F32)).astype(BF16)
            lhs = jnp.concatenate([hi, lo], axis=1)
            for jj in range(2):
                j = ti * 2 + jj
                bre = jnp.dot(lhs, wb_ref[0, j, 0], preferred_element_type=F32)
                bim = jnp.dot(lhs, wb_ref[0, j, 1], preferred_element_type=F32)
                for c in range(halves):
                    sre(b, c)[set_rows(j), :] = bre[:, c * LANES:(c + 1) * LANES]
                    sim(b, c)[set_rows(j), :] = bim[:, c * LANES:(c + 1) * LANES]

    @pl.when(tci == 0)
    def _():
        for b in range(nb):
            hst_re[b] = h0re_ref[b, 0]
            hst_im[b] = h0im_ref[b, 0]

    lam_r = lre_ref[0]
    lam_i = lim_ref[0]
    lr = [lam_r[:, c * LANES:(c + 1) * LANES] for c in range(halves)]
    li = [lam_i[:, c * LANES:(c + 1) * LANES] for c in range(halves)]

    def step(t, carry):
        r0 = pl.multiple_of(t * S5_SETS, S5_SETS)
        out = []
        for b in range(nb):
            for c in range(halves):
                hr, hi_ = carry[2 * (b * halves + c)], carry[2 * (b * halves + c) + 1]
                nr = lr[c] * hr - li[c] * hi_ + sre(b, c)[pl.ds(r0, S5_SETS), :]
                ni = lr[c] * hi_ + li[c] * hr + sim(b, c)[pl.ds(r0, S5_SETS), :]
                sre(b, c)[pl.ds(r0, S5_SETS), :] = nr
                sim(b, c)[pl.ds(r0, S5_SETS), :] = ni
                out += [nr, ni]
        return tuple(out)

    init = []
    for b in range(nb):
        h_in = hst_re[b]
        g_in = hst_im[b]
        for c in range(halves):
            init += [h_in[:, c * LANES:(c + 1) * LANES], g_in[:, c * LANES:(c + 1) * LANES]]
    fin = lax.fori_loop(0, tc_len, step, tuple(init), unroll=8)
    for b in range(nb):
        for c in range(halves):
            hst_re[b, :, c * LANES:(c + 1) * LANES] = fin[2 * (b * halves + c)]
            hst_im[b, :, c * LANES:(c + 1) * LANES] = fin[2 * (b * halves + c) + 1]

    @pl.when(tci == last_step // tc_len)
    def _():
        r0 = (last_step % tc_len) * S5_SETS
        for b in range(nb):
            for c in range(halves):
                hre_ref[b, 0, :, c * LANES:(c + 1) * LANES] = sre(b, c)[pl.ds(r0, S5_SETS), :]
                him_ref[b, 0, :, c * LANES:(c + 1) * LANES] = sim(b, c)[pl.ds(r0, S5_SETS), :]

    for b in range(nb):
        for c in range(S5_BLK_CH // LANES):
            parts = []
            for s in range(2):
                j = 2 * c + s
                parts += [sre(b, k)[set_rows(j), :].astype(BF16) for k in range(halves)]
                parts += [sim(b, k)[set_rows(j), :].astype(BF16) for k in range(halves)]
            lhs = jnp.concatenate(parts, axis=1)
            y = jnp.dot(lhs, wc_ref[0, c], preferred_element_type=F32)
            y = y + d_ref[:, c * LANES:(c + 1) * LANES] * u_ref[b, :, c * LANES:(c + 1) * LANES]
            z_ref[b, :, c * LANES:(c + 1) * LANES] = jax.nn.gelu(y).astype(BF16)


def _s5_weights(lam_re, lam_im, log_step, b_re, b_im, c_re, c_im):
    g, p = lam_re.shape
    n = SSM_GROUP
    nblk = g // S5_BLK_GROUPS
    lam = lax.complex(lam_re.astype(F32), lam_im.astype(F32))
    dt = jnp.exp(log_step.astype(F32))[:, None]
    lam_bar = jnp.exp(lam * dt)
    b_bar = ((lam_bar - 1.0) / lam)[..., None] * lax.complex(b_re.astype(F32), b_im.astype(F32))
    eye_q = jnp.eye(S5_SET_GROUPS, dtype=F32)
    eye_h = jnp.eye(2, dtype=F32)

    def b_operand(bm):
        t = bm.reshape(nblk, S5_SETS // 2, 2, S5_SET_GROUPS, p, n)
        w = jnp.einsum("bthqpn,hk,qr->bthkrnqp", t, eye_h, eye_q)
        w = w.reshape(nblk, S5_SETS, LANES, S5_SET_LANES)
        return w

    wb = jnp.stack([b_operand(jnp.real(b_bar)), b_operand(jnp.imag(b_bar))], axis=2)
    wb = wb.astype(BF16)
    wb = jnp.concatenate([wb, wb], axis=3)

    def c_operand(cm):
        t = cm.reshape(nblk, S5_SETS // 2, 2, S5_SET_GROUPS, n, p)
        return jnp.einsum("bcsqnp,st,qr->bcsqptrn", t, eye_h, eye_q)

    cre = c_operand(c_re.astype(F32))
    cim = c_operand(-c_im.astype(F32))
    wc = jnp.stack([cre, cim], axis=3).reshape(nblk, S5_SETS // 2, 4 * S5_SET_LANES, LANES).astype(BF16)
    lre = jnp.real(lam_bar).reshape(nblk, S5_SETS, S5_SET_LANES)
    lim = jnp.imag(lam_bar).reshape(nblk, S5_SETS, S5_SET_LANES)
    return wb, wc, lre, lim


def s5_scan(u, h0_re, h0_im, weights, d_skip, seq_valid, tc_len):
    wb, wc, lre, lim = weights
    bsz, lp, d = u.shape
    nblk = d // S5_BLK_CH
    n_tc = lp // tc_len
    h0_re = h0_re.astype(F32).reshape(bsz, nblk, S5_SETS, S5_SET_LANES)
    h0_im = h0_im.astype(F32).reshape(bsz, nblk, S5_SETS, S5_SET_LANES)
    nb = math.gcd(bsz, 2)
    state_spec = pl.BlockSpec((nb, 1, S5_SETS, S5_SET_LANES), lambda k, b, t: (b, k, 0, 0))
    z, hre, him = pl.pallas_call(
        functools.partial(_s5_body, tc_len=tc_len, last_step=seq_valid - 1, nb=nb),
        grid=(nblk, bsz // nb, n_tc),
        in_specs=[pl.BlockSpec((nb, tc_len, S5_BLK_CH), lambda k, b, t: (b, t, k)),
                  pl.BlockSpec((1, S5_SETS, 2, 2 * LANES, S5_SET_LANES), lambda k, b, t: (k, 0, 0, 0, 0)),
                  pl.BlockSpec((1, S5_SETS // 2, 4 * S5_SET_LANES, LANES), lambda k, b, t: (k, 0, 0, 0)),
                  pl.BlockSpec((1, S5_SETS, S5_SET_LANES), lambda k, b, t: (k, 0, 0)),
                  pl.BlockSpec((1, S5_SETS, S5_SET_LANES), lambda k, b, t: (k, 0, 0)),
                  pl.BlockSpec((1, S5_BLK_CH), lambda k, b, t: (0, k)),
                  state_spec, state_spec],
        out_specs=[pl.BlockSpec((nb, tc_len, S5_BLK_CH), lambda k, b, t: (b, t, k)), state_spec, state_spec],
        out_shape=[jax.ShapeDtypeStruct((bsz, lp, d), BF16),
                   jax.ShapeDtypeStruct((bsz, nblk, S5_SETS, S5_SET_LANES), F32),
                   jax.ShapeDtypeStruct((bsz, nblk, S5_SETS, S5_SET_LANES), F32)],
        scratch_shapes=[pltpu.VMEM((tc_len * S5_SETS, LANES), F32)] * (nb * 2 * S5_SET_LANES // LANES) + [
                        pltpu.VMEM((nb, S5_SETS, S5_SET_LANES), F32),
                        pltpu.VMEM((nb, S5_SETS, S5_SET_LANES), F32)],
        compiler_params=_cparams(("arbitrary", "arbitrary", "arbitrary")),
        name="s5_scan",
    )(u, wb, wc, lre, lim, d_skip.reshape(1, d).astype(F32), h0_re, h0_im)
    g = d // SSM_GROUP
    return z, hre.reshape(bsz, g, SSM_STATE), him.reshape(bsz, g, SSM_STATE)


def _cmp_proj_body(pt_ref, x_hbm, w_ref, o_ref, stage, sems, *, pages_per_step):
    n_heads2 = 2 * N_KV_HEADS
    chunks = PAGE_SIZE // CMP_STRIDE
    step = pl.program_id(0) * pl.num_programs(1) + pl.program_id(1)
    n_steps = pl.num_programs(0) * pl.num_programs(1)
    slot = lax.rem(step, 2)

    def page_copy(st, sl, p, c):
        page = pt_ref[st * pages_per_step + p]
        return pltpu.make_async_copy(x_hbm.at[page, :, c, :], stage.at[sl, p, c], sems.at[sl])

    def start_step(st, sl):
        for p in range(pages_per_step):
            for c in range(n_heads2):
                page_copy(st, sl, p, c).start()

    @pl.when(step == 0)
    def _():
        start_step(step, slot)

    @pl.when(step + 1 < n_steps)
    def _():
        start_step(step + 1, 1 - slot)

    for p in range(pages_per_step):
        for c in range(n_heads2):
            page_copy(step, slot, p, c).wait()

    def chunk_rows(c, j):
        return jnp.concatenate([stage[slot, p, c, pl.ds(j, chunks, stride=CMP_STRIDE), :]
                                for p in range(pages_per_step)], axis=0)

    for c in range(n_heads2):
        s, h = divmod(c, N_KV_HEADS)
        acc = None
        for jp in range(CMP_STRIDE // 2):
            lhs = jnp.concatenate([chunk_rows(c, 2 * jp), chunk_rows(c, 2 * jp + 1)], axis=1).astype(BF16)
            t = jnp.dot(lhs, w_ref[s, jp], preferred_element_type=F32)
            acc = t if acc is None else acc + t
        o_ref[0, s, h] = acc


def cmp_project(rows, page_table, w1cat, pages_per_step=16):
    bsz, npg = page_table.shape
    pps = math.gcd(npg, pages_per_step)
    chunks = PAGE_SIZE // CMP_STRIDE
    hid2 = w1cat.shape[-1]
    grid_spec = pltpu.PrefetchScalarGridSpec(
        num_scalar_prefetch=1,
        grid=(bsz, npg // pps),
        in_specs=[pl.BlockSpec(memory_space=pl.ANY),
                  pl.BlockSpec(w1cat.shape, lambda b, g, pt: (0, 0, 0, 0))],
        out_specs=pl.BlockSpec((1, 2, N_KV_HEADS, pps * chunks, hid2), lambda b, g, pt: (b, 0, 0, g, 0)),
        scratch_shapes=[pltpu.VMEM((2, pps, 2 * N_KV_HEADS, PAGE_SIZE, HEAD_DIM), F32),
                        pltpu.SemaphoreType.DMA((2,))],
    )
    return pl.pallas_call(
        functools.partial(_cmp_proj_body, pages_per_step=pps),
        grid_spec=grid_spec,
        out_shape=jax.ShapeDtypeStruct((bsz, 2, N_KV_HEADS, npg * chunks, hid2), F32),
        compiler_params=_cparams(("arbitrary", "arbitrary")),
        name="cmp_project",
    )(page_table.reshape(-1).astype(I32), rows, w1cat)


def _cmp_finish_body(ab_ref, pe_ref, w1_ref, b1_ref, w2_ref, b2_ref, o_ref):
    ab = ab_ref[0, 0, 0]
    n16 = ab.shape[0]
    hid = ab.shape[1] // 2
    a = ab[:, :hid]
    bnext = pltpu.roll(ab[:, hid:], n16 - 1, axis=0)
    c = jnp.dot(pe_ref[0], w1_ref[0], preferred_element_type=F32)[0:1, :] + b1_ref[0]
    pre = a + bnext + c
    y = jnp.dot(jax.nn.gelu(pre).astype(BF16), w2_ref[0], preferred_element_type=F32) + b2_ref[0]
    row = lax.broadcasted_iota(I32, y.shape, 0)
    o_ref[0, 0, 0] = jnp.where(row < n16 - 1, y, 0.0).astype(o_ref.dtype)


def cmp_finish(ab, pe, w1, b1, w2, b2):
    bsz, _, kvh, n16, hid2 = ab.shape
    hid = hid2 // 2
    pe_rows = jnp.broadcast_to(pe.reshape(2, 1, -1), (2, SUBLANES, pe.shape[1] * pe.shape[2])).astype(BF16)
    return pl.pallas_call(
        _cmp_finish_body,
        grid=(2, bsz, kvh),
        in_specs=[pl.BlockSpec((1, 1, 1, n16, hid2), lambda s, b, h: (b, s, h, 0, 0)),
                  pl.BlockSpec((1, SUBLANES, pe_rows.shape[2]), lambda s, b, h: (s, 0, 0)),
                  pl.BlockSpec((1,) + w1.shape[1:], lambda s, b, h: (s, 0, 0)),
                  pl.BlockSpec((1, 1, hid), lambda s, b, h: (s, 0, 0)),
                  pl.BlockSpec((1, hid, HEAD_DIM), lambda s, b, h: (s, 0, 0)),
                  pl.BlockSpec((1, 1, HEAD_DIM), lambda s, b, h: (s, 0, 0))],
        out_specs=pl.BlockSpec((1, 1, 1, n16, HEAD_DIM), lambda s, b, h: (b, s, h, 0, 0)),
        out_shape=jax.ShapeDtypeStruct((bsz, 2, kvh, n16, HEAD_DIM), BF16),
        compiler_params=_cparams(("arbitrary", "arbitrary", "arbitrary")),
        name="cmp_finish",
    )(ab, pe_rows, w1.astype(BF16), b1.reshape(2, 1, hid).astype(F32), w2.astype(BF16),
      b2.reshape(2, 1, HEAD_DIM).astype(F32))


def _masked_softmax(s, mask):
    sm = jnp.where(mask, s, NEG_BIG)
    m = jnp.max(sm, axis=-1, keepdims=True)
    m = jnp.where(m > 0.5 * NEG_BIG, m, 0.0)
    e = jnp.where(mask, jnp.exp(sm - m), 0.0)
    den = jnp.sum(e, axis=-1, keepdims=True)
    return e / jnp.where(den > 0, den, 1.0)


def _dot_nt(a, b):
    return lax.dot_general(a, b, (((1,), (1,)), ((), ())), preferred_element_type=F32)


def _overlap(n_c, n_j, c_axis):
    shape = (n_c, n_j) if c_axis == 0 else (n_j, n_c)
    c = lax.broadcasted_iota(I32, shape, c_axis) * CMP_STRIDE
    j = lax.broadcasted_iota(I32, shape, 1 - c_axis) * SEL_BLOCK
    return ((c < j + SEL_BLOCK) & (c + CMP_BLOCK > j)).astype(F32)


def _block_scores(imp, q_pos, n_blk):
    blk = lax.broadcasted_iota(I32, imp.shape, 1)
    q_blk = jnp.right_shift(q_pos, SEL_SHIFT)
    forced = (blk == 0) | (blk == q_blk) | (blk == q_blk - 1)
    future = blk * SEL_BLOCK > q_pos
    v = jnp.where(future, NEG_BIG, jnp.where(forced, -NEG_BIG, imp))
    return jnp.where(blk < n_blk, v, 2.0 * NEG_BIG)


def _rank_desc(v, n):
    lane = lax.broadcasted_iota(I32, v.shape, 1)
    rank = jnp.zeros(v.shape, F32)
    for i in range(n):
        vi = v[:, i:i + 1]
        before = (vi > v) | ((vi == v) & (lane > i))
        rank = rank + jnp.where(before, 1.0, 0.0)
    return rank


def _nsa_prompt_body(q_ref, gt_ref, ck_ref, cvt_ref, ks_ref, vs_ref, kw_ref, vw_ref, o_ref, m_ref, acc_ref,
                     vst_ref, vwt_ref, *, n_cmp, n_blk, gqa, sel_tile, win_keys):
    qb = pl.program_id(2)
    nq = Q_BLOCK
    gq = gqa * nq
    c2 = (HEAD_DIM ** -0.5) * LOG2_E

    @pl.when(qb == 0)
    def _():
        ones = jnp.ones((ONES_ROWS, LANES), BF16)
        for src, dst in ((vs_ref, vst_ref), (vw_ref, vwt_ref)):
            for t in range(dst.shape[0]):
                dst[t, :HEAD_DIM, :] = src[0, t * LANES:(t + 1) * LANES, :].T.astype(BF16)
                dst[t, HEAD_DIM:, :] = ones

    qt = jnp.concatenate([q_ref[0, :, g * HEAD_DIM:(g + 1) * HEAD_DIM].astype(F32).T.astype(BF16)
                          for g in range(gqa)], axis=1)
    q_pos = qb * nq + lax.broadcasted_iota(I32, (1, nq), 1)

    def per_head(x):
        return jnp.concatenate([x] * gqa, axis=1)

    def normalise(acc):
        den = acc[HEAD_DIM:HEAD_DIM + 1, :]
        return acc[:HEAD_DIM, :] * (1.0 / jnp.where(den > 0, den, 1.0))

    ck = ck_ref[0, 0, 0]
    n16 = ck.shape[0]
    s = jnp.dot(ck, qt, preferred_element_type=F32)
    cidx = lax.broadcasted_iota(I32, (n16, 1), 0)
    cbias = jnp.where((cidx * CMP_STRIDE + (CMP_BLOCK - 1) <= q_pos) & (cidx < n_cmp), 0.0, NEG_BIG)
    sb = s + per_head(cbias)
    m = jnp.maximum(jnp.max(sb, axis=0, keepdims=True), M_FLOOR)
    e = jnp.exp2((sb - m) * c2)
    den = jnp.sum(e, axis=0, keepdims=True)
    p = e * jnp.where(den > 0, 1.0 / den, 0.0)
    o_cmp = jnp.dot(cvt_ref[0, 0], p.astype(BF16), preferred_element_type=F32)

    psum = p[:, 0:nq]
    for g in range(1, gqa):
        psum = psum + p[:, g * nq:(g + 1) * nq]
    imp = jnp.dot(_overlap(n16, LANES, 1), psum, preferred_element_type=F32, precision=lax.Precision.HIGHEST)
    blk = lax.broadcasted_iota(I32, (LANES, 1), 0)
    q_blk = jnp.right_shift(q_pos, SEL_SHIFT)
    forced = (blk == 0) | (blk == q_blk) | (blk == q_blk - 1)
    v = jnp.where(blk * SEL_BLOCK > q_pos, NEG_BIG, jnp.where(forced, -NEG_BIG, imp))
    v = jnp.where(blk < n_blk, v, 2.0 * NEG_BIG)
    rank = jnp.zeros(v.shape, F32)
    for i in range(n_blk):
        vi = v[i:i + 1, :]
        rank = rank + jnp.where((vi > v) | ((vi == v) & (blk > i)), 1.0, 0.0)
    sel_bias = jnp.where((rank < min(N_SEL, n_blk)) & (blk < n_blk), 0.0, NEG_BIG).astype(BF16)

    m_ref[...] = jnp.full(m_ref.shape, M_FLOOR, F32)
    acc_ref[...] = jnp.zeros(acc_ref.shape, F32)
    tiles = sel_tile // LANES

    def sel_step(kt, _):
        k0 = pl.multiple_of(kt * sel_tile, sel_tile)
        s = jnp.dot(ks_ref[0, pl.ds(k0, sel_tile), :].astype(BF16), qt, preferred_element_type=F32)
        key = k0 + lax.broadcasted_iota(I32, (sel_tile, 1), 0)
        expand = jnp.where(jnp.right_shift(key, SEL_SHIFT) == lax.broadcasted_iota(I32, (sel_tile, LANES), 1), 1.0, 0.0)
        bias = jnp.dot(expand.astype(BF16), sel_bias, preferred_element_type=F32)
        bias = jnp.where(key <= q_pos, bias, NEG_BIG)
        sb = s + per_head(bias)
        m_old = m_ref[0:1, :]
        m_new = jnp.maximum(m_old, jnp.max(sb, axis=0, keepdims=True))
        alpha = jnp.exp2((m_old - m_new) * c2)
        e = jnp.exp2((sb - m_new) * c2).astype(BF16)
        vt = jnp.concatenate([vst_ref[kt * tiles + i] for i in range(tiles)], axis=1)
        acc_ref[...] = alpha * acc_ref[...] + jnp.dot(vt, e, preferred_element_type=F32)
        m_ref[...] = jnp.broadcast_to(m_new, m_ref.shape)
        return 0

    lax.fori_loop(0, (qb * nq + nq + sel_tile - 1) // sel_tile, sel_step, 0)
    o_sel = normalise(acc_ref[...])

    w_tile = jnp.maximum(qb + 1 - win_keys // nq, 0)
    w0 = pl.multiple_of(w_tile * nq, nq)
    s = jnp.dot(kw_ref[0, pl.ds(w0, win_keys), :].astype(BF16), qt, preferred_element_type=F32)
    key = w0 + lax.broadcasted_iota(I32, (win_keys, 1), 0)
    wbias = jnp.where((key <= q_pos) & (key > q_pos - WINDOW), 0.0, NEG_BIG)
    sb = s + per_head(wbias)
    m = jnp.maximum(jnp.max(sb, axis=0, keepdims=True), M_FLOOR)
    e = jnp.exp2((sb - m) * c2).astype(BF16)
    vt = jnp.concatenate([vwt_ref[w_tile + i] for i in range(win_keys // LANES)], axis=1)
    o_win = normalise(jnp.dot(vt, e, preferred_element_type=F32))

    gt = gt_ref[0, 0, 0]
    for g in range(gqa):
        cols = slice(g * nq, (g + 1) * nq)
        o = (gt[3 * g:3 * g + 1, :] * o_cmp[:, cols] + gt[3 * g + 1:3 * g + 2, :] * o_sel[:, cols]
             + gt[3 * g + 2:3 * g + 3, :] * o_win[:, cols])
        o_ref[0, :, g * HEAD_DIM:(g + 1) * HEAD_DIM] = o.T.astype(o_ref.dtype)


def nsa_prompt_attend(q, gates, cmp, k_sel, v_sel, k_win, v_win, bsz, t, sel_tile=512):
    hd_all = q.shape[1]
    kvh = N_KV_HEADS
    gqa = hd_all // HEAD_DIM // kvh
    n16 = cmp.shape[3]
    n_blk = -(-t // SEL_BLOCK)
    nqb = t // Q_BLOCK
    gq = gqa * Q_BLOCK
    vrows = HEAD_DIM + ONES_ROWS
    assert t % Q_BLOCK == 0 and n_blk <= LANES and n16 * CMP_STRIDE == t
    sel_tile = math.gcd(t, sel_tile)
    win_keys = min(WINDOW + Q_BLOCK, t)
    gt = gates[:, :3 * kvh * gqa].reshape(bsz, nqb, Q_BLOCK, kvh, 3 * gqa).transpose(0, 3, 1, 4, 2)
    gt = jnp.pad(gt, ((0, 0), (0, 0), (0, 0), (0, -3 * gqa % SUBLANES), (0, 0)))
    cvt = cmp[:, 1].transpose(0, 1, 3, 2)
    (ks_arr, ks_col), (vs_arr, vs_col), (kw_arr, kw_col), (vw_arr, vw_col) = k_sel, v_sel, k_win, v_win
    k_spec = lambda col: pl.BlockSpec((1, t, HEAD_DIM), lambda b, h, i, col=col: (b, 0, col + h))
    return pl.pallas_call(
        functools.partial(_nsa_prompt_body, n_cmp=n16 - 1, n_blk=n_blk, gqa=gqa, sel_tile=sel_tile, win_keys=win_keys),
        grid=(bsz, kvh, nqb),
        in_specs=[pl.BlockSpec((1, Q_BLOCK, gqa * HEAD_DIM), lambda b, h, i: (b, i, h)),
                  pl.BlockSpec((1, 1, 1, gt.shape[3], Q_BLOCK), lambda b, h, i: (b, h, i, 0, 0)),
                  pl.BlockSpec((1, 1, 1, n16, HEAD_DIM), lambda b, h, i: (b, 0, h, 0, 0)),
                  pl.BlockSpec((1, 1, HEAD_DIM, n16), lambda b, h, i: (b, h, 0, 0)),
                  k_spec(ks_col), k_spec(vs_col), k_spec(kw_col), k_spec(vw_col)],
        out_specs=pl.BlockSpec((1, Q_BLOCK, gqa * HEAD_DIM), lambda b, h, i: (b, i, h)),
        out_shape=jax.ShapeDtypeStruct((bsz, t, hd_all), BF16),
        scratch_shapes=[pltpu.VMEM((SUBLANES, gq), F32), pltpu.VMEM((vrows, gq), F32),
                        pltpu.VMEM((t // LANES, vrows, LANES), BF16), pltpu.VMEM((t // LANES, vrows, LANES), BF16)],
        compiler_params=_cparams(("parallel", "parallel", "arbitrary")),
        name="nsa_prompt",
    )(q.reshape(bsz, t, hd_all), gt, cmp, cvt, ks_arr.reshape(bsz, t, -1), vs_arr.reshape(bsz, t, -1),
      kw_arr.reshape(bsz, t, -1), vw_arr.reshape(bsz, t, -1)).reshape(bsz * t, hd_all)


def _nsa_sample_select_body(q_ref, ck_ref, cv_ref, ocmp_ref, idx_ref, *, n_cmp, n_blk, blk_lanes, past_len, tq_pad, gqa):
    scale = HEAD_DIM ** -0.5
    q = jnp.concatenate([q_ref[0, :, g * HEAD_DIM:(g + 1) * HEAD_DIM] for g in range(gqa)], axis=0)
    ck = ck_ref[0, 0, 0]
    cv = cv_ref[0, 0, 0]
    n16 = ck.shape[0]
    q_pos = past_len + lax.broadcasted_iota(I32, (tq_pad, 1), 0)
    s = (_dot_nt(q, ck) * scale).reshape(gqa, tq_pad, n16)
    cidx = lax.broadcasted_iota(I32, (tq_pad, n16), 1)
    cmask = (cidx * CMP_STRIDE + (CMP_BLOCK - 1) <= q_pos) & (cidx < n_cmp)
    p = _masked_softmax(s, cmask[None])
    ocmp_ref[0, 0] = jnp.dot(p.reshape(gqa * tq_pad, n16).astype(BF16), cv, preferred_element_type=F32)
    psum = jnp.sum(p, axis=0)
    imp = jnp.dot(psum, _overlap(n16, blk_lanes, 0), preferred_element_type=F32, precision=lax.Precision.HIGHEST)
    rank = _rank_desc(_block_scores(imp, q_pos, n_blk), n_blk)
    lane = lax.broadcasted_iota(I32, rank.shape, 1)
    lane_f = lane.astype(F32)
    out_lane = lax.broadcasted_iota(I32, (tq_pad, LANES), 1)
    idx = jnp.zeros((tq_pad, LANES), F32)
    for k in range(min(N_SEL, n_blk)):
        hit = (rank == float(k)) & (lane < n_blk)
        idx_k = jnp.sum(jnp.where(hit, lane_f, 0.0), axis=-1, keepdims=True)
        idx = jnp.where(out_lane == k, idx_k, idx)
    idx_ref[0, 0] = idx.astype(I32)


def _nsa_sample_attend_body(idx_ref, pt_ref, q_ref, gt_ref, ocmp_ref, kwc_ref, vwc_ref,
                            kwn_ref, vwn_ref, cache_ref, newblk_ref, o_ref, kbuf, vbuf, kwbuf, vwbuf, sems,
                            *, n_sel, n_cache_blk, n_pages, past_len, win_buf, tq, tq_pad, gqa):
    b = pl.program_id(0)
    h = pl.program_id(1)
    scale = HEAD_DIM ** -0.5
    blk_per_page = PAGE_SIZE // SEL_BLOCK

    bufs = ((2, kbuf, 0), (3, vbuf, 1))
    step = b * N_KV_HEADS + h
    n_steps = pl.num_programs(0) * N_KV_HEADS
    cur = _mod(step, 2)

    def from_new(bb, hh, sl, t, k, slot, buf, sem):
        return pltpu.make_async_copy(newblk_ref.at[bb, :, slot * N_KV_HEADS + hh], buf.at[sl, t, k], sems.at[sl, sem])

    def start_gather(st, sl):
        bb = _div(st, N_KV_HEADS)
        hh = _mod(st, N_KV_HEADS)
        for t in range(tq):
            for k in range(n_sel):
                blk = idx_ref[(st * tq + t) * n_sel + k]
                in_cache = blk < n_cache_blk
                blk_c = jnp.minimum(blk, n_cache_blk - 1)
                page = pt_ref[bb * n_pages + _div(blk_c, blk_per_page)]
                off = pl.multiple_of(_mod(blk_c, blk_per_page) * SEL_BLOCK, SEL_BLOCK)
                for slot, buf, sem in bufs:
                    @pl.when(in_cache)
                    def _():
                        pltpu.make_async_copy(cache_ref.at[page, pl.ds(off, SEL_BLOCK), slot * N_KV_HEADS + hh],
                                              buf.at[sl, t, k], sems.at[sl, sem]).start()

                    @pl.when(jnp.logical_not(in_cache))
                    def _():
                        from_new(bb, hh, sl, t, k, slot, buf, sem).start()

    @pl.when(step == 0)
    def _():
        start_gather(step, cur)

    @pl.when(step + 1 < n_steps)
    def _():
        start_gather(step + 1, 1 - cur)

    q = jnp.concatenate([q_ref[0, :, g * HEAD_DIM:(g + 1) * HEAD_DIM] for g in range(gqa)], axis=0)
    rows = gqa * tq_pad
    row_t = lax.rem(lax.broadcasted_iota(I32, (rows, 1), 0), tq_pad)
    q_pos = past_len + row_t

    kwbuf[...] = jnp.zeros(kwbuf.shape, F32)
    vwbuf[...] = jnp.zeros(vwbuf.shape, F32)
    kwbuf[0:win_buf, :] = kwc_ref[0]
    vwbuf[0:win_buf, :] = vwc_ref[0]
    kwbuf[win_buf:win_buf + tq_pad, :] = kwn_ref[0]
    vwbuf[win_buf:win_buf + tq_pad, :] = vwn_ref[0]
    nw = kwbuf.shape[0]
    wlane = lax.broadcasted_iota(I32, (rows, nw), 1)
    wpos = past_len - win_buf + wlane
    wmask = (wpos <= q_pos) & (wpos > q_pos - WINDOW) & (wpos >= 0) & (wlane < win_buf + tq)
    s = _dot_nt(q, kwbuf[...].astype(BF16)) * scale
    p = _masked_softmax(s, wmask)
    o_win = jnp.dot(p.astype(BF16), vwbuf[...].astype(BF16), preferred_element_type=F32)

    for t in range(tq):
        for k in range(n_sel):
            for slot, buf, sem in bufs:
                from_new(b, h, cur, t, k, slot, buf, sem).wait()

    o_sel = jnp.zeros((rows, HEAD_DIM), F32)
    nk = n_sel * SEL_BLOCK
    klane = lax.broadcasted_iota(I32, (rows, nk), 1)
    for t in range(tq):
        kpos = jnp.zeros((rows, nk), I32)
        for k in range(n_sel):
            blk = idx_ref[((b * N_KV_HEADS + h) * tq + t) * n_sel + k]
            kpos = jnp.where(jnp.right_shift(klane, SEL_SHIFT) == k,
                             blk * SEL_BLOCK + jnp.bitwise_and(klane, SEL_BLOCK - 1), kpos)
        mask = (kpos <= q_pos) & (row_t == t)
        kt = kbuf[cur, t].reshape(nk, HEAD_DIM).astype(BF16)
        vt = vbuf[cur, t].reshape(nk, HEAD_DIM).astype(BF16)
        p = _masked_softmax(_dot_nt(q, kt) * scale, mask)
        o_sel = o_sel + jnp.dot(p.astype(BF16), vt, preferred_element_type=F32)

    o_cmp = ocmp_ref[0, 0]
    gt = gt_ref[0, 0]
    for g in range(gqa):
        r = slice(g * tq_pad, (g + 1) * tq_pad)
        o = (gt[:, 3 * g:3 * g + 1] * o_cmp[r] + gt[:, 3 * g + 1:3 * g + 2] * o_sel[r]
             + gt[:, 3 * g + 2:3 * g + 3] * o_win[r])
        o_ref[0, :, g * HEAD_DIM:(g + 1) * HEAD_DIM] = o.astype(o_ref.dtype)


def nsa_sample_attend(q, gates, cmp, new_rows, win_new, cache_kv, cache_win, page_table, tq):
    bsz, tq_pad, hd_all = q.shape
    kvh = N_KV_HEADS
    gqa = hd_all // HEAD_DIM // kvh
    n_pages = page_table.shape[1]
    past_len = n_pages * PAGE_SIZE
    n16 = cmp.shape[3]
    n_cmp = n16 - 1
    n_cache_blk = past_len // SEL_BLOCK
    n_blk = -(-(past_len + tq) // SEL_BLOCK)
    assert n_blk == n_cache_blk + 1 and past_len % SEL_BLOCK == 0 and tq <= SEL_BLOCK
    n_sel = min(N_SEL, n_blk)
    blk_lanes = -(-n_blk // LANES) * LANES
    win_buf = cache_win.shape[1]
    cmp_spec = lambda s: pl.BlockSpec((1, 1, 1, n16, HEAD_DIM), lambda b, h, s=s: (b, s, h, 0, 0))
    ocmp, idx = pl.pallas_call(
        functools.partial(_nsa_sample_select_body, n_cmp=n_cmp, n_blk=n_blk, blk_lanes=blk_lanes,
                          past_len=past_len, tq_pad=tq_pad, gqa=gqa),
        grid=(bsz, kvh),
        in_specs=[pl.BlockSpec((1, tq_pad, gqa * HEAD_DIM), lambda b, h: (b, 0, h)), cmp_spec(0), cmp_spec(1)],
        out_specs=[pl.BlockSpec((1, 1, gqa * tq_pad, HEAD_DIM), lambda b, h: (b, h, 0, 0)),
                   pl.BlockSpec((1, 1, tq_pad, LANES), lambda b, h: (b, h, 0, 0))],
        out_shape=[jax.ShapeDtypeStruct((bsz, kvh, gqa * tq_pad, HEAD_DIM), F32),
                   jax.ShapeDtypeStruct((bsz, kvh, tq_pad, LANES), I32)],
        compiler_params=_cparams(("parallel", "parallel")),
        name="nsa_sample_select",
    )(q, cmp, cmp)
    idx_flat = idx[:, :, :tq, :n_sel].reshape(-1)

    n_slots = cache_kv.shape[2]
    cache4 = cache_kv.reshape(cache_kv.shape[0], PAGE_SIZE, n_slots * kvh, HEAD_DIM)
    newblk = jnp.zeros((bsz, SEL_BLOCK, n_slots * kvh, HEAD_DIM), F32).at[:, :tq].set(new_rows)
    cwin = cache_win.astype(F32).reshape(bsz, win_buf, 2 * kvh * HEAD_DIM)
    nw = -(-(win_buf + tq_pad) // LANES) * LANES
    kv_spec = lambda slot: pl.BlockSpec((1, tq_pad, HEAD_DIM), lambda b, h, *_, slot=slot: (b, 0, slot * kvh + h))
    cw_spec = lambda slot: pl.BlockSpec((1, win_buf, HEAD_DIM), lambda b, h, *_, slot=slot: (b, 0, slot * kvh + h))
    grid_spec = pltpu.PrefetchScalarGridSpec(
        num_scalar_prefetch=2,
        grid=(bsz, kvh),
        in_specs=[pl.BlockSpec((1, tq_pad, gqa * HEAD_DIM), lambda b, h, *_: (b, 0, h)),
                  pl.BlockSpec((1, 1, tq_pad, LANES), lambda b, h, *_: (b, h, 0, 0)),
                  pl.BlockSpec((1, 1, gqa * tq_pad, HEAD_DIM), lambda b, h, *_: (b, h, 0, 0)),
                  cw_spec(0), cw_spec(1), kv_spec(0), kv_spec(1),
                  pl.BlockSpec(memory_space=pl.ANY), pl.BlockSpec(memory_space=pl.ANY)],
        out_specs=pl.BlockSpec((1, tq_pad, gqa * HEAD_DIM), lambda b, h, *_: (b, 0, h)),
        scratch_shapes=[pltpu.VMEM((2, tq, n_sel, SEL_BLOCK, HEAD_DIM), F32),
                        pltpu.VMEM((2, tq, n_sel, SEL_BLOCK, HEAD_DIM), F32),
                        pltpu.VMEM((nw, HEAD_DIM), F32), pltpu.VMEM((nw, HEAD_DIM), F32),
                        pltpu.SemaphoreType.DMA((2, 2))],
    )
    return pl.pallas_call(
        functools.partial(_nsa_sample_attend_body, n_sel=n_sel, n_cache_blk=n_cache_blk, n_pages=n_pages,
                          past_len=past_len, win_buf=win_buf, tq=tq, tq_pad=tq_pad, gqa=gqa),
        grid_spec=grid_spec,
        out_shape=jax.ShapeDtypeStruct((bsz, tq_pad, hd_all), BF16),
        compiler_params=_cparams(("arbitrary", "arbitrary")),
        name="nsa_sample_attend",
    )(idx_flat, page_table.reshape(-1).astype(I32), q, gates, ocmp, cwin, cwin, win_new, win_new, cache4, newblk)


def _pad_rows(x, rows):
    return jnp.pad(x, ((0, 0), (0, rows - x.shape[1]), (0, 0)))


@jax.jit
def _step(x_prompt, x_sample, state_ssm_re, state_ssm_im, state_ffn_conv, cache_kv, cache_win, page_table,
          attn_norm, ffn_norm, final_norm, ssm_lam_re, ssm_lam_im, ssm_log_step, ssm_b_re, ssm_b_im,
          ssm_c_re, ssm_c_im, ssm_d, ssm_w_glu, ffn_w_in, ffn_conv_w, ffn_conv_b, ffn_w_down,
          kv_norm, w_kv, cmp_w1, cmp_b1, cmp_w2, cmp_b2, cmp_pe, w_qg, w_o):
    bp, tp, d = x_prompt.shape
    bs, ts, _ = x_sample.shape
    depth = attn_norm.shape[0]
    n_a = ssm_lam_re.shape[0]
    kvh = N_KV_HEADS
    hd_all = w_o.shape[1]
    n_heads = hd_all // HEAD_DIM
    gqa = n_heads // kvh
    ts_pad = -(-ts // SUBLANES) * SUBLANES
    n_pages = page_table.shape[1]

    xp = x_prompt.reshape(bp * tp, d).astype(F32)
    xs = x_sample.reshape(bs * ts, d).astype(F32)

    w1cat = jnp.concatenate([cmp_w1[:, :CMP_STRIDE * HEAD_DIM], cmp_w1[:, CMP_STRIDE * HEAD_DIM:]], axis=-1)
    w1cat = w1cat.reshape(2, CMP_STRIDE // 2, 2 * HEAD_DIM, w1cat.shape[-1]).astype(BF16)

    ssm_re_p, ssm_im_p, ssm_re_s, ssm_im_s, conv_p, conv_s = [], [], [], [], [], []
    outs = {}
    for layer in range(depth):
        if layer < n_a:
            a = layer
            wts = _s5_weights(ssm_lam_re[a], ssm_lam_im[a], ssm_log_step[a], ssm_b_re[a], ssm_b_im[a],
                              ssm_c_re[a], ssm_c_im[a])
            w_glu = cast_weight(ssm_w_glu, a)
            up = rmsnorm(xp, [attn_norm[layer]], F32)[0].reshape(bp, tp, d)
            zeros = jnp.zeros((bp, d // SSM_GROUP, SSM_STATE), F32)
            zp, hre, him = s5_scan(up, zeros, zeros, wts, ssm_d[a], tp, math.gcd(tp, 512))
            ssm_re_p.append(hre)
            ssm_im_p.append(him)
            xp = glu_matmul(zp.reshape(bp * tp, d), w_glu, xp)
            us = _pad_rows(rmsnorm(xs, [attn_norm[layer]], F32)[0].reshape(bs, ts, d), ts_pad)
            zs, hre, him = s5_scan(us, state_ssm_re[a], state_ssm_im[a], wts, ssm_d[a], ts, ts_pad)
            ssm_re_s.append(hre)
            ssm_im_s.append(him)
            xs = glu_matmul(zs[:, :ts].reshape(bs * ts, d), w_glu, xs)
        else:
            bl = layer - n_a
            if layer == n_a:
                w_kv_b = cast_weight(w_kv[None], 0)
                n_kv4 = 4 * kvh * HEAD_DIM
                sp, hp_first = rmsnorm(xp, [kv_norm, attn_norm[layer]], BF16)
                ss, hs_first = rmsnorm(xs, [kv_norm, attn_norm[layer]], BF16)
                kvr_p, kvw_p = matmul_split(sp, w_kv_b, n_kv4)
                kvr_s, kvw_s = matmul_split(ss, w_kv_b, n_kv4)
                outs["kv_rows_p"] = kvr_p.reshape(bp, tp, 4, kvh, HEAD_DIM)
                outs["kv_rows_s"] = kvr_s.reshape(bs, ts, 4, kvh, HEAD_DIM)
                outs["win_p"] = kvw_p.reshape(bp, tp, 2, kvh, HEAD_DIM)[:, max(tp - WINDOW, 0):]
                win_s = kvw_s.reshape(bs, ts, 2, kvh, HEAD_DIM)
                win_buf = cache_win.shape[1]
                outs["win_s"] = jnp.concatenate([cache_win.astype(F32), win_s], axis=1)[:, -win_buf:]
                kvw_s3 = _pad_rows(kvw_s.reshape(bs, ts, kvw_s.shape[1]), ts_pad)
                assert tp % PAGE_SIZE == 0 and (n_pages * PAGE_SIZE + ts) // CMP_STRIDE == n_pages * PAGE_SIZE // CMP_STRIDE
                pt_p = jnp.arange(bp * (tp // PAGE_SIZE), dtype=I32).reshape(bp, tp // PAGE_SIZE)
                ab_p = cmp_project(kvr_p.reshape(bp * tp // PAGE_SIZE, PAGE_SIZE, 4 * kvh, HEAD_DIM), pt_p, w1cat)
                cmp_p = cmp_finish(ab_p, cmp_pe, cmp_w1, cmp_b1, cmp_w2, cmp_b2)
                cache4 = cache_kv.astype(F32).reshape(cache_kv.shape[0], PAGE_SIZE, -1, HEAD_DIM)
                ab_s = cmp_project(cache4, page_table, w1cat)
                cmp_s = cmp_finish(ab_s, cmp_pe, cmp_w1, cmp_b1, cmp_w2, cmp_b2)
            w_q = cast_weight(w_qg, bl, 0, hd_all)
            n_gate = w_qg.shape[2] - hd_all
            w_g = jnp.pad(w_qg[bl][:, hd_all:], ((0, 0), (0, LANES - n_gate))).astype(BF16)
            w_o_b = cast_weight(w_o, bl)

            def gate_layout(gt, bsz, t):
                gt = gt[:, :n_gate].reshape(bsz, t, kvh, 3 * gqa).transpose(0, 2, 1, 3)
                return jnp.pad(gt, ((0, 0), (0, 0), (0, 0), (0, LANES - 3 * gqa)))

            hp = hp_first if layer == n_a else rmsnorm(xp, [attn_norm[layer]], BF16)[0]
            q_p = matmul(hp, w_q, out_dtype=BF16, tm=1024, tn=512)
            g_p = matmul(hp, w_g, act="sigmoid", tm=1024)
            o_p = nsa_prompt_attend(q_p, g_p, cmp_p, (kvr_p, 2 * kvh), (kvr_p, 3 * kvh), (kvw_p, 0), (kvw_p, kvh), bp, tp)
            xp = matmul(o_p, w_o_b, res=xp, tm=1024, tn=512)

            hs = hs_first if layer == n_a else rmsnorm(xs, [attn_norm[layer]], BF16)[0]
            q_s = _pad_rows(matmul(hs, w_q, out_dtype=BF16).reshape(bs, ts, hd_all), ts_pad)
            g_s = gate_layout(matmul(hs, w_g, act="sigmoid"), bs, ts)
            g_s = jnp.pad(g_s, ((0, 0), (0, 0), (0, ts_pad - ts), (0, 0)))
            o_s = nsa_sample_attend(q_s, g_s, cmp_s, kvr_s.reshape(bs, ts, 4 * kvh, HEAD_DIM), kvw_s3,
                                    cache_kv.astype(F32), cache_win, page_table, ts)
            xs = matmul(o_s[:, :ts].reshape(bs * ts, hd_all), w_o_b, res=xs)

        w_down = cast_weight(ffn_w_down, layer)
        f = w_down.shape[0]
        act, st, act_s, st_s = ffn_up(rmsnorm(xp, [ffn_norm[layer]], BF16)[0], rmsnorm(xs, [ffn_norm[layer]], BF16)[0],
                                      ffn_w_in.astype(F32), layer, ffn_conv_w[layer], ffn_conv_b[layer],
                                      jnp.zeros((bp, CONV_W - 1, f), F32), state_ffn_conv[layer], tp, ts)
        conv_p.append(st)
        conv_s.append(st_s)
        xp = matmul(act, w_down, res=xp, tm=512, tn=512)
        xs = matmul(act_s, w_down, res=xs, tm=512, tn=256)

    y_p = rmsnorm(xp, [final_norm], F32)[0].reshape(bp, tp, d)
    y_s = rmsnorm(xs, [final_norm], F32)[0].reshape(bs, ts, d)
    return (y_p, y_s, jnp.stack(ssm_re_p), jnp.stack(ssm_im_p), jnp.stack(ssm_re_s), jnp.stack(ssm_im_s),
            jnp.stack(conv_p), jnp.stack(conv_s), outs["kv_rows_p"], outs["kv_rows_s"], outs["win_p"], outs["win_s"])


def kernel(x_prompt, x_sample, state_ssm_re, state_ssm_im, state_ffn_conv, cache_kv, cache_win, page_table, attn_norm, ffn_norm, final_norm, ssm_lam_re, ssm_lam_im, ssm_log_step, ssm_b_re, ssm_b_im, ssm_c_re, ssm_c_im, ssm_d, ssm_w_glu, ffn_w_in, ffn_conv_w, ffn_conv_b, ffn_w_down, kv_norm, w_kv, cmp_w1, cmp_b1, cmp_w2, cmp_b2, cmp_pe, w_qg, w_o):
    return _step(x_prompt, x_sample, state_ssm_re, state_ssm_im, state_ffn_conv, cache_kv, cache_win, page_table,
                 attn_norm, ffn_norm, final_norm, ssm_lam_re, ssm_lam_im, ssm_log_step, ssm_b_re, ssm_b_im,
                 ssm_c_re, ssm_c_im, ssm_d, ssm_w_glu, ffn_w_in, ffn_conv_w, ffn_conv_b, ffn_w_down,
                 kv_norm, w_kv, cmp_w1, cmp_b1, cmp_w2, cmp_b2, cmp_pe, w_qg, w_o)
```

```python
import functools
import math

import jax
import jax.numpy as jnp
from jax import lax
from jax.experimental import pallas as pl
from jax.experimental.pallas import tpu as pltpu

F32 = jnp.float32
BF16 = jnp.bfloat16
I32 = jnp.int32

RMS_EPS = 1e-6
LANES = 128
SUBLANES = 8
VMEM_LIMIT = 56 * 1024 * 1024
ROW_CHUNK = 128

SSM_GROUP = 16
SSM_STATE = 64
CONV_W = 3
HEAD_DIM = 128
N_KV_HEADS = 4
CMP_STRIDE = 16
CMP_BLOCK = 32
SEL_BLOCK = 64
SEL_SHIFT = SEL_BLOCK.bit_length() - 1
N_SEL = 16
WINDOW = 512
Q_BLOCK = 128
PAGE_SIZE = 128
NEG_BIG = -1e30
M_FLOOR = 0.1 * NEG_BIG
LOG2_E = math.log2(math.e)
ONES_ROWS = 16

S5_SET_GROUPS = 4
S5_SETS = 8
S5_BLK_GROUPS = S5_SET_GROUPS * S5_SETS
S5_BLK_CH = S5_BLK_GROUPS * SSM_GROUP
S5_SET_LANES = S5_SET_GROUPS * SSM_STATE


def _cparams(sem, vmem=VMEM_LIMIT):
    return pltpu.CompilerParams(dimension_semantics=sem, vmem_limit_bytes=vmem)


def _div(s, n):
    return jnp.right_shift(s, n.bit_length() - 1) if n & (n - 1) == 0 else s // n


def _mod(s, n):
    return jnp.bitwise_and(s, n - 1) if n & (n - 1) == 0 else s % n


def _cast_body(w_ref, o_ref):
    o_ref[...] = w_ref[...].astype(o_ref.dtype)


def cast_weight(w_all, layer, col0=0, ncols=None, tr=512):
    _, k, n = w_all.shape
    ncols = n - col0 if ncols is None else ncols
    tr = math.gcd(k, tr)
    tc = math.gcd(math.gcd(ncols, col0) if col0 else ncols, 4096)
    c0 = col0 // tc
    return pl.pallas_call(
        _cast_body,
        grid=(k // tr, ncols // tc),
        in_specs=[pl.BlockSpec((None, tr, tc), lambda i, j: (layer, i, c0 + j))],
        out_specs=pl.BlockSpec((tr, tc), lambda i, j: (i, j)),
        out_shape=jax.ShapeDtypeStruct((k, ncols), BF16),
        compiler_params=_cparams(("parallel", "parallel")),
        name="cast_weight",
    )(w_all.astype(F32))


def _rmsnorm_body(x_ref, g_ref, *o_refs):
    x = x_ref[...].astype(F32)
    y = x * lax.rsqrt(jnp.mean(x * x, axis=-1, keepdims=True) + RMS_EPS)
    for k, o_ref in enumerate(o_refs):
        o_ref[...] = (y * g_ref[k:k + 1, :]).astype(o_ref.dtype)


def rmsnorm(x, gains, out_dtype):
    m, d = x.shape
    tm = math.gcd(m, 256)
    n = len(gains)
    g = jnp.stack([gi.reshape(d).astype(F32) for gi in gains])
    outs = pl.pallas_call(
        _rmsnorm_body,
        grid=(m // tm,),
        in_specs=[pl.BlockSpec((tm, d), lambda i: (i, 0)), pl.BlockSpec((n, d), lambda i: (0, 0))],
        out_specs=[pl.BlockSpec((tm, d), lambda i: (i, 0))] * n,
        out_shape=[jax.ShapeDtypeStruct((m, d), out_dtype)] * n,
        compiler_params=_cparams(("parallel",)),
        name="rmsnorm",
    )(x, g)
    return outs


def _mm_body(*refs, has_res, act):
    if has_res:
        x_ref, w_ref, res_ref, o_ref = refs
    else:
        x_ref, w_ref, o_ref = refs
    tm = x_ref.shape[0]
    chunk = math.gcd(tm, ROW_CHUNK)
    for r in range(0, tm, chunk):
        y = jnp.dot(x_ref[r:r + chunk, :], w_ref[...], preferred_element_type=F32)
        if act == "sigmoid":
            y = jax.nn.sigmoid(y)
        if has_res:
            y = res_ref[r:r + chunk, :] + y
        o_ref[r:r + chunk, :] = y.astype(o_ref.dtype)


def matmul(x, w, res=None, out_dtype=F32, act=None, tm=512, tn=256):
    m, k = x.shape
    n = w.shape[1]
    tm = math.gcd(m, tm)
    tn = math.gcd(n, tn)
    in_specs = [pl.BlockSpec((tm, k), lambda i, j: (i, 0)), pl.BlockSpec((k, tn), lambda i, j: (0, j))]
    args = [x, w]
    if res is not None:
        in_specs.append(pl.BlockSpec((tm, tn), lambda i, j: (i, j)))
        args.append(res)
    return pl.pallas_call(
        functools.partial(_mm_body, has_res=res is not None, act=act),
        grid=(m // tm, n // tn),
        in_specs=in_specs,
        out_specs=pl.BlockSpec((tm, tn), lambda i, j: (i, j)),
        out_shape=jax.ShapeDtypeStruct((m, n), out_dtype),
        compiler_params=_cparams(("parallel", "arbitrary")),
        name="matmul",
    )(*args)


def _mm_split_body(x_ref, w_ref, o1_ref, o2_ref, *, n1_tiles):
    j = pl.program_id(1)
    y = jnp.dot(x_ref[...], w_ref[...], preferred_element_type=F32)

    @pl.when(j < n1_tiles)
    def _():
        o1_ref[...] = y

    @pl.when(j >= n1_tiles)
    def _():
        o2_ref[...] = y


def matmul_split(x, w, n1, tm=1024, tn=512):
    m, k = x.shape
    n = w.shape[1]
    tm = math.gcd(m, tm)
    tn = math.gcd(math.gcd(n1, n - n1), tn)
    t1 = n1 // tn
    return pl.pallas_call(
        functools.partial(_mm_split_body, n1_tiles=t1),
        grid=(m // tm, n // tn),
        in_specs=[pl.BlockSpec((tm, k), lambda i, j: (i, 0)), pl.BlockSpec((k, tn), lambda i, j: (0, j))],
        out_specs=[pl.BlockSpec((tm, tn), lambda i, j: (i, jnp.minimum(j, t1 - 1))),
                   pl.BlockSpec((tm, tn), lambda i, j: (i, jnp.maximum(j - t1, 0)))],
        out_shape=[jax.ShapeDtypeStruct((m, n1), F32), jax.ShapeDtypeStruct((m, n - n1), F32)],
        compiler_params=_cparams(("parallel", "arbitrary")),
        name="matmul_split",
    )(x, w)


def _glu_body(x_ref, wa_ref, wb_ref, res_ref, o_ref):
    tm = x_ref.shape[0]
    chunk = math.gcd(tm, ROW_CHUNK)
    for r in range(0, tm, chunk):
        x = x_ref[r:r + chunk, :]
        a = jnp.dot(x, wa_ref[...], preferred_element_type=F32)
        b = jnp.dot(x, wb_ref[...], preferred_element_type=F32)
        o_ref[r:r + chunk, :] = res_ref[r:r + chunk, :] + a * jax.nn.sigmoid(b)


def glu_matmul(x, w, res, tm=1024, tn=512):
    m, k = x.shape
    n = w.shape[1] // 2
    tm = math.gcd(m, tm)
    tn = math.gcd(n, tn)
    nt = n // tn
    return pl.pallas_call(
        _glu_body,
        grid=(m // tm, nt),
        in_specs=[pl.BlockSpec((tm, k), lambda i, j: (i, 0)),
                  pl.BlockSpec((k, tn), lambda i, j: (0, j)),
                  pl.BlockSpec((k, tn), lambda i, j: (0, j + nt)),
                  pl.BlockSpec((tm, tn), lambda i, j: (i, j))],
        out_specs=pl.BlockSpec((tm, tn), lambda i, j: (i, j)),
        out_shape=jax.ShapeDtypeStruct((m, n), F32),
        compiler_params=_cparams(("parallel", "arbitrary")),
        name="glu_matmul",
    )(x, w, w, res)


def _conv_gate(g, g1, g2, cw_ref, cb_ref, val):
    gc = cb_ref[...] + cw_ref[0:1, :] * g2 + cw_ref[1:2, :] * g1 + cw_ref[2:3, :] * g
    return (gc * jax.nn.sigmoid(gc) * val).astype(BF16)


def _ffn_up_body(*refs, n_m, tiles_per_seq, short_len, chunk, n_slabs):
    x_slabs = refs[:n_slabs]
    (xs_ref, wv_ref, wg_ref, cw_ref, cb_ref, p1_ref, p2_ref, cwp_ref, cbp_ref, init_ref,
     act_ref, st_ref, acts_ref, gs_ref, wv_s, wg_s, val_a, g_a, val_b, g_b) = refs[n_slabs:]
    slab_rows = x_slabs[0].shape[0]
    s = pl.program_id(0)
    i = _mod(s, n_m)

    @pl.when(s == 0)
    def _():
        val_b[...] = jnp.zeros(val_b.shape, F32)
        g_b[...] = jnp.zeros(g_b.shape, F32)

    @pl.when(i == 0)
    def _():
        wv_s[...] = wv_ref[...].astype(BF16)
        wg_s[...] = wg_ref[...].astype(BF16)
        xs = xs_ref[...]
        val = jnp.dot(xs, wv_s[...], preferred_element_type=F32)
        g = jnp.dot(xs, wg_s[...], preferred_element_type=F32)
        tmod = lax.rem(lax.broadcasted_iota(I32, g.shape, 0), short_len)
        g1 = jnp.where(tmod >= 1, pltpu.roll(g, 1, axis=0), p1_ref[...])
        g2 = jnp.where(tmod >= 2, pltpu.roll(g, 2, axis=0), p2_ref[...])
        acts_ref[...] = _conv_gate(g, g1, g2, cw_ref, cb_ref, val)
        gs_ref[...] = g

    def step(val_w, g_w, val_r, g_r):
        tm = n_slabs * slab_rows
        for r in range(0, tm, chunk):
            x = x_slabs[r // slab_rows][r % slab_rows:r % slab_rows + chunk, :]
            val_w[r:r + chunk, :] = jnp.dot(x, wv_s[...], preferred_element_type=F32)
            g_w[SUBLANES + r:SUBLANES + r + chunk, :] = jnp.dot(x, wg_s[...], preferred_element_type=F32)
            gext = g_r[r:r + chunk + SUBLANES, :]
            g1 = pltpu.roll(gext, 1, axis=0)[SUBLANES:]
            g2 = pltpu.roll(gext, 2, axis=0)[SUBLANES:]
            act_ref[r:r + chunk, :] = _conv_gate(gext[SUBLANES:], g1, g2, cwp_ref, cbp_ref, val_r[r:r + chunk, :])
        tail = g_r[tm:, :]
        st_ref[0] = tail
        g_w[:SUBLANES, :] = jnp.where(_mod(i, tiles_per_seq) == 0, init_ref[0], tail)

    parity = _mod(s, 2)

    @pl.when(parity == 0)
    def _():
        step(val_a, g_a, val_b, g_b)

    @pl.when(parity == 1)
    def _():
        step(val_b, g_b, val_a, g_a)


def ffn_up(h, hs, w_in_all, layer, conv_w, conv_b, buf, buf_s, seq_len, seq_short, tm=1024, tn=256, n_slabs=1):
    m, k = h.shape
    ms = hs.shape[0]
    f = w_in_all.shape[2] // 2
    bsz = m // seq_len
    bs = ms // seq_short
    assert seq_len % SUBLANES == 0 and seq_len >= 2 * SUBLANES and seq_short >= 2
    tn = math.gcd(f, tn)
    n_n = f // tn
    tm = math.gcd(seq_len, tm)
    tps = seq_len // tm
    n_m = m // tm
    chunk = math.gcd(tm, ROW_CHUNK)
    n_slabs = math.gcd(tm // chunk, n_slabs)
    slab = tm // n_slabs
    cw = jnp.zeros((SUBLANES, f), F32).at[:CONV_W].set(conv_w.astype(F32))
    cb = conv_b.reshape(1, f).astype(F32)
    init = jnp.zeros((bsz, SUBLANES, f), F32).at[:, SUBLANES - 2:].set(buf.astype(F32))
    buf_s = buf_s.astype(F32)
    pos = jnp.arange(ms) % seq_short
    b0 = jnp.repeat(buf_s[:, 0], seq_short, axis=0)
    b1 = jnp.repeat(buf_s[:, 1], seq_short, axis=0)
    p1 = jnp.where((pos == 0)[:, None], b1, 0.0)
    p2 = jnp.where((pos == 0)[:, None], b0, jnp.where((pos == 1)[:, None], b1, 0.0))

    def col(s):
        return jnp.minimum(_div(s, n_m), n_n - 1)

    def prev(s):
        sp = jnp.maximum(s - 1, 0)
        return _mod(sp, n_m), _div(sp, n_m)

    act, st, act_s, g_s = pl.pallas_call(
        functools.partial(_ffn_up_body, n_m=n_m, tiles_per_seq=tps, short_len=seq_short, chunk=chunk, n_slabs=n_slabs),
        grid=(n_n * n_m + 1,),
        in_specs=[pl.BlockSpec((slab, k), lambda s, c=c: (_mod(s, n_m) * n_slabs + c, 0)) for c in range(n_slabs)] + [
                  pl.BlockSpec((ms, k), lambda s: (0, 0)),
                  pl.BlockSpec((None, k, tn), lambda s: (layer, 0, col(s))),
                  pl.BlockSpec((None, k, tn), lambda s: (layer, 0, col(s) + n_n)),
                  pl.BlockSpec((SUBLANES, tn), lambda s: (0, col(s))),
                  pl.BlockSpec((1, tn), lambda s: (0, col(s))),
                  pl.BlockSpec((ms, tn), lambda s: (0, col(s))),
                  pl.BlockSpec((ms, tn), lambda s: (0, col(s))),
                  pl.BlockSpec((SUBLANES, tn), lambda s: (0, prev(s)[1])),
                  pl.BlockSpec((1, tn), lambda s: (0, prev(s)[1])),
                  pl.BlockSpec((1, SUBLANES, tn), lambda s: (_div(_mod(s, n_m), tps), 0, col(s)))],
        out_specs=[pl.BlockSpec((tm, tn), lambda s: prev(s)),
                   pl.BlockSpec((1, SUBLANES, tn), lambda s: (prev(s)[0], 0, prev(s)[1])),
                   pl.BlockSpec((ms, tn), lambda s: (0, col(s))),
                   pl.BlockSpec((ms, tn), lambda s: (0, col(s)))],
        out_shape=[jax.ShapeDtypeStruct((m, f), BF16), jax.ShapeDtypeStruct((n_m, SUBLANES, f), F32),
                   jax.ShapeDtypeStruct((ms, f), BF16), jax.ShapeDtypeStruct((ms, f), F32)],
        scratch_shapes=[pltpu.VMEM((k, tn), BF16), pltpu.VMEM((k, tn), BF16)]
                       + [pltpu.VMEM((tm, tn), F32), pltpu.VMEM((tm + SUBLANES, tn), F32)] * 2,
        compiler_params=_cparams(("arbitrary",)),
        name="ffn_up",
    )(*([h] * n_slabs), hs, w_in_all, w_in_all, cw, cb, p1, p2, cw, cb, init)
    return (act, st[tps - 1::tps, SUBLANES - 2:], act_s, g_s.reshape(bs, seq_short, f)[:, seq_short - 2:])


def _s5_body(*refs, tc_len, last_step, nb):
    (u_ref, wb_ref, wc_ref, lre_ref, lim_ref, d_ref, h0re_ref, h0im_ref, z_ref, hre_ref, him_ref) = refs[:11]
    halves = S5_SET_LANES // LANES
    n_buf = 2 * halves
    bufs = refs[11:11 + nb * n_buf]
    hst_re, hst_im = refs[11 + nb * n_buf:]
    tci = pl.program_id(2)

    def sre(b, c):
        return bufs[b * n_buf + c]

    def sim(b, c):
        return bufs[b * n_buf + halves + c]

    def set_rows(j):
        return pl.ds(j, tc_len, stride=S5_SETS)

    for b in range(nb):
        for ti in range(S5_BLK_CH // LANES):
            ut = u_ref[b, :, ti * LANES:(ti + 1) * LANES]
            hi = ut.astype(BF16)
            lo = (ut - hi.astype(F32)).astype(BF16)
            lhs = jnp.concatenate([hi, lo], axis=1)
            for jj in range(2):
                j = ti * 2 + jj
                bre = jnp.dot(lhs, wb_ref[0, j, 0], preferred_element_type=F32)
                bim = jnp.dot(lhs, wb_ref[0, j, 1], preferred_element_type=F32)
                for c in range(halves):
                    sre(b, c)[set_rows(j), :] = bre[:, c * LANES:(c + 1) * LANES]
                    sim(b, c)[set_rows(j), :] = bim[:, c * LANES:(c + 1) * LANES]

    @pl.when(tci == 0)
    def _():
        for b in range(nb):
            hst_re[b] = h0re_ref[b, 0]
            hst_im[b] = h0im_ref[b, 0]

    lam_r = lre_ref[0]
    lam_i = lim_ref[0]
    lr = [lam_r[:, c * LANES:(c + 1) * LANES] for c in range(halves)]
    li = [lam_i[:, c * LANES:(c + 1) * LANES] for c in range(halves)]

    def step(t, carry):
        r0 = pl.multiple_of(t * S5_SETS, S5_SETS)
        out = []
        for b in range(nb):
            for c in range(halves):
                hr, hi_ = carry[2 * (b * halves + c)], carry[2 * (b * halves + c) + 1]
                nr = lr[c] * hr - li[c] * hi_ + sre(b, c)[pl.ds(r0, S5_SETS), :]
                ni = lr[c] * hi_ + li[c] * hr + sim(b, c)[pl.ds(r0, S5_SETS), :]
                sre(b, c)[pl.ds(r0, S5_SETS), :] = nr
                sim(b, c)[pl.ds(r0, S5_SETS), :] = ni
                out += [nr, ni]
        return tuple(out)

    init = []
    for b in range(nb):
        h_in = hst_re[b]
        g_in = hst_im[b]
        for c in range(halves):
            init += [h_in[:, c * LANES:(c + 1) * LANES], g_in[:, c * LANES:(c + 1) * LANES]]
    fin = lax.fori_loop(0, tc_len, step, tuple(init), unroll=8)
    for b in range(nb):
        for c in range(halves):
            hst_re[b, :, c * LANES:(c + 1) * LANES] = fin[2 * (b * halves + c)]
            hst_im[b, :, c * LANES:(c + 1) * LANES] = fin[2 * (b * halves + c) + 1]

    @pl.when(tci == last_step // tc_len)
    def _():
        r0 = (last_step % tc_len) * S5_SETS
        for b in range(nb):
            for c in range(halves):
                hre_ref[b, 0, :, c * LANES:(c + 1) * LANES] = sre(b, c)[pl.ds(r0, S5_SETS), :]
                him_ref[b, 0, :, c * LANES:(c + 1) * LANES] = sim(b, c)[pl.ds(r0, S5_SETS), :]

    for b in range(nb):
        for c in range(S5_BLK_CH // LANES):
            parts = []
            for s in range(2):
                j = 2 * c + s
                parts += [sre(b, k)[set_rows(j), :].astype(BF16) for k in range(halves)]
                parts += [sim(b, k)[set_rows(j), :].astype(BF16) for k in range(halves)]
            lhs = jnp.concatenate(parts, axis=1)
            y = jnp.dot(lhs, wc_ref[0, c], preferred_element_type=F32)
            y = y + d_ref[:, c * LANES:(c + 1) * LANES] * u_ref[b, :, c * LANES:(c + 1) * LANES]
            z_ref[b, :, c * LANES:(c + 1) * LANES] = jax.nn.gelu(y).astype(BF16)


def _s5_weights(lam_re, lam_im, log_step, b_re, b_im, c_re, c_im):
    g, p = lam_re.shape
    n = SSM_GROUP
    nblk = g // S5_BLK_GROUPS
    lam = lax.complex(lam_re.astype(F32), lam_im.astype(F32))
    dt = jnp.exp(log_step.astype(F32))[:, None]
    lam_bar = jnp.exp(lam * dt)
    b_bar = ((lam_bar - 1.0) / lam)[..., None] * lax.complex(b_re.astype(F32), b_im.astype(F32))
    eye_q = jnp.eye(S5_SET_GROUPS, dtype=F32)
    eye_h = jnp.eye(2, dtype=F32)

    def b_operand(bm):
        t = bm.reshape(nblk, S5_SETS // 2, 2, S5_SET_GROUPS, p, n)
        w = jnp.einsum("bthqpn,hk,qr->bthkrnqp", t, eye_h, eye_q)
        w = w.reshape(nblk, S5_SETS, LANES, S5_SET_LANES)
        return w

    wb = jnp.stack([b_operand(jnp.real(b_bar)), b_operand(jnp.imag(b_bar))], axis=2)
    wb = wb.astype(BF16)
    wb = jnp.concatenate([wb, wb], axis=3)

    def c_operand(cm):
        t = cm.reshape(nblk, S5_SETS // 2, 2, S5_SET_GROUPS, n, p)
        return jnp.einsum("bcsqnp,st,qr->bcsqptrn", t, eye_h, eye_q)

    cre = c_operand(c_re.astype(F32))
    cim = c_operand(-c_im.astype(F32))
    wc = jnp.stack([cre, cim], axis=3).reshape(nblk, S5_SETS // 2, 4 * S5_SET_LANES, LANES).astype(BF16)
    lre = jnp.real(lam_bar).reshape(nblk, S5_SETS, S5_SET_LANES)
    lim = jnp.imag(lam_bar).reshape(nblk, S5_SETS, S5_SET_LANES)
    return wb, wc, lre, lim


def s5_scan(u, h0_re, h0_im, weights, d_skip, seq_valid, tc_len):
    wb, wc, lre, lim = weights
    bsz, lp, d = u.shape
    nblk = d // S5_BLK_CH
    n_tc = lp // tc_len
    h0_re = h0_re.astype(F32).reshape(bsz, nblk, S5_SETS, S5_SET_LANES)
    h0_im = h0_im.astype(F32).reshape(bsz, nblk, S5_SETS, S5_SET_LANES)
    nb = math.gcd(bsz, 2)
    state_spec = pl.BlockSpec((nb, 1, S5_SETS, S5_SET_LANES), lambda k, b, t: (b, k, 0, 0))
    z, hre, him = pl.pallas_call(
        functools.partial(_s5_body, tc_len=tc_len, last_step=seq_valid - 1, nb=nb),
        grid=(nblk, bsz // nb, n_tc),
        in_specs=[pl.BlockSpec((nb, tc_len, S5_BLK_CH), lambda k, b, t: (b, t, k)),
                  pl.BlockSpec((1, S5_SETS, 2, 2 * LANES, S5_SET_LANES), lambda k, b, t: (k, 0, 0, 0, 0)),
                  pl.BlockSpec((1, S5_SETS // 2, 4 * S5_SET_LANES, LANES), lambda k, b, t: (k, 0, 0, 0)),
                  pl.BlockSpec((1, S5_SETS, S5_SET_LANES), lambda k, b, t: (k, 0, 0)),
                  pl.BlockSpec((1, S5_SETS, S5_SET_LANES), lambda k, b, t: (k, 0, 0)),
                  pl.BlockSpec((1, S5_BLK_CH), lambda k, b, t: (0, k)),
                  state_spec, state_spec],
        out_specs=[pl.BlockSpec((nb, tc_len, S5_BLK_CH), lambda k, b, t: (b, t, k)), state_spec, state_spec],
        out_shape=[jax.ShapeDtypeStruct((bsz, lp, d), BF16),
                   jax.ShapeDtypeStruct((bsz, nblk, S5_SETS, S5_SET_LANES), F32),
                   jax.ShapeDtypeStruct((bsz, nblk, S5_SETS, S5_SET_LANES), F32)],
        scratch_shapes=[pltpu.VMEM((tc_len * S5_SETS, LANES), F32)] * (nb * 2 * S5_SET_LANES // LANES) + [
                        pltpu.VMEM((nb, S5_SETS, S5_SET_LANES), F32),
                        pltpu.VMEM((nb, S5_SETS, S5_SET_LANES), F32)],
        compiler_params=_cparams(("arbitrary", "arbitrary", "arbitrary")),
        name="s5_scan",
    )(u, wb, wc, lre, lim, d_skip.reshape(1, d).astype(F32), h0_re, h0_im)
    g = d // SSM_GROUP
    return z, hre.reshape(bsz, g, SSM_STATE), him.reshape(bsz, g, SSM_STATE)


def _cmp_proj_body(pt_ref, x_hbm, w_ref, o_ref, stage, sems, *, pages_per_step):
    n_heads2 = 2 * N_KV_HEADS
    chunks = PAGE_SIZE // CMP_STRIDE
    step = pl.program_id(0) * pl.num_programs(1) + pl.program_id(1)
    n_steps = pl.num_programs(0) * pl.num_programs(1)
    slot = lax.rem(step, 2)

    def page_copy(st, sl, p, c):
        page = pt_ref[st * pages_per_step + p]
        return pltpu.make_async_copy(x_hbm.at[page, :, c, :], stage.at[sl, p, c], sems.at[sl])

    def start_step(st, sl):
        for p in range(pages_per_step):
            for c in range(n_heads2):
                page_copy(st, sl, p, c).start()

    @pl.when(step == 0)
    def _():
        start_step(step, slot)

    @pl.when(step + 1 < n_steps)
    def _():
        start_step(step + 1, 1 - slot)

    for p in range(pages_per_step):
        for c in range(n_heads2):
            page_copy(step, slot, p, c).wait()

    def chunk_rows(c, j):
        return jnp.concatenate([stage[slot, p, c, pl.ds(j, chunks, stride=CMP_STRIDE), :]
                                for p in range(pages_per_step)], axis=0)

    for c in range(n_heads2):
        s, h = divmod(c, N_KV_HEADS)
        acc = None
        for jp in range(CMP_STRIDE // 2):
            lhs = jnp.concatenate([chunk_rows(c, 2 * jp), chunk_rows(c, 2 * jp + 1)], axis=1).astype(BF16)
            t = jnp.dot(lhs, w_ref[s, jp], preferred_element_type=F32)
            acc = t if acc is None else acc + t
        o_ref[0, s, h] = acc


def cmp_project(rows, page_table, w1cat, pages_per_step=16):
    bsz, npg = page_table.shape
    pps = math.gcd(npg, pages_per_step)
    chunks = PAGE_SIZE // CMP_STRIDE
    hid2 = w1cat.shape[-1]
    grid_spec = pltpu.PrefetchScalarGridSpec(
        num_scalar_prefetch=1,
        grid=(bsz, npg // pps),
        in_specs=[pl.BlockSpec(memory_space=pl.ANY),
                  pl.BlockSpec(w1cat.shape, lambda b, g, pt: (0, 0, 0, 0))],
        out_specs=pl.BlockSpec((1, 2, N_KV_HEADS, pps * chunks, hid2), lambda b, g, pt: (b, 0, 0, g, 0)),
        scratch_shapes=[pltpu.VMEM((2, pps, 2 * N_KV_HEADS, PAGE_SIZE, HEAD_DIM), F32),
                        pltpu.SemaphoreType.DMA((2,))],
    )
    return pl.pallas_call(
        functools.partial(_cmp_proj_body, pages_per_step=pps),
        grid_spec=grid_spec,
        out_shape=jax.ShapeDtypeStruct((bsz, 2, N_KV_HEADS, npg * chunks, hid2), F32),
        compiler_params=_cparams(("arbitrary", "arbitrary")),
        name="cmp_project",
    )(page_table.reshape(-1).astype(I32), rows, w1cat)


def _cmp_finish_body(ab_ref, pe_ref, w1_ref, b1_ref, w2_ref, b2_ref, o_ref):
    ab = ab_ref[0, 0, 0]
    n16 = ab.shape[0]
    hid = ab.shape[1] // 2
    a = ab[:, :hid]
    bnext = pltpu.roll(ab[:, hid:], n16 - 1, axis=0)
    c = jnp.dot(pe_ref[0], w1_ref[0], preferred_element_type=F32)[0:1, :] + b1_ref[0]
    pre = a + bnext + c
    y = jnp.dot(jax.nn.gelu(pre).astype(BF16), w2_ref[0], preferred_element_type=F32) + b2_ref[0]
    row = lax.broadcasted_iota(I32, y.shape, 0)
    o_ref[0, 0, 0] = jnp.where(row < n16 - 1, y, 0.0).astype(o_ref.dtype)


def cmp_finish(ab, pe, w1, b1, w2, b2):
    bsz, _, kvh, n16, hid2 = ab.shape
    hid = hid2 // 2
    pe_rows = jnp.broadcast_to(pe.reshape(2, 1, -1), (2, SUBLANES, pe.shape[1] * pe.shape[2])).astype(BF16)
    return pl.pallas_call(
        _cmp_finish_body,
        grid=(2, bsz, kvh),
        in_specs=[pl.BlockSpec((1, 1, 1, n16, hid2), lambda s, b, h: (b, s, h, 0, 0)),
                  pl.BlockSpec((1, SUBLANES, pe_rows.shape[2]), lambda s, b, h: (s, 0, 0)),
                  pl.BlockSpec((1,) + w1.shape[1:], lambda s, b, h: (s, 0, 0)),
                  pl.BlockSpec((1, 1, hid), lambda s, b, h: (s, 0, 0)),
                  pl.BlockSpec((1, hid, HEAD_DIM), lambda s, b, h: (s, 0, 0)),
                  pl.BlockSpec((1, 1, HEAD_DIM), lambda s, b, h: (s, 0, 0))],
        out_specs=pl.BlockSpec((1, 1, 1, n16, HEAD_DIM), lambda s, b, h: (b, s, h, 0, 0)),
        out_shape=jax.ShapeDtypeStruct((bsz, 2, kvh, n16, HEAD_DIM), BF16),
        compiler_params=_cparams(("arbitrary", "arbitrary", "arbitrary")),
        name="cmp_finish",
    )(ab, pe_rows, w1.astype(BF16), b1.reshape(2, 1, hid).astype(F32), w2.astype(BF16),
      b2.reshape(2, 1, HEAD_DIM).astype(F32))


def _masked_softmax(s, mask):
    sm = jnp.where(mask, s, NEG_BIG)
    m = jnp.max(sm, axis=-1, keepdims=True)
    m = jnp.where(m > 0.5 * NEG_BIG, m, 0.0)
    e = jnp.where(mask, jnp.exp(sm - m), 0.0)
    den = jnp.sum(e, axis=-1, keepdims=True)
    return e / jnp.where(den > 0, den, 1.0)


def _dot_nt(a, b):
    return lax.dot_general(a, b, (((1,), (1,)), ((), ())), preferred_element_type=F32)


def _overlap(n_c, n_j, c_axis):
    shape = (n_c, n_j) if c_axis == 0 else (n_j, n_c)
    c = lax.broadcasted_iota(I32, shape, c_axis) * CMP_STRIDE
    j = lax.broadcasted_iota(I32, shape, 1 - c_axis) * SEL_BLOCK
    return ((c < j + SEL_BLOCK) & (c + CMP_BLOCK > j)).astype(F32)


def _block_scores(imp, q_pos, n_blk):
    blk = lax.broadcasted_iota(I32, imp.shape, 1)
    q_blk = jnp.right_shift(q_pos, SEL_SHIFT)
    forced = (blk == 0) | (blk == q_blk) | (blk == q_blk - 1)
    future = blk * SEL_BLOCK > q_pos
    v = jnp.where(future, NEG_BIG, jnp.where(forced, -NEG_BIG, imp))
    return jnp.where(blk < n_blk, v, 2.0 * NEG_BIG)


def _rank_desc(v, n):
    lane = lax.broadcasted_iota(I32, v.shape, 1)
    rank = jnp.zeros(v.shape, F32)
    for i in range(n):
        vi = v[:, i:i + 1]
        before = (vi > v) | ((vi == v) & (lane > i))
        rank = rank + jnp.where(before, 1.0, 0.0)
    return rank


def _nsa_prompt_body(q_ref, gt_ref, ck_ref, cvt_ref, ks_ref, vs_ref, kw_ref, vw_ref, o_ref, m_ref, acc_ref,
                     vst_ref, vwt_ref, *, n_cmp, n_blk, gqa, sel_tile, win_keys):
    qb = pl.program_id(2)
    nq = Q_BLOCK
    gq = gqa * nq
    c2 = (HEAD_DIM ** -0.5) * LOG2_E

    @pl.when(qb == 0)
    def _():
        ones = jnp.ones((ONES_ROWS, LANES), BF16)
        for src, dst in ((vs_ref, vst_ref), (vw_ref, vwt_ref)):
            for t in range(dst.shape[0]):
                dst[t, :HEAD_DIM, :] = src[0, t * LANES:(t + 1) * LANES, :].T.astype(BF16)
                dst[t, HEAD_DIM:, :] = ones

    qt = jnp.concatenate([q_ref[0, :, g * HEAD_DIM:(g + 1) * HEAD_DIM].astype(F32).T.astype(BF16)
                          for g in range(gqa)], axis=1)
    q_pos = qb * nq + lax.broadcasted_iota(I32, (1, nq), 1)

    def per_head(x):
        return jnp.concatenate([x] * gqa, axis=1)

    def normalise(acc):
        den = acc[HEAD_DIM:HEAD_DIM + 1, :]
        return acc[:HEAD_DIM, :] * (1.0 / jnp.where(den > 0, den, 1.0))

    ck = ck_ref[0, 0, 0]
    n16 = ck.shape[0]
    s = jnp.dot(ck, qt, preferred_element_type=F32)
    cidx = lax.broadcasted_iota(I32, (n16, 1), 0)
    cbias = jnp.where((cidx * CMP_STRIDE + (CMP_BLOCK - 1) <= q_pos) & (cidx < n_cmp), 0.0, NEG_BIG)
    sb = s + per_head(cbias)
    m = jnp.maximum(jnp.max(sb, axis=0, keepdims=True), M_FLOOR)
    e = jnp.exp2((sb - m) * c2)
    den = jnp.sum(e, axis=0, keepdims=True)
    p = e * jnp.where(den > 0, 1.0 / den, 0.0)
    o_cmp = jnp.dot(cvt_ref[0, 0], p.astype(BF16), preferred_element_type=F32)

    psum = p[:, 0:nq]
    for g in range(1, gqa):
        psum = psum + p[:, g * nq:(g + 1) * nq]
    nb_rows = -(-n_blk // ONES_ROWS) * ONES_ROWS
    imp = jnp.dot(_overlap(n16, nb_rows, 1), psum, preferred_element_type=F32, precision=lax.Precision.HIGHEST)
    blk = lax.broadcasted_iota(I32, (nb_rows, 1), 0)
    q_blk = jnp.right_shift(q_pos, SEL_SHIFT)
    forced = (blk == 0) | (blk == q_blk) | (blk == q_blk - 1)
    v = jnp.where(blk * SEL_BLOCK > q_pos, NEG_BIG, jnp.where(forced, -NEG_BIG, imp))
    v = jnp.where(blk < n_blk, v, 2.0 * NEG_BIG)
    rank = jnp.zeros(v.shape, F32)
    for i in range(n_blk):
        vi = v[i:i + 1, :]
        rank = rank + jnp.where((vi > v) | ((vi == v) & (blk > i)), 1.0, 0.0)
    sel_bias = jnp.where((rank < min(N_SEL, n_blk)) & (blk < n_blk), 0.0, NEG_BIG).astype(BF16)
    if nb_rows < LANES:
        sel_bias = jnp.concatenate([sel_bias, jnp.zeros((LANES - nb_rows, nq), BF16)], axis=0)

    m_ref[...] = jnp.full(m_ref.shape, M_FLOOR, F32)
    acc_ref[...] = jnp.zeros(acc_ref.shape, F32)
    tiles = sel_tile // LANES

    def sel_step(kt, _):
        k0 = pl.multiple_of(kt * sel_tile, sel_tile)
        s = jnp.dot(ks_ref[0, pl.ds(k0, sel_tile), :].astype(BF16), qt, preferred_element_type=F32)
        key = k0 + lax.broadcasted_iota(I32, (sel_tile, 1), 0)
        expand = jnp.where(jnp.right_shift(key, SEL_SHIFT) == lax.broadcasted_iota(I32, (sel_tile, LANES), 1), 1.0, 0.0)
        bias = jnp.dot(expand.astype(BF16), sel_bias, preferred_element_type=F32)
        bias = jnp.where(key <= q_pos, bias, NEG_BIG)
        sb = s + per_head(bias)
        m_old = m_ref[0:1, :]
        m_new = jnp.maximum(m_old, jnp.max(sb, axis=0, keepdims=True))
        alpha = jnp.exp2((m_old - m_new) * c2)
        e = jnp.exp2((sb - m_new) * c2).astype(BF16)
        vt = jnp.concatenate([vst_ref[kt * tiles + i] for i in range(tiles)], axis=1)
        acc_ref[...] = alpha * acc_ref[...] + jnp.dot(vt, e, preferred_element_type=F32)
        m_ref[...] = jnp.broadcast_to(m_new, m_ref.shape)
        return 0

    lax.fori_loop(0, (qb * nq + nq + sel_tile - 1) // sel_tile, sel_step, 0)
    o_sel = normalise(acc_ref[...])

    w_tile = jnp.maximum(qb + 1 - win_keys // nq, 0)
    w0 = pl.multiple_of(w_tile * nq, nq)
    s = jnp.dot(kw_ref[0, pl.ds(w0, win_keys), :].astype(BF16), qt, preferred_element_type=F32)
    key = w0 + lax.broadcasted_iota(I32, (win_keys, 1), 0)
    wbias = jnp.where((key <= q_pos) & (key > q_pos - WINDOW), 0.0, NEG_BIG)
    sb = s + per_head(wbias)
    m = jnp.maximum(jnp.max(sb, axis=0, keepdims=True), M_FLOOR)
    e = jnp.exp2((sb - m) * c2).astype(BF16)
    vt = jnp.concatenate([vwt_ref[w_tile + i] for i in range(win_keys // LANES)], axis=1)
    o_win = normalise(jnp.dot(vt, e, preferred_element_type=F32))

    gt = gt_ref[0, 0, 0]
    for g in range(gqa):
        cols = slice(g * nq, (g + 1) * nq)
        o = (gt[3 * g:3 * g + 1, :] * o_cmp[:, cols] + gt[3 * g + 1:3 * g + 2, :] * o_sel[:, cols]
             + gt[3 * g + 2:3 * g + 3, :] * o_win[:, cols])
        o_ref[0, :, g * HEAD_DIM:(g + 1) * HEAD_DIM] = o.T.astype(o_ref.dtype)


def nsa_prompt_attend(q, gates, cmp, k_sel, v_sel, k_win, v_win, bsz, t, sel_tile=512):
    hd_all = q.shape[1]
    kvh = N_KV_HEADS
    gqa = hd_all // HEAD_DIM // kvh
    n16 = cmp.shape[3]
    n_blk = -(-t // SEL_BLOCK)
    nqb = t // Q_BLOCK
    gq = gqa * Q_BLOCK
    vrows = HEAD_DIM + ONES_ROWS
    assert t % Q_BLOCK == 0 and n_blk <= LANES and n16 * CMP_STRIDE == t
    sel_tile = math.gcd(t, sel_tile)
    win_keys = min(WINDOW + Q_BLOCK, t)
    gt = gates[:, :3 * kvh * gqa].reshape(bsz, nqb, Q_BLOCK, kvh, 3 * gqa).transpose(0, 3, 1, 4, 2)
    gt = jnp.pad(gt, ((0, 0), (0, 0), (0, 0), (0, -3 * gqa % SUBLANES), (0, 0)))
    cvt = cmp[:, 1].transpose(0, 1, 3, 2)
    (ks_arr, ks_col), (vs_arr, vs_col), (kw_arr, kw_col), (vw_arr, vw_col) = k_sel, v_sel, k_win, v_win
    k_spec = lambda col: pl.BlockSpec((1, t, HEAD_DIM), lambda b, h, i, col=col: (b, 0, col + h))
    return pl.pallas_call(
        functools.partial(_nsa_prompt_body, n_cmp=n16 - 1, n_blk=n_blk, gqa=gqa, sel_tile=sel_tile, win_keys=win_keys),
        grid=(bsz, kvh, nqb),
        in_specs=[pl.BlockSpec((1, Q_BLOCK, gqa * HEAD_DIM), lambda b, h, i: (b, i, h)),
                  pl.BlockSpec((1, 1, 1, gt.shape[3], Q_BLOCK), lambda b, h, i: (b, h, i, 0, 0)),
                  pl.BlockSpec((1, 1, 1, n16, HEAD_DIM), lambda b, h, i: (b, 0, h, 0, 0)),
                  pl.BlockSpec((1, 1, HEAD_DIM, n16), lambda b, h, i: (b, h, 0, 0)),
                  k_spec(ks_col), k_spec(vs_col), k_spec(kw_col), k_spec(vw_col)],
        out_specs=pl.BlockSpec((1, Q_BLOCK, gqa * HEAD_DIM), lambda b, h, i: (b, i, h)),
        out_shape=jax.ShapeDtypeStruct((bsz, t, hd_all), BF16),
        scratch_shapes=[pltpu.VMEM((SUBLANES, gq), F32), pltpu.VMEM((vrows, gq), F32),
                        pltpu.VMEM((t // LANES, vrows, LANES), BF16), pltpu.VMEM((t // LANES, vrows, LANES), BF16)],
        compiler_params=_cparams(("parallel", "parallel", "arbitrary")),
        name="nsa_prompt",
    )(q.reshape(bsz, t, hd_all), gt, cmp, cvt, ks_arr.reshape(bsz, t, -1), vs_arr.reshape(bsz, t, -1),
      kw_arr.reshape(bsz, t, -1), vw_arr.reshape(bsz, t, -1)).reshape(bsz * t, hd_all)


def _nsa_sample_select_body(q_ref, ck_ref, cv_ref, ocmp_ref, idx_ref, *, n_cmp, n_blk, blk_lanes, past_len, tq_pad, gqa):
    scale = HEAD_DIM ** -0.5
    q = jnp.concatenate([q_ref[0, :, g * HEAD_DIM:(g + 1) * HEAD_DIM] for g in range(gqa)], axis=0)
    ck = ck_ref[0, 0, 0]
    cv = cv_ref[0, 0, 0]
    n16 = ck.shape[0]
    q_pos = past_len + lax.broadcasted_iota(I32, (tq_pad, 1), 0)
    s = (_dot_nt(q, ck) * scale).reshape(gqa, tq_pad, n16)
    cidx = lax.broadcasted_iota(I32, (tq_pad, n16), 1)
    cmask = (cidx * CMP_STRIDE + (CMP_BLOCK - 1) <= q_pos) & (cidx < n_cmp)
    p = _masked_softmax(s, cmask[None])
    ocmp_ref[0, 0] = jnp.dot(p.reshape(gqa * tq_pad, n16).astype(BF16), cv, preferred_element_type=F32)
    psum = jnp.sum(p, axis=0)
    imp = jnp.dot(psum, _overlap(n16, blk_lanes, 0), preferred_element_type=F32, precision=lax.Precision.HIGHEST)
    rank = _rank_desc(_block_scores(imp, q_pos, n_blk), n_blk)
    lane = lax.broadcasted_iota(I32, rank.shape, 1)
    lane_f = lane.astype(F32)
    out_lane = lax.broadcasted_iota(I32, (tq_pad, LANES), 1)
    idx = jnp.zeros((tq_pad, LANES), F32)
    for k in range(min(N_SEL, n_blk)):
        hit = (rank == float(k)) & (lane < n_blk)
        idx_k = jnp.sum(jnp.where(hit, lane_f, 0.0), axis=-1, keepdims=True)
        idx = jnp.where(out_lane == k, idx_k, idx)
    idx_ref[0, 0] = idx.astype(I32)


def _nsa_sample_attend_body(idx_ref, pt_ref, q_ref, gt_ref, ocmp_ref, kwc_ref, vwc_ref,
                            kwn_ref, vwn_ref, cache_ref, newblk_ref, o_ref, kbuf, vbuf, kwbuf, vwbuf, sems,
                            *, n_sel, n_cache_blk, n_pages, past_len, win_buf, tq, tq_pad, gqa):
    b = pl.program_id(0)
    h = pl.program_id(1)
    scale = HEAD_DIM ** -0.5
    blk_per_page = PAGE_SIZE // SEL_BLOCK

    bufs = ((2, kbuf, 0), (3, vbuf, 1))
    step = b * N_KV_HEADS + h
    n_steps = pl.num_programs(0) * N_KV_HEADS
    cur = _mod(step, 2)

    def from_new(bb, hh, sl, t, k, slot, buf, sem):
        return pltpu.make_async_copy(newblk_ref.at[bb, :, slot * N_KV_HEADS + hh], buf.at[sl, t, k], sems.at[sl, sem])

    def start_gather(st, sl):
        bb = _div(st, N_KV_HEADS)
        hh = _mod(st, N_KV_HEADS)
        for t in range(tq):
            for k in range(n_sel):
                blk = idx_ref[(st * tq + t) * n_sel + k]
                in_cache = blk < n_cache_blk
                blk_c = jnp.minimum(blk, n_cache_blk - 1)
                page = pt_ref[bb * n_pages + _div(blk_c, blk_per_page)]
                off = pl.multiple_of(_mod(blk_c, blk_per_page) * SEL_BLOCK, SEL_BLOCK)
                for slot, buf, sem in bufs:
                    @pl.when(in_cache)
                    def _():
                        pltpu.make_async_copy(cache_ref.at[page, pl.ds(off, SEL_BLOCK), slot * N_KV_HEADS + hh],
                                              buf.at[sl, t, k], sems.at[sl, sem]).start()

                    @pl.when(jnp.logical_not(in_cache))
                    def _():
                        from_new(bb, hh, sl, t, k, slot, buf, sem).start()

    @pl.when(step == 0)
    def _():
        start_gather(step, cur)

    @pl.when(step + 1 < n_steps)
    def _():
        start_gather(step + 1, 1 - cur)

    q = jnp.concatenate([q_ref[0, :, g * HEAD_DIM:(g + 1) * HEAD_DIM] for g in range(gqa)], axis=0)
    rows = gqa * tq_pad
    row_t = lax.rem(lax.broadcasted_iota(I32, (rows, 1), 0), tq_pad)
    q_pos = past_len + row_t

    kwbuf[...] = jnp.zeros(kwbuf.shape, F32)
    vwbuf[...] = jnp.zeros(vwbuf.shape, F32)
    kwbuf[0:win_buf, :] = kwc_ref[0]
    vwbuf[0:win_buf, :] = vwc_ref[0]
    kwbuf[win_buf:win_buf + tq_pad, :] = kwn_ref[0]
    vwbuf[win_buf:win_buf + tq_pad, :] = vwn_ref[0]
    nw = kwbuf.shape[0]
    wlane = lax.broadcasted_iota(I32, (rows, nw), 1)
    wpos = past_len - win_buf + wlane
    wmask = (wpos <= q_pos) & (wpos > q_pos - WINDOW) & (wpos >= 0) & (wlane < win_buf + tq)
    s = _dot_nt(q, kwbuf[...].astype(BF16)) * scale
    p = _masked_softmax(s, wmask)
    o_win = jnp.dot(p.astype(BF16), vwbuf[...].astype(BF16), preferred_element_type=F32)

    for t in range(tq):
        for k in range(n_sel):
            for slot, buf, sem in bufs:
                from_new(b, h, cur, t, k, slot, buf, sem).wait()

    o_sel = jnp.zeros((rows, HEAD_DIM), F32)
    nk = n_sel * SEL_BLOCK
    klane = lax.broadcasted_iota(I32, (rows, nk), 1)
    for t in range(tq):
        kpos = jnp.zeros((rows, nk), I32)
        for k in range(n_sel):
            blk = idx_ref[((b * N_KV_HEADS + h) * tq + t) * n_sel + k]
            kpos = jnp.where(jnp.right_shift(klane, SEL_SHIFT) == k,
                             blk * SEL_BLOCK + jnp.bitwise_and(klane, SEL_BLOCK - 1), kpos)
        mask = (kpos <= q_pos) & (row_t == t)
        kt = kbuf[cur, t].reshape(nk, HEAD_DIM).astype(BF16)
        vt = vbuf[cur, t].reshape(nk, HEAD_DIM).astype(BF16)
        p = _masked_softmax(_dot_nt(q, kt) * scale, mask)
        o_sel = o_sel + jnp.dot(p.astype(BF16), vt, preferred_element_type=F32)

    o_cmp = ocmp_ref[0, 0]
    gt = gt_ref[0, 0]
    for g in range(gqa):
        r = slice(g * tq_pad, (g + 1) * tq_pad)
        o = (gt[:, 3 * g:3 * g + 1] * o_cmp[r] + gt[:, 3 * g + 1:3 * g + 2] * o_sel[r]
             + gt[:, 3 * g + 2:3 * g + 3] * o_win[r])
        o_ref[0, :, g * HEAD_DIM:(g + 1) * HEAD_DIM] = o.astype(o_ref.dtype)


def nsa_sample_attend(q, gates, cmp, new_rows, win_new, cache_kv, cache_win, page_table, tq):
    bsz, tq_pad, hd_all = q.shape
    kvh = N_KV_HEADS
    gqa = hd_all // HEAD_DIM // kvh
    n_pages = page_table.shape[1]
    past_len = n_pages * PAGE_SIZE
    n16 = cmp.shape[3]
    n_cmp = n16 - 1
    n_cache_blk = past_len // SEL_BLOCK
    n_blk = -(-(past_len + tq) // SEL_BLOCK)
    assert n_blk == n_cache_blk + 1 and past_len % SEL_BLOCK == 0 and tq <= SEL_BLOCK
    n_sel = min(N_SEL, n_blk)
    blk_lanes = -(-n_blk // LANES) * LANES
    win_buf = cache_win.shape[1]
    cmp_spec = lambda s: pl.BlockSpec((1, 1, 1, n16, HEAD_DIM), lambda b, h, s=s: (b, s, h, 0, 0))
    ocmp, idx = pl.pallas_call(
        functools.partial(_nsa_sample_select_body, n_cmp=n_cmp, n_blk=n_blk, blk_lanes=blk_lanes,
                          past_len=past_len, tq_pad=tq_pad, gqa=gqa),
        grid=(bsz, kvh),
        in_specs=[pl.BlockSpec((1, tq_pad, gqa * HEAD_DIM), lambda b, h: (b, 0, h)), cmp_spec(0), cmp_spec(1)],
        out_specs=[pl.BlockSpec((1, 1, gqa * tq_pad, HEAD_DIM), lambda b, h: (b, h, 0, 0)),
                   pl.BlockSpec((1, 1, tq_pad, LANES), lambda b, h: (b, h, 0, 0))],
        out_shape=[jax.ShapeDtypeStruct((bsz, kvh, gqa * tq_pad, HEAD_DIM), F32),
                   jax.ShapeDtypeStruct((bsz, kvh, tq_pad, LANES), I32)],
        compiler_params=_cparams(("parallel", "parallel")),
        name="nsa_sample_select",
    )(q, cmp, cmp)
    idx_flat = idx[:, :, :tq, :n_sel].reshape(-1)

    n_slots = cache_kv.shape[2]
    cache4 = cache_kv.reshape(cache_kv.shape[0], PAGE_SIZE, n_slots * kvh, HEAD_DIM)
    newblk = jnp.zeros((bsz, SEL_BLOCK, n_slots * kvh, HEAD_DIM), F32).at[:, :tq].set(new_rows)
    cwin = cache_win.astype(F32).reshape(bsz, win_buf, 2 * kvh * HEAD_DIM)
    nw = -(-(win_buf + tq_pad) // LANES) * LANES
    kv_spec = lambda slot: pl.BlockSpec((1, tq_pad, HEAD_DIM), lambda b, h, *_, slot=slot: (b, 0, slot * kvh + h))
    cw_spec = lambda slot: pl.BlockSpec((1, win_buf, HEAD_DIM), lambda b, h, *_, slot=slot: (b, 0, slot * kvh + h))
    grid_spec = pltpu.PrefetchScalarGridSpec(
        num_scalar_prefetch=2,
        grid=(bsz, kvh),
        in_specs=[pl.BlockSpec((1, tq_pad, gqa * HEAD_DIM), lambda b, h, *_: (b, 0, h)),
                  pl.BlockSpec((1, 1, tq_pad, LANES), lambda b, h, *_: (b, h, 0, 0)),
                  pl.BlockSpec((1, 1, gqa * tq_pad, HEAD_DIM), lambda b, h, *_: (b, h, 0, 0)),
                  cw_spec(0), cw_spec(1), kv_spec(0), kv_spec(1),
                  pl.BlockSpec(memory_space=pl.ANY), pl.BlockSpec(memory_space=pl.ANY)],
        out_specs=pl.BlockSpec((1, tq_pad, gqa * HEAD_DIM), lambda b, h, *_: (b, 0, h)),
        scratch_shapes=[pltpu.VMEM((2, tq, n_sel, SEL_BLOCK, HEAD_DIM), F32),
                        pltpu.VMEM((2, tq, n_sel, SEL_BLOCK, HEAD_DIM), F32),
                        pltpu.VMEM((nw, HEAD_DIM), F32), pltpu.VMEM((nw, HEAD_DIM), F32),
                        pltpu.SemaphoreType.DMA((2, 2))],
    )
    return pl.pallas_call(
        functools.partial(_nsa_sample_attend_body, n_sel=n_sel, n_cache_blk=n_cache_blk, n_pages=n_pages,
                          past_len=past_len, win_buf=win_buf, tq=tq, tq_pad=tq_pad, gqa=gqa),
        grid_spec=grid_spec,
        out_shape=jax.ShapeDtypeStruct((bsz, tq_pad, hd_all), BF16),
        compiler_params=_cparams(("arbitrary", "arbitrary")),
        name="nsa_sample_attend",
    )(idx_flat, page_table.reshape(-1).astype(I32), q, gates, ocmp, cwin, cwin, win_new, win_new, cache4, newblk)


def _pad_rows(x, rows):
    return jnp.pad(x, ((0, 0), (0, rows - x.shape[1]), (0, 0)))


@jax.jit
def _step(x_prompt, x_sample, state_ssm_re, state_ssm_im, state_ffn_conv, cache_kv, cache_win, page_table,
          attn_norm, ffn_norm, final_norm, ssm_lam_re, ssm_lam_im, ssm_log_step, ssm_b_re, ssm_b_im,
          ssm_c_re, ssm_c_im, ssm_d, ssm_w_glu, ffn_w_in, ffn_conv_w, ffn_conv_b, ffn_w_down,
          kv_norm, w_kv, cmp_w1, cmp_b1, cmp_w2, cmp_b2, cmp_pe, w_qg, w_o):
    bp, tp, d = x_prompt.shape
    bs, ts, _ = x_sample.shape
    depth = attn_norm.shape[0]
    n_a = ssm_lam_re.shape[0]
    kvh = N_KV_HEADS
    hd_all = w_o.shape[1]
    n_heads = hd_all // HEAD_DIM
    gqa = n_heads // kvh
    ts_pad = -(-ts // SUBLANES) * SUBLANES
    n_pages = page_table.shape[1]

    xp = x_prompt.reshape(bp * tp, d).astype(F32)
    xs = x_sample.reshape(bs * ts, d).astype(F32)

    w1cat = jnp.concatenate([cmp_w1[:, :CMP_STRIDE * HEAD_DIM], cmp_w1[:, CMP_STRIDE * HEAD_DIM:]], axis=-1)
    w1cat = w1cat.reshape(2, CMP_STRIDE // 2, 2 * HEAD_DIM, w1cat.shape[-1]).astype(BF16)

    ssm_re_p, ssm_im_p, ssm_re_s, ssm_im_s, conv_p, conv_s = [], [], [], [], [], []
    outs = {}
    for layer in range(depth):
        if layer < n_a:
            a = layer
            wts = _s5_weights(ssm_lam_re[a], ssm_lam_im[a], ssm_log_step[a], ssm_b_re[a], ssm_b_im[a],
                              ssm_c_re[a], ssm_c_im[a])
            w_glu = cast_weight(ssm_w_glu, a)
            up = rmsnorm(xp, [attn_norm[layer]], F32)[0].reshape(bp, tp, d)
            zeros = jnp.zeros((bp, d // SSM_GROUP, SSM_STATE), F32)
            zp, hre, him = s5_scan(up, zeros, zeros, wts, ssm_d[a], tp, math.gcd(tp, 512))
            ssm_re_p.append(hre)
            ssm_im_p.append(him)
            xp = glu_matmul(zp.reshape(bp * tp, d), w_glu, xp)
            us = _pad_rows(rmsnorm(xs, [attn_norm[layer]], F32)[0].reshape(bs, ts, d), ts_pad)
            zs, hre, him = s5_scan(us, state_ssm_re[a], state_ssm_im[a], wts, ssm_d[a], ts, ts_pad)
            ssm_re_s.append(hre)
            ssm_im_s.append(him)
            xs = glu_matmul(zs[:, :ts].reshape(bs * ts, d), w_glu, xs)
        else:
            bl = layer - n_a
            if layer == n_a:
                w_kv_b = cast_weight(w_kv[None], 0)
                n_kv4 = 4 * kvh * HEAD_DIM
                sp, hp_first = rmsnorm(xp, [kv_norm, attn_norm[layer]], BF16)
                ss, hs_first = rmsnorm(xs, [kv_norm, attn_norm[layer]], BF16)
                kvr_p, kvw_p = matmul_split(sp, w_kv_b, n_kv4)
                kvr_s, kvw_s = matmul_split(ss, w_kv_b, n_kv4)
                outs["kv_rows_p"] = kvr_p.reshape(bp, tp, 4, kvh, HEAD_DIM)
                outs["kv_rows_s"] = kvr_s.reshape(bs, ts, 4, kvh, HEAD_DIM)
                outs["win_p"] = kvw_p.reshape(bp, tp, 2, kvh, HEAD_DIM)[:, max(tp - WINDOW, 0):]
                win_s = kvw_s.reshape(bs, ts, 2, kvh, HEAD_DIM)
                win_buf = cache_win.shape[1]
                outs["win_s"] = jnp.concatenate([cache_win.astype(F32), win_s], axis=1)[:, -win_buf:]
                kvw_s3 = _pad_rows(kvw_s.reshape(bs, ts, kvw_s.shape[1]), ts_pad)
                assert tp % PAGE_SIZE == 0 and (n_pages * PAGE_SIZE + ts) // CMP_STRIDE == n_pages * PAGE_SIZE // CMP_STRIDE
                pt_p = jnp.arange(bp * (tp // PAGE_SIZE), dtype=I32).reshape(bp, tp // PAGE_SIZE)
                ab_p = cmp_project(kvr_p.reshape(bp * tp // PAGE_SIZE, PAGE_SIZE, 4 * kvh, HEAD_DIM), pt_p, w1cat)
                cmp_p = cmp_finish(ab_p, cmp_pe, cmp_w1, cmp_b1, cmp_w2, cmp_b2)
                cache4 = cache_kv.astype(F32).reshape(cache_kv.shape[0], PAGE_SIZE, -1, HEAD_DIM)
                ab_s = cmp_project(cache4, page_table, w1cat)
                cmp_s = cmp_finish(ab_s, cmp_pe, cmp_w1, cmp_b1, cmp_w2, cmp_b2)
            w_qg_b = w_qg[bl].astype(BF16)
            w_q = w_qg_b[:, :hd_all]
            n_gate = w_qg.shape[2] - hd_all
            w_g = jnp.pad(w_qg_b[:, hd_all:], ((0, 0), (0, LANES - n_gate)))
            w_o_b = cast_weight(w_o, bl)

            def gate_layout(gt, bsz, t):
                gt = gt[:, :n_gate].reshape(bsz, t, kvh, 3 * gqa).transpose(0, 2, 1, 3)
                return jnp.pad(gt, ((0, 0), (0, 0), (0, 0), (0, LANES - 3 * gqa)))

            hp = hp_first if layer == n_a else rmsnorm(xp, [attn_norm[layer]], BF16)[0]
            q_p = matmul(hp, w_q, out_dtype=BF16, tm=1024, tn=512)
            g_p = matmul(hp, w_g, act="sigmoid", tm=1024)
            o_p = nsa_prompt_attend(q_p, g_p, cmp_p, (kvr_p, 2 * kvh), (kvr_p, 3 * kvh), (kvw_p, 0), (kvw_p, kvh), bp, tp)
            xp = matmul(o_p, w_o_b, res=xp, tm=1024, tn=512)

            hs = hs_first if layer == n_a else rmsnorm(xs, [attn_norm[layer]], BF16)[0]
            q_s = _pad_rows(matmul(hs, w_q, out_dtype=BF16).reshape(bs, ts, hd_all), ts_pad)
            g_s = gate_layout(matmul(hs, w_g, act="sigmoid"), bs, ts)
            g_s = jnp.pad(g_s, ((0, 0), (0, 0), (0, ts_pad - ts), (0, 0)))
            o_s = nsa_sample_attend(q_s, g_s, cmp_s, kvr_s.reshape(bs, ts, 4 * kvh, HEAD_DIM), kvw_s3,
                                    cache_kv.astype(F32), cache_win, page_table, ts)
            xs = matmul(o_s[:, :ts].reshape(bs * ts, hd_all), w_o_b, res=xs)

        w_down = cast_weight(ffn_w_down, layer)
        f = w_down.shape[0]
        act, st, act_s, st_s = ffn_up(rmsnorm(xp, [ffn_norm[layer]], BF16)[0], rmsnorm(xs, [ffn_norm[layer]], BF16)[0],
                                      ffn_w_in.astype(F32), layer, ffn_conv_w[layer], ffn_conv_b[layer],
                                      jnp.zeros((bp, CONV_W - 1, f), F32), state_ffn_conv[layer], tp, ts)
        conv_p.append(st)
        conv_s.append(st_s)
        xp = matmul(act, w_down, res=xp, tm=512, tn=512)
        xs = matmul(act_s, w_down, res=xs, tm=512, tn=256)

    y_p = rmsnorm(xp, [final_norm], F32)[0].reshape(bp, tp, d)
    y_s = rmsnorm(xs, [final_norm], F32)[0].reshape(bs, ts, d)
    return (y_p, y_s, jnp.stack(ssm_re_p), jnp.stack(ssm_im_p), jnp.stack(ssm_re_s), jnp.stack(ssm_im_s),
            jnp.stack(conv_p), jnp.stack(conv_s), outs["kv_rows_p"], outs["kv_rows_s"], outs["win_p"], outs["win_s"])


def kernel(x_prompt, x_sample, state_ssm_re, state_ssm_im, state_ffn_conv, cache_kv, cache_win, page_table, attn_norm, ffn_norm, final_norm, ssm_lam_re, ssm_lam_im, ssm_log_step, ssm_b_re, ssm_b_im, ssm_c_re, ssm_c_im, ssm_d, ssm_w_glu, ffn_w_in, ffn_conv_w, ffn_conv_b, ffn_w_down, kv_norm, w_kv, cmp_w1, cmp_b1, cmp_w2, cmp_b2, cmp_pe, w_qg, w_o):
    return _step(x_prompt, x_sample, state_ssm_re, state_ssm_im, state_ffn_conv, cache_kv, cache_win, page_table,
                 attn_norm, ffn_norm, final_norm, ssm_lam_re, ssm_lam_im, ssm_log_step, ssm_b_re, ssm_b_im,
                 ssm_c_re, ssm_c_im, ssm_d, ssm_w_glu, ffn_w_in, ffn_conv_w, ffn_conv_b, ffn_w_down,
                 kv_norm, w_kv, cmp_w1, cmp_b1, cmp_w2, cmp_b2, cmp_pe, w_qg, w_o)
```

```python
import functools
import math

import jax
import jax.numpy as jnp
from jax import lax
from jax.experimental import pallas as pl
from jax.experimental.pallas import tpu as pltpu

F32 = jnp.float32
BF16 = jnp.bfloat16
I32 = jnp.int32

RMS_EPS = 1e-6
LANES = 128
SUBLANES = 8
VMEM_LIMIT = 56 * 1024 * 1024
ROW_CHUNK = 128

SSM_GROUP = 16
SSM_STATE = 64
CONV_W = 3
HEAD_DIM = 128
N_KV_HEADS = 4
CMP_STRIDE = 16
CMP_BLOCK = 32
SEL_BLOCK = 64
SEL_SHIFT = SEL_BLOCK.bit_length() - 1
N_SEL = 16
WINDOW = 512
Q_BLOCK = 128
PAGE_SIZE = 128
NEG_BIG = -1e30
M_FLOOR = 0.1 * NEG_BIG
LOG2_E = math.log2(math.e)
ONES_ROWS = 16

S5_SET_GROUPS = 4
S5_SETS = 8
S5_BLK_GROUPS = S5_SET_GROUPS * S5_SETS
S5_BLK_CH = S5_BLK_GROUPS * SSM_GROUP
S5_SET_LANES = S5_SET_GROUPS * SSM_STATE


def _cparams(sem, vmem=VMEM_LIMIT):
    return pltpu.CompilerParams(dimension_semantics=sem, vmem_limit_bytes=vmem)


def _div(s, n):
    return jnp.right_shift(s, n.bit_length() - 1) if n & (n - 1) == 0 else s // n


def _mod(s, n):
    return jnp.bitwise_and(s, n - 1) if n & (n - 1) == 0 else s % n


def _cast_body(w_ref, o_ref):
    o_ref[...] = w_ref[...].astype(o_ref.dtype)


def cast_weight(w_all, layer, col0=0, ncols=None, tr=512):
    _, k, n = w_all.shape
    ncols = n - col0 if ncols is None else ncols
    tr = math.gcd(k, tr)
    tc = math.gcd(math.gcd(ncols, col0) if col0 else ncols, 4096)
    c0 = col0 // tc
    return pl.pallas_call(
        _cast_body,
        grid=(k // tr, ncols // tc),
        in_specs=[pl.BlockSpec((None, tr, tc), lambda i, j: (layer, i, c0 + j))],
        out_specs=pl.BlockSpec((tr, tc), lambda i, j: (i, j)),
        out_shape=jax.ShapeDtypeStruct((k, ncols), BF16),
        compiler_params=_cparams(("parallel", "parallel")),
        name="cast_weight",
    )(w_all.astype(F32))


def _rmsnorm_body(x_ref, g_ref, *o_refs):
    x = x_ref[...].astype(F32)
    y = x * lax.rsqrt(jnp.mean(x * x, axis=-1, keepdims=True) + RMS_EPS)
    for k, o_ref in enumerate(o_refs):
        o_ref[...] = (y * g_ref[k:k + 1, :]).astype(o_ref.dtype)


def rmsnorm(x, gains, out_dtype):
    m, d = x.shape
    tm = math.gcd(m, 256)
    n = len(gains)
    g = jnp.stack([gi.reshape(d).astype(F32) for gi in gains])
    outs = pl.pallas_call(
        _rmsnorm_body,
        grid=(m // tm,),
        in_specs=[pl.BlockSpec((tm, d), lambda i: (i, 0)), pl.BlockSpec((n, d), lambda i: (0, 0))],
        out_specs=[pl.BlockSpec((tm, d), lambda i: (i, 0))] * n,
        out_shape=[jax.ShapeDtypeStruct((m, d), out_dtype)] * n,
        compiler_params=_cparams(("parallel",)),
        name="rmsnorm",
    )(x, g)
    return outs


def _mm_body(*refs, has_res, act):
    if has_res:
        x_ref, w_ref, res_ref, o_ref = refs
    else:
        x_ref, w_ref, o_ref = refs
    tm = x_ref.shape[0]
    chunk = math.gcd(tm, ROW_CHUNK)
    for r in range(0, tm, chunk):
        y = jnp.dot(x_ref[r:r + chunk, :], w_ref[...], preferred_element_type=F32)
        if act == "sigmoid":
            y = jax.nn.sigmoid(y)
        if has_res:
            y = res_ref[r:r + chunk, :] + y
        o_ref[r:r + chunk, :] = y.astype(o_ref.dtype)


def matmul(x, w, res=None, out_dtype=F32, act=None, tm=512, tn=256):
    m, k = x.shape
    n = w.shape[1]
    tm = math.gcd(m, tm)
    tn = math.gcd(n, tn)
    in_specs = [pl.BlockSpec((tm, k), lambda i, j: (i, 0)), pl.BlockSpec((k, tn), lambda i, j: (0, j))]
    args = [x, w]
    if res is not None:
        in_specs.append(pl.BlockSpec((tm, tn), lambda i, j: (i, j)))
        args.append(res)
    return pl.pallas_call(
        functools.partial(_mm_body, has_res=res is not None, act=act),
        grid=(m // tm, n // tn),
        in_specs=in_specs,
        out_specs=pl.BlockSpec((tm, tn), lambda i, j: (i, j)),
        out_shape=jax.ShapeDtypeStruct((m, n), out_dtype),
        compiler_params=_cparams(("parallel", "arbitrary")),
        name="matmul",
    )(*args)


def _mm_split_body(x_ref, w_ref, o1_ref, o2_ref, *, n1_tiles):
    j = pl.program_id(1)
    y = jnp.dot(x_ref[...], w_ref[...], preferred_element_type=F32)

    @pl.when(j < n1_tiles)
    def _():
        o1_ref[...] = y

    @pl.when(j >= n1_tiles)
    def _():
        o2_ref[...] = y


def matmul_split(x, w, n1, tm=1024, tn=512):
    m, k = x.shape
    n = w.shape[1]
    tm = math.gcd(m, tm)
    tn = math.gcd(math.gcd(n1, n - n1), tn)
    t1 = n1 // tn
    return pl.pallas_call(
        functools.partial(_mm_split_body, n1_tiles=t1),
        grid=(m // tm, n // tn),
        in_specs=[pl.BlockSpec((tm, k), lambda i, j: (i, 0)), pl.BlockSpec((k, tn), lambda i, j: (0, j))],
        out_specs=[pl.BlockSpec((tm, tn), lambda i, j: (i, jnp.minimum(j, t1 - 1))),
                   pl.BlockSpec((tm, tn), lambda i, j: (i, jnp.maximum(j - t1, 0)))],
        out_shape=[jax.ShapeDtypeStruct((m, n1), F32), jax.ShapeDtypeStruct((m, n - n1), F32)],
        compiler_params=_cparams(("parallel", "arbitrary")),
        name="matmul_split",
    )(x, w)


def _glu_body(x_ref, wa_ref, wb_ref, res_ref, o_ref):
    tm = x_ref.shape[0]
    chunk = math.gcd(tm, ROW_CHUNK)
    for r in range(0, tm, chunk):
        x = x_ref[r:r + chunk, :]
        a = jnp.dot(x, wa_ref[...], preferred_element_type=F32)
        b = jnp.dot(x, wb_ref[...], preferred_element_type=F32)
        o_ref[r:r + chunk, :] = res_ref[r:r + chunk, :] + a * jax.nn.sigmoid(b)


def glu_matmul(x, w, res, tm=1024, tn=512):
    m, k = x.shape
    n = w.shape[1] // 2
    tm = math.gcd(m, tm)
    tn = math.gcd(n, tn)
    nt = n // tn
    return pl.pallas_call(
        _glu_body,
        grid=(m // tm, nt),
        in_specs=[pl.BlockSpec((tm, k), lambda i, j: (i, 0)),
                  pl.BlockSpec((k, tn), lambda i, j: (0, j)),
                  pl.BlockSpec((k, tn), lambda i, j: (0, j + nt)),
                  pl.BlockSpec((tm, tn), lambda i, j: (i, j))],
        out_specs=pl.BlockSpec((tm, tn), lambda i, j: (i, j)),
        out_shape=jax.ShapeDtypeStruct((m, n), F32),
        compiler_params=_cparams(("parallel", "arbitrary")),
        name="glu_matmul",
    )(x, w, w, res)


def _conv_gate(g, g1, g2, cw_ref, cb_ref, val):
    gc = cb_ref[...] + cw_ref[0:1, :] * g2 + cw_ref[1:2, :] * g1 + cw_ref[2:3, :] * g
    return (gc * jax.nn.sigmoid(gc) * val).astype(BF16)


def _ffn_up_body(*refs, n_m, tiles_per_seq, short_len, chunk, n_slabs):
    x_slabs = refs[:n_slabs]
    (xs_ref, wv_ref, wg_ref, cw_ref, cb_ref, p1_ref, p2_ref, cwp_ref, cbp_ref, init_ref, wd_ref,
     act_ref, st_ref, acts_ref, gs_ref, wdb_ref, wv_s, wg_s, val_a, g_a, val_b, g_b) = refs[n_slabs:]
    slab_rows = x_slabs[0].shape[0]
    s = pl.program_id(0)
    i = _mod(s, n_m)

    @pl.when(s == 0)
    def _():
        val_b[...] = jnp.zeros(val_b.shape, F32)
        g_b[...] = jnp.zeros(g_b.shape, F32)

    @pl.when(i == 0)
    def _():
        wv_s[...] = wv_ref[...].astype(BF16)
        wg_s[...] = wg_ref[...].astype(BF16)
        xs = xs_ref[...]
        val = jnp.dot(xs, wv_s[...], preferred_element_type=F32)
        g = jnp.dot(xs, wg_s[...], preferred_element_type=F32)
        tmod = lax.rem(lax.broadcasted_iota(I32, g.shape, 0), short_len)
        g1 = jnp.where(tmod >= 1, pltpu.roll(g, 1, axis=0), p1_ref[...])
        g2 = jnp.where(tmod >= 2, pltpu.roll(g, 2, axis=0), p2_ref[...])
        acts_ref[...] = _conv_gate(g, g1, g2, cw_ref, cb_ref, val)
        gs_ref[...] = g

    def step(val_w, g_w, val_r, g_r):
        tm = n_slabs * slab_rows
        wdb_ref[...] = wd_ref[...].astype(BF16)
        for r in range(0, tm, chunk):
            x = x_slabs[r // slab_rows][r % slab_rows:r % slab_rows + chunk, :]
            val_w[r:r + chunk, :] = jnp.dot(x, wv_s[...], preferred_element_type=F32)
            g_w[SUBLANES + r:SUBLANES + r + chunk, :] = jnp.dot(x, wg_s[...], preferred_element_type=F32)
            gext = g_r[r:r + chunk + SUBLANES, :]
            g1 = pltpu.roll(gext, 1, axis=0)[SUBLANES:]
            g2 = pltpu.roll(gext, 2, axis=0)[SUBLANES:]
            act_ref[r:r + chunk, :] = _conv_gate(gext[SUBLANES:], g1, g2, cwp_ref, cbp_ref, val_r[r:r + chunk, :])
        tail = g_r[tm:, :]
        st_ref[0] = tail
        g_w[:SUBLANES, :] = jnp.where(_mod(i, tiles_per_seq) == 0, init_ref[0], tail)

    parity = _mod(s, 2)

    @pl.when(parity == 0)
    def _():
        step(val_a, g_a, val_b, g_b)

    @pl.when(parity == 1)
    def _():
        step(val_b, g_b, val_a, g_a)


def ffn_up(h, hs, w_in_all, w_down_all, layer, conv_w, conv_b, buf, buf_s, seq_len, seq_short, tm=1024, tn=256,
           n_slabs=1):
    m, k = h.shape
    ms = hs.shape[0]
    f = w_in_all.shape[2] // 2
    bsz = m // seq_len
    bs = ms // seq_short
    assert seq_len % SUBLANES == 0 and seq_len >= 2 * SUBLANES and seq_short >= 2
    tn = math.gcd(f, tn)
    n_n = f // tn
    tm = math.gcd(seq_len, tm)
    tps = seq_len // tm
    n_m = m // tm
    chunk = math.gcd(tm, ROW_CHUNK)
    n_slabs = math.gcd(tm // chunk, n_slabs)
    slab = tm // n_slabs
    n_steps = n_n * n_m
    d_out = w_down_all.shape[2]
    rd = f // n_steps
    assert f % n_steps == 0 and rd % ONES_ROWS == 0
    cw = jnp.zeros((SUBLANES, f), F32).at[:CONV_W].set(conv_w.astype(F32))
    cb = conv_b.reshape(1, f).astype(F32)
    init = jnp.zeros((bsz, SUBLANES, f), F32).at[:, SUBLANES - 2:].set(buf.astype(F32))
    buf_s = buf_s.astype(F32)
    pos = jnp.arange(ms) % seq_short
    b0 = jnp.repeat(buf_s[:, 0], seq_short, axis=0)
    b1 = jnp.repeat(buf_s[:, 1], seq_short, axis=0)
    p1 = jnp.where((pos == 0)[:, None], b1, 0.0)
    p2 = jnp.where((pos == 0)[:, None], b0, jnp.where((pos == 1)[:, None], b1, 0.0))

    def col(s):
        return jnp.minimum(_div(s, n_m), n_n - 1)

    def prev(s):
        sp = jnp.maximum(s - 1, 0)
        return _mod(sp, n_m), _div(sp, n_m)

    act, st, act_s, g_s, w_down_b = pl.pallas_call(
        functools.partial(_ffn_up_body, n_m=n_m, tiles_per_seq=tps, short_len=seq_short, chunk=chunk, n_slabs=n_slabs),
        grid=(n_steps + 1,),
        in_specs=[pl.BlockSpec((slab, k), lambda s, c=c: (_mod(s, n_m) * n_slabs + c, 0)) for c in range(n_slabs)] + [
                  pl.BlockSpec((ms, k), lambda s: (0, 0)),
                  pl.BlockSpec((None, k, tn), lambda s: (layer, 0, col(s))),
                  pl.BlockSpec((None, k, tn), lambda s: (layer, 0, col(s) + n_n)),
                  pl.BlockSpec((SUBLANES, tn), lambda s: (0, col(s))),
                  pl.BlockSpec((1, tn), lambda s: (0, col(s))),
                  pl.BlockSpec((ms, tn), lambda s: (0, col(s))),
                  pl.BlockSpec((ms, tn), lambda s: (0, col(s))),
                  pl.BlockSpec((SUBLANES, tn), lambda s: (0, prev(s)[1])),
                  pl.BlockSpec((1, tn), lambda s: (0, prev(s)[1])),
                  pl.BlockSpec((1, SUBLANES, tn), lambda s: (_div(_mod(s, n_m), tps), 0, col(s))),
                  pl.BlockSpec((None, rd, d_out), lambda s: (layer, jnp.minimum(s, n_steps - 1), 0))],
        out_specs=[pl.BlockSpec((tm, tn), lambda s: prev(s)),
                   pl.BlockSpec((1, SUBLANES, tn), lambda s: (prev(s)[0], 0, prev(s)[1])),
                   pl.BlockSpec((ms, tn), lambda s: (0, col(s))),
                   pl.BlockSpec((ms, tn), lambda s: (0, col(s))),
                   pl.BlockSpec((rd, d_out), lambda s: (jnp.minimum(s, n_steps - 1), 0))],
        out_shape=[jax.ShapeDtypeStruct((m, f), BF16), jax.ShapeDtypeStruct((n_m, SUBLANES, f), F32),
                   jax.ShapeDtypeStruct((ms, f), BF16), jax.ShapeDtypeStruct((ms, f), F32),
                   jax.ShapeDtypeStruct((f, d_out), BF16)],
        scratch_shapes=[pltpu.VMEM((k, tn), BF16), pltpu.VMEM((k, tn), BF16)]
                       + [pltpu.VMEM((tm, tn), F32), pltpu.VMEM((tm + SUBLANES, tn), F32)] * 2,
        compiler_params=_cparams(("arbitrary",)),
        name="ffn_up",
    )(*([h] * n_slabs), hs, w_in_all, w_in_all, cw, cb, p1, p2, cw, cb, init, w_down_all)
    return (act, st[tps - 1::tps, SUBLANES - 2:], act_s, g_s.reshape(bs, seq_short, f)[:, seq_short - 2:], w_down_b)


def _s5_body(*refs, tc_len, last_step, nb):
    (u_ref, wb_ref, wc_ref, lre_ref, lim_ref, d_ref, h0re_ref, h0im_ref, z_ref, hre_ref, him_ref) = refs[:11]
    halves = S5_SET_LANES // LANES
    n_buf = 2 * halves
    bufs = refs[11:11 + nb * n_buf]
    hst_re, hst_im = refs[11 + nb * n_buf:]
    tci = pl.program_id(2)

    def sre(b, c):
        return bufs[b * n_buf + c]

    def sim(b, c):
        return bufs[b * n_buf + halves + c]

    def set_rows(j):
        return pl.ds(j, tc_len, stride=S5_SETS)

    for b in range(nb):
        for ti in range(S5_BLK_CH // LANES):
            ut = u_ref[b, :, ti * LANES:(ti + 1) * LANES]
            hi = ut.astype(BF16)
            lo = (ut - hi.astype(F32)).astype(BF16)
            lhs = jnp.concatenate([hi, lo], axis=1)
            for jj in range(2):
                j = ti * 2 + jj
                bre = jnp.dot(lhs, wb_ref[0, j, 0], preferred_element_type=F32)
                bim = jnp.dot(lhs, wb_ref[0, j, 1], preferred_element_type=F32)
                for c in range(halves):
                    sre(b, c)[set_rows(j), :] = bre[:, c * LANES:(c + 1) * LANES]
                    sim(b, c)[set_rows(j), :] = bim[:, c * LANES:(c + 1) * LANES]

    @pl.when(tci == 0)
    def _():
        for b in range(nb):
            hst_re[b] = h0re_ref[b, 0]
            hst_im[b] = h0im_ref[b, 0]

    lam_r = lre_ref[0]
    lam_i = lim_ref[0]
    lr = [lam_r[:, c * LANES:(c + 1) * LANES] for c in range(halves)]
    li = [lam_i[:, c * LANES:(c + 1) * LANES] for c in range(halves)]

    def step(t, carry):
        r0 = pl.multiple_of(t * S5_SETS, S5_SETS)
        out = []
        for b in range(nb):
            for c in range(halves):
                hr, hi_ = carry[2 * (b * halves + c)], carry[2 * (b * halves + c) + 1]
                nr = lr[c] * hr - li[c] * hi_ + sre(b, c)[pl.ds(r0, S5_SETS), :]
                ni = lr[c] * hi_ + li[c] * hr + sim(b, c)[pl.ds(r0, S5_SETS), :]
                sre(b, c)[pl.ds(r0, S5_SETS), :] = nr
                sim(b, c)[pl.ds(r0, S5_SETS), :] = ni
                out += [nr, ni]
        return tuple(out)

    init = []
    for b in range(nb):
        h_in = hst_re[b]
        g_in = hst_im[b]
        for c in range(halves):
            init += [h_in[:, c * LANES:(c + 1) * LANES], g_in[:, c * LANES:(c + 1) * LANES]]
    fin = lax.fori_loop(0, tc_len, step, tuple(init), unroll=8)
    for b in range(nb):
        for c in range(halves):
            hst_re[b, :, c * LANES:(c + 1) * LANES] = fin[2 * (b * halves + c)]
            hst_im[b, :, c * LANES:(c + 1) * LANES] = fin[2 * (b * halves + c) + 1]

    @pl.when(tci == last_step // tc_len)
    def _():
        r0 = (last_step % tc_len) * S5_SETS
        for b in range(nb):
            for c in range(halves):
                hre_ref[b, 0, :, c * LANES:(c + 1) * LANES] = sre(b, c)[pl.ds(r0, S5_SETS), :]
                him_ref[b, 0, :, c * LANES:(c + 1) * LANES] = sim(b, c)[pl.ds(r0, S5_SETS), :]

    for b in range(nb):
        for c in range(S5_BLK_CH // LANES):
            parts = []
            for s in range(2):
                j = 2 * c + s
                parts += [sre(b, k)[set_rows(j), :].astype(BF16) for k in range(halves)]
                parts += [sim(b, k)[set_rows(j), :].astype(BF16) for k in range(halves)]
            lhs = jnp.concatenate(parts, axis=1)
            y = jnp.dot(lhs, wc_ref[0, c], preferred_element_type=F32)
            y = y + d_ref[:, c * LANES:(c + 1) * LANES] * u_ref[b, :, c * LANES:(c + 1) * LANES]
            z_ref[b, :, c * LANES:(c + 1) * LANES] = jax.nn.gelu(y).astype(BF16)


def _s5_weights(lam_re, lam_im, log_step, b_re, b_im, c_re, c_im):
    g, p = lam_re.shape
    n = SSM_GROUP
    nblk = g // S5_BLK_GROUPS
    lam = lax.complex(lam_re.astype(F32), lam_im.astype(F32))
    dt = jnp.exp(log_step.astype(F32))[:, None]
    lam_bar = jnp.exp(lam * dt)
    b_bar = ((lam_bar - 1.0) / lam)[..., None] * lax.complex(b_re.astype(F32), b_im.astype(F32))
    eye_q = jnp.eye(S5_SET_GROUPS, dtype=F32)
    eye_h = jnp.eye(2, dtype=F32)

    def b_operand(bm):
        t = bm.reshape(nblk, S5_SETS // 2, 2, S5_SET_GROUPS, p, n)
        w = jnp.einsum("bthqpn,hk,qr->bthkrnqp", t, eye_h, eye_q)
        w = w.reshape(nblk, S5_SETS, LANES, S5_SET_LANES)
        return w

    wb = jnp.stack([b_operand(jnp.real(b_bar)), b_operand(jnp.imag(b_bar))], axis=2)
    wb = wb.astype(BF16)
    wb = jnp.concatenate([wb, wb], axis=3)

    def c_operand(cm):
        t = cm.reshape(nblk, S5_SETS // 2, 2, S5_SET_GROUPS, n, p)
        return jnp.einsum("bcsqnp,st,qr->bcsqptrn", t, eye_h, eye_q)

    cre = c_operand(c_re.astype(F32))
    cim = c_operand(-c_im.astype(F32))
    wc = jnp.stack([cre, cim], axis=3).reshape(nblk, S5_SETS // 2, 4 * S5_SET_LANES, LANES).astype(BF16)
    lre = jnp.real(lam_bar).reshape(nblk, S5_SETS, S5_SET_LANES)
    lim = jnp.imag(lam_bar).reshape(nblk, S5_SETS, S5_SET_LANES)
    return wb, wc, lre, lim


def s5_scan(u, h0_re, h0_im, weights, d_skip, seq_valid, tc_len):
    wb, wc, lre, lim = weights
    bsz, lp, d = u.shape
    nblk = d // S5_BLK_CH
    n_tc = lp // tc_len
    h0_re = h0_re.astype(F32).reshape(bsz, nblk, S5_SETS, S5_SET_LANES)
    h0_im = h0_im.astype(F32).reshape(bsz, nblk, S5_SETS, S5_SET_LANES)
    nb = math.gcd(bsz, 2)
    state_spec = pl.BlockSpec((nb, 1, S5_SETS, S5_SET_LANES), lambda k, b, t: (b, k, 0, 0))
    z, hre, him = pl.pallas_call(
        functools.partial(_s5_body, tc_len=tc_len, last_step=seq_valid - 1, nb=nb),
        grid=(nblk, bsz // nb, n_tc),
        in_specs=[pl.BlockSpec((nb, tc_len, S5_BLK_CH), lambda k, b, t: (b, t, k)),
                  pl.BlockSpec((1, S5_SETS, 2, 2 * LANES, S5_SET_LANES), lambda k, b, t: (k, 0, 0, 0, 0)),
                  pl.BlockSpec((1, S5_SETS // 2, 4 * S5_SET_LANES, LANES), lambda k, b, t: (k, 0, 0, 0)),
                  pl.BlockSpec((1, S5_SETS, S5_SET_LANES), lambda k, b, t: (k, 0, 0)),
                  pl.BlockSpec((1, S5_SETS, S5_SET_LANES), lambda k, b, t: (k, 0, 0)),
                  pl.BlockSpec((1, S5_BLK_CH), lambda k, b, t: (0, k)),
                  state_spec, state_spec],
        out_specs=[pl.BlockSpec((nb, tc_len, S5_BLK_CH), lambda k, b, t: (b, t, k)), state_spec, state_spec],
        out_shape=[jax.ShapeDtypeStruct((bsz, lp, d), BF16),
                   jax.ShapeDtypeStruct((bsz, nblk, S5_SETS, S5_SET_LANES), F32),
                   jax.ShapeDtypeStruct((bsz, nblk, S5_SETS, S5_SET_LANES), F32)],
        scratch_shapes=[pltpu.VMEM((tc_len * S5_SETS, LANES), F32)] * (nb * 2 * S5_SET_LANES // LANES) + [
                        pltpu.VMEM((nb, S5_SETS, S5_SET_LANES), F32),
                        pltpu.VMEM((nb, S5_SETS, S5_SET_LANES), F32)],
        compiler_params=_cparams(("arbitrary", "arbitrary", "arbitrary")),
        name="s5_scan",
    )(u, wb, wc, lre, lim, d_skip.reshape(1, d).astype(F32), h0_re, h0_im)
    g = d // SSM_GROUP
    return z, hre.reshape(bsz, g, SSM_STATE), him.reshape(bsz, g, SSM_STATE)


def _cmp_proj_body(pt_ref, x_hbm, w_ref, o_ref, stage, sems, *, pages_per_step):
    n_heads2 = 2 * N_KV_HEADS
    chunks = PAGE_SIZE // CMP_STRIDE
    step = pl.program_id(0) * pl.num_programs(1) + pl.program_id(1)
    n_steps = pl.num_programs(0) * pl.num_programs(1)
    slot = lax.rem(step, 2)

    def page_copy(st, sl, p, c):
        page = pt_ref[st * pages_per_step + p]
        return pltpu.make_async_copy(x_hbm.at[page, :, c, :], stage.at[sl, p, c], sems.at[sl])

    def start_step(st, sl):
        for p in range(pages_per_step):
            for c in range(n_heads2):
                page_copy(st, sl, p, c).start()

    @pl.when(step == 0)
    def _():
        start_step(step, slot)

    @pl.when(step + 1 < n_steps)
    def _():
        start_step(step + 1, 1 - slot)

    for p in range(pages_per_step):
        for c in range(n_heads2):
            page_copy(step, slot, p, c).wait()

    def chunk_rows(c, j):
        return jnp.concatenate([stage[slot, p, c, pl.ds(j, chunks, stride=CMP_STRIDE), :]
                                for p in range(pages_per_step)], axis=0)

    for c in range(n_heads2):
        s, h = divmod(c, N_KV_HEADS)
        acc = None
        for jp in range(CMP_STRIDE // 2):
            lhs = jnp.concatenate([chunk_rows(c, 2 * jp), chunk_rows(c, 2 * jp + 1)], axis=1).astype(BF16)
            t = jnp.dot(lhs, w_ref[s, jp], preferred_element_type=F32)
            acc = t if acc is None else acc + t
        o_ref[0, s, h] = acc


def cmp_project(rows, page_table, w1cat, pages_per_step=16):
    bsz, npg = page_table.shape
    pps = math.gcd(npg, pages_per_step)
    chunks = PAGE_SIZE // CMP_STRIDE
    hid2 = w1cat.shape[-1]
    grid_spec = pltpu.PrefetchScalarGridSpec(
        num_scalar_prefetch=1,
        grid=(bsz, npg // pps),
        in_specs=[pl.BlockSpec(memory_space=pl.ANY),
                  pl.BlockSpec(w1cat.shape, lambda b, g, pt: (0, 0, 0, 0))],
        out_specs=pl.BlockSpec((1, 2, N_KV_HEADS, pps * chunks, hid2), lambda b, g, pt: (b, 0, 0, g, 0)),
        scratch_shapes=[pltpu.VMEM((2, pps, 2 * N_KV_HEADS, PAGE_SIZE, HEAD_DIM), F32),
                        pltpu.SemaphoreType.DMA((2,))],
    )
    return pl.pallas_call(
        functools.partial(_cmp_proj_body, pages_per_step=pps),
        grid_spec=grid_spec,
        out_shape=jax.ShapeDtypeStruct((bsz, 2, N_KV_HEADS, npg * chunks, hid2), F32),
        compiler_params=_cparams(("arbitrary", "arbitrary")),
        name="cmp_project",
    )(page_table.reshape(-1).astype(I32), rows, w1cat)


def _cmp_finish_body(ab_ref, pe_ref, w1_ref, b1_ref, w2_ref, b2_ref, o_ref):
    ab = ab_ref[0, 0, 0]
    n16 = ab.shape[0]
    hid = ab.shape[1] // 2
    a = ab[:, :hid]
    bnext = pltpu.roll(ab[:, hid:], n16 - 1, axis=0)
    c = jnp.dot(pe_ref[0], w1_ref[0], preferred_element_type=F32)[0:1, :] + b1_ref[0]
    pre = a + bnext + c
    y = jnp.dot(jax.nn.gelu(pre).astype(BF16), w2_ref[0], preferred_element_type=F32) + b2_ref[0]
    row = lax.broadcasted_iota(I32, y.shape, 0)
    o_ref[0, 0, 0] = jnp.where(row < n16 - 1, y, 0.0).astype(o_ref.dtype)


def cmp_finish(ab, pe, w1, b1, w2, b2):
    bsz, _, kvh, n16, hid2 = ab.shape
    hid = hid2 // 2
    pe_rows = jnp.broadcast_to(pe.reshape(2, 1, -1), (2, SUBLANES, pe.shape[1] * pe.shape[2])).astype(BF16)
    return pl.pallas_call(
        _cmp_finish_body,
        grid=(2, bsz, kvh),
        in_specs=[pl.BlockSpec((1, 1, 1, n16, hid2), lambda s, b, h: (b, s, h, 0, 0)),
                  pl.BlockSpec((1, SUBLANES, pe_rows.shape[2]), lambda s, b, h: (s, 0, 0)),
                  pl.BlockSpec((1,) + w1.shape[1:], lambda s, b, h: (s, 0, 0)),
                  pl.BlockSpec((1, 1, hid), lambda s, b, h: (s, 0, 0)),
                  pl.BlockSpec((1, hid, HEAD_DIM), lambda s, b, h: (s, 0, 0)),
                  pl.BlockSpec((1, 1, HEAD_DIM), lambda s, b, h: (s, 0, 0))],
        out_specs=pl.BlockSpec((1, 1, 1, n16, HEAD_DIM), lambda s, b, h: (b, s, h, 0, 0)),
        out_shape=jax.ShapeDtypeStruct((bsz, 2, kvh, n16, HEAD_DIM), BF16),
        compiler_params=_cparams(("arbitrary", "arbitrary", "arbitrary")),
        name="cmp_finish",
    )(ab, pe_rows, w1.astype(BF16), b1.reshape(2, 1, hid).astype(F32), w2.astype(BF16),
      b2.reshape(2, 1, HEAD_DIM).astype(F32))


def _masked_softmax(s, mask):
    sm = jnp.where(mask, s, NEG_BIG)
    m = jnp.max(sm, axis=-1, keepdims=True)
    m = jnp.where(m > 0.5 * NEG_BIG, m, 0.0)
    e = jnp.where(mask, jnp.exp(sm - m), 0.0)
    den = jnp.sum(e, axis=-1, keepdims=True)
    return e / jnp.where(den > 0, den, 1.0)


def _dot_nt(a, b):
    return lax.dot_general(a, b, (((1,), (1,)), ((), ())), preferred_element_type=F32)


def _overlap(n_c, n_j, c_axis):
    shape = (n_c, n_j) if c_axis == 0 else (n_j, n_c)
    c = lax.broadcasted_iota(I32, shape, c_axis) * CMP_STRIDE
    j = lax.broadcasted_iota(I32, shape, 1 - c_axis) * SEL_BLOCK
    return ((c < j + SEL_BLOCK) & (c + CMP_BLOCK > j)).astype(F32)


def _block_scores(imp, q_pos, n_blk):
    blk = lax.broadcasted_iota(I32, imp.shape, 1)
    q_blk = jnp.right_shift(q_pos, SEL_SHIFT)
    forced = (blk == 0) | (blk == q_blk) | (blk == q_blk - 1)
    future = blk * SEL_BLOCK > q_pos
    v = jnp.where(future, NEG_BIG, jnp.where(forced, -NEG_BIG, imp))
    return jnp.where(blk < n_blk, v, 2.0 * NEG_BIG)


def _rank_desc(v, n):
    lane = lax.broadcasted_iota(I32, v.shape, 1)
    rank = jnp.zeros(v.shape, F32)
    for i in range(n):
        vi = v[:, i:i + 1]
        before = (vi > v) | ((vi == v) & (lane > i))
        rank = rank + jnp.where(before, 1.0, 0.0)
    return rank


def _nsa_prompt_body(q_ref, gt_ref, ck_ref, cvt_ref, ks_ref, vs_ref, kw_ref, vw_ref, o_ref, m_ref, acc_ref,
                     vst_ref, vwt_ref, *, n_cmp, n_blk, gqa, sel_tile, win_keys):
    qb = pl.program_id(2)
    nq = Q_BLOCK
    gq = gqa * nq
    c2 = (HEAD_DIM ** -0.5) * LOG2_E

    @pl.when(qb == 0)
    def _():
        ones = jnp.ones((ONES_ROWS, LANES), BF16)
        for src, dst in ((vs_ref, vst_ref), (vw_ref, vwt_ref)):
            for t in range(dst.shape[0]):
                dst[t, :HEAD_DIM, :] = src[0, t * LANES:(t + 1) * LANES, :].T.astype(BF16)
                dst[t, HEAD_DIM:, :] = ones

    qt = jnp.concatenate([q_ref[0, :, g * HEAD_DIM:(g + 1) * HEAD_DIM].astype(F32).T.astype(BF16)
                          for g in range(gqa)], axis=1)
    q_pos = qb * nq + lax.broadcasted_iota(I32, (1, nq), 1)

    def per_head(x):
        return jnp.concatenate([x] * gqa, axis=1)

    def normalise(acc):
        den = acc[HEAD_DIM:HEAD_DIM + 1, :]
        return acc[:HEAD_DIM, :] * (1.0 / jnp.where(den > 0, den, 1.0))

    ck = ck_ref[0, 0, 0]
    n16 = ck.shape[0]
    s = jnp.dot(ck, qt, preferred_element_type=F32)
    cidx = lax.broadcasted_iota(I32, (n16, 1), 0)
    cbias = jnp.where((cidx * CMP_STRIDE + (CMP_BLOCK - 1) <= q_pos) & (cidx < n_cmp), 0.0, NEG_BIG)
    sb = s + per_head(cbias)
    m = jnp.maximum(jnp.max(sb, axis=0, keepdims=True), M_FLOOR)
    e = jnp.exp2((sb - m) * c2)
    den = jnp.sum(e, axis=0, keepdims=True)
    p = e * jnp.where(den > 0, 1.0 / den, 0.0)
    o_cmp = jnp.dot(cvt_ref[0, 0], p.astype(BF16), preferred_element_type=F32)

    psum = p[:, 0:nq]
    for g in range(1, gqa):
        psum = psum + p[:, g * nq:(g + 1) * nq]
    nb_rows = -(-n_blk // ONES_ROWS) * ONES_ROWS
    imp = jnp.dot(_overlap(n16, nb_rows, 1), psum, preferred_element_type=F32, precision=lax.Precision.HIGHEST)
    blk = lax.broadcasted_iota(I32, (nb_rows, 1), 0)
    q_blk = jnp.right_shift(q_pos, SEL_SHIFT)
    forced = (blk == 0) | (blk == q_blk) | (blk == q_blk - 1)
    v = jnp.where(blk * SEL_BLOCK > q_pos, NEG_BIG, jnp.where(forced, -NEG_BIG, imp))
    v = jnp.where(blk < n_blk, v, 2.0 * NEG_BIG)
    rank = jnp.zeros(v.shape, F32)
    for i in range(n_blk):
        vi = v[i:i + 1, :]
        rank = rank + jnp.where((vi > v) | ((vi == v) & (blk > i)), 1.0, 0.0)
    sel_bias = jnp.where((rank < min(N_SEL, n_blk)) & (blk < n_blk), 0.0, NEG_BIG).astype(BF16)
    if nb_rows < LANES:
        sel_bias = jnp.concatenate([sel_bias, jnp.zeros((LANES - nb_rows, nq), BF16)], axis=0)

    m_ref[...] = jnp.full(m_ref.shape, M_FLOOR, F32)
    acc_ref[...] = jnp.zeros(acc_ref.shape, F32)
    tiles = sel_tile // LANES

    def sel_step(kt, _):
        k0 = pl.multiple_of(kt * sel_tile, sel_tile)
        s = jnp.dot(ks_ref[0, pl.ds(k0, sel_tile), :].astype(BF16), qt, preferred_element_type=F32)
        key = k0 + lax.broadcasted_iota(I32, (sel_tile, 1), 0)
        expand = jnp.where(jnp.right_shift(key, SEL_SHIFT) == lax.broadcasted_iota(I32, (sel_tile, LANES), 1), 1.0, 0.0)
        bias = jnp.dot(expand.astype(BF16), sel_bias, preferred_element_type=F32)
        bias = jnp.where(key <= q_pos, bias, NEG_BIG)
        sb = s + per_head(bias)
        m_old = m_ref[0:1, :]
        m_new = jnp.maximum(m_old, jnp.max(sb, axis=0, keepdims=True))
        alpha = jnp.exp2((m_old - m_new) * c2)
        e = jnp.exp2((sb - m_new) * c2).astype(BF16)
        vt = jnp.concatenate([vst_ref[kt * tiles + i] for i in range(tiles)], axis=1)
        acc_ref[...] = alpha * acc_ref[...] + jnp.dot(vt, e, preferred_element_type=F32)
        m_ref[...] = jnp.broadcast_to(m_new, m_ref.shape)
        return 0

    lax.fori_loop(0, (qb * nq + nq + sel_tile - 1) // sel_tile, sel_step, 0)
    o_sel = normalise(acc_ref[...])

    w_tile = jnp.maximum(qb + 1 - win_keys // nq, 0)
    w0 = pl.multiple_of(w_tile * nq, nq)
    s = jnp.dot(kw_ref[0, pl.ds(w0, win_keys), :].astype(BF16), qt, preferred_element_type=F32)
    key = w0 + lax.broadcasted_iota(I32, (win_keys, 1), 0)
    wbias = jnp.where((key <= q_pos) & (key > q_pos - WINDOW), 0.0, NEG_BIG)
    sb = s + per_head(wbias)
    m = jnp.maximum(jnp.max(sb, axis=0, keepdims=True), M_FLOOR)
    e = jnp.exp2((sb - m) * c2).astype(BF16)
    vt = jnp.concatenate([vwt_ref[w_tile + i] for i in range(win_keys // LANES)], axis=1)
    o_win = normalise(jnp.dot(vt, e, preferred_element_type=F32))

    gt = gt_ref[0, 0, 0]
    for g in range(gqa):
        cols = slice(g * nq, (g + 1) * nq)
        o = (gt[3 * g:3 * g + 1, :] * o_cmp[:, cols] + gt[3 * g + 1:3 * g + 2, :] * o_sel[:, cols]
             + gt[3 * g + 2:3 * g + 3, :] * o_win[:, cols])
        o_ref[0, :, g * HEAD_DIM:(g + 1) * HEAD_DIM] = o.T.astype(o_ref.dtype)


def nsa_prompt_attend(q, gates, cmp, k_sel, v_sel, k_win, v_win, bsz, t, sel_tile=512):
    hd_all = q.shape[1]
    kvh = N_KV_HEADS
    gqa = hd_all // HEAD_DIM // kvh
    n16 = cmp.shape[3]
    n_blk = -(-t // SEL_BLOCK)
    nqb = t // Q_BLOCK
    gq = gqa * Q_BLOCK
    vrows = HEAD_DIM + ONES_ROWS
    assert t % Q_BLOCK == 0 and n_blk <= LANES and n16 * CMP_STRIDE == t
    sel_tile = math.gcd(t, sel_tile)
    win_keys = min(WINDOW + Q_BLOCK, t)
    gt = gates[:, :3 * kvh * gqa].reshape(bsz, nqb, Q_BLOCK, kvh, 3 * gqa).transpose(0, 3, 1, 4, 2)
    gt = jnp.pad(gt, ((0, 0), (0, 0), (0, 0), (0, -3 * gqa % SUBLANES), (0, 0)))
    cvt = cmp[:, 1].transpose(0, 1, 3, 2)
    (ks_arr, ks_col), (vs_arr, vs_col), (kw_arr, kw_col), (vw_arr, vw_col) = k_sel, v_sel, k_win, v_win
    k_spec = lambda col: pl.BlockSpec((1, t, HEAD_DIM), lambda b, h, i, col=col: (b, 0, col + h))
    return pl.pallas_call(
        functools.partial(_nsa_prompt_body, n_cmp=n16 - 1, n_blk=n_blk, gqa=gqa, sel_tile=sel_tile, win_keys=win_keys),
        grid=(bsz, kvh, nqb),
        in_specs=[pl.BlockSpec((1, Q_BLOCK, gqa * HEAD_DIM), lambda b, h, i: (b, i, h)),
                  pl.BlockSpec((1, 1, 1, gt.shape[3], Q_BLOCK), lambda b, h, i: (b, h, i, 0, 0)),
                  pl.BlockSpec((1, 1, 1, n16, HEAD_DIM), lambda b, h, i: (b, 0, h, 0, 0)),
                  pl.BlockSpec((1, 1, HEAD_DIM, n16), lambda b, h, i: (b, h, 0, 0)),
                  k_spec(ks_col), k_spec(vs_col), k_spec(kw_col), k_spec(vw_col)],
        out_specs=pl.BlockSpec((1, Q_BLOCK, gqa * HEAD_DIM), lambda b, h, i: (b, i, h)),
        out_shape=jax.ShapeDtypeStruct((bsz, t, hd_all), BF16),
        scratch_shapes=[pltpu.VMEM((SUBLANES, gq), F32), pltpu.VMEM((vrows, gq), F32),
                        pltpu.VMEM((t // LANES, vrows, LANES), BF16), pltpu.VMEM((t // LANES, vrows, LANES), BF16)],
        compiler_params=_cparams(("parallel", "parallel", "arbitrary")),
        name="nsa_prompt",
    )(q.reshape(bsz, t, hd_all), gt, cmp, cvt, ks_arr.reshape(bsz, t, -1), vs_arr.reshape(bsz, t, -1),
      kw_arr.reshape(bsz, t, -1), vw_arr.reshape(bsz, t, -1)).reshape(bsz * t, hd_all)


def _nsa_sample_select_body(q_ref, ck_ref, cv_ref, ocmp_ref, idx_ref, *, n_cmp, n_blk, blk_lanes, past_len, tq_pad, gqa):
    scale = HEAD_DIM ** -0.5
    q = jnp.concatenate([q_ref[0, :, g * HEAD_DIM:(g + 1) * HEAD_DIM] for g in range(gqa)], axis=0)
    ck = ck_ref[0, 0, 0]
    cv = cv_ref[0, 0, 0]
    n16 = ck.shape[0]
    q_pos = past_len + lax.broadcasted_iota(I32, (tq_pad, 1), 0)
    s = (_dot_nt(q, ck) * scale).reshape(gqa, tq_pad, n16)
    cidx = lax.broadcasted_iota(I32, (tq_pad, n16), 1)
    cmask = (cidx * CMP_STRIDE + (CMP_BLOCK - 1) <= q_pos) & (cidx < n_cmp)
    p = _masked_softmax(s, cmask[None])
    ocmp_ref[0, 0] = jnp.dot(p.reshape(gqa * tq_pad, n16).astype(BF16), cv, preferred_element_type=F32)
    psum = jnp.sum(p, axis=0)
    imp = jnp.dot(psum, _overlap(n16, blk_lanes, 0), preferred_element_type=F32, precision=lax.Precision.HIGHEST)
    rank = _rank_desc(_block_scores(imp, q_pos, n_blk), n_blk)
    lane = lax.broadcasted_iota(I32, rank.shape, 1)
    lane_f = lane.astype(F32)
    out_lane = lax.broadcasted_iota(I32, (tq_pad, LANES), 1)
    idx = jnp.zeros((tq_pad, LANES), F32)
    for k in range(min(N_SEL, n_blk)):
        hit = (rank == float(k)) & (lane < n_blk)
        idx_k = jnp.sum(jnp.where(hit, lane_f, 0.0), axis=-1, keepdims=True)
        idx = jnp.where(out_lane == k, idx_k, idx)
    idx_ref[0, 0] = idx.astype(I32)


def _nsa_sample_attend_body(idx_ref, pt_ref, q_ref, gt_ref, ocmp_ref, kwc_ref, vwc_ref,
                            kwn_ref, vwn_ref, cache_ref, newblk_ref, o_ref, kbuf, vbuf, kwbuf, vwbuf, sems,
                            *, n_sel, n_cache_blk, n_pages, past_len, win_buf, tq, tq_pad, gqa):
    b = pl.program_id(0)
    h = pl.program_id(1)
    scale = HEAD_DIM ** -0.5
    blk_per_page = PAGE_SIZE // SEL_BLOCK

    bufs = ((2, kbuf, 0), (3, vbuf, 1))
    step = b * N_KV_HEADS + h
    n_steps = pl.num_programs(0) * N_KV_HEADS
    cur = _mod(step, 2)

    def from_new(bb, hh, sl, t, k, slot, buf, sem):
        return pltpu.make_async_copy(newblk_ref.at[bb, :, slot * N_KV_HEADS + hh], buf.at[sl, t, k], sems.at[sl, sem])

    def start_gather(st, sl):
        bb = _div(st, N_KV_HEADS)
        hh = _mod(st, N_KV_HEADS)
        for t in range(tq):
            for k in range(n_sel):
                blk = idx_ref[(st * tq + t) * n_sel + k]
                in_cache = blk < n_cache_blk
                blk_c = jnp.minimum(blk, n_cache_blk - 1)
                page = pt_ref[bb * n_pages + _div(blk_c, blk_per_page)]
                off = pl.multiple_of(_mod(blk_c, blk_per_page) * SEL_BLOCK, SEL_BLOCK)
                for slot, buf, sem in bufs:
                    @pl.when(in_cache)
                    def _():
                        pltpu.make_async_copy(cache_ref.at[page, pl.ds(off, SEL_BLOCK), slot * N_KV_HEADS + hh],
                                              buf.at[sl, t, k], sems.at[sl, sem]).start()

                    @pl.when(jnp.logical_not(in_cache))
                    def _():
                        from_new(bb, hh, sl, t, k, slot, buf, sem).start()

    @pl.when(step == 0)
    def _():
        start_gather(step, cur)

    @pl.when(step + 1 < n_steps)
    def _():
        start_gather(step + 1, 1 - cur)

    q = jnp.concatenate([q_ref[0, :, g * HEAD_DIM:(g + 1) * HEAD_DIM] for g in range(gqa)], axis=0)
    rows = gqa * tq_pad
    row_t = lax.rem(lax.broadcasted_iota(I32, (rows, 1), 0), tq_pad)
    q_pos = past_len + row_t

    kwbuf[...] = jnp.zeros(kwbuf.shape, F32)
    vwbuf[...] = jnp.zeros(vwbuf.shape, F32)
    kwbuf[0:win_buf, :] = kwc_ref[0]
    vwbuf[0:win_buf, :] = vwc_ref[0]
    kwbuf[win_buf:win_buf + tq_pad, :] = kwn_ref[0]
    vwbuf[win_buf:win_buf + tq_pad, :] = vwn_ref[0]
    nw = kwbuf.shape[0]
    wlane = lax.broadcasted_iota(I32, (rows, nw), 1)
    wpos = past_len - win_buf + wlane
    wmask = (wpos <= q_pos) & (wpos > q_pos - WINDOW) & (wpos >= 0) & (wlane < win_buf + tq)
    s = _dot_nt(q, kwbuf[...].astype(BF16)) * scale
    p = _masked_softmax(s, wmask)
    o_win = jnp.dot(p.astype(BF16), vwbuf[...].astype(BF16), preferred_element_type=F32)

    for t in range(tq):
        for k in range(n_sel):
            for slot, buf, sem in bufs:
                from_new(b, h, cur, t, k, slot, buf, sem).wait()

    o_sel = jnp.zeros((rows, HEAD_DIM), F32)
    nk = n_sel * SEL_BLOCK
    klane = lax.broadcasted_iota(I32, (rows, nk), 1)
    for t in range(tq):
        kpos = jnp.zeros((rows, nk), I32)
        for k in range(n_sel):
            blk = idx_ref[((b * N_KV_HEADS + h) * tq + t) * n_sel + k]
            kpos = jnp.where(jnp.right_shift(klane, SEL_SHIFT) == k,
                             blk * SEL_BLOCK + jnp.bitwise_and(klane, SEL_BLOCK - 1), kpos)
        mask = (kpos <= q_pos) & (row_t == t)
        kt = kbuf[cur, t].reshape(nk, HEAD_DIM).astype(BF16)
        vt = vbuf[cur, t].reshape(nk, HEAD_DIM).astype(BF16)
        p = _masked_softmax(_dot_nt(q, kt) * scale, mask)
        o_sel = o_sel + jnp.dot(p.astype(BF16), vt, preferred_element_type=F32)

    o_cmp = ocmp_ref[0, 0]
    gt = gt_ref[0, 0]
    for g in range(gqa):
        r = slice(g * tq_pad, (g + 1) * tq_pad)
        o = (gt[:, 3 * g:3 * g + 1] * o_cmp[r] + gt[:, 3 * g + 1:3 * g + 2] * o_sel[r]
             + gt[:, 3 * g + 2:3 * g + 3] * o_win[r])
        o_ref[0, :, g * HEAD_DIM:(g + 1) * HEAD_DIM] = o.astype(o_ref.dtype)


def nsa_sample_attend(q, gates, cmp, new_rows, win_new, cache_kv, cache_win, page_table, tq):
    bsz, tq_pad, hd_all = q.shape
    kvh = N_KV_HEADS
    gqa = hd_all // HEAD_DIM // kvh
    n_pages = page_table.shape[1]
    past_len = n_pages * PAGE_SIZE
    n16 = cmp.shape[3]
    n_cmp = n16 - 1
    n_cache_blk = past_len // SEL_BLOCK
    n_blk = -(-(past_len + tq) // SEL_BLOCK)
    assert n_blk == n_cache_blk + 1 and past_len % SEL_BLOCK == 0 and tq <= SEL_BLOCK
    n_sel = min(N_SEL, n_blk)
    blk_lanes = -(-n_blk // LANES) * LANES
    win_buf = cache_win.shape[1]
    cmp_spec = lambda s: pl.BlockSpec((1, 1, 1, n16, HEAD_DIM), lambda b, h, s=s: (b, s, h, 0, 0))
    ocmp, idx = pl.pallas_call(
        functools.partial(_nsa_sample_select_body, n_cmp=n_cmp, n_blk=n_blk, blk_lanes=blk_lanes,
                          past_len=past_len, tq_pad=tq_pad, gqa=gqa),
        grid=(bsz, kvh),
        in_specs=[pl.BlockSpec((1, tq_pad, gqa * HEAD_DIM), lambda b, h: (b, 0, h)), cmp_spec(0), cmp_spec(1)],
        out_specs=[pl.BlockSpec((1, 1, gqa * tq_pad, HEAD_DIM), lambda b, h: (b, h, 0, 0)),
                   pl.BlockSpec((1, 1, tq_pad, LANES), lambda b, h: (b, h, 0, 0))],
        out_shape=[jax.ShapeDtypeStruct((bsz, kvh, gqa * tq_pad, HEAD_DIM), F32),
                   jax.ShapeDtypeStruct((bsz, kvh, tq_pad, LANES), I32)],
        compiler_params=_cparams(("parallel", "parallel")),
        name="nsa_sample_select",
    )(q, cmp, cmp)
    idx_flat = idx[:, :, :tq, :n_sel].reshape(-1)

    n_slots = cache_kv.shape[2]
    cache4 = cache_kv.reshape(cache_kv.shape[0], PAGE_SIZE, n_slots * kvh, HEAD_DIM)
    newblk = jnp.zeros((bsz, SEL_BLOCK, n_slots * kvh, HEAD_DIM), F32).at[:, :tq].set(new_rows)
    cwin = cache_win.astype(F32).reshape(bsz, win_buf, 2 * kvh * HEAD_DIM)
    nw = -(-(win_buf + tq_pad) // LANES) * LANES
    kv_spec = lambda slot: pl.BlockSpec((1, tq_pad, HEAD_DIM), lambda b, h, *_, slot=slot: (b, 0, slot * kvh + h))
    cw_spec = lambda slot: pl.BlockSpec((1, win_buf, HEAD_DIM), lambda b, h, *_, slot=slot: (b, 0, slot * kvh + h))
    grid_spec = pltpu.PrefetchScalarGridSpec(
        num_scalar_prefetch=2,
        grid=(bsz, kvh),
        in_specs=[pl.BlockSpec((1, tq_pad, gqa * HEAD_DIM), lambda b, h, *_: (b, 0, h)),
                  pl.BlockSpec((1, 1, tq_pad, LANES), lambda b, h, *_: (b, h, 0, 0)),
                  pl.BlockSpec((1, 1, gqa * tq_pad, HEAD_DIM), lambda b, h, *_: (b, h, 0, 0)),
                  cw_spec(0), cw_spec(1), kv_spec(0), kv_spec(1),
                  pl.BlockSpec(memory_space=pl.ANY), pl.BlockSpec(memory_space=pl.ANY)],
        out_specs=pl.BlockSpec((1, tq_pad, gqa * HEAD_DIM), lambda b, h, *_: (b, 0, h)),
        scratch_shapes=[pltpu.VMEM((2, tq, n_sel, SEL_BLOCK, HEAD_DIM), F32),
                        pltpu.VMEM((2, tq, n_sel, SEL_BLOCK, HEAD_DIM), F32),
                        pltpu.VMEM((nw, HEAD_DIM), F32), pltpu.VMEM((nw, HEAD_DIM), F32),
                        pltpu.SemaphoreType.DMA((2, 2))],
    )
    return pl.pallas_call(
        functools.partial(_nsa_sample_attend_body, n_sel=n_sel, n_cache_blk=n_cache_blk, n_pages=n_pages,
                          past_len=past_len, win_buf=win_buf, tq=tq, tq_pad=tq_pad, gqa=gqa),
        grid_spec=grid_spec,
        out_shape=jax.ShapeDtypeStruct((bsz, tq_pad, hd_all), BF16),
        compiler_params=_cparams(("arbitrary", "arbitrary")),
        name="nsa_sample_attend",
    )(idx_flat, page_table.reshape(-1).astype(I32), q, gates, ocmp, cwin, cwin, win_new, win_new, cache4, newblk)


def _pad_rows(x, rows):
    return jnp.pad(x, ((0, 0), (0, rows - x.shape[1]), (0, 0)))


@jax.jit
def _step(x_prompt, x_sample, state_ssm_re, state_ssm_im, state_ffn_conv, cache_kv, cache_win, page_table,
          attn_norm, ffn_norm, final_norm, ssm_lam_re, ssm_lam_im, ssm_log_step, ssm_b_re, ssm_b_im,
          ssm_c_re, ssm_c_im, ssm_d, ssm_w_glu, ffn_w_in, ffn_conv_w, ffn_conv_b, ffn_w_down,
          kv_norm, w_kv, cmp_w1, cmp_b1, cmp_w2, cmp_b2, cmp_pe, w_qg, w_o):
    bp, tp, d = x_prompt.shape
    bs, ts, _ = x_sample.shape
    depth = attn_norm.shape[0]
    n_a = ssm_lam_re.shape[0]
    kvh = N_KV_HEADS
    hd_all = w_o.shape[1]
    n_heads = hd_all // HEAD_DIM
    gqa = n_heads // kvh
    ts_pad = -(-ts // SUBLANES) * SUBLANES
    n_pages = page_table.shape[1]

    xp = x_prompt.reshape(bp * tp, d).astype(F32)
    xs = x_sample.reshape(bs * ts, d).astype(F32)

    w1cat = jnp.concatenate([cmp_w1[:, :CMP_STRIDE * HEAD_DIM], cmp_w1[:, CMP_STRIDE * HEAD_DIM:]], axis=-1)
    w1cat = w1cat.reshape(2, CMP_STRIDE // 2, 2 * HEAD_DIM, w1cat.shape[-1]).astype(BF16)

    ssm_re_p, ssm_im_p, ssm_re_s, ssm_im_s, conv_p, conv_s = [], [], [], [], [], []
    outs = {}
    for layer in range(depth):
        if layer < n_a:
            a = layer
            wts = _s5_weights(ssm_lam_re[a], ssm_lam_im[a], ssm_log_step[a], ssm_b_re[a], ssm_b_im[a],
                              ssm_c_re[a], ssm_c_im[a])
            w_glu = cast_weight(ssm_w_glu, a)
            up = rmsnorm(xp, [attn_norm[layer]], F32)[0].reshape(bp, tp, d)
            zeros = jnp.zeros((bp, d // SSM_GROUP, SSM_STATE), F32)
            zp, hre, him = s5_scan(up, zeros, zeros, wts, ssm_d[a], tp, math.gcd(tp, 512))
            ssm_re_p.append(hre)
            ssm_im_p.append(him)
            xp = glu_matmul(zp.reshape(bp * tp, d), w_glu, xp)
            us = _pad_rows(rmsnorm(xs, [attn_norm[layer]], F32)[0].reshape(bs, ts, d), ts_pad)
            zs, hre, him = s5_scan(us, state_ssm_re[a], state_ssm_im[a], wts, ssm_d[a], ts, ts_pad)
            ssm_re_s.append(hre)
            ssm_im_s.append(him)
            xs = glu_matmul(zs[:, :ts].reshape(bs * ts, d), w_glu, xs)
        else:
            bl = layer - n_a
            if layer == n_a:
                w_kv_b = cast_weight(w_kv[None], 0)
                n_kv4 = 4 * kvh * HEAD_DIM
                sp, hp_first = rmsnorm(xp, [kv_norm, attn_norm[layer]], BF16)
                ss, hs_first = rmsnorm(xs, [kv_norm, attn_norm[layer]], BF16)
                kvr_p, kvw_p = matmul_split(sp, w_kv_b, n_kv4)
                kvr_s, kvw_s = matmul_split(ss, w_kv_b, n_kv4)
                outs["kv_rows_p"] = kvr_p.reshape(bp, tp, 4, kvh, HEAD_DIM)
                outs["kv_rows_s"] = kvr_s.reshape(bs, ts, 4, kvh, HEAD_DIM)
                outs["win_p"] = kvw_p.reshape(bp, tp, 2, kvh, HEAD_DIM)[:, max(tp - WINDOW, 0):]
                win_s = kvw_s.reshape(bs, ts, 2, kvh, HEAD_DIM)
                win_buf = cache_win.shape[1]
                outs["win_s"] = jnp.concatenate([cache_win.astype(F32), win_s], axis=1)[:, -win_buf:]
                kvw_s3 = _pad_rows(kvw_s.reshape(bs, ts, kvw_s.shape[1]), ts_pad)
                assert tp % PAGE_SIZE == 0 and (n_pages * PAGE_SIZE + ts) // CMP_STRIDE == n_pages * PAGE_SIZE // CMP_STRIDE
                pt_p = jnp.arange(bp * (tp // PAGE_SIZE), dtype=I32).reshape(bp, tp // PAGE_SIZE)
                ab_p = cmp_project(kvr_p.reshape(bp * tp // PAGE_SIZE, PAGE_SIZE, 4 * kvh, HEAD_DIM), pt_p, w1cat)
                cmp_p = cmp_finish(ab_p, cmp_pe, cmp_w1, cmp_b1, cmp_w2, cmp_b2)
                cache4 = cache_kv.astype(F32).reshape(cache_kv.shape[0], PAGE_SIZE, -1, HEAD_DIM)
                ab_s = cmp_project(cache4, page_table, w1cat)
                cmp_s = cmp_finish(ab_s, cmp_pe, cmp_w1, cmp_b1, cmp_w2, cmp_b2)
            w_qg_b = w_qg[bl].astype(BF16)
            w_q = w_qg_b[:, :hd_all]
            n_gate = w_qg.shape[2] - hd_all
            w_g = jnp.pad(w_qg_b[:, hd_all:], ((0, 0), (0, LANES - n_gate)))
            w_o_b = cast_weight(w_o, bl)

            def gate_layout(gt, bsz, t):
                gt = gt[:, :n_gate].reshape(bsz, t, kvh, 3 * gqa).transpose(0, 2, 1, 3)
                return jnp.pad(gt, ((0, 0), (0, 0), (0, 0), (0, LANES - 3 * gqa)))

            hp = hp_first if layer == n_a else rmsnorm(xp, [attn_norm[layer]], BF16)[0]
            q_p = matmul(hp, w_q, out_dtype=BF16, tm=1024, tn=512)
            g_p = matmul(hp, w_g, act="sigmoid", tm=1024)
            o_p = nsa_prompt_attend(q_p, g_p, cmp_p, (kvr_p, 2 * kvh), (kvr_p, 3 * kvh), (kvw_p, 0), (kvw_p, kvh), bp, tp)
            xp = matmul(o_p, w_o_b, res=xp, tm=1024, tn=512)

            hs = hs_first if layer == n_a else rmsnorm(xs, [attn_norm[layer]], BF16)[0]
            q_s = _pad_rows(matmul(hs, w_q, out_dtype=BF16).reshape(bs, ts, hd_all), ts_pad)
            g_s = gate_layout(matmul(hs, w_g, act="sigmoid"), bs, ts)
            g_s = jnp.pad(g_s, ((0, 0), (0, 0), (0, ts_pad - ts), (0, 0)))
            o_s = nsa_sample_attend(q_s, g_s, cmp_s, kvr_s.reshape(bs, ts, 4 * kvh, HEAD_DIM), kvw_s3,
                                    cache_kv.astype(F32), cache_win, page_table, ts)
            xs = matmul(o_s[:, :ts].reshape(bs * ts, hd_all), w_o_b, res=xs)

        f = ffn_w_down.shape[1]
        act, st, act_s, st_s, w_down = ffn_up(
            rmsnorm(xp, [ffn_norm[layer]], BF16)[0], rmsnorm(xs, [ffn_norm[layer]], BF16)[0],
            ffn_w_in.astype(F32), ffn_w_down.astype(F32), layer, ffn_conv_w[layer], ffn_conv_b[layer],
            jnp.zeros((bp, CONV_W - 1, f), F32), state_ffn_conv[layer], tp, ts)
        conv_p.append(st)
        conv_s.append(st_s)
        xp = matmul(act, w_down, res=xp, tm=512, tn=512)
        xs = matmul(act_s, w_down, res=xs, tm=512, tn=256)

    y_p = rmsnorm(xp, [final_norm], F32)[0].reshape(bp, tp, d)
    y_s = rmsnorm(xs, [final_norm], F32)[0].reshape(bs, ts, d)
    return (y_p, y_s, jnp.stack(ssm_re_p), jnp.stack(ssm_im_p), jnp.stack(ssm_re_s), jnp.stack(ssm_im_s),
            jnp.stack(conv_p), jnp.stack(conv_s), outs["kv_rows_p"], outs["kv_rows_s"], outs["win_p"], outs["win_s"])


def kernel(x_prompt, x_sample, state_ssm_re, state_ssm_im, state_ffn_conv, cache_kv, cache_win, page_table, attn_norm, ffn_norm, final_norm, ssm_lam_re, ssm_lam_im, ssm_log_step, ssm_b_re, ssm_b_im, ssm_c_re, ssm_c_im, ssm_d, ssm_w_glu, ffn_w_in, ffn_conv_w, ffn_conv_b, ffn_w_down, kv_norm, w_kv, cmp_w1, cmp_b1, cmp_w2, cmp_b2, cmp_pe, w_qg, w_o):
    return _step(x_prompt, x_sample, state_ssm_re, state_ssm_im, state_ffn_conv, cache_kv, cache_win, page_table,
                 attn_norm, ffn_norm, final_norm, ssm_lam_re, ssm_lam_im, ssm_log_step, ssm_b_re, ssm_b_im,
                 ssm_c_re, ssm_c_im, ssm_d, ssm_w_glu, ffn_w_in, ffn_conv_w, ffn_conv_b, ffn_w_down,
                 kv_norm, w_kv, cmp_w1, cmp_b1, cmp_w2, cmp_b2, cmp_pe, w_qg, w_o)
```

```python
import functools
import math

import jax
import jax.numpy as jnp
from jax import lax
from jax.experimental import pallas as pl
from jax.experimental.pallas import tpu as pltpu

F32 = jnp.float32
BF16 = jnp.bfloat16
I32 = jnp.int32

RMS_EPS = 1e-6
LANES = 128
SUBLANES = 8
VMEM_LIMIT = 56 * 1024 * 1024
ROW_CHUNK = 128

SSM_GROUP = 16
SSM_STATE = 64
CONV_W = 3
HEAD_DIM = 128
N_KV_HEADS = 4
CMP_STRIDE = 16
CMP_BLOCK = 32
SEL_BLOCK = 64
SEL_SHIFT = SEL_BLOCK.bit_length() - 1
N_SEL = 16
WINDOW = 512
Q_BLOCK = 128
PAGE_SIZE = 128
NEG_BIG = -1e30
M_FLOOR = 0.1 * NEG_BIG
LOG2_E = math.log2(math.e)
ONES_ROWS = 16

S5_SET_GROUPS = 4
S5_SETS = 8
S5_BLK_GROUPS = S5_SET_GROUPS * S5_SETS
S5_BLK_CH = S5_BLK_GROUPS * SSM_GROUP
S5_SET_LANES = S5_SET_GROUPS * SSM_STATE


def _cparams(sem, vmem=VMEM_LIMIT):
    return pltpu.CompilerParams(dimension_semantics=sem, vmem_limit_bytes=vmem)


def _div(s, n):
    return jnp.right_shift(s, n.bit_length() - 1) if n & (n - 1) == 0 else s // n


def _mod(s, n):
    return jnp.bitwise_and(s, n - 1) if n & (n - 1) == 0 else s % n


def _cast_body(w_ref, o_ref):
    o_ref[...] = w_ref[...].astype(o_ref.dtype)


def cast_weight(w_all, layer, col0=0, ncols=None, tr=512):
    _, k, n = w_all.shape
    ncols = n - col0 if ncols is None else ncols
    tr = math.gcd(k, tr)
    tc = math.gcd(math.gcd(ncols, col0) if col0 else ncols, 4096)
    c0 = col0 // tc
    return pl.pallas_call(
        _cast_body,
        grid=(k // tr, ncols // tc),
        in_specs=[pl.BlockSpec((None, tr, tc), lambda i, j: (layer, i, c0 + j))],
        out_specs=pl.BlockSpec((tr, tc), lambda i, j: (i, j)),
        out_shape=jax.ShapeDtypeStruct((k, ncols), BF16),
        compiler_params=_cparams(("parallel", "parallel")),
        name="cast_weight",
    )(w_all.astype(F32))


def _rmsnorm_body(x_ref, g_ref, *o_refs):
    x = x_ref[...].astype(F32)
    y = x * lax.rsqrt(jnp.mean(x * x, axis=-1, keepdims=True) + RMS_EPS)
    for k, o_ref in enumerate(o_refs):
        o_ref[...] = (y * g_ref[k:k + 1, :]).astype(o_ref.dtype)


def rmsnorm(x, gains, out_dtype):
    m, d = x.shape
    tm = math.gcd(m, 512)
    n = len(gains)
    g = jnp.stack([gi.reshape(d).astype(F32) for gi in gains])
    outs = pl.pallas_call(
        _rmsnorm_body,
        grid=(m // tm,),
        in_specs=[pl.BlockSpec((tm, d), lambda i: (i, 0)), pl.BlockSpec((n, d), lambda i: (0, 0))],
        out_specs=[pl.BlockSpec((tm, d), lambda i: (i, 0))] * n,
        out_shape=[jax.ShapeDtypeStruct((m, d), out_dtype)] * n,
        compiler_params=_cparams(("parallel",)),
        name="rmsnorm",
    )(x, g)
    return outs


def _mm_body(*refs, has_res, act):
    if has_res:
        x_ref, w_ref, res_ref, o_ref = refs
    else:
        x_ref, w_ref, o_ref = refs
    tm = x_ref.shape[0]
    chunk = math.gcd(tm, ROW_CHUNK)
    for r in range(0, tm, chunk):
        y = jnp.dot(x_ref[r:r + chunk, :], w_ref[...], preferred_element_type=F32)
        if act == "sigmoid":
            y = jax.nn.sigmoid(y)
        if has_res:
            y = res_ref[r:r + chunk, :] + y
        o_ref[r:r + chunk, :] = y.astype(o_ref.dtype)


def matmul(x, w, res=None, out_dtype=F32, act=None, tm=512, tn=256):
    m, k = x.shape
    n = w.shape[1]
    tm = math.gcd(m, tm)
    tn = math.gcd(n, tn)
    in_specs = [pl.BlockSpec((tm, k), lambda i, j: (i, 0)), pl.BlockSpec((k, tn), lambda i, j: (0, j))]
    args = [x, w]
    if res is not None:
        in_specs.append(pl.BlockSpec((tm, tn), lambda i, j: (i, j)))
        args.append(res)
    return pl.pallas_call(
        functools.partial(_mm_body, has_res=res is not None, act=act),
        grid=(m // tm, n // tn),
        in_specs=in_specs,
        out_specs=pl.BlockSpec((tm, tn), lambda i, j: (i, j)),
        out_shape=jax.ShapeDtypeStruct((m, n), out_dtype),
        compiler_params=_cparams(("parallel", "arbitrary")),
        name="matmul",
    )(*args)


def _mm_split_body(x_ref, w_ref, o1_ref, o2_ref, *, n1_tiles):
    j = pl.program_id(1)
    y = jnp.dot(x_ref[...], w_ref[...], preferred_element_type=F32)

    @pl.when(j < n1_tiles)
    def _():
        o1_ref[...] = y

    @pl.when(j >= n1_tiles)
    def _():
        o2_ref[...] = y


def matmul_split(x, w, n1, tm=1024, tn=512):
    m, k = x.shape
    n = w.shape[1]
    tm = math.gcd(m, tm)
    tn = math.gcd(math.gcd(n1, n - n1), tn)
    t1 = n1 // tn
    return pl.pallas_call(
        functools.partial(_mm_split_body, n1_tiles=t1),
        grid=(m // tm, n // tn),
        in_specs=[pl.BlockSpec((tm, k), lambda i, j: (i, 0)), pl.BlockSpec((k, tn), lambda i, j: (0, j))],
        out_specs=[pl.BlockSpec((tm, tn), lambda i, j: (i, jnp.minimum(j, t1 - 1))),
                   pl.BlockSpec((tm, tn), lambda i, j: (i, jnp.maximum(j - t1, 0)))],
        out_shape=[jax.ShapeDtypeStruct((m, n1), F32), jax.ShapeDtypeStruct((m, n - n1), F32)],
        compiler_params=_cparams(("parallel", "arbitrary")),
        name="matmul_split",
    )(x, w)


def _glu_body(x_ref, wa_ref, wb_ref, res_ref, o_ref):
    tm = x_ref.shape[0]
    chunk = math.gcd(tm, ROW_CHUNK)
    for r in range(0, tm, chunk):
        x = x_ref[r:r + chunk, :]
        a = jnp.dot(x, wa_ref[...], preferred_element_type=F32)
        b = jnp.dot(x, wb_ref[...], preferred_element_type=F32)
        o_ref[r:r + chunk, :] = res_ref[r:r + chunk, :] + a * jax.nn.sigmoid(b)


def glu_matmul(x, w, res, tm=1024, tn=512):
    m, k = x.shape
    n = w.shape[1] // 2
    tm = math.gcd(m, tm)
    tn = math.gcd(n, tn)
    nt = n // tn
    return pl.pallas_call(
        _glu_body,
        grid=(m // tm, nt),
        in_specs=[pl.BlockSpec((tm, k), lambda i, j: (i, 0)),
                  pl.BlockSpec((k, tn), lambda i, j: (0, j)),
                  pl.BlockSpec((k, tn), lambda i, j: (0, j + nt)),
                  pl.BlockSpec((tm, tn), lambda i, j: (i, j))],
        out_specs=pl.BlockSpec((tm, tn), lambda i, j: (i, j)),
        out_shape=jax.ShapeDtypeStruct((m, n), F32),
        compiler_params=_cparams(("parallel", "arbitrary")),
        name="glu_matmul",
    )(x, w, w, res)


def _conv_gate(g, g1, g2, cw_ref, cb_ref, val):
    gc = cb_ref[...] + cw_ref[0:1, :] * g2 + cw_ref[1:2, :] * g1 + cw_ref[2:3, :] * g
    return (gc * jax.nn.sigmoid(gc) * val).astype(BF16)


def _ffn_up_body(x_ref, xs_ref, wv_ref, wg_ref, cw_ref, cb_ref, p1_ref, p2_ref, cwp_ref, cbp_ref, init_ref, wd_ref,
                 act_ref, st_ref, acts_ref, gs_ref, wdb_ref, wv_s, wg_s, val_a, g_a, val_b, g_b,
                 *, n_m, tiles_per_seq, short_len, chunk):
    s = pl.program_id(0)
    i = _mod(s, n_m)

    @pl.when(s == 0)
    def _():
        val_b[...] = jnp.zeros(val_b.shape, F32)
        g_b[...] = jnp.zeros(g_b.shape, F32)

    @pl.when(i == 0)
    def _():
        wv_s[...] = wv_ref[...].astype(BF16)
        wg_s[...] = wg_ref[...].astype(BF16)
        xs = xs_ref[...]
        val = jnp.dot(xs, wv_s[...], preferred_element_type=F32)
        g = jnp.dot(xs, wg_s[...], preferred_element_type=F32)
        tmod = lax.rem(lax.broadcasted_iota(I32, g.shape, 0), short_len)
        g1 = jnp.where(tmod >= 1, pltpu.roll(g, 1, axis=0), p1_ref[...])
        g2 = jnp.where(tmod >= 2, pltpu.roll(g, 2, axis=0), p2_ref[...])
        acts_ref[...] = _conv_gate(g, g1, g2, cw_ref, cb_ref, val)
        gs_ref[...] = g

    def step(val_w, g_w, val_r, g_r):
        tm = x_ref.shape[0]
        wdb_ref[...] = wd_ref[...].astype(BF16)
        for r in range(0, tm, chunk):
            x = x_ref[r:r + chunk, :]
            val_w[r:r + chunk, :] = jnp.dot(x, wv_s[...], preferred_element_type=F32)
            g_w[SUBLANES + r:SUBLANES + r + chunk, :] = jnp.dot(x, wg_s[...], preferred_element_type=F32)
            gext = g_r[r:r + chunk + SUBLANES, :]
            g1 = pltpu.roll(gext, 1, axis=0)[SUBLANES:]
            g2 = pltpu.roll(gext, 2, axis=0)[SUBLANES:]
            act_ref[r:r + chunk, :] = _conv_gate(gext[SUBLANES:], g1, g2, cwp_ref, cbp_ref, val_r[r:r + chunk, :])
        tail = g_r[tm:, :]
        st_ref[0] = tail
        g_w[:SUBLANES, :] = jnp.where(_mod(i, tiles_per_seq) == 0, init_ref[0], tail)

    parity = _mod(s, 2)

    @pl.when(parity == 0)
    def _():
        step(val_a, g_a, val_b, g_b)

    @pl.when(parity == 1)
    def _():
        step(val_b, g_b, val_a, g_a)


def ffn_up(h, hs, w_in_all, w_down_all, layer, conv_w, conv_b, buf, buf_s, seq_len, seq_short, tm=1024, tn=256):
    m, k = h.shape
    ms = hs.shape[0]
    f = w_in_all.shape[2] // 2
    bsz = m // seq_len
    bs = ms // seq_short
    assert seq_len % SUBLANES == 0 and seq_len >= 2 * SUBLANES and seq_short >= 2
    tn = math.gcd(f, tn)
    n_n = f // tn
    tm = math.gcd(seq_len, tm)
    tps = seq_len // tm
    n_m = m // tm
    chunk = math.gcd(tm, ROW_CHUNK)
    n_steps = n_n * n_m
    d_out = w_down_all.shape[2]
    rd = f // n_steps
    assert f % n_steps == 0 and rd % ONES_ROWS == 0
    cw = jnp.zeros((SUBLANES, f), F32).at[:CONV_W].set(conv_w.astype(F32))
    cb = conv_b.reshape(1, f).astype(F32)
    init = jnp.zeros((bsz, SUBLANES, f), F32).at[:, SUBLANES - 2:].set(buf.astype(F32))
    buf_s = buf_s.astype(F32)
    pos = jnp.arange(ms) % seq_short
    b0 = jnp.repeat(buf_s[:, 0], seq_short, axis=0)
    b1 = jnp.repeat(buf_s[:, 1], seq_short, axis=0)
    p1 = jnp.where((pos == 0)[:, None], b1, 0.0)
    p2 = jnp.where((pos == 0)[:, None], b0, jnp.where((pos == 1)[:, None], b1, 0.0))

    def col(s):
        return jnp.minimum(_div(s, n_m), n_n - 1)

    def prev(s):
        sp = jnp.maximum(s - 1, 0)
        return _mod(sp, n_m), _div(sp, n_m)

    act, st, act_s, g_s, w_down_b = pl.pallas_call(
        functools.partial(_ffn_up_body, n_m=n_m, tiles_per_seq=tps, short_len=seq_short, chunk=chunk),
        grid=(n_steps + 1,),
        in_specs=[pl.BlockSpec((tm, k), lambda s: (_mod(s, n_m), 0)),
                  pl.BlockSpec((ms, k), lambda s: (0, 0)),
                  pl.BlockSpec((None, k, tn), lambda s: (layer, 0, col(s))),
                  pl.BlockSpec((None, k, tn), lambda s: (layer, 0, col(s) + n_n)),
                  pl.BlockSpec((SUBLANES, tn), lambda s: (0, col(s))),
                  pl.BlockSpec((1, tn), lambda s: (0, col(s))),
                  pl.BlockSpec((ms, tn), lambda s: (0, col(s))),
                  pl.BlockSpec((ms, tn), lambda s: (0, col(s))),
                  pl.BlockSpec((SUBLANES, tn), lambda s: (0, prev(s)[1])),
                  pl.BlockSpec((1, tn), lambda s: (0, prev(s)[1])),
                  pl.BlockSpec((1, SUBLANES, tn), lambda s: (_div(_mod(s, n_m), tps), 0, col(s))),
                  pl.BlockSpec((None, rd, d_out), lambda s: (layer, jnp.minimum(s, n_steps - 1), 0))],
        out_specs=[pl.BlockSpec((tm, tn), lambda s: prev(s)),
                   pl.BlockSpec((1, SUBLANES, tn), lambda s: (prev(s)[0], 0, prev(s)[1])),
                   pl.BlockSpec((ms, tn), lambda s: (0, col(s))),
                   pl.BlockSpec((ms, tn), lambda s: (0, col(s))),
                   pl.BlockSpec((rd, d_out), lambda s: (jnp.minimum(s, n_steps - 1), 0))],
        out_shape=[jax.ShapeDtypeStruct((m, f), BF16), jax.ShapeDtypeStruct((n_m, SUBLANES, f), F32),
                   jax.ShapeDtypeStruct((ms, f), BF16), jax.ShapeDtypeStruct((ms, f), F32),
                   jax.ShapeDtypeStruct((f, d_out), BF16)],
        scratch_shapes=[pltpu.VMEM((k, tn), BF16), pltpu.VMEM((k, tn), BF16)]
                       + [pltpu.VMEM((tm, tn), F32), pltpu.VMEM((tm + SUBLANES, tn), F32)] * 2,
        compiler_params=_cparams(("arbitrary",)),
        name="ffn_up",
    )(h, hs, w_in_all, w_in_all, cw, cb, p1, p2, cw, cb, init, w_down_all)
    return (act, st[tps - 1::tps, SUBLANES - 2:], act_s, g_s.reshape(bs, seq_short, f)[:, seq_short - 2:], w_down_b)


def _s5_body(*refs, tc_len, last_step, nb):
    (u_ref, wb_ref, wc_ref, lre_ref, lim_ref, d_ref, h0re_ref, h0im_ref, z_ref, hre_ref, him_ref) = refs[:11]
    halves = S5_SET_LANES // LANES
    n_buf = 2 * halves
    bufs = refs[11:11 + nb * n_buf]
    hst_re, hst_im = refs[11 + nb * n_buf:]
    tci = pl.program_id(2)

    def sre(b, c):
        return bufs[b * n_buf + c]

    def sim(b, c):
        return bufs[b * n_buf + halves + c]

    def set_rows(j):
        return pl.ds(j, tc_len, stride=S5_SETS)

    for b in range(nb):
        for ti in range(S5_BLK_CH // LANES):
            ut = u_ref[b, :, ti * LANES:(ti + 1) * LANES]
            hi = ut.astype(BF16)
            lo = (ut - hi.astype(F32)).astype(BF16)
            lhs = jnp.concatenate([hi, lo], axis=1)
            for jj in range(2):
                j = ti * 2 + jj
                bre = jnp.dot(lhs, wb_ref[0, j, 0], preferred_element_type=F32)
                bim = jnp.dot(lhs, wb_ref[0, j, 1], preferred_element_type=F32)
                for c in range(halves):
                    sre(b, c)[set_rows(j), :] = bre[:, c * LANES:(c + 1) * LANES]
                    sim(b, c)[set_rows(j), :] = bim[:, c * LANES:(c + 1) * LANES]

    @pl.when(tci == 0)
    def _():
        for b in range(nb):
            hst_re[b] = h0re_ref[b, 0]
            hst_im[b] = h0im_ref[b, 0]

    lam_r = lre_ref[0]
    lam_i = lim_ref[0]
    lr = [lam_r[:, c * LANES:(c + 1) * LANES] for c in range(halves)]
    li = [lam_i[:, c * LANES:(c + 1) * LANES] for c in range(halves)]

    def step(t, carry):
        r0 = pl.multiple_of(t * S5_SETS, S5_SETS)
        out = []
        for b in range(nb):
            for c in range(halves):
                hr, hi_ = carry[2 * (b * halves + c)], carry[2 * (b * halves + c) + 1]
                nr = lr[c] * hr - li[c] * hi_ + sre(b, c)[pl.ds(r0, S5_SETS), :]
                ni = lr[c] * hi_ + li[c] * hr + sim(b, c)[pl.ds(r0, S5_SETS), :]
                sre(b, c)[pl.ds(r0, S5_SETS), :] = nr
                sim(b, c)[pl.ds(r0, S5_SETS), :] = ni
                out += [nr, ni]
        return tuple(out)

    init = []
    for b in range(nb):
        h_in = hst_re[b]
        g_in = hst_im[b]
        for c in range(halves):
            init += [h_in[:, c * LANES:(c + 1) * LANES], g_in[:, c * LANES:(c + 1) * LANES]]
    fin = lax.fori_loop(0, tc_len, step, tuple(init), unroll=8)
    for b in range(nb):
        for c in range(halves):
            hst_re[b, :, c * LANES:(c + 1) * LANES] = fin[2 * (b * halves + c)]
            hst_im[b, :, c * LANES:(c + 1) * LANES] = fin[2 * (b * halves + c) + 1]

    @pl.when(tci == last_step // tc_len)
    def _():
        r0 = (last_step % tc_len) * S5_SETS
        for b in range(nb):
            for c in range(halves):
                hre_ref[b, 0, :, c * LANES:(c + 1) * LANES] = sre(b, c)[pl.ds(r0, S5_SETS), :]
                him_ref[b, 0, :, c * LANES:(c + 1) * LANES] = sim(b, c)[pl.ds(r0, S5_SETS), :]

    for b in range(nb):
        for c in range(S5_BLK_CH // LANES):
            parts = []
            for s in range(2):
                j = 2 * c + s
                parts += [sre(b, k)[set_rows(j), :].astype(BF16) for k in range(halves)]
                parts += [sim(b, k)[set_rows(j), :].astype(BF16) for k in range(halves)]
            lhs = jnp.concatenate(parts, axis=1)
            y = jnp.dot(lhs, wc_ref[0, c], preferred_element_type=F32)
            y = y + d_ref[:, c * LANES:(c + 1) * LANES] * u_ref[b, :, c * LANES:(c + 1) * LANES]
            z_ref[b, :, c * LANES:(c + 1) * LANES] = jax.nn.gelu(y).astype(BF16)


def _s5_weights(lam_re, lam_im, log_step, b_re, b_im, c_re, c_im):
    g, p = lam_re.shape
    n = SSM_GROUP
    nblk = g // S5_BLK_GROUPS
    lam = lax.complex(lam_re.astype(F32), lam_im.astype(F32))
    dt = jnp.exp(log_step.astype(F32))[:, None]
    lam_bar = jnp.exp(lam * dt)
    b_bar = ((lam_bar - 1.0) / lam)[..., None] * lax.complex(b_re.astype(F32), b_im.astype(F32))
    eye_q = jnp.eye(S5_SET_GROUPS, dtype=F32)
    eye_h = jnp.eye(2, dtype=F32)

    def b_operand(bm):
        t = bm.reshape(nblk, S5_SETS // 2, 2, S5_SET_GROUPS, p, n)
        w = jnp.einsum("bthqpn,hk,qr->bthkrnqp", t, eye_h, eye_q)
        w = w.reshape(nblk, S5_SETS, LANES, S5_SET_LANES)
        return w

    wb = jnp.stack([b_operand(jnp.real(b_bar)), b_operand(jnp.imag(b_bar))], axis=2)
    wb = wb.astype(BF16)
    wb = jnp.concatenate([wb, wb], axis=3)

    def c_operand(cm):
        t = cm.reshape(nblk, S5_SETS // 2, 2, S5_SET_GROUPS, n, p)
        return jnp.einsum("bcsqnp,st,qr->bcsqptrn", t, eye_h, eye_q)

    cre = c_operand(c_re.astype(F32))
    cim = c_operand(-c_im.astype(F32))
    wc = jnp.stack([cre, cim], axis=3).reshape(nblk, S5_SETS // 2, 4 * S5_SET_LANES, LANES).astype(BF16)
    lre = jnp.real(lam_bar).reshape(nblk, S5_SETS, S5_SET_LANES)
    lim = jnp.imag(lam_bar).reshape(nblk, S5_SETS, S5_SET_LANES)
    return wb, wc, lre, lim


def s5_scan(u, h0_re, h0_im, weights, d_skip, seq_valid, tc_len):
    wb, wc, lre, lim = weights
    bsz, lp, d = u.shape
    nblk = d // S5_BLK_CH
    n_tc = lp // tc_len
    h0_re = h0_re.astype(F32).reshape(bsz, nblk, S5_SETS, S5_SET_LANES)
    h0_im = h0_im.astype(F32).reshape(bsz, nblk, S5_SETS, S5_SET_LANES)
    nb = math.gcd(bsz, 2)
    state_spec = pl.BlockSpec((nb, 1, S5_SETS, S5_SET_LANES), lambda k, b, t: (b, k, 0, 0))
    z, hre, him = pl.pallas_call(
        functools.partial(_s5_body, tc_len=tc_len, last_step=seq_valid - 1, nb=nb),
        grid=(nblk, bsz // nb, n_tc),
        in_specs=[pl.BlockSpec((nb, tc_len, S5_BLK_CH), lambda k, b, t: (b, t, k)),
                  pl.BlockSpec((1, S5_SETS, 2, 2 * LANES, S5_SET_LANES), lambda k, b, t: (k, 0, 0, 0, 0)),
                  pl.BlockSpec((1, S5_SETS // 2, 4 * S5_SET_LANES, LANES), lambda k, b, t: (k, 0, 0, 0)),
                  pl.BlockSpec((1, S5_SETS, S5_SET_LANES), lambda k, b, t: (k, 0, 0)),
                  pl.BlockSpec((1, S5_SETS, S5_SET_LANES), lambda k, b, t: (k, 0, 0)),
                  pl.BlockSpec((1, S5_BLK_CH), lambda k, b, t: (0, k)),
                  state_spec, state_spec],
        out_specs=[pl.BlockSpec((nb, tc_len, S5_BLK_CH), lambda k, b, t: (b, t, k)), state_spec, state_spec],
        out_shape=[jax.ShapeDtypeStruct((bsz, lp, d), BF16),
                   jax.ShapeDtypeStruct((bsz, nblk, S5_SETS, S5_SET_LANES), F32),
                   jax.ShapeDtypeStruct((bsz, nblk, S5_SETS, S5_SET_LANES), F32)],
        scratch_shapes=[pltpu.VMEM((tc_len * S5_SETS, LANES), F32)] * (nb * 2 * S5_SET_LANES // LANES) + [
                        pltpu.VMEM((nb, S5_SETS, S5_SET_LANES), F32),
                        pltpu.VMEM((nb, S5_SETS, S5_SET_LANES), F32)],
        compiler_params=_cparams(("arbitrary", "arbitrary", "arbitrary")),
        name="s5_scan",
    )(u, wb, wc, lre, lim, d_skip.reshape(1, d).astype(F32), h0_re, h0_im)
    g = d // SSM_GROUP
    return z, hre.reshape(bsz, g, SSM_STATE), him.reshape(bsz, g, SSM_STATE)


def _cmp_proj_body(pt_ref, x_hbm, w_ref, o_ref, stage, sems, *, pages_per_step):
    n_heads2 = 2 * N_KV_HEADS
    chunks = PAGE_SIZE // CMP_STRIDE
    step = pl.program_id(0) * pl.num_programs(1) + pl.program_id(1)
    n_steps = pl.num_programs(0) * pl.num_programs(1)
    slot = lax.rem(step, 2)

    def page_copy(st, sl, p, c):
        page = pt_ref[st * pages_per_step + p]
        return pltpu.make_async_copy(x_hbm.at[page, :, c, :], stage.at[sl, p, c], sems.at[sl])

    def start_step(st, sl):
        for p in range(pages_per_step):
            for c in range(n_heads2):
                page_copy(st, sl, p, c).start()

    @pl.when(step == 0)
    def _():
        start_step(step, slot)

    @pl.when(step + 1 < n_steps)
    def _():
        start_step(step + 1, 1 - slot)

    for p in range(pages_per_step):
        for c in range(n_heads2):
            page_copy(step, slot, p, c).wait()

    def chunk_rows(c, j):
        return jnp.concatenate([stage[slot, p, c, pl.ds(j, chunks, stride=CMP_STRIDE), :]
                                for p in range(pages_per_step)], axis=0)

    for c in range(n_heads2):
        s, h = divmod(c, N_KV_HEADS)
        acc = None
        for jp in range(CMP_STRIDE // 2):
            lhs = jnp.concatenate([chunk_rows(c, 2 * jp), chunk_rows(c, 2 * jp + 1)], axis=1).astype(BF16)
            t = jnp.dot(lhs, w_ref[s, jp], preferred_element_type=F32)
            acc = t if acc is None else acc + t
        o_ref[0, s, h] = acc


def cmp_project(rows, page_table, w1cat, pages_per_step=16):
    bsz, npg = page_table.shape
    pps = math.gcd(npg, pages_per_step)
    chunks = PAGE_SIZE // CMP_STRIDE
    hid2 = w1cat.shape[-1]
    grid_spec = pltpu.PrefetchScalarGridSpec(
        num_scalar_prefetch=1,
        grid=(bsz, npg // pps),
        in_specs=[pl.BlockSpec(memory_space=pl.ANY),
                  pl.BlockSpec(w1cat.shape, lambda b, g, pt: (0, 0, 0, 0))],
        out_specs=pl.BlockSpec((1, 2, N_KV_HEADS, pps * chunks, hid2), lambda b, g, pt: (b, 0, 0, g, 0)),
        scratch_shapes=[pltpu.VMEM((2, pps, 2 * N_KV_HEADS, PAGE_SIZE, HEAD_DIM), F32),
                        pltpu.SemaphoreType.DMA((2,))],
    )
    return pl.pallas_call(
        functools.partial(_cmp_proj_body, pages_per_step=pps),
        grid_spec=grid_spec,
        out_shape=jax.ShapeDtypeStruct((bsz, 2, N_KV_HEADS, npg * chunks, hid2), F32),
        compiler_params=_cparams(("arbitrary", "arbitrary")),
        name="cmp_project",
    )(page_table.reshape(-1).astype(I32), rows, w1cat)


def _cmp_finish_body(ab_ref, pe_ref, w1_ref, b1_ref, w2_ref, b2_ref, o_ref):
    ab = ab_ref[0, 0, 0]
    n16 = ab.shape[0]
    hid = ab.shape[1] // 2
    a = ab[:, :hid]
    bnext = pltpu.roll(ab[:, hid:], n16 - 1, axis=0)
    c = jnp.dot(pe_ref[0], w1_ref[0], preferred_element_type=F32)[0:1, :] + b1_ref[0]
    pre = a + bnext + c
    y = jnp.dot(jax.nn.gelu(pre).astype(BF16), w2_ref[0], preferred_element_type=F32) + b2_ref[0]
    row = lax.broadcasted_iota(I32, y.shape, 0)
    o_ref[0, 0, 0] = jnp.where(row < n16 - 1, y, 0.0).astype(o_ref.dtype)


def cmp_finish(ab, pe, w1, b1, w2, b2):
    bsz, _, kvh, n16, hid2 = ab.shape
    hid = hid2 // 2
    pe_rows = jnp.broadcast_to(pe.reshape(2, 1, -1), (2, SUBLANES, pe.shape[1] * pe.shape[2])).astype(BF16)
    return pl.pallas_call(
        _cmp_finish_body,
        grid=(2, bsz, kvh),
        in_specs=[pl.BlockSpec((1, 1, 1, n16, hid2), lambda s, b, h: (b, s, h, 0, 0)),
                  pl.BlockSpec((1, SUBLANES, pe_rows.shape[2]), lambda s, b, h: (s, 0, 0)),
                  pl.BlockSpec((1,) + w1.shape[1:], lambda s, b, h: (s, 0, 0)),
                  pl.BlockSpec((1, 1, hid), lambda s, b, h: (s, 0, 0)),
                  pl.BlockSpec((1, hid, HEAD_DIM), lambda s, b, h: (s, 0, 0)),
                  pl.BlockSpec((1, 1, HEAD_DIM), lambda s, b, h: (s, 0, 0))],
        out_specs=pl.BlockSpec((1, 1, 1, n16, HEAD_DIM), lambda s, b, h: (b, s, h, 0, 0)),
        out_shape=jax.ShapeDtypeStruct((bsz, 2, kvh, n16, HEAD_DIM), BF16),
        compiler_params=_cparams(("arbitrary", "arbitrary", "arbitrary")),
        name="cmp_finish",
    )(ab, pe_rows, w1.astype(BF16), b1.reshape(2, 1, hid).astype(F32), w2.astype(BF16),
      b2.reshape(2, 1, HEAD_DIM).astype(F32))


def _masked_softmax(s, mask):
    sm = jnp.where(mask, s, NEG_BIG)
    m = jnp.max(sm, axis=-1, keepdims=True)
    m = jnp.where(m > 0.5 * NEG_BIG, m, 0.0)
    e = jnp.where(mask, jnp.exp(sm - m), 0.0)
    den = jnp.sum(e, axis=-1, keepdims=True)
    return e / jnp.where(den > 0, den, 1.0)


def _dot_nt(a, b):
    return lax.dot_general(a, b, (((1,), (1,)), ((), ())), preferred_element_type=F32)


def _overlap(n_c, n_j, c_axis):
    shape = (n_c, n_j) if c_axis == 0 else (n_j, n_c)
    c = lax.broadcasted_iota(I32, shape, c_axis) * CMP_STRIDE
    j = lax.broadcasted_iota(I32, shape, 1 - c_axis) * SEL_BLOCK
    return ((c < j + SEL_BLOCK) & (c + CMP_BLOCK > j)).astype(F32)


def _block_scores(imp, q_pos, n_blk):
    blk = lax.broadcasted_iota(I32, imp.shape, 1)
    q_blk = jnp.right_shift(q_pos, SEL_SHIFT)
    forced = (blk == 0) | (blk == q_blk) | (blk == q_blk - 1)
    future = blk * SEL_BLOCK > q_pos
    v = jnp.where(future, NEG_BIG, jnp.where(forced, -NEG_BIG, imp))
    return jnp.where(blk < n_blk, v, 2.0 * NEG_BIG)


def _rank_desc(v, n):
    lane = lax.broadcasted_iota(I32, v.shape, 1)
    rank = jnp.zeros(v.shape, F32)
    for i in range(n):
        vi = v[:, i:i + 1]
        before = (vi > v) | ((vi == v) & (lane > i))
        rank = rank + jnp.where(before, 1.0, 0.0)
    return rank


def _nsa_prompt_body(q_ref, gt_ref, ck_ref, cvt_ref, ks_ref, vs_ref, kw_ref, vw_ref, o_ref, m_ref, acc_ref,
                     vst_ref, vwt_ref, *, n_cmp, n_blk, gqa, sel_tile, win_keys):
    qb = pl.program_id(2)
    nq = Q_BLOCK
    gq = gqa * nq
    c2 = (HEAD_DIM ** -0.5) * LOG2_E

    @pl.when(qb == 0)
    def _():
        ones = jnp.ones((ONES_ROWS, LANES), BF16)
        for src, dst in ((vs_ref, vst_ref), (vw_ref, vwt_ref)):
            for t in range(dst.shape[0]):
                dst[t, :HEAD_DIM, :] = src[0, t * LANES:(t + 1) * LANES, :].T.astype(BF16)
                dst[t, HEAD_DIM:, :] = ones

    qt = jnp.concatenate([q_ref[0, :, g * HEAD_DIM:(g + 1) * HEAD_DIM].astype(F32).T.astype(BF16)
                          for g in range(gqa)], axis=1)
    q_pos = qb * nq + lax.broadcasted_iota(I32, (1, nq), 1)

    def per_head(x):
        return jnp.concatenate([x] * gqa, axis=1)

    def normalise(acc):
        den = acc[HEAD_DIM:HEAD_DIM + 1, :]
        return acc[:HEAD_DIM, :] * (1.0 / jnp.where(den > 0, den, 1.0))

    ck = ck_ref[0, 0, 0]
    n16 = ck.shape[0]
    s = jnp.dot(ck, qt, preferred_element_type=F32)
    cidx = lax.broadcasted_iota(I32, (n16, 1), 0)
    cbias = jnp.where((cidx * CMP_STRIDE + (CMP_BLOCK - 1) <= q_pos) & (cidx < n_cmp), 0.0, NEG_BIG)
    sb = s + per_head(cbias)
    m = jnp.maximum(jnp.max(sb, axis=0, keepdims=True), M_FLOOR)
    e = jnp.exp2((sb - m) * c2)
    den = jnp.sum(e, axis=0, keepdims=True)
    p = e * jnp.where(den > 0, 1.0 / den, 0.0)
    o_cmp = jnp.dot(cvt_ref[0, 0], p.astype(BF16), preferred_element_type=F32)

    psum = p[:, 0:nq]
    for g in range(1, gqa):
        psum = psum + p[:, g * nq:(g + 1) * nq]
    nb_rows = -(-n_blk // ONES_ROWS) * ONES_ROWS
    imp = jnp.dot(_overlap(n16, nb_rows, 1), psum, preferred_element_type=F32, precision=lax.Precision.HIGHEST)
    blk = lax.broadcasted_iota(I32, (nb_rows, 1), 0)
    q_blk = jnp.right_shift(q_pos, SEL_SHIFT)
    forced = (blk == 0) | (blk == q_blk) | (blk == q_blk - 1)
    v = jnp.where(blk * SEL_BLOCK > q_pos, NEG_BIG, jnp.where(forced, -NEG_BIG, imp))
    v = jnp.where(blk < n_blk, v, 2.0 * NEG_BIG)
    rank = jnp.zeros(v.shape, F32)
    for i in range(n_blk):
        vi = v[i:i + 1, :]
        rank = rank + jnp.where((vi > v) | ((vi == v) & (blk > i)), 1.0, 0.0)
    sel_bias = jnp.where((rank < min(N_SEL, n_blk)) & (blk < n_blk), 0.0, NEG_BIG).astype(BF16)
    if nb_rows < LANES:
        sel_bias = jnp.concatenate([sel_bias, jnp.zeros((LANES - nb_rows, nq), BF16)], axis=0)

    m_ref[...] = jnp.full(m_ref.shape, M_FLOOR, F32)
    acc_ref[...] = jnp.zeros(acc_ref.shape, F32)
    tiles = sel_tile // LANES

    def sel_step(kt, _):
        k0 = pl.multiple_of(kt * sel_tile, sel_tile)
        s = jnp.dot(ks_ref[0, pl.ds(k0, sel_tile), :].astype(BF16), qt, preferred_element_type=F32)
        key = k0 + lax.broadcasted_iota(I32, (sel_tile, 1), 0)
        expand = jnp.where(jnp.right_shift(key, SEL_SHIFT) == lax.broadcasted_iota(I32, (sel_tile, LANES), 1), 1.0, 0.0)
        bias = jnp.dot(expand.astype(BF16), sel_bias, preferred_element_type=F32)
        bias = jnp.where(key <= q_pos, bias, NEG_BIG)
        sb = s + per_head(bias)
        m_old = m_ref[0:1, :]
        m_new = jnp.maximum(m_old, jnp.max(sb, axis=0, keepdims=True))
        alpha = jnp.exp2((m_old - m_new) * c2)
        e = jnp.exp2((sb - m_new) * c2).astype(BF16)
        vt = jnp.concatenate([vst_ref[kt * tiles + i] for i in range(tiles)], axis=1)
        acc_ref[...] = alpha * acc_ref[...] + jnp.dot(vt, e, preferred_element_type=F32)
        m_ref[...] = jnp.broadcast_to(m_new, m_ref.shape)
        return 0

    lax.fori_loop(0, (qb * nq + nq + sel_tile - 1) // sel_tile, sel_step, 0)
    o_sel = normalise(acc_ref[...])

    w_tile = jnp.maximum(qb + 1 - win_keys // nq, 0)
    w0 = pl.multiple_of(w_tile * nq, nq)
    s = jnp.dot(kw_ref[0, pl.ds(w0, win_keys), :].astype(BF16), qt, preferred_element_type=F32)
    key = w0 + lax.broadcasted_iota(I32, (win_keys, 1), 0)
    wbias = jnp.where((key <= q_pos) & (key > q_pos - WINDOW), 0.0, NEG_BIG)
    sb = s + per_head(wbias)
    m = jnp.maximum(jnp.max(sb, axis=0, keepdims=True), M_FLOOR)
    e = jnp.exp2((sb - m) * c2).astype(BF16)
    vt = jnp.concatenate([vwt_ref[w_tile + i] for i in range(win_keys // LANES)], axis=1)
    o_win = normalise(jnp.dot(vt, e, preferred_element_type=F32))

    gt = gt_ref[0, 0, 0]
    for g in range(gqa):
        cols = slice(g * nq, (g + 1) * nq)
        o = (gt[3 * g:3 * g + 1, :] * o_cmp[:, cols] + gt[3 * g + 1:3 * g + 2, :] * o_sel[:, cols]
             + gt[3 * g + 2:3 * g + 3, :] * o_win[:, cols])
        o_ref[0, :, g * HEAD_DIM:(g + 1) * HEAD_DIM] = o.T.astype(o_ref.dtype)


def nsa_prompt_attend(q, gates, cmp, k_sel, v_sel, k_win, v_win, bsz, t, sel_tile=512):
    hd_all = q.shape[1]
    kvh = N_KV_HEADS
    gqa = hd_all // HEAD_DIM // kvh
    n16 = cmp.shape[3]
    n_blk = -(-t // SEL_BLOCK)
    nqb = t // Q_BLOCK
    gq = gqa * Q_BLOCK
    vrows = HEAD_DIM + ONES_ROWS
    assert t % Q_BLOCK == 0 and n_blk <= LANES and n16 * CMP_STRIDE == t
    sel_tile = math.gcd(t, sel_tile)
    win_keys = min(WINDOW + Q_BLOCK, t)
    gt = gates[:, :3 * kvh * gqa].reshape(bsz, nqb, Q_BLOCK, kvh, 3 * gqa).transpose(0, 3, 1, 4, 2)
    gt = jnp.pad(gt, ((0, 0), (0, 0), (0, 0), (0, -3 * gqa % SUBLANES), (0, 0)))
    cvt = cmp[:, 1].transpose(0, 1, 3, 2)
    (ks_arr, ks_col), (vs_arr, vs_col), (kw_arr, kw_col), (vw_arr, vw_col) = k_sel, v_sel, k_win, v_win
    k_spec = lambda col: pl.BlockSpec((1, t, HEAD_DIM), lambda b, h, i, col=col: (b, 0, col + h))
    return pl.pallas_call(
        functools.partial(_nsa_prompt_body, n_cmp=n16 - 1, n_blk=n_blk, gqa=gqa, sel_tile=sel_tile, win_keys=win_keys),
        grid=(bsz, kvh, nqb),
        in_specs=[pl.BlockSpec((1, Q_BLOCK, gqa * HEAD_DIM), lambda b, h, i: (b, i, h)),
                  pl.BlockSpec((1, 1, 1, gt.shape[3], Q_BLOCK), lambda b, h, i: (b, h, i, 0, 0)),
                  pl.BlockSpec((1, 1, 1, n16, HEAD_DIM), lambda b, h, i: (b, 0, h, 0, 0)),
                  pl.BlockSpec((1, 1, HEAD_DIM, n16), lambda b, h, i: (b, h, 0, 0)),
                  k_spec(ks_col), k_spec(vs_col), k_spec(kw_col), k_spec(vw_col)],
        out_specs=pl.BlockSpec((1, Q_BLOCK, gqa * HEAD_DIM), lambda b, h, i: (b, i, h)),
        out_shape=jax.ShapeDtypeStruct((bsz, t, hd_all), BF16),
        scratch_shapes=[pltpu.VMEM((SUBLANES, gq), F32), pltpu.VMEM((vrows, gq), F32),
                        pltpu.VMEM((t // LANES, vrows, LANES), BF16), pltpu.VMEM((t // LANES, vrows, LANES), BF16)],
        compiler_params=_cparams(("parallel", "parallel", "arbitrary")),
        name="nsa_prompt",
    )(q.reshape(bsz, t, hd_all), gt, cmp, cvt, ks_arr.reshape(bsz, t, -1), vs_arr.reshape(bsz, t, -1),
      kw_arr.reshape(bsz, t, -1), vw_arr.reshape(bsz, t, -1)).reshape(bsz * t, hd_all)


def _nsa_sample_select_body(q_ref, ck_ref, cv_ref, ocmp_ref, idx_ref, *, n_cmp, n_blk, blk_lanes, past_len, tq_pad, gqa):
    scale = HEAD_DIM ** -0.5
    q = jnp.concatenate([q_ref[0, :, g * HEAD_DIM:(g + 1) * HEAD_DIM] for g in range(gqa)], axis=0)
    ck = ck_ref[0, 0, 0]
    cv = cv_ref[0, 0, 0]
    n16 = ck.shape[0]
    q_pos = past_len + lax.broadcasted_iota(I32, (tq_pad, 1), 0)
    s = (_dot_nt(q, ck) * scale).reshape(gqa, tq_pad, n16)
    cidx = lax.broadcasted_iota(I32, (tq_pad, n16), 1)
    cmask = (cidx * CMP_STRIDE + (CMP_BLOCK - 1) <= q_pos) & (cidx < n_cmp)
    p = _masked_softmax(s, cmask[None])
    ocmp_ref[0, 0] = jnp.dot(p.reshape(gqa * tq_pad, n16).astype(BF16), cv, preferred_element_type=F32)
    psum = jnp.sum(p, axis=0)
    imp = jnp.dot(psum, _overlap(n16, blk_lanes, 0), preferred_element_type=F32, precision=lax.Precision.HIGHEST)
    rank = _rank_desc(_block_scores(imp, q_pos, n_blk), n_blk)
    lane = lax.broadcasted_iota(I32, rank.shape, 1)
    lane_f = lane.astype(F32)
    out_lane = lax.broadcasted_iota(I32, (tq_pad, LANES), 1)
    idx = jnp.zeros((tq_pad, LANES), F32)
    for k in range(min(N_SEL, n_blk)):
        hit = (rank == float(k)) & (lane < n_blk)
        idx_k = jnp.sum(jnp.where(hit, lane_f, 0.0), axis=-1, keepdims=True)
        idx = jnp.where(out_lane == k, idx_k, idx)
    idx_ref[0, 0] = idx.astype(I32)


def _nsa_sample_attend_body(idx_ref, pt_ref, q_ref, gt_ref, ocmp_ref, kwc_ref, vwc_ref,
                            kwn_ref, vwn_ref, cache_ref, newblk_ref, o_ref, kbuf, vbuf, kwbuf, vwbuf, sems,
                            *, n_sel, n_cache_blk, n_pages, past_len, win_buf, tq, tq_pad, gqa):
    b = pl.program_id(0)
    h = pl.program_id(1)
    scale = HEAD_DIM ** -0.5
    blk_per_page = PAGE_SIZE // SEL_BLOCK

    bufs = ((2, kbuf, 0), (3, vbuf, 1))
    step = b * N_KV_HEADS + h
    n_steps = pl.num_programs(0) * N_KV_HEADS
    cur = _mod(step, 2)

    def from_new(bb, hh, sl, t, k, slot, buf, sem):
        return pltpu.make_async_copy(newblk_ref.at[bb, :, slot * N_KV_HEADS + hh], buf.at[sl, t, k], sems.at[sl, sem])

    def start_gather(st, sl):
        bb = _div(st, N_KV_HEADS)
        hh = _mod(st, N_KV_HEADS)
        for t in range(tq):
            for k in range(n_sel):
                blk = idx_ref[(st * tq + t) * n_sel + k]
                in_cache = blk < n_cache_blk
                blk_c = jnp.minimum(blk, n_cache_blk - 1)
                page = pt_ref[bb * n_pages + _div(blk_c, blk_per_page)]
                off = pl.multiple_of(_mod(blk_c, blk_per_page) * SEL_BLOCK, SEL_BLOCK)
                for slot, buf, sem in bufs:
                    @pl.when(in_cache)
                    def _():
                        pltpu.make_async_copy(cache_ref.at[page, pl.ds(off, SEL_BLOCK), slot * N_KV_HEADS + hh],
                                              buf.at[sl, t, k], sems.at[sl, sem]).start()

                    @pl.when(jnp.logical_not(in_cache))
                    def _():
                        from_new(bb, hh, sl, t, k, slot, buf, sem).start()

    @pl.when(step == 0)
    def _():
        start_gather(step, cur)

    @pl.when(step + 1 < n_steps)
    def _():
        start_gather(step + 1, 1 - cur)

    q = jnp.concatenate([q_ref[0, :, g * HEAD_DIM:(g + 1) * HEAD_DIM] for g in range(gqa)], axis=0)
    rows = gqa * tq_pad
    row_t = lax.rem(lax.broadcasted_iota(I32, (rows, 1), 0), tq_pad)
    q_pos = past_len + row_t

    kwbuf[...] = jnp.zeros(kwbuf.shape, F32)
    vwbuf[...] = jnp.zeros(vwbuf.shape, F32)
    kwbuf[0:win_buf, :] = kwc_ref[0]
    vwbuf[0:win_buf, :] = vwc_ref[0]
    kwbuf[win_buf:win_buf + tq_pad, :] = kwn_ref[0]
    vwbuf[win_buf:win_buf + tq_pad, :] = vwn_ref[0]
    nw = kwbuf.shape[0]
    wlane = lax.broadcasted_iota(I32, (rows, nw), 1)
    wpos = past_len - win_buf + wlane
    wmask = (wpos <= q_pos) & (wpos > q_pos - WINDOW) & (wpos >= 0) & (wlane < win_buf + tq)
    s = _dot_nt(q, kwbuf[...].astype(BF16)) * scale
    p = _masked_softmax(s, wmask)
    o_win = jnp.dot(p.astype(BF16), vwbuf[...].astype(BF16), preferred_element_type=F32)

    for t in range(tq):
        for k in range(n_sel):
            for slot, buf, sem in bufs:
                from_new(b, h, cur, t, k, slot, buf, sem).wait()

    o_sel = jnp.zeros((rows, HEAD_DIM), F32)
    nk = n_sel * SEL_BLOCK
    klane = lax.broadcasted_iota(I32, (rows, nk), 1)
    for t in range(tq):
        kpos = jnp.zeros((rows, nk), I32)
        for k in range(n_sel):
            blk = idx_ref[((b * N_KV_HEADS + h) * tq + t) * n_sel + k]
            kpos = jnp.where(jnp.right_shift(klane, SEL_SHIFT) == k,
                             blk * SEL_BLOCK + jnp.bitwise_and(klane, SEL_BLOCK - 1), kpos)
        mask = (kpos <= q_pos) & (row_t == t)
        kt = kbuf[cur, t].reshape(nk, HEAD_DIM).astype(BF16)
        vt = vbuf[cur, t].reshape(nk, HEAD_DIM).astype(BF16)
        p = _masked_softmax(_dot_nt(q, kt) * scale, mask)
        o_sel = o_sel + jnp.dot(p.astype(BF16), vt, preferred_element_type=F32)

    o_cmp = ocmp_ref[0, 0]
    gt = gt_ref[0, 0]
    for g in range(gqa):
        r = slice(g * tq_pad, (g + 1) * tq_pad)
        o = (gt[:, 3 * g:3 * g + 1] * o_cmp[r] + gt[:, 3 * g + 1:3 * g + 2] * o_sel[r]
             + gt[:, 3 * g + 2:3 * g + 3] * o_win[r])
        o_ref[0, :, g * HEAD_DIM:(g + 1) * HEAD_DIM] = o.astype(o_ref.dtype)


def nsa_sample_attend(q, gates, cmp, new_rows, win_new, cache_kv, cache_win, page_table, tq):
    bsz, tq_pad, hd_all = q.shape
    kvh = N_KV_HEADS
    gqa = hd_all // HEAD_DIM // kvh
    n_pages = page_table.shape[1]
    past_len = n_pages * PAGE_SIZE
    n16 = cmp.shape[3]
    n_cmp = n16 - 1
    n_cache_blk = past_len // SEL_BLOCK
    n_blk = -(-(past_len + tq) // SEL_BLOCK)
    assert n_blk == n_cache_blk + 1 and past_len % SEL_BLOCK == 0 and tq <= SEL_BLOCK
    n_sel = min(N_SEL, n_blk)
    blk_lanes = -(-n_blk // LANES) * LANES
    win_buf = cache_win.shape[1]
    cmp_spec = lambda s: pl.BlockSpec((1, 1, 1, n16, HEAD_DIM), lambda b, h, s=s: (b, s, h, 0, 0))
    ocmp, idx = pl.pallas_call(
        functools.partial(_nsa_sample_select_body, n_cmp=n_cmp, n_blk=n_blk, blk_lanes=blk_lanes,
                          past_len=past_len, tq_pad=tq_pad, gqa=gqa),
        grid=(bsz, kvh),
        in_specs=[pl.BlockSpec((1, tq_pad, gqa * HEAD_DIM), lambda b, h: (b, 0, h)), cmp_spec(0), cmp_spec(1)],
        out_specs=[pl.BlockSpec((1, 1, gqa * tq_pad, HEAD_DIM), lambda b, h: (b, h, 0, 0)),
                   pl.BlockSpec((1, 1, tq_pad, LANES), lambda b, h: (b, h, 0, 0))],
        out_shape=[jax.ShapeDtypeStruct((bsz, kvh, gqa * tq_pad, HEAD_DIM), F32),
                   jax.ShapeDtypeStruct((bsz, kvh, tq_pad, LANES), I32)],
        compiler_params=_cparams(("parallel", "parallel")),
        name="nsa_sample_select",
    )(q, cmp, cmp)
    idx_flat = idx[:, :, :tq, :n_sel].reshape(-1)

    n_slots = cache_kv.shape[2]
    cache4 = cache_kv.reshape(cache_kv.shape[0], PAGE_SIZE, n_slots * kvh, HEAD_DIM)
    newblk = jnp.zeros((bsz, SEL_BLOCK, n_slots * kvh, HEAD_DIM), F32).at[:, :tq].set(new_rows)
    cwin = cache_win.astype(F32).reshape(bsz, win_buf, 2 * kvh * HEAD_DIM)
    nw = -(-(win_buf + tq_pad) // LANES) * LANES
    kv_spec = lambda slot: pl.BlockSpec((1, tq_pad, HEAD_DIM), lambda b, h, *_, slot=slot: (b, 0, slot * kvh + h))
    cw_spec = lambda slot: pl.BlockSpec((1, win_buf, HEAD_DIM), lambda b, h, *_, slot=slot: (b, 0, slot * kvh + h))
    grid_spec = pltpu.PrefetchScalarGridSpec(
        num_scalar_prefetch=2,
        grid=(bsz, kvh),
        in_specs=[pl.BlockSpec((1, tq_pad, gqa * HEAD_DIM), lambda b, h, *_: (b, 0, h)),
                  pl.BlockSpec((1, 1, tq_pad, LANES), lambda b, h, *_: (b, h, 0, 0)),
                  pl.BlockSpec((1, 1, gqa * tq_pad, HEAD_DIM), lambda b, h, *_: (b, h, 0, 0)),
                  cw_spec(0), cw_spec(1), kv_spec(0), kv_spec(1),
                  pl.BlockSpec(memory_space=pl.ANY), pl.BlockSpec(memory_space=pl.ANY)],
        out_specs=pl.BlockSpec((1, tq_pad, gqa * HEAD_DIM), lambda b, h, *_: (b, 0, h)),
        scratch_shapes=[pltpu.VMEM((2, tq, n_sel, SEL_BLOCK, HEAD_DIM), F32),
                        pltpu.VMEM((2, tq, n_sel, SEL_BLOCK, HEAD_DIM), F32),
                        pltpu.VMEM((nw, HEAD_DIM), F32), pltpu.VMEM((nw, HEAD_DIM), F32),
                        pltpu.SemaphoreType.DMA((2, 2))],
    )
    return pl.pallas_call(
        functools.partial(_nsa_sample_attend_body, n_sel=n_sel, n_cache_blk=n_cache_blk, n_pages=n_pages,
                          past_len=past_len, win_buf=win_buf, tq=tq, tq_pad=tq_pad, gqa=gqa),
        grid_spec=grid_spec,
        out_shape=jax.ShapeDtypeStruct((bsz, tq_pad, hd_all), BF16),
        compiler_params=_cparams(("arbitrary", "arbitrary")),
        name="nsa_sample_attend",
    )(idx_flat, page_table.reshape(-1).astype(I32), q, gates, ocmp, cwin, cwin, win_new, win_new, cache4, newblk)


def _pad_rows(x, rows):
    return jnp.pad(x, ((0, 0), (0, rows - x.shape[1]), (0, 0)))


@jax.jit
def _step(x_prompt, x_sample, state_ssm_re, state_ssm_im, state_ffn_conv, cache_kv, cache_win, page_table,
          attn_norm, ffn_norm, final_norm, ssm_lam_re, ssm_lam_im, ssm_log_step, ssm_b_re, ssm_b_im,
          ssm_c_re, ssm_c_im, ssm_d, ssm_w_glu, ffn_w_in, ffn_conv_w, ffn_conv_b, ffn_w_down,
          kv_norm, w_kv, cmp_w1, cmp_b1, cmp_w2, cmp_b2, cmp_pe, w_qg, w_o):
    bp, tp, d = x_prompt.shape
    bs, ts, _ = x_sample.shape
    depth = attn_norm.shape[0]
    n_a = ssm_lam_re.shape[0]
    kvh = N_KV_HEADS
    hd_all = w_o.shape[1]
    n_heads = hd_all // HEAD_DIM
    gqa = n_heads // kvh
    ts_pad = -(-ts // SUBLANES) * SUBLANES
    n_pages = page_table.shape[1]

    xp = x_prompt.reshape(bp * tp, d).astype(F32)
    xs = x_sample.reshape(bs * ts, d).astype(F32)

    w1cat = jnp.concatenate([cmp_w1[:, :CMP_STRIDE * HEAD_DIM], cmp_w1[:, CMP_STRIDE * HEAD_DIM:]], axis=-1)
    w1cat = w1cat.reshape(2, CMP_STRIDE // 2, 2 * HEAD_DIM, w1cat.shape[-1]).astype(BF16)

    ssm_re_p, ssm_im_p, ssm_re_s, ssm_im_s, conv_p, conv_s = [], [], [], [], [], []
    outs = {}
    for layer in range(depth):
        if layer < n_a:
            a = layer
            wts = _s5_weights(ssm_lam_re[a], ssm_lam_im[a], ssm_log_step[a], ssm_b_re[a], ssm_b_im[a],
                              ssm_c_re[a], ssm_c_im[a])
            w_glu = cast_weight(ssm_w_glu, a)
            up = rmsnorm(xp, [attn_norm[layer]], F32)[0].reshape(bp, tp, d)
            zeros = jnp.zeros((bp, d // SSM_GROUP, SSM_STATE), F32)
            zp, hre, him = s5_scan(up, zeros, zeros, wts, ssm_d[a], tp, math.gcd(tp, 512))
            ssm_re_p.append(hre)
            ssm_im_p.append(him)
            xp = glu_matmul(zp.reshape(bp * tp, d), w_glu, xp)
            us = _pad_rows(rmsnorm(xs, [attn_norm[layer]], F32)[0].reshape(bs, ts, d), ts_pad)
            zs, hre, him = s5_scan(us, state_ssm_re[a], state_ssm_im[a], wts, ssm_d[a], ts, ts_pad)
            ssm_re_s.append(hre)
            ssm_im_s.append(him)
            xs = glu_matmul(zs[:, :ts].reshape(bs * ts, d), w_glu, xs)
        else:
            bl = layer - n_a
            if layer == n_a:
                w_kv_b = cast_weight(w_kv[None], 0)
                n_kv4 = 4 * kvh * HEAD_DIM
                sp, hp_first = rmsnorm(xp, [kv_norm, attn_norm[layer]], BF16)
                ss, hs_first = rmsnorm(xs, [kv_norm, attn_norm[layer]], BF16)
                kvr_p, kvw_p = matmul_split(sp, w_kv_b, n_kv4)
                kvr_s, kvw_s = matmul_split(ss, w_kv_b, n_kv4)
                outs["kv_rows_p"] = kvr_p.reshape(bp, tp, 4, kvh, HEAD_DIM)
                outs["kv_rows_s"] = kvr_s.reshape(bs, ts, 4, kvh, HEAD_DIM)
                outs["win_p"] = kvw_p.reshape(bp, tp, 2, kvh, HEAD_DIM)[:, max(tp - WINDOW, 0):]
                win_s = kvw_s.reshape(bs, ts, 2, kvh, HEAD_DIM)
                win_buf = cache_win.shape[1]
                outs["win_s"] = jnp.concatenate([cache_win.astype(F32), win_s], axis=1)[:, -win_buf:]
                kvw_s3 = _pad_rows(kvw_s.reshape(bs, ts, kvw_s.shape[1]), ts_pad)
                assert tp % PAGE_SIZE == 0 and (n_pages * PAGE_SIZE + ts) // CMP_STRIDE == n_pages * PAGE_SIZE // CMP_STRIDE
                pt_p = jnp.arange(bp * (tp // PAGE_SIZE), dtype=I32).reshape(bp, tp // PAGE_SIZE)
                ab_p = cmp_project(kvr_p.reshape(bp * tp // PAGE_SIZE, PAGE_SIZE, 4 * kvh, HEAD_DIM), pt_p, w1cat)
                cmp_p = cmp_finish(ab_p, cmp_pe, cmp_w1, cmp_b1, cmp_w2, cmp_b2)
                cache4 = cache_kv.astype(F32).reshape(cache_kv.shape[0], PAGE_SIZE, -1, HEAD_DIM)
                ab_s = cmp_project(cache4, page_table, w1cat)
                cmp_s = cmp_finish(ab_s, cmp_pe, cmp_w1, cmp_b1, cmp_w2, cmp_b2)
            w_qg_b = w_qg[bl].astype(BF16)
            w_q = w_qg_b[:, :hd_all]
            n_gate = w_qg.shape[2] - hd_all
            w_g = jnp.pad(w_qg_b[:, hd_all:], ((0, 0), (0, LANES - n_gate)))
            w_o_b = cast_weight(w_o, bl)

            def gate_layout(gt, bsz, t):
                gt = gt[:, :n_gate].reshape(bsz, t, kvh, 3 * gqa).transpose(0, 2, 1, 3)
                return jnp.pad(gt, ((0, 0), (0, 0), (0, 0), (0, LANES - 3 * gqa)))

            hp = hp_first if layer == n_a else rmsnorm(xp, [attn_norm[layer]], BF16)[0]
            q_p = matmul(hp, w_q, out_dtype=BF16, tm=1024, tn=1024)
            g_p = matmul(hp, w_g, act="sigmoid", tm=1024)
            o_p = nsa_prompt_attend(q_p, g_p, cmp_p, (kvr_p, 2 * kvh), (kvr_p, 3 * kvh), (kvw_p, 0), (kvw_p, kvh), bp, tp)
            xp = matmul(o_p, w_o_b, res=xp, tm=1024, tn=1024)

            hs = hs_first if layer == n_a else rmsnorm(xs, [attn_norm[layer]], BF16)[0]
            q_s = _pad_rows(matmul(hs, w_q, out_dtype=BF16).reshape(bs, ts, hd_all), ts_pad)
            g_s = gate_layout(matmul(hs, w_g, act="sigmoid"), bs, ts)
            g_s = jnp.pad(g_s, ((0, 0), (0, 0), (0, ts_pad - ts), (0, 0)))
            o_s = nsa_sample_attend(q_s, g_s, cmp_s, kvr_s.reshape(bs, ts, 4 * kvh, HEAD_DIM), kvw_s3,
                                    cache_kv.astype(F32), cache_win, page_table, ts)
            xs = matmul(o_s[:, :ts].reshape(bs * ts, hd_all), w_o_b, res=xs)

        f = ffn_w_down.shape[1]
        act, st, act_s, st_s, w_down = ffn_up(
            rmsnorm(xp, [ffn_norm[layer]], BF16)[0], rmsnorm(xs, [ffn_norm[layer]], BF16)[0],
            ffn_w_in.astype(F32), ffn_w_down.astype(F32), layer, ffn_conv_w[layer], ffn_conv_b[layer],
            jnp.zeros((bp, CONV_W - 1, f), F32), state_ffn_conv[layer], tp, ts)
        conv_p.append(st)
        conv_s.append(st_s)
        xp = matmul(act, w_down, res=xp, tm=512, tn=512)
        xs = matmul(act_s, w_down, res=xs, tm=512, tn=256)

    y_p = rmsnorm(xp, [final_norm], F32)[0].reshape(bp, tp, d)
    y_s = rmsnorm(xs, [final_norm], F32)[0].reshape(bs, ts, d)
    return (y_p, y_s, jnp.stack(ssm_re_p), jnp.stack(ssm_im_p), jnp.stack(ssm_re_s), jnp.stack(ssm_im_s),
            jnp.stack(conv_p), jnp.stack(conv_s), outs["kv_rows_p"], outs["kv_rows_s"], outs["win_p"], outs["win_s"])


def kernel(x_prompt, x_sample, state_ssm_re, state_ssm_im, state_ffn_conv, cache_kv, cache_win, page_table, attn_norm, ffn_norm, final_norm, ssm_lam_re, ssm_lam_im, ssm_log_step, ssm_b_re, ssm_b_im, ssm_c_re, ssm_c_im, ssm_d, ssm_w_glu, ffn_w_in, ffn_conv_w, ffn_conv_b, ffn_w_down, kv_norm, w_kv, cmp_w1, cmp_b1, cmp_w2, cmp_b2, cmp_pe, w_qg, w_o):
    return _step(x_prompt, x_sample, state_ssm_re, state_ssm_im, state_ffn_conv, cache_kv, cache_win, page_table,
                 attn_norm, ffn_norm, final_norm, ssm_lam_re, ssm_lam_im, ssm_log_step, ssm_b_re, ssm_b_im,
                 ssm_c_re, ssm_c_im, ssm_d, ssm_w_glu, ffn_w_in, ffn_conv_w, ffn_conv_b, ffn_w_down,
                 kv_norm, w_kv, cmp_w1, cmp_b1, cmp_w2, cmp_b2, cmp_pe, w_qg, w_o)
```

```python
import functools
import math

import jax
import jax.numpy as jnp
from jax import lax
from jax.experimental import pallas as pl
from jax.experimental.pallas import tpu as pltpu

F32 = jnp.float32
BF16 = jnp.bfloat16
I32 = jnp.int32

RMS_EPS = 1e-6
LANES = 128
SUBLANES = 8
VMEM_LIMIT = 56 * 1024 * 1024
ROW_CHUNK = 128
X_RING = 3

SSM_GROUP = 16
SSM_STATE = 64
CONV_W = 3
HEAD_DIM = 128
N_KV_HEADS = 4
CMP_STRIDE = 16
CMP_BLOCK = 32
SEL_BLOCK = 64
SEL_SHIFT = SEL_BLOCK.bit_length() - 1
N_SEL = 16
WINDOW = 512
Q_BLOCK = 128
PAGE_SIZE = 128
NEG_BIG = -1e30
M_FLOOR = 0.1 * NEG_BIG
LOG2_E = math.log2(math.e)
ONES_ROWS = 16

S5_SET_GROUPS = 4
S5_SETS = 8
S5_BLK_GROUPS = S5_SET_GROUPS * S5_SETS
S5_BLK_CH = S5_BLK_GROUPS * SSM_GROUP
S5_SET_LANES = S5_SET_GROUPS * SSM_STATE


def _cparams(sem, vmem=VMEM_LIMIT):
    return pltpu.CompilerParams(dimension_semantics=sem, vmem_limit_bytes=vmem)


def _div(s, n):
    return jnp.right_shift(s, n.bit_length() - 1) if n & (n - 1) == 0 else s // n


def _mod(s, n):
    return jnp.bitwise_and(s, n - 1) if n & (n - 1) == 0 else s % n


def _cast_body(w_ref, o_ref):
    o_ref[...] = w_ref[...].astype(o_ref.dtype)


def cast_weight(w_all, layer, col0=0, ncols=None, tr=512):
    _, k, n = w_all.shape
    ncols = n - col0 if ncols is None else ncols
    tr = math.gcd(k, tr)
    tc = math.gcd(math.gcd(ncols, col0) if col0 else ncols, 4096)
    c0 = col0 // tc
    return pl.pallas_call(
        _cast_body,
        grid=(k // tr, ncols // tc),
        in_specs=[pl.BlockSpec((None, tr, tc), lambda i, j: (layer, i, c0 + j))],
        out_specs=pl.BlockSpec((tr, tc), lambda i, j: (i, j)),
        out_shape=jax.ShapeDtypeStruct((k, ncols), BF16),
        compiler_params=_cparams(("parallel", "parallel")),
        name="cast_weight",
    )(w_all.astype(F32))


def _rmsnorm_body(x_ref, g_ref, *o_refs):
    x = x_ref[...].astype(F32)
    y = x * lax.rsqrt(jnp.mean(x * x, axis=-1, keepdims=True) + RMS_EPS)
    for k, o_ref in enumerate(o_refs):
        o_ref[...] = (y * g_ref[k:k + 1, :]).astype(o_ref.dtype)


def rmsnorm(x, gains, out_dtype):
    m, d = x.shape
    tm = math.gcd(m, 512)
    n = len(gains)
    g = jnp.stack([gi.reshape(d).astype(F32) for gi in gains])
    outs = pl.pallas_call(
        _rmsnorm_body,
        grid=(m // tm,),
        in_specs=[pl.BlockSpec((tm, d), lambda i: (i, 0)), pl.BlockSpec((n, d), lambda i: (0, 0))],
        out_specs=[pl.BlockSpec((tm, d), lambda i: (i, 0))] * n,
        out_shape=[jax.ShapeDtypeStruct((m, d), out_dtype)] * n,
        compiler_params=_cparams(("parallel",)),
        name="rmsnorm",
    )(x, g)
    return outs


def _mm_body(*refs, has_res, act):
    if has_res:
        x_ref, w_ref, res_ref, o_ref = refs
    else:
        x_ref, w_ref, o_ref = refs
    tm = x_ref.shape[0]
    chunk = math.gcd(tm, ROW_CHUNK)
    for r in range(0, tm, chunk):
        y = jnp.dot(x_ref[r:r + chunk, :], w_ref[...], preferred_element_type=F32)
        if act == "sigmoid":
            y = jax.nn.sigmoid(y)
        if has_res:
            y = res_ref[r:r + chunk, :] + y
        o_ref[r:r + chunk, :] = y.astype(o_ref.dtype)


def matmul(x, w, res=None, out_dtype=F32, act=None, tm=512, tn=256):
    m, k = x.shape
    n = w.shape[1]
    tm = math.gcd(m, tm)
    tn = math.gcd(n, tn)
    in_specs = [pl.BlockSpec((tm, k), lambda i, j: (i, 0)), pl.BlockSpec((k, tn), lambda i, j: (0, j))]
    args = [x, w]
    if res is not None:
        in_specs.append(pl.BlockSpec((tm, tn), lambda i, j: (i, j)))
        args.append(res)
    return pl.pallas_call(
        functools.partial(_mm_body, has_res=res is not None, act=act),
        grid=(m // tm, n // tn),
        in_specs=in_specs,
        out_specs=pl.BlockSpec((tm, tn), lambda i, j: (i, j)),
        out_shape=jax.ShapeDtypeStruct((m, n), out_dtype),
        compiler_params=_cparams(("parallel", "arbitrary")),
        name="matmul",
    )(*args)


def _mm_split_body(x_ref, w_ref, o1_ref, o2_ref, *, n1_tiles):
    j = pl.program_id(1)
    y = jnp.dot(x_ref[...], w_ref[...], preferred_element_type=F32)

    @pl.when(j < n1_tiles)
    def _():
        o1_ref[...] = y

    @pl.when(j >= n1_tiles)
    def _():
        o2_ref[...] = y


def matmul_split(x, w, n1, tm=1024, tn=512):
    m, k = x.shape
    n = w.shape[1]
    tm = math.gcd(m, tm)
    tn = math.gcd(math.gcd(n1, n - n1), tn)
    t1 = n1 // tn
    return pl.pallas_call(
        functools.partial(_mm_split_body, n1_tiles=t1),
        grid=(m // tm, n // tn),
        in_specs=[pl.BlockSpec((tm, k), lambda i, j: (i, 0)), pl.BlockSpec((k, tn), lambda i, j: (0, j))],
        out_specs=[pl.BlockSpec((tm, tn), lambda i, j: (i, jnp.minimum(j, t1 - 1))),
                   pl.BlockSpec((tm, tn), lambda i, j: (i, jnp.maximum(j - t1, 0)))],
        out_shape=[jax.ShapeDtypeStruct((m, n1), F32), jax.ShapeDtypeStruct((m, n - n1), F32)],
        compiler_params=_cparams(("parallel", "arbitrary")),
        name="matmul_split",
    )(x, w)


def _glu_body(x_ref, wa_ref, wb_ref, res_ref, o_ref):
    tm = x_ref.shape[0]
    chunk = math.gcd(tm, ROW_CHUNK)
    for r in range(0, tm, chunk):
        x = x_ref[r:r + chunk, :]
        a = jnp.dot(x, wa_ref[...], preferred_element_type=F32)
        b = jnp.dot(x, wb_ref[...], preferred_element_type=F32)
        o_ref[r:r + chunk, :] = res_ref[r:r + chunk, :] + a * jax.nn.sigmoid(b)


def glu_matmul(x, w, res, tm=1024, tn=512):
    m, k = x.shape
    n = w.shape[1] // 2
    tm = math.gcd(m, tm)
    tn = math.gcd(n, tn)
    nt = n // tn
    return pl.pallas_call(
        _glu_body,
        grid=(m // tm, nt),
        in_specs=[pl.BlockSpec((tm, k), lambda i, j: (i, 0)),
                  pl.BlockSpec((k, tn), lambda i, j: (0, j)),
                  pl.BlockSpec((k, tn), lambda i, j: (0, j + nt)),
                  pl.BlockSpec((tm, tn), lambda i, j: (i, j))],
        out_specs=pl.BlockSpec((tm, tn), lambda i, j: (i, j)),
        out_shape=jax.ShapeDtypeStruct((m, n), F32),
        compiler_params=_cparams(("parallel", "arbitrary")),
        name="glu_matmul",
    )(x, w, w, res)


def _conv_gate(g, g1, g2, cw_ref, cb_ref, val):
    gc = cb_ref[...] + cw_ref[0:1, :] * g2 + cw_ref[1:2, :] * g1 + cw_ref[2:3, :] * g
    return (gc * jax.nn.sigmoid(gc) * val).astype(BF16)


def _ffn_up_body(x_hbm, xs_ref, wv_ref, wg_ref, cw_ref, cb_ref, p1_ref, p2_ref, cwp_ref, cbp_ref, init_ref, wd_ref,
                 act_ref, st_ref, acts_ref, gs_ref, wdb_ref, wv_s, wg_s, val_a, g_a, val_b, g_b, xbuf, xsem,
                 *, n_m, tiles_per_seq, short_len, chunk):
    s = pl.program_id(0)
    i = _mod(s, n_m)
    last = pl.num_programs(0) - 1
    tm = xbuf.shape[1]
    slot = lax.rem(s, X_RING)

    def x_copy(st):
        row0 = pl.multiple_of(_mod(st, n_m) * tm, tm)
        sl = lax.rem(st, X_RING)
        return pltpu.make_async_copy(x_hbm.at[pl.ds(row0, tm), :], xbuf.at[sl], xsem.at[sl])

    @pl.when(s == 0)
    def _():
        for st in range(X_RING - 1):
            x_copy(st).start()

    @pl.when(s + X_RING - 1 <= last)
    def _():
        x_copy(s + X_RING - 1).start()

    x_copy(s).wait()

    @pl.when(s == 0)
    def _():
        val_b[...] = jnp.zeros(val_b.shape, F32)
        g_b[...] = jnp.zeros(g_b.shape, F32)

    @pl.when(i == 0)
    def _():
        wv_s[...] = wv_ref[...].astype(BF16)
        wg_s[...] = wg_ref[...].astype(BF16)
        xs = xs_ref[...]
        val = jnp.dot(xs, wv_s[...], preferred_element_type=F32)
        g = jnp.dot(xs, wg_s[...], preferred_element_type=F32)
        tmod = lax.rem(lax.broadcasted_iota(I32, g.shape, 0), short_len)
        g1 = jnp.where(tmod >= 1, pltpu.roll(g, 1, axis=0), p1_ref[...])
        g2 = jnp.where(tmod >= 2, pltpu.roll(g, 2, axis=0), p2_ref[...])
        acts_ref[...] = _conv_gate(g, g1, g2, cw_ref, cb_ref, val)
        gs_ref[...] = g

    def step(val_w, g_w, val_r, g_r):
        wdb_ref[...] = wd_ref[...].astype(BF16)
        for r in range(0, tm, chunk):
            x = xbuf[slot, r:r + chunk, :]
            val_w[r:r + chunk, :] = jnp.dot(x, wv_s[...], preferred_element_type=F32)
            g_w[SUBLANES + r:SUBLANES + r + chunk, :] = jnp.dot(x, wg_s[...], preferred_element_type=F32)
            gext = g_r[r:r + chunk + SUBLANES, :]
            g1 = pltpu.roll(gext, 1, axis=0)[SUBLANES:]
            g2 = pltpu.roll(gext, 2, axis=0)[SUBLANES:]
            act_ref[r:r + chunk, :] = _conv_gate(gext[SUBLANES:], g1, g2, cwp_ref, cbp_ref, val_r[r:r + chunk, :])
        tail = g_r[tm:, :]
        st_ref[0] = tail
        g_w[:SUBLANES, :] = jnp.where(_mod(i, tiles_per_seq) == 0, init_ref[0], tail)

    parity = _mod(s, 2)

    @pl.when(parity == 0)
    def _():
        step(val_a, g_a, val_b, g_b)

    @pl.when(parity == 1)
    def _():
        step(val_b, g_b, val_a, g_a)


def ffn_up(h, hs, w_in_all, w_down_all, layer, conv_w, conv_b, buf, buf_s, seq_len, seq_short, tm=1024, tn=256):
    m, k = h.shape
    ms = hs.shape[0]
    f = w_in_all.shape[2] // 2
    bsz = m // seq_len
    bs = ms // seq_short
    assert seq_len % SUBLANES == 0 and seq_len >= 2 * SUBLANES and seq_short >= 2
    tn = math.gcd(f, tn)
    n_n = f // tn
    tm = math.gcd(seq_len, tm)
    tps = seq_len // tm
    n_m = m // tm
    chunk = math.gcd(tm, ROW_CHUNK)
    n_steps = n_n * n_m
    d_out = w_down_all.shape[2]
    rd = f // n_steps
    assert f % n_steps == 0 and rd % ONES_ROWS == 0
    cw = jnp.zeros((SUBLANES, f), F32).at[:CONV_W].set(conv_w.astype(F32))
    cb = conv_b.reshape(1, f).astype(F32)
    init = jnp.zeros((bsz, SUBLANES, f), F32).at[:, SUBLANES - 2:].set(buf.astype(F32))
    buf_s = buf_s.astype(F32)
    pos = jnp.arange(ms) % seq_short
    b0 = jnp.repeat(buf_s[:, 0], seq_short, axis=0)
    b1 = jnp.repeat(buf_s[:, 1], seq_short, axis=0)
    p1 = jnp.where((pos == 0)[:, None], b1, 0.0)
    p2 = jnp.where((pos == 0)[:, None], b0, jnp.where((pos == 1)[:, None], b1, 0.0))

    def col(s):
        return jnp.minimum(_div(s, n_m), n_n - 1)

    def prev(s):
        sp = jnp.maximum(s - 1, 0)
        return _mod(sp, n_m), _div(sp, n_m)

    act, st, act_s, g_s, w_down_b = pl.pallas_call(
        functools.partial(_ffn_up_body, n_m=n_m, tiles_per_seq=tps, short_len=seq_short, chunk=chunk),
        grid=(n_steps + 1,),
        in_specs=[pl.BlockSpec(memory_space=pl.ANY),
                  pl.BlockSpec((ms, k), lambda s: (0, 0)),
                  pl.BlockSpec((None, k, tn), lambda s: (layer, 0, col(s))),
                  pl.BlockSpec((None, k, tn), lambda s: (layer, 0, col(s) + n_n)),
                  pl.BlockSpec((SUBLANES, tn), lambda s: (0, col(s))),
                  pl.BlockSpec((1, tn), lambda s: (0, col(s))),
                  pl.BlockSpec((ms, tn), lambda s: (0, col(s))),
                  pl.BlockSpec((ms, tn), lambda s: (0, col(s))),
                  pl.BlockSpec((SUBLANES, tn), lambda s: (0, prev(s)[1])),
                  pl.BlockSpec((1, tn), lambda s: (0, prev(s)[1])),
                  pl.BlockSpec((1, SUBLANES, tn), lambda s: (_div(_mod(s, n_m), tps), 0, col(s))),
                  pl.BlockSpec((None, rd, d_out), lambda s: (layer, jnp.minimum(s, n_steps - 1), 0))],
        out_specs=[pl.BlockSpec((tm, tn), lambda s: prev(s)),
                   pl.BlockSpec((1, SUBLANES, tn), lambda s: (prev(s)[0], 0, prev(s)[1])),
                   pl.BlockSpec((ms, tn), lambda s: (0, col(s))),
                   pl.BlockSpec((ms, tn), lambda s: (0, col(s))),
                   pl.BlockSpec((rd, d_out), lambda s: (jnp.minimum(s, n_steps - 1), 0))],
        out_shape=[jax.ShapeDtypeStruct((m, f), BF16), jax.ShapeDtypeStruct((n_m, SUBLANES, f), F32),
                   jax.ShapeDtypeStruct((ms, f), BF16), jax.ShapeDtypeStruct((ms, f), F32),
                   jax.ShapeDtypeStruct((f, d_out), BF16)],
        scratch_shapes=[pltpu.VMEM((k, tn), BF16), pltpu.VMEM((k, tn), BF16)]
                       + [pltpu.VMEM((tm, tn), F32), pltpu.VMEM((tm + SUBLANES, tn), F32)] * 2
                       + [pltpu.VMEM((X_RING, tm, k), BF16), pltpu.SemaphoreType.DMA((X_RING,))],
        compiler_params=_cparams(("arbitrary",)),
        name="ffn_up",
    )(h, hs, w_in_all, w_in_all, cw, cb, p1, p2, cw, cb, init, w_down_all)
    return (act, st[tps - 1::tps, SUBLANES - 2:], act_s, g_s.reshape(bs, seq_short, f)[:, seq_short - 2:], w_down_b)


def _s5_body(*refs, tc_len, last_step, nb):
    (u_ref, wb_ref, wc_ref, lre_ref, lim_ref, d_ref, h0re_ref, h0im_ref, z_ref, hre_ref, him_ref) = refs[:11]
    halves = S5_SET_LANES // LANES
    n_buf = 2 * halves
    bufs = refs[11:11 + nb * n_buf]
    hst_re, hst_im = refs[11 + nb * n_buf:]
    tci = pl.program_id(2)

    def sre(b, c):
        return bufs[b * n_buf + c]

    def sim(b, c):
        return bufs[b * n_buf + halves + c]

    def set_rows(j):
        return pl.ds(j, tc_len, stride=S5_SETS)

    for b in range(nb):
        for ti in range(S5_BLK_CH // LANES):
            ut = u_ref[b, :, ti * LANES:(ti + 1) * LANES]
            hi = ut.astype(BF16)
            lo = (ut - hi.astype(F32)).astype(BF16)
            lhs = jnp.concatenate([hi, lo], axis=1)
            for jj in range(2):
                j = ti * 2 + jj
                bre = jnp.dot(lhs, wb_ref[0, j, 0], preferred_element_type=F32)
                bim = jnp.dot(lhs, wb_ref[0, j, 1], preferred_element_type=F32)
                for c in range(halves):
                    sre(b, c)[set_rows(j), :] = bre[:, c * LANES:(c + 1) * LANES]
                    sim(b, c)[set_rows(j), :] = bim[:, c * LANES:(c + 1) * LANES]

    @pl.when(tci == 0)
    def _():
        for b in range(nb):
            hst_re[b] = h0re_ref[b, 0]
            hst_im[b] = h0im_ref[b, 0]

    lam_r = lre_ref[0]
    lam_i = lim_ref[0]
    lr = [lam_r[:, c * LANES:(c + 1) * LANES] for c in range(halves)]
    li = [lam_i[:, c * LANES:(c + 1) * LANES] for c in range(halves)]

    def step(t, carry):
        r0 = pl.multiple_of(t * S5_SETS, S5_SETS)
        out = []
        for b in range(nb):
            for c in range(halves):
                hr, hi_ = carry[2 * (b * halves + c)], carry[2 * (b * halves + c) + 1]
                nr = lr[c] * hr - li[c] * hi_ + sre(b, c)[pl.ds(r0, S5_SETS), :]
                ni = lr[c] * hi_ + li[c] * hr + sim(b, c)[pl.ds(r0, S5_SETS), :]
                sre(b, c)[pl.ds(r0, S5_SETS), :] = nr
                sim(b, c)[pl.ds(r0, S5_SETS), :] = ni
                out += [nr, ni]
        return tuple(out)

    init = []
    for b in range(nb):
        h_in = hst_re[b]
        g_in = hst_im[b]
        for c in range(halves):
            init += [h_in[:, c * LANES:(c + 1) * LANES], g_in[:, c * LANES:(c + 1) * LANES]]
    fin = lax.fori_loop(0, tc_len, step, tuple(init), unroll=8)
    for b in range(nb):
        for c in range(halves):
            hst_re[b, :, c * LANES:(c + 1) * LANES] = fin[2 * (b * halves + c)]
            hst_im[b, :, c * LANES:(c + 1) * LANES] = fin[2 * (b * halves + c) + 1]

    @pl.when(tci == last_step // tc_len)
    def _():
        r0 = (last_step % tc_len) * S5_SETS
        for b in range(nb):
            for c in range(halves):
                hre_ref[b, 0, :, c * LANES:(c + 1) * LANES] = sre(b, c)[pl.ds(r0, S5_SETS), :]
                him_ref[b, 0, :, c * LANES:(c + 1) * LANES] = sim(b, c)[pl.ds(r0, S5_SETS), :]

    for b in range(nb):
        for c in range(S5_BLK_CH // LANES):
            parts = []
            for s in range(2):
                j = 2 * c + s
                parts += [sre(b, k)[set_rows(j), :].astype(BF16) for k in range(halves)]
                parts += [sim(b, k)[set_rows(j), :].astype(BF16) for k in range(halves)]
            lhs = jnp.concatenate(parts, axis=1)
            y = jnp.dot(lhs, wc_ref[0, c], preferred_element_type=F32)
            y = y + d_ref[:, c * LANES:(c + 1) * LANES] * u_ref[b, :, c * LANES:(c + 1) * LANES]
            z_ref[b, :, c * LANES:(c + 1) * LANES] = jax.nn.gelu(y).astype(BF16)


def _s5_weights(lam_re, lam_im, log_step, b_re, b_im, c_re, c_im):
    g, p = lam_re.shape
    n = SSM_GROUP
    nblk = g // S5_BLK_GROUPS
    lam = lax.complex(lam_re.astype(F32), lam_im.astype(F32))
    dt = jnp.exp(log_step.astype(F32))[:, None]
    lam_bar = jnp.exp(lam * dt)
    b_bar = ((lam_bar - 1.0) / lam)[..., None] * lax.complex(b_re.astype(F32), b_im.astype(F32))
    eye_q = jnp.eye(S5_SET_GROUPS, dtype=F32)
    eye_h = jnp.eye(2, dtype=F32)

    def b_operand(bm):
        t = bm.reshape(nblk, S5_SETS // 2, 2, S5_SET_GROUPS, p, n)
        w = jnp.einsum("bthqpn,hk,qr->bthkrnqp", t, eye_h, eye_q)
        w = w.reshape(nblk, S5_SETS, LANES, S5_SET_LANES)
        return w

    wb = jnp.stack([b_operand(jnp.real(b_bar)), b_operand(jnp.imag(b_bar))], axis=2)
    wb = wb.astype(BF16)
    wb = jnp.concatenate([wb, wb], axis=3)

    def c_operand(cm):
        t = cm.reshape(nblk, S5_SETS // 2, 2, S5_SET_GROUPS, n, p)
        return jnp.einsum("bcsqnp,st,qr->bcsqptrn", t, eye_h, eye_q)

    cre = c_operand(c_re.astype(F32))
    cim = c_operand(-c_im.astype(F32))
    wc = jnp.stack([cre, cim], axis=3).reshape(nblk, S5_SETS // 2, 4 * S5_SET_LANES, LANES).astype(BF16)
    lre = jnp.real(lam_bar).reshape(nblk, S5_SETS, S5_SET_LANES)
    lim = jnp.imag(lam_bar).reshape(nblk, S5_SETS, S5_SET_LANES)
    return wb, wc, lre, lim


def s5_scan(u, h0_re, h0_im, weights, d_skip, seq_valid, tc_len):
    wb, wc, lre, lim = weights
    bsz, lp, d = u.shape
    nblk = d // S5_BLK_CH
    n_tc = lp // tc_len
    h0_re = h0_re.astype(F32).reshape(bsz, nblk, S5_SETS, S5_SET_LANES)
    h0_im = h0_im.astype(F32).reshape(bsz, nblk, S5_SETS, S5_SET_LANES)
    nb = math.gcd(bsz, 2)
    state_spec = pl.BlockSpec((nb, 1, S5_SETS, S5_SET_LANES), lambda k, b, t: (b, k, 0, 0))
    z, hre, him = pl.pallas_call(
        functools.partial(_s5_body, tc_len=tc_len, last_step=seq_valid - 1, nb=nb),
        grid=(nblk, bsz // nb, n_tc),
        in_specs=[pl.BlockSpec((nb, tc_len, S5_BLK_CH), lambda k, b, t: (b, t, k)),
                  pl.BlockSpec((1, S5_SETS, 2, 2 * LANES, S5_SET_LANES), lambda k, b, t: (k, 0, 0, 0, 0)),
                  pl.BlockSpec((1, S5_SETS // 2, 4 * S5_SET_LANES, LANES), lambda k, b, t: (k, 0, 0, 0)),
                  pl.BlockSpec((1, S5_SETS, S5_SET_LANES), lambda k, b, t: (k, 0, 0)),
                  pl.BlockSpec((1, S5_SETS, S5_SET_LANES), lambda k, b, t: (k, 0, 0)),
                  pl.BlockSpec((1, S5_BLK_CH), lambda k, b, t: (0, k)),
                  state_spec, state_spec],
        out_specs=[pl.BlockSpec((nb, tc_len, S5_BLK_CH), lambda k, b, t: (b, t, k)), state_spec, state_spec],
        out_shape=[jax.ShapeDtypeStruct((bsz, lp, d), BF16),
                   jax.ShapeDtypeStruct((bsz, nblk, S5_SETS, S5_SET_LANES), F32),
                   jax.ShapeDtypeStruct((bsz, nblk, S5_SETS, S5_SET_LANES), F32)],
        scratch_shapes=[pltpu.VMEM((tc_len * S5_SETS, LANES), F32)] * (nb * 2 * S5_SET_LANES // LANES) + [
                        pltpu.VMEM((nb, S5_SETS, S5_SET_LANES), F32),
                        pltpu.VMEM((nb, S5_SETS, S5_SET_LANES), F32)],
        compiler_params=_cparams(("arbitrary", "arbitrary", "arbitrary")),
        name="s5_scan",
    )(u, wb, wc, lre, lim, d_skip.reshape(1, d).astype(F32), h0_re, h0_im)
    g = d // SSM_GROUP
    return z, hre.reshape(bsz, g, SSM_STATE), him.reshape(bsz, g, SSM_STATE)


def _cmp_proj_body(pt_ref, x_hbm, w_ref, o_ref, stage, sems, *, pages_per_step):
    n_heads2 = 2 * N_KV_HEADS
    chunks = PAGE_SIZE // CMP_STRIDE
    step = pl.program_id(0) * pl.num_programs(1) + pl.program_id(1)
    n_steps = pl.num_programs(0) * pl.num_programs(1)
    slot = lax.rem(step, 2)

    def page_copy(st, sl, p, c):
        page = pt_ref[st * pages_per_step + p]
        return pltpu.make_async_copy(x_hbm.at[page, :, c, :], stage.at[sl, p, c], sems.at[sl])

    def start_step(st, sl):
        for p in range(pages_per_step):
            for c in range(n_heads2):
                page_copy(st, sl, p, c).start()

    @pl.when(step == 0)
    def _():
        start_step(step, slot)

    @pl.when(step + 1 < n_steps)
    def _():
        start_step(step + 1, 1 - slot)

    for p in range(pages_per_step):
        for c in range(n_heads2):
            page_copy(step, slot, p, c).wait()

    def chunk_rows(c, j):
        return jnp.concatenate([stage[slot, p, c, pl.ds(j, chunks, stride=CMP_STRIDE), :]
                                for p in range(pages_per_step)], axis=0)

    for c in range(n_heads2):
        s, h = divmod(c, N_KV_HEADS)
        acc = None
        for jp in range(CMP_STRIDE // 2):
            lhs = jnp.concatenate([chunk_rows(c, 2 * jp), chunk_rows(c, 2 * jp + 1)], axis=1).astype(BF16)
            t = jnp.dot(lhs, w_ref[s, jp], preferred_element_type=F32)
            acc = t if acc is None else acc + t
        o_ref[0, s, h] = acc


def cmp_project(rows, page_table, w1cat, pages_per_step=16):
    bsz, npg = page_table.shape
    pps = math.gcd(npg, pages_per_step)
    chunks = PAGE_SIZE // CMP_STRIDE
    hid2 = w1cat.shape[-1]
    grid_spec = pltpu.PrefetchScalarGridSpec(
        num_scalar_prefetch=1,
        grid=(bsz, npg // pps),
        in_specs=[pl.BlockSpec(memory_space=pl.ANY),
                  pl.BlockSpec(w1cat.shape, lambda b, g, pt: (0, 0, 0, 0))],
        out_specs=pl.BlockSpec((1, 2, N_KV_HEADS, pps * chunks, hid2), lambda b, g, pt: (b, 0, 0, g, 0)),
        scratch_shapes=[pltpu.VMEM((2, pps, 2 * N_KV_HEADS, PAGE_SIZE, HEAD_DIM), F32),
                        pltpu.SemaphoreType.DMA((2,))],
    )
    return pl.pallas_call(
        functools.partial(_cmp_proj_body, pages_per_step=pps),
        grid_spec=grid_spec,
        out_shape=jax.ShapeDtypeStruct((bsz, 2, N_KV_HEADS, npg * chunks, hid2), F32),
        compiler_params=_cparams(("arbitrary", "arbitrary")),
        name="cmp_project",
    )(page_table.reshape(-1).astype(I32), rows, w1cat)


def _cmp_finish_body(ab_ref, pe_ref, w1_ref, b1_ref, w2_ref, b2_ref, o_ref):
    ab = ab_ref[0, 0, 0]
    n16 = ab.shape[0]
    hid = ab.shape[1] // 2
    a = ab[:, :hid]
    bnext = pltpu.roll(ab[:, hid:], n16 - 1, axis=0)
    c = jnp.dot(pe_ref[0], w1_ref[0], preferred_element_type=F32)[0:1, :] + b1_ref[0]
    pre = a + bnext + c
    y = jnp.dot(jax.nn.gelu(pre).astype(BF16), w2_ref[0], preferred_element_type=F32) + b2_ref[0]
    row = lax.broadcasted_iota(I32, y.shape, 0)
    o_ref[0, 0, 0] = jnp.where(row < n16 - 1, y, 0.0).astype(o_ref.dtype)


def cmp_finish(ab, pe, w1, b1, w2, b2):
    bsz, _, kvh, n16, hid2 = ab.shape
    hid = hid2 // 2
    pe_rows = jnp.broadcast_to(pe.reshape(2, 1, -1), (2, SUBLANES, pe.shape[1] * pe.shape[2])).astype(BF16)
    return pl.pallas_call(
        _cmp_finish_body,
        grid=(2, bsz, kvh),
        in_specs=[pl.BlockSpec((1, 1, 1, n16, hid2), lambda s, b, h: (b, s, h, 0, 0)),
                  pl.BlockSpec((1, SUBLANES, pe_rows.shape[2]), lambda s, b, h: (s, 0, 0)),
                  pl.BlockSpec((1,) + w1.shape[1:], lambda s, b, h: (s, 0, 0)),
                  pl.BlockSpec((1, 1, hid), lambda s, b, h: (s, 0, 0)),
                  pl.BlockSpec((1, hid, HEAD_DIM), lambda s, b, h: (s, 0, 0)),
                  pl.BlockSpec((1, 1, HEAD_DIM), lambda s, b, h: (s, 0, 0))],
        out_specs=pl.BlockSpec((1, 1, 1, n16, HEAD_DIM), lambda s, b, h: (b, s, h, 0, 0)),
        out_shape=jax.ShapeDtypeStruct((bsz, 2, kvh, n16, HEAD_DIM), BF16),
        compiler_params=_cparams(("arbitrary", "arbitrary", "arbitrary")),
        name="cmp_finish",
    )(ab, pe_rows, w1.astype(BF16), b1.reshape(2, 1, hid).astype(F32), w2.astype(BF16),
      b2.reshape(2, 1, HEAD_DIM).astype(F32))


def _masked_softmax(s, mask):
    sm = jnp.where(mask, s, NEG_BIG)
    m = jnp.max(sm, axis=-1, keepdims=True)
    m = jnp.where(m > 0.5 * NEG_BIG, m, 0.0)
    e = jnp.where(mask, jnp.exp(sm - m), 0.0)
    den = jnp.sum(e, axis=-1, keepdims=True)
    return e / jnp.where(den > 0, den, 1.0)


def _dot_nt(a, b):
    return lax.dot_general(a, b, (((1,), (1,)), ((), ())), preferred_element_type=F32)


def _overlap(n_c, n_j, c_axis):
    shape = (n_c, n_j) if c_axis == 0 else (n_j, n_c)
    c = lax.broadcasted_iota(I32, shape, c_axis) * CMP_STRIDE
    j = lax.broadcasted_iota(I32, shape, 1 - c_axis) * SEL_BLOCK
    return ((c < j + SEL_BLOCK) & (c + CMP_BLOCK > j)).astype(F32)


def _block_scores(imp, q_pos, n_blk):
    blk = lax.broadcasted_iota(I32, imp.shape, 1)
    q_blk = jnp.right_shift(q_pos, SEL_SHIFT)
    forced = (blk == 0) | (blk == q_blk) | (blk == q_blk - 1)
    future = blk * SEL_BLOCK > q_pos
    v = jnp.where(future, NEG_BIG, jnp.where(forced, -NEG_BIG, imp))
    return jnp.where(blk < n_blk, v, 2.0 * NEG_BIG)


def _rank_desc(v, n):
    lane = lax.broadcasted_iota(I32, v.shape, 1)
    rank = jnp.zeros(v.shape, F32)
    for i in range(n):
        vi = v[:, i:i + 1]
        before = (vi > v) | ((vi == v) & (lane > i))
        rank = rank + jnp.where(before, 1.0, 0.0)
    return rank


def _nsa_prompt_body(q_ref, gt_ref, ck_ref, cvt_ref, ks_ref, vs_ref, kw_ref, vw_ref, o_ref, m_ref, acc_ref,
                     vst_ref, vwt_ref, *, n_cmp, n_blk, gqa, sel_tile, win_keys):
    qb = pl.program_id(2)
    nq = Q_BLOCK
    gq = gqa * nq
    c2 = (HEAD_DIM ** -0.5) * LOG2_E

    @pl.when(qb == 0)
    def _():
        ones = jnp.ones((ONES_ROWS, LANES), BF16)
        for src, dst in ((vs_ref, vst_ref), (vw_ref, vwt_ref)):
            for t in range(dst.shape[0]):
                dst[t, :HEAD_DIM, :] = src[0, t * LANES:(t + 1) * LANES, :].T.astype(BF16)
                dst[t, HEAD_DIM:, :] = ones

    qt = jnp.concatenate([q_ref[0, :, g * HEAD_DIM:(g + 1) * HEAD_DIM].astype(F32).T.astype(BF16)
                          for g in range(gqa)], axis=1)
    q_pos = qb * nq + lax.broadcasted_iota(I32, (1, nq), 1)

    def per_head(x):
        return jnp.concatenate([x] * gqa, axis=1)

    def normalise(acc):
        den = acc[HEAD_DIM:HEAD_DIM + 1, :]
        return acc[:HEAD_DIM, :] * (1.0 / jnp.where(den > 0, den, 1.0))

    ck = ck_ref[0, 0, 0]
    n16 = ck.shape[0]
    s = jnp.dot(ck, qt, preferred_element_type=F32)
    cidx = lax.broadcasted_iota(I32, (n16, 1), 0)
    cbias = jnp.where((cidx * CMP_STRIDE + (CMP_BLOCK - 1) <= q_pos) & (cidx < n_cmp), 0.0, NEG_BIG)
    sb = s + per_head(cbias)
    m = jnp.maximum(jnp.max(sb, axis=0, keepdims=True), M_FLOOR)
    e = jnp.exp2((sb - m) * c2)
    den = jnp.sum(e, axis=0, keepdims=True)
    p = e * jnp.where(den > 0, 1.0 / den, 0.0)
    o_cmp = jnp.dot(cvt_ref[0, 0], p.astype(BF16), preferred_element_type=F32)

    psum = p[:, 0:nq]
    for g in range(1, gqa):
        psum = psum + p[:, g * nq:(g + 1) * nq]
    nb_rows = -(-n_blk // ONES_ROWS) * ONES_ROWS
    imp = jnp.dot(_overlap(n16, nb_rows, 1), psum, preferred_element_type=F32, precision=lax.Precision.HIGHEST)
    blk = lax.broadcasted_iota(I32, (nb_rows, 1), 0)
    q_blk = jnp.right_shift(q_pos, SEL_SHIFT)
    forced = (blk == 0) | (blk == q_blk) | (blk == q_blk - 1)
    v = jnp.where(blk * SEL_BLOCK > q_pos, NEG_BIG, jnp.where(forced, -NEG_BIG, imp))
    v = jnp.where(blk < n_blk, v, 2.0 * NEG_BIG)
    rank = jnp.zeros(v.shape, F32)
    for i in range(n_blk):
        vi = v[i:i + 1, :]
        rank = rank + jnp.where((vi > v) | ((vi == v) & (blk > i)), 1.0, 0.0)
    sel_bias = jnp.where((rank < min(N_SEL, n_blk)) & (blk < n_blk), 0.0, NEG_BIG).astype(BF16)
    if nb_rows < LANES:
        sel_bias = jnp.concatenate([sel_bias, jnp.zeros((LANES - nb_rows, nq), BF16)], axis=0)

    m_ref[...] = jnp.full(m_ref.shape, M_FLOOR, F32)
    acc_ref[...] = jnp.zeros(acc_ref.shape, F32)
    tiles = sel_tile // LANES

    def sel_step(kt, _):
        k0 = pl.multiple_of(kt * sel_tile, sel_tile)
        s = jnp.dot(ks_ref[0, pl.ds(k0, sel_tile), :].astype(BF16), qt, preferred_element_type=F32)
        key = k0 + lax.broadcasted_iota(I32, (sel_tile, 1), 0)
        expand = jnp.where(jnp.right_shift(key, SEL_SHIFT) == lax.broadcasted_iota(I32, (sel_tile, LANES), 1), 1.0, 0.0)
        bias = jnp.dot(expand.astype(BF16), sel_bias, preferred_element_type=F32)
        bias = jnp.where(key <= q_pos, bias, NEG_BIG)
        sb = s + per_head(bias)
        m_old = m_ref[0:1, :]
        m_new = jnp.maximum(m_old, jnp.max(sb, axis=0, keepdims=True))
        alpha = jnp.exp2((m_old - m_new) * c2)
        e = jnp.exp2((sb - m_new) * c2).astype(BF16)
        vt = jnp.concatenate([vst_ref[kt * tiles + i] for i in range(tiles)], axis=1)
        acc_ref[...] = alpha * acc_ref[...] + jnp.dot(vt, e, preferred_element_type=F32)
        m_ref[...] = jnp.broadcast_to(m_new, m_ref.shape)
        return 0

    lax.fori_loop(0, (qb * nq + nq + sel_tile - 1) // sel_tile, sel_step, 0)
    o_sel = normalise(acc_ref[...])

    w_tile = jnp.maximum(qb + 1 - win_keys // nq, 0)
    w0 = pl.multiple_of(w_tile * nq, nq)
    s = jnp.dot(kw_ref[0, pl.ds(w0, win_keys), :].astype(BF16), qt, preferred_element_type=F32)
    key = w0 + lax.broadcasted_iota(I32, (win_keys, 1), 0)
    wbias = jnp.where((key <= q_pos) & (key > q_pos - WINDOW), 0.0, NEG_BIG)
    sb = s + per_head(wbias)
    m = jnp.maximum(jnp.max(sb, axis=0, keepdims=True), M_FLOOR)
    e = jnp.exp2((sb - m) * c2).astype(BF16)
    vt = jnp.concatenate([vwt_ref[w_tile + i] for i in range(win_keys // LANES)], axis=1)
    o_win = normalise(jnp.dot(vt, e, preferred_element_type=F32))

    gt = gt_ref[0, 0, 0]
    for g in range(gqa):
        cols = slice(g * nq, (g + 1) * nq)
        o = (gt[3 * g:3 * g + 1, :] * o_cmp[:, cols] + gt[3 * g + 1:3 * g + 2, :] * o_sel[:, cols]
             + gt[3 * g + 2:3 * g + 3, :] * o_win[:, cols])
        o_ref[0, :, g * HEAD_DIM:(g + 1) * HEAD_DIM] = o.T.astype(o_ref.dtype)


def nsa_prompt_attend(q, gates, cmp, k_sel, v_sel, k_win, v_win, bsz, t, sel_tile=512):
    hd_all = q.shape[1]
    kvh = N_KV_HEADS
    gqa = hd_all // HEAD_DIM // kvh
    n16 = cmp.shape[3]
    n_blk = -(-t // SEL_BLOCK)
    nqb = t // Q_BLOCK
    gq = gqa * Q_BLOCK
    vrows = HEAD_DIM + ONES_ROWS
    assert t % Q_BLOCK == 0 and n_blk <= LANES and n16 * CMP_STRIDE == t
    sel_tile = math.gcd(t, sel_tile)
    win_keys = min(WINDOW + Q_BLOCK, t)
    gt = gates[:, :3 * kvh * gqa].reshape(bsz, nqb, Q_BLOCK, kvh, 3 * gqa).transpose(0, 3, 1, 4, 2)
    gt = jnp.pad(gt, ((0, 0), (0, 0), (0, 0), (0, -3 * gqa % SUBLANES), (0, 0)))
    cvt = cmp[:, 1].transpose(0, 1, 3, 2)
    (ks_arr, ks_col), (vs_arr, vs_col), (kw_arr, kw_col), (vw_arr, vw_col) = k_sel, v_sel, k_win, v_win
    k_spec = lambda col: pl.BlockSpec((1, t, HEAD_DIM), lambda b, h, i, col=col: (b, 0, col + h))
    return pl.pallas_call(
        functools.partial(_nsa_prompt_body, n_cmp=n16 - 1, n_blk=n_blk, gqa=gqa, sel_tile=sel_tile, win_keys=win_keys),
        grid=(bsz, kvh, nqb),
        in_specs=[pl.BlockSpec((1, Q_BLOCK, gqa * HEAD_DIM), lambda b, h, i: (b, i, h)),
                  pl.BlockSpec((1, 1, 1, gt.shape[3], Q_BLOCK), lambda b, h, i: (b, h, i, 0, 0)),
                  pl.BlockSpec((1, 1, 1, n16, HEAD_DIM), lambda b, h, i: (b, 0, h, 0, 0)),
                  pl.BlockSpec((1, 1, HEAD_DIM, n16), lambda b, h, i: (b, h, 0, 0)),
                  k_spec(ks_col), k_spec(vs_col), k_spec(kw_col), k_spec(vw_col)],
        out_specs=pl.BlockSpec((1, Q_BLOCK, gqa * HEAD_DIM), lambda b, h, i: (b, i, h)),
        out_shape=jax.ShapeDtypeStruct((bsz, t, hd_all), BF16),
        scratch_shapes=[pltpu.VMEM((SUBLANES, gq), F32), pltpu.VMEM((vrows, gq), F32),
                        pltpu.VMEM((t // LANES, vrows, LANES), BF16), pltpu.VMEM((t // LANES, vrows, LANES), BF16)],
        compiler_params=_cparams(("parallel", "parallel", "arbitrary")),
        name="nsa_prompt",
    )(q.reshape(bsz, t, hd_all), gt, cmp, cvt, ks_arr.reshape(bsz, t, -1), vs_arr.reshape(bsz, t, -1),
      kw_arr.reshape(bsz, t, -1), vw_arr.reshape(bsz, t, -1)).reshape(bsz * t, hd_all)


def _nsa_sample_select_body(q_ref, ck_ref, cv_ref, ocmp_ref, idx_ref, *, n_cmp, n_blk, blk_lanes, past_len, tq_pad, gqa):
    scale = HEAD_DIM ** -0.5
    q = jnp.concatenate([q_ref[0, :, g * HEAD_DIM:(g + 1) * HEAD_DIM] for g in range(gqa)], axis=0)
    ck = ck_ref[0, 0, 0]
    cv = cv_ref[0, 0, 0]
    n16 = ck.shape[0]
    q_pos = past_len + lax.broadcasted_iota(I32, (tq_pad, 1), 0)
    s = (_dot_nt(q, ck) * scale).reshape(gqa, tq_pad, n16)
    cidx = lax.broadcasted_iota(I32, (tq_pad, n16), 1)
    cmask = (cidx * CMP_STRIDE + (CMP_BLOCK - 1) <= q_pos) & (cidx < n_cmp)
    p = _masked_softmax(s, cmask[None])
    ocmp_ref[0, 0] = jnp.dot(p.reshape(gqa * tq_pad, n16).astype(BF16), cv, preferred_element_type=F32)
    psum = jnp.sum(p, axis=0)
    imp = jnp.dot(psum, _overlap(n16, blk_lanes, 0), preferred_element_type=F32, precision=lax.Precision.HIGHEST)
    rank = _rank_desc(_block_scores(imp, q_pos, n_blk), n_blk)
    lane = lax.broadcasted_iota(I32, rank.shape, 1)
    lane_f = lane.astype(F32)
    out_lane = lax.broadcasted_iota(I32, (tq_pad, LANES), 1)
    idx = jnp.zeros((tq_pad, LANES), F32)
    for k in range(min(N_SEL, n_blk)):
        hit = (rank == float(k)) & (lane < n_blk)
        idx_k = jnp.sum(jnp.where(hit, lane_f, 0.0), axis=-1, keepdims=True)
        idx = jnp.where(out_lane == k, idx_k, idx)
    idx_ref[0, 0] = idx.astype(I32)


def _nsa_sample_attend_body(idx_ref, pt_ref, q_ref, gt_ref, ocmp_ref, kwc_ref, vwc_ref,
                            kwn_ref, vwn_ref, cache_ref, newblk_ref, o_ref, kbuf, vbuf, kwbuf, vwbuf, sems,
                            *, n_sel, n_cache_blk, n_pages, past_len, win_buf, tq, tq_pad, gqa):
    b = pl.program_id(0)
    h = pl.program_id(1)
    scale = HEAD_DIM ** -0.5
    blk_per_page = PAGE_SIZE // SEL_BLOCK

    bufs = ((2, kbuf, 0), (3, vbuf, 1))
    step = b * N_KV_HEADS + h
    n_steps = pl.num_programs(0) * N_KV_HEADS
    cur = _mod(step, 2)

    def from_new(bb, hh, sl, t, k, slot, buf, sem):
        return pltpu.make_async_copy(newblk_ref.at[bb, :, slot * N_KV_HEADS + hh], buf.at[sl, t, k], sems.at[sl, sem])

    def start_gather(st, sl):
        bb = _div(st, N_KV_HEADS)
        hh = _mod(st, N_KV_HEADS)
        for t in range(tq):
            for k in range(n_sel):
                blk = idx_ref[(st * tq + t) * n_sel + k]
                in_cache = blk < n_cache_blk
                blk_c = jnp.minimum(blk, n_cache_blk - 1)
                page = pt_ref[bb * n_pages + _div(blk_c, blk_per_page)]
                off = pl.multiple_of(_mod(blk_c, blk_per_page) * SEL_BLOCK, SEL_BLOCK)
                for slot, buf, sem in bufs:
                    @pl.when(in_cache)
                    def _():
                        pltpu.make_async_copy(cache_ref.at[page, pl.ds(off, SEL_BLOCK), slot * N_KV_HEADS + hh],
                                              buf.at[sl, t, k], sems.at[sl, sem]).start()

                    @pl.when(jnp.logical_not(in_cache))
                    def _():
                        from_new(bb, hh, sl, t, k, slot, buf, sem).start()

    @pl.when(step == 0)
    def _():
        start_gather(step, cur)

    @pl.when(step + 1 < n_steps)
    def _():
        start_gather(step + 1, 1 - cur)

    q = jnp.concatenate([q_ref[0, :, g * HEAD_DIM:(g + 1) * HEAD_DIM] for g in range(gqa)], axis=0)
    rows = gqa * tq_pad
    row_t = lax.rem(lax.broadcasted_iota(I32, (rows, 1), 0), tq_pad)
    q_pos = past_len + row_t

    kwbuf[...] = jnp.zeros(kwbuf.shape, F32)
    vwbuf[...] = jnp.zeros(vwbuf.shape, F32)
    kwbuf[0:win_buf, :] = kwc_ref[0]
    vwbuf[0:win_buf, :] = vwc_ref[0]
    kwbuf[win_buf:win_buf + tq_pad, :] = kwn_ref[0]
    vwbuf[win_buf:win_buf + tq_pad, :] = vwn_ref[0]
    nw = kwbuf.shape[0]
    wlane = lax.broadcasted_iota(I32, (rows, nw), 1)
    wpos = past_len - win_buf + wlane
    wmask = (wpos <= q_pos) & (wpos > q_pos - WINDOW) & (wpos >= 0) & (wlane < win_buf + tq)
    s = _dot_nt(q, kwbuf[...].astype(BF16)) * scale
    p = _masked_softmax(s, wmask)
    o_win = jnp.dot(p.astype(BF16), vwbuf[...].astype(BF16), preferred_element_type=F32)

    for t in range(tq):
        for k in range(n_sel):
            for slot, buf, sem in bufs:
                from_new(b, h, cur, t, k, slot, buf, sem).wait()

    o_sel = jnp.zeros((rows, HEAD_DIM), F32)
    nk = n_sel * SEL_BLOCK
    klane = lax.broadcasted_iota(I32, (rows, nk), 1)
    for t in range(tq):
        kpos = jnp.zeros((rows, nk), I32)
        for k in range(n_sel):
            blk = idx_ref[((b * N_KV_HEADS + h) * tq + t) * n_sel + k]
            kpos = jnp.where(jnp.right_shift(klane, SEL_SHIFT) == k,
                             blk * SEL_BLOCK + jnp.bitwise_and(klane, SEL_BLOCK - 1), kpos)
        mask = (kpos <= q_pos) & (row_t == t)
        kt = kbuf[cur, t].reshape(nk, HEAD_DIM).astype(BF16)
        vt = vbuf[cur, t].reshape(nk, HEAD_DIM).astype(BF16)
        p = _masked_softmax(_dot_nt(q, kt) * scale, mask)
        o_sel = o_sel + jnp.dot(p.astype(BF16), vt, preferred_element_type=F32)

    o_cmp = ocmp_ref[0, 0]
    gt = gt_ref[0, 0]
    for g in range(gqa):
        r = slice(g * tq_pad, (g + 1) * tq_pad)
        o = (gt[:, 3 * g:3 * g + 1] * o_cmp[r] + gt[:, 3 * g + 1:3 * g + 2] * o_sel[r]
             + gt[:, 3 * g + 2:3 * g + 3] * o_win[r])
        o_ref[0, :, g * HEAD_DIM:(g + 1) * HEAD_DIM] = o.astype(o_ref.dtype)


def nsa_sample_attend(q, gates, cmp, new_rows, win_new, cache_kv, cache_win, page_table, tq):
    bsz, tq_pad, hd_all = q.shape
    kvh = N_KV_HEADS
    gqa = hd_all // HEAD_DIM // kvh
    n_pages = page_table.shape[1]
    past_len = n_pages * PAGE_SIZE
    n16 = cmp.shape[3]
    n_cmp = n16 - 1
    n_cache_blk = past_len // SEL_BLOCK
    n_blk = -(-(past_len + tq) // SEL_BLOCK)
    assert n_blk == n_cache_blk + 1 and past_len % SEL_BLOCK == 0 and tq <= SEL_BLOCK
    n_sel = min(N_SEL, n_blk)
    blk_lanes = -(-n_blk // LANES) * LANES
    win_buf = cache_win.shape[1]
    cmp_spec = lambda s: pl.BlockSpec((1, 1, 1, n16, HEAD_DIM), lambda b, h, s=s: (b, s, h, 0, 0))
    ocmp, idx = pl.pallas_call(
        functools.partial(_nsa_sample_select_body, n_cmp=n_cmp, n_blk=n_blk, blk_lanes=blk_lanes,
                          past_len=past_len, tq_pad=tq_pad, gqa=gqa),
        grid=(bsz, kvh),
        in_specs=[pl.BlockSpec((1, tq_pad, gqa * HEAD_DIM), lambda b, h: (b, 0, h)), cmp_spec(0), cmp_spec(1)],
        out_specs=[pl.BlockSpec((1, 1, gqa * tq_pad, HEAD_DIM), lambda b, h: (b, h, 0, 0)),
                   pl.BlockSpec((1, 1, tq_pad, LANES), lambda b, h: (b, h, 0, 0))],
        out_shape=[jax.ShapeDtypeStruct((bsz, kvh, gqa * tq_pad, HEAD_DIM), F32),
                   jax.ShapeDtypeStruct((bsz, kvh, tq_pad, LANES), I32)],
        compiler_params=_cparams(("parallel", "parallel")),
        name="nsa_sample_select",
    )(q, cmp, cmp)
    idx_flat = idx[:, :, :tq, :n_sel].reshape(-1)

    n_slots = cache_kv.shape[2]
    cache4 = cache_kv.reshape(cache_kv.shape[0], PAGE_SIZE, n_slots * kvh, HEAD_DIM)
    newblk = jnp.zeros((bsz, SEL_BLOCK, n_slots * kvh, HEAD_DIM), F32).at[:, :tq].set(new_rows)
    cwin = cache_win.astype(F32).reshape(bsz, win_buf, 2 * kvh * HEAD_DIM)
    nw = -(-(win_buf + tq_pad) // LANES) * LANES
    kv_spec = lambda slot: pl.BlockSpec((1, tq_pad, HEAD_DIM), lambda b, h, *_, slot=slot: (b, 0, slot * kvh + h))
    cw_spec = lambda slot: pl.BlockSpec((1, win_buf, HEAD_DIM), lambda b, h, *_, slot=slot: (b, 0, slot * kvh + h))
    grid_spec = pltpu.PrefetchScalarGridSpec(
        num_scalar_prefetch=2,
        grid=(bsz, kvh),
        in_specs=[pl.BlockSpec((1, tq_pad, gqa * HEAD_DIM), lambda b, h, *_: (b, 0, h)),
                  pl.BlockSpec((1, 1, tq_pad, LANES), lambda b, h, *_: (b, h, 0, 0)),
                  pl.BlockSpec((1, 1, gqa * tq_pad, HEAD_DIM), lambda b, h, *_: (b, h, 0, 0)),
                  cw_spec(0), cw_spec(1), kv_spec(0), kv_spec(1),
                  pl.BlockSpec(memory_space=pl.ANY), pl.BlockSpec(memory_space=pl.ANY)],
        out_specs=pl.BlockSpec((1, tq_pad, gqa * HEAD_DIM), lambda b, h, *_: (b, 0, h)),
        scratch_shapes=[pltpu.VMEM((2, tq, n_sel, SEL_BLOCK, HEAD_DIM), F32),
                        pltpu.VMEM((2, tq, n_sel, SEL_BLOCK, HEAD_DIM), F32),
                        pltpu.VMEM((nw, HEAD_DIM), F32), pltpu.VMEM((nw, HEAD_DIM), F32),
                        pltpu.SemaphoreType.DMA((2, 2))],
    )
    return pl.pallas_call(
        functools.partial(_nsa_sample_attend_body, n_sel=n_sel, n_cache_blk=n_cache_blk, n_pages=n_pages,
                          past_len=past_len, win_buf=win_buf, tq=tq, tq_pad=tq_pad, gqa=gqa),
        grid_spec=grid_spec,
        out_shape=jax.ShapeDtypeStruct((bsz, tq_pad, hd_all), BF16),
        compiler_params=_cparams(("arbitrary", "arbitrary")),
        name="nsa_sample_attend",
    )(idx_flat, page_table.reshape(-1).astype(I32), q, gates, ocmp, cwin, cwin, win_new, win_new, cache4, newblk)


def _pad_rows(x, rows):
    return jnp.pad(x, ((0, 0), (0, rows - x.shape[1]), (0, 0)))


@jax.jit
def _step(x_prompt, x_sample, state_ssm_re, state_ssm_im, state_ffn_conv, cache_kv, cache_win, page_table,
          attn_norm, ffn_norm, final_norm, ssm_lam_re, ssm_lam_im, ssm_log_step, ssm_b_re, ssm_b_im,
          ssm_c_re, ssm_c_im, ssm_d, ssm_w_glu, ffn_w_in, ffn_conv_w, ffn_conv_b, ffn_w_down,
          kv_norm, w_kv, cmp_w1, cmp_b1, cmp_w2, cmp_b2, cmp_pe, w_qg, w_o):
    bp, tp, d = x_prompt.shape
    bs, ts, _ = x_sample.shape
    depth = attn_norm.shape[0]
    n_a = ssm_lam_re.shape[0]
    kvh = N_KV_HEADS
    hd_all = w_o.shape[1]
    n_heads = hd_all // HEAD_DIM
    gqa = n_heads // kvh
    ts_pad = -(-ts // SUBLANES) * SUBLANES
    n_pages = page_table.shape[1]

    xp = x_prompt.reshape(bp * tp, d).astype(F32)
    xs = x_sample.reshape(bs * ts, d).astype(F32)

    w1cat = jnp.concatenate([cmp_w1[:, :CMP_STRIDE * HEAD_DIM], cmp_w1[:, CMP_STRIDE * HEAD_DIM:]], axis=-1)
    w1cat = w1cat.reshape(2, CMP_STRIDE // 2, 2 * HEAD_DIM, w1cat.shape[-1]).astype(BF16)

    ssm_re_p, ssm_im_p, ssm_re_s, ssm_im_s, conv_p, conv_s = [], [], [], [], [], []
    outs = {}
    for layer in range(depth):
        if layer < n_a:
            a = layer
            wts = _s5_weights(ssm_lam_re[a], ssm_lam_im[a], ssm_log_step[a], ssm_b_re[a], ssm_b_im[a],
                              ssm_c_re[a], ssm_c_im[a])
            w_glu = cast_weight(ssm_w_glu, a)
            up = rmsnorm(xp, [attn_norm[layer]], F32)[0].reshape(bp, tp, d)
            zeros = jnp.zeros((bp, d // SSM_GROUP, SSM_STATE), F32)
            zp, hre, him = s5_scan(up, zeros, zeros, wts, ssm_d[a], tp, math.gcd(tp, 512))
            ssm_re_p.append(hre)
            ssm_im_p.append(him)
            xp = glu_matmul(zp.reshape(bp * tp, d), w_glu, xp)
            us = _pad_rows(rmsnorm(xs, [attn_norm[layer]], F32)[0].reshape(bs, ts, d), ts_pad)
            zs, hre, him = s5_scan(us, state_ssm_re[a], state_ssm_im[a], wts, ssm_d[a], ts, ts_pad)
            ssm_re_s.append(hre)
            ssm_im_s.append(him)
            xs = glu_matmul(zs[:, :ts].reshape(bs * ts, d), w_glu, xs)
        else:
            bl = layer - n_a
            if layer == n_a:
                w_kv_b = cast_weight(w_kv[None], 0)
                n_kv4 = 4 * kvh * HEAD_DIM
                sp, hp_first = rmsnorm(xp, [kv_norm, attn_norm[layer]], BF16)
                ss, hs_first = rmsnorm(xs, [kv_norm, attn_norm[layer]], BF16)
                kvr_p, kvw_p = matmul_split(sp, w_kv_b, n_kv4)
                kvr_s, kvw_s = matmul_split(ss, w_kv_b, n_kv4)
                outs["kv_rows_p"] = kvr_p.reshape(bp, tp, 4, kvh, HEAD_DIM)
                outs["kv_rows_s"] = kvr_s.reshape(bs, ts, 4, kvh, HEAD_DIM)
                outs["win_p"] = kvw_p.reshape(bp, tp, 2, kvh, HEAD_DIM)[:, max(tp - WINDOW, 0):]
                win_s = kvw_s.reshape(bs, ts, 2, kvh, HEAD_DIM)
                win_buf = cache_win.shape[1]
                outs["win_s"] = jnp.concatenate([cache_win.astype(F32), win_s], axis=1)[:, -win_buf:]
                kvw_s3 = _pad_rows(kvw_s.reshape(bs, ts, kvw_s.shape[1]), ts_pad)
                assert tp % PAGE_SIZE == 0 and (n_pages * PAGE_SIZE + ts) // CMP_STRIDE == n_pages * PAGE_SIZE // CMP_STRIDE
                pt_p = jnp.arange(bp * (tp // PAGE_SIZE), dtype=I32).reshape(bp, tp // PAGE_SIZE)
                ab_p = cmp_project(kvr_p.reshape(bp * tp // PAGE_SIZE, PAGE_SIZE, 4 * kvh, HEAD_DIM), pt_p, w1cat)
                cmp_p = cmp_finish(ab_p, cmp_pe, cmp_w1, cmp_b1, cmp_w2, cmp_b2)
                cache4 = cache_kv.astype(F32).reshape(cache_kv.shape[0], PAGE_SIZE, -1, HEAD_DIM)
                ab_s = cmp_project(cache4, page_table, w1cat)
                cmp_s = cmp_finish(ab_s, cmp_pe, cmp_w1, cmp_b1, cmp_w2, cmp_b2)
            w_qg_b = w_qg[bl].astype(BF16)
            w_q = w_qg_b[:, :hd_all]
            n_gate = w_qg.shape[2] - hd_all
            w_g = jnp.pad(w_qg_b[:, hd_all:], ((0, 0), (0, LANES - n_gate)))
            w_o_b = cast_weight(w_o, bl)

            def gate_layout(gt, bsz, t):
                gt = gt[:, :n_gate].reshape(bsz, t, kvh, 3 * gqa).transpose(0, 2, 1, 3)
                return jnp.pad(gt, ((0, 0), (0, 0), (0, 0), (0, LANES - 3 * gqa)))

            hp = hp_first if layer == n_a else rmsnorm(xp, [attn_norm[layer]], BF16)[0]
            q_p = matmul(hp, w_q, out_dtype=BF16, tm=1024, tn=1024)
            g_p = matmul(hp, w_g, act="sigmoid", tm=1024)
            o_p = nsa_prompt_attend(q_p, g_p, cmp_p, (kvr_p, 2 * kvh), (kvr_p, 3 * kvh), (kvw_p, 0), (kvw_p, kvh), bp, tp)
            xp = matmul(o_p, w_o_b, res=xp, tm=1024, tn=1024)

            hs = hs_first if layer == n_a else rmsnorm(xs, [attn_norm[layer]], BF16)[0]
            q_s = _pad_rows(matmul(hs, w_q, out_dtype=BF16).reshape(bs, ts, hd_all), ts_pad)
            g_s = gate_layout(matmul(hs, w_g, act="sigmoid"), bs, ts)
            g_s = jnp.pad(g_s, ((0, 0), (0, 0), (0, ts_pad - ts), (0, 0)))
            o_s = nsa_sample_attend(q_s, g_s, cmp_s, kvr_s.reshape(bs, ts, 4 * kvh, HEAD_DIM), kvw_s3,
                                    cache_kv.astype(F32), cache_win, page_table, ts)
            xs = matmul(o_s[:, :ts].reshape(bs * ts, hd_all), w_o_b, res=xs)

        f = ffn_w_down.shape[1]
        act, st, act_s, st_s, w_down = ffn_up(
            rmsnorm(xp, [ffn_norm[layer]], BF16)[0], rmsnorm(xs, [ffn_norm[layer]], BF16)[0],
            ffn_w_in.astype(F32), ffn_w_down.astype(F32), layer, ffn_conv_w[layer], ffn_conv_b[layer],
            jnp.zeros((bp, CONV_W - 1, f), F32), state_ffn_conv[layer], tp, ts)
        conv_p.append(st)
        conv_s.append(st_s)
        xp = matmul(act, w_down, res=xp, tm=512, tn=512)
        xs = matmul(act_s, w_down, res=xs, tm=512, tn=256)

    y_p = rmsnorm(xp, [final_norm], F32)[0].reshape(bp, tp, d)
    y_s = rmsnorm(xs, [final_norm], F32)[0].reshape(bs, ts, d)
    return (y_p, y_s, jnp.stack(ssm_re_p), jnp.stack(ssm_im_p), jnp.stack(ssm_re_s), jnp.stack(ssm_im_s),
            jnp.stack(conv_p), jnp.stack(conv_s), outs["kv_rows_p"], outs["kv_rows_s"], outs["win_p"], outs["win_s"])


def kernel(x_prompt, x_sample, state_ssm_re, state_ssm_im, state_ffn_conv, cache_kv, cache_win, page_table, attn_norm, ffn_norm, final_norm, ssm_lam_re, ssm_lam_im, ssm_log_step, ssm_b_re, ssm_b_im, ssm_c_re, ssm_c_im, ssm_d, ssm_w_glu, ffn_w_in, ffn_conv_w, ffn_conv_b, ffn_w_down, kv_norm, w_kv, cmp_w1, cmp_b1, cmp_w2, cmp_b2, cmp_pe, w_qg, w_o):
    return _step(x_prompt, x_sample, state_ssm_re, state_ssm_im, state_ffn_conv, cache_kv, cache_win, page_table,
                 attn_norm, ffn_norm, final_norm, ssm_lam_re, ssm_lam_im, ssm_log_step, ssm_b_re, ssm_b_im,
                 ssm_c_re, ssm_c_im, ssm_d, ssm_w_glu, ffn_w_in, ffn_conv_w, ffn_conv_b, ffn_w_down,
                 kv_norm, w_kv, cmp_w1, cmp_b1, cmp_w2, cmp_b2, cmp_pe, w_qg, w_o)
```
